```python
import math
import jax, jax.numpy as jnp
from jax import lax
import numpy as np

D_MODEL = 1024
BATCH = 4
SEQ = 4096
DEPTH = 1
DEC_BATCH = 32
DEC_SEQ = 64
PAST_LEN = 1024

CHUNK = 64
N_HEADS = 4
HEAD_DIM = 64
V_DIM = 2 * HEAD_DIM
ATTN_WIDTH = N_HEADS * V_DIM
SSM_WIDTH = 512
GROUP_SIZE = 16
N_GROUPS = SSM_WIDTH // GROUP_SIZE
STATE_DIM = 64
D_FF = 4 * D_MODEL
ROPE_THETA = 10000.0
Q_BLOCK = 128
LN_EPS = 1e-5
RMS_EPS = 1e-5
NEG_INF = -1e30
DEEPNORM_ALPHA = (2.0 * DEPTH) ** 0.25
DEEPNORM_BETA = (8.0 * DEPTH) ** -0.25

Q_OFF = 0
K_OFF = Q_OFF + N_HEADS * 2 * HEAD_DIM
V_OFF = K_OFF + N_HEADS * 2 * HEAD_DIM
U_OFF = V_OFF + N_HEADS * V_DIM
G_OFF = U_OFF + SSM_WIDTH
IN_COLS = G_OFF + 2 * D_MODEL

kernel_name = "diffattn_s5_gated_streaming_step"


def layer_norm(x, g, b):
    xf = x.astype(jnp.float32)
    mu = jnp.mean(xf, -1, keepdims=True)
    var = jnp.mean(jnp.square(xf - mu), -1, keepdims=True)
    return ((xf - mu) * lax.rsqrt(var + LN_EPS) * g + b).astype(x.dtype)


def rope(t, pos):
    inv = 1.0 / (ROPE_THETA ** (jnp.arange(0, HEAD_DIM, 2, dtype=jnp.float32) / HEAD_DIM))
    ang = pos.astype(jnp.float32)[:, None] * inv[None, :]
    c = jnp.cos(ang)[None, :, None, None, :]
    s = jnp.sin(ang)[None, :, None, None, :]
    tf = t.astype(jnp.float32)
    t1, t2 = tf[..., :HEAD_DIM // 2], tf[..., HEAD_DIM // 2:]
    return jnp.concatenate([t1 * c - t2 * s, t1 * s + t2 * c], axis=-1).astype(t.dtype)


def split_projection(x, w_in):
    bsz, n = x.shape[0], x.shape[1]
    z = x @ w_in
    q = z[..., Q_OFF:K_OFF].reshape(bsz, n, N_HEADS, 2, HEAD_DIM)
    k = z[..., K_OFF:V_OFF].reshape(bsz, n, N_HEADS, 2, HEAD_DIM)
    v = z[..., V_OFF:U_OFF].reshape(bsz, n, N_HEADS, V_DIM)
    u = z[..., U_OFF:G_OFF]
    g = z[..., G_OFF:].reshape(bsz, n, 2, D_MODEL)
    return q, k, v, u, g


def diff_lambda(lq1, lk1, lq2, lk2, lam_init):
    f = jnp.float32
    return (jnp.exp(jnp.sum(lq1.astype(f) * lk1.astype(f)))
            - jnp.exp(jnp.sum(lq2.astype(f) * lk2.astype(f))) + lam_init)


def diff_attend(q, k, v, mask, lam):
    s = jnp.einsum('bqhmd,bkhmd->bhmqk', q, k, preferred_element_type=jnp.float32) * (HEAD_DIM ** -0.5)
    if mask is not None:
        s = jnp.where(mask, s, NEG_INF)
    p = jax.nn.softmax(s, axis=-1)
    a = p[:, :, 0] - lam * p[:, :, 1]
    return jnp.einsum('bhqk,bkhe->bqhe', a, v.astype(jnp.float32))


def diff_post(o, gain, lam_init):
    o = o * lax.rsqrt(jnp.mean(jnp.square(o), -1, keepdims=True) + RMS_EPS) * gain.astype(jnp.float32) * (1.0 - lam_init)
    return o.reshape(o.shape[0], o.shape[1], ATTN_WIDTH)


def prompt_attention(q, k, v, lam):
    bsz, n = q.shape[0], q.shape[1]
    nb = n // Q_BLOCK
    qb = jnp.moveaxis(q.reshape(bsz, nb, Q_BLOCK, N_HEADS, 2, HEAD_DIM), 1, 0)
    k_chunk = jnp.arange(n) // CHUNK

    def one_block(args):
        q_i, i = args
        q_chunk = (i * Q_BLOCK + jnp.arange(Q_BLOCK)) // CHUNK
        return diff_attend(q_i, k, v, k_chunk[None, :] <= q_chunk[:, None], lam)

    o = lax.map(one_block, (qb, jnp.arange(nb)))
    return jnp.moveaxis(o, 0, 1).reshape(bsz, n, N_HEADS, V_DIM)


def _cplx_affine_combine(e1, e2):
    a1r, a1i, b1r, b1i = e1
    a2r, a2i, b2r, b2i = e2
    return (a1r * a2r - a1i * a2i,
            a1r * a2i + a1i * a2r,
            a2r * b1r - a2i * b1i + b2r,
            a2r * b1i + a2i * b1r + b2i)


def s5_ssm(u, h0_re, h0_im, a_re, a_im, b_re, b_im, c_re, c_im, d, log_dt):
    f = jnp.float32
    bsz, n = u.shape[0], u.shape[1]
    a_re, a_im = a_re.astype(f), a_im.astype(f)
    b_re, b_im = b_re.astype(f), b_im.astype(f)
    dt = jnp.exp(log_dt.astype(f))[:, None]
    mag = jnp.exp(a_re * dt)
    ar, ai = mag * jnp.cos(a_im * dt), mag * jnp.sin(a_im * dt)
    den = jnp.square(a_re) + jnp.square(a_im)
    cr = ((ar - 1.0) * a_re + ai * a_im) / den
    ci = (ai * a_re - (ar - 1.0) * a_im) / den
    bbr = cr[..., None] * b_re - ci[..., None] * b_im
    bbi = cr[..., None] * b_im + ci[..., None] * b_re
    uf = u.astype(f)
    ug = uf.reshape(bsz, n, N_GROUPS, GROUP_SIZE)
    bur = jnp.einsum('blgc,gpc->blgp', ug, bbr)
    bui = jnp.einsum('blgc,gpc->blgp', ug, bbi)
    shape = (1, n, N_GROUPS, STATE_DIM)
    acr, aci, sr, si = lax.associative_scan(
        _cplx_affine_combine,
        (jnp.broadcast_to(ar, shape), jnp.broadcast_to(ai, shape), bur, bui), axis=1)
    h0r, h0i = h0_re.astype(f)[:, None], h0_im.astype(f)[:, None]
    sr = sr + acr * h0r - aci * h0i
    si = si + acr * h0i + aci * h0r
    y = (jnp.einsum('blgp,gcp->blgc', sr, c_re.astype(f))
         - jnp.einsum('blgp,gcp->blgc', si, c_im.astype(f)))
    y = y.reshape(bsz, n, SSM_WIDTH) + d.astype(f) * uf
    return y.astype(u.dtype), sr[:, -1], si[:, -1]


def merge_and_mlp(x, attn_o, ssm_y, g, w_attn_proj, w_glu_a, w_glu_b, w_out,
                  ln1_g, ln1_b, w_ff1, w_ff2, ln2_g, ln2_b):
    a_branch = attn_o.astype(x.dtype) @ w_attn_proj
    s_act = jax.nn.gelu(ssm_y)
    s_branch = (s_act @ w_glu_a) * jax.nn.sigmoid(s_act @ w_glu_b)
    m = jax.nn.sigmoid(g[:, :, 0]) * a_branch + jax.nn.sigmoid(g[:, :, 1]) * s_branch
    h = layer_norm(DEEPNORM_ALPHA * x + m @ w_out, ln1_g, ln1_b)
    f = jnp.square(jax.nn.relu(h @ w_ff1)) @ w_ff2
    return layer_norm(DEEPNORM_ALPHA * h + f, ln2_g, ln2_b)


def setup_inputs(seed: int = 0) -> dict:
    key = jax.random.key(seed)
    ks = jax.random.split(key, 32)
    f = jnp.float32

    def nrm(k, shape, scale):
        return jax.random.normal(k, shape, f) * scale

    x_prompt = nrm(ks[0], (BATCH, SEQ, D_MODEL), 1.0)
    x_sample = nrm(ks[1], (DEC_BATCH, DEC_SEQ, D_MODEL), 1.0)
    cache_k = nrm(ks[2], (DEPTH, DEC_BATCH, PAST_LEN, N_HEADS, V_DIM), 1.0)
    cache_v = nrm(ks[3], (DEPTH, DEC_BATCH, PAST_LEN, N_HEADS, V_DIM), DEEPNORM_BETA)
    state_ssm_re = nrm(ks[4], (DEPTH, DEC_BATCH, N_GROUPS, STATE_DIM), 0.3)
    state_ssm_im = nrm(ks[5], (DEPTH, DEC_BATCH, N_GROUPS, STATE_DIM), 0.3)

    col_scale = jnp.ones((IN_COLS,), f).at[V_OFF:U_OFF].set(DEEPNORM_BETA)
    w_in = nrm(ks[6], (DEPTH, D_MODEL, IN_COLS), D_MODEL ** -0.5) * col_scale
    lambda_q1 = nrm(ks[7], (DEPTH, HEAD_DIM), 0.1)
    lambda_k1 = nrm(ks[8], (DEPTH, HEAD_DIM), 0.1)
    lambda_q2 = nrm(ks[9], (DEPTH, HEAD_DIM), 0.1)
    lambda_k2 = nrm(ks[10], (DEPTH, HEAD_DIM), 0.1)
    subln_gain = 1.0 + nrm(ks[11], (DEPTH, V_DIM), 0.02)

    n_idx = jnp.arange(STATE_DIM, dtype=f)
    ssm_a_re = -0.5 + nrm(ks[12], (DEPTH, N_GROUPS, STATE_DIM), 1e-3)
    ssm_a_im = jnp.pi * n_idx + nrm(ks[13], (DEPTH, N_GROUPS, STATE_DIM), 1e-3)
    ssm_b_re = nrm(ks[14], (DEPTH, N_GROUPS, STATE_DIM, GROUP_SIZE), (2.0 * GROUP_SIZE) ** -0.5)
    ssm_b_im = nrm(ks[15], (DEPTH, N_GROUPS, STATE_DIM, GROUP_SIZE), (2.0 * GROUP_SIZE) ** -0.5)
    ssm_c_re = nrm(ks[16], (DEPTH, N_GROUPS, GROUP_SIZE, STATE_DIM), (2.0 * STATE_DIM) ** -0.5)
    ssm_c_im = nrm(ks[17], (DEPTH, N_GROUPS, GROUP_SIZE, STATE_DIM), (2.0 * STATE_DIM) ** -0.5)
    ssm_d = nrm(ks[18], (DEPTH, SSM_WIDTH), 1.0)
    ssm_log_dt = jax.random.uniform(ks[19], (DEPTH, N_GROUPS), f, math.log(1e-3), math.log(1e-1))

    w_attn_proj = nrm(ks[20], (DEPTH, ATTN_WIDTH, D_MODEL), ATTN_WIDTH ** -0.5)
    w_glu_a = nrm(ks[21], (DEPTH, SSM_WIDTH, D_MODEL), SSM_WIDTH ** -0.5)
    w_glu_b = nrm(ks[22], (DEPTH, SSM_WIDTH, D_MODEL), SSM_WIDTH ** -0.5)
    w_out = nrm(ks[23], (DEPTH, D_MODEL, D_MODEL), DEEPNORM_BETA * D_MODEL ** -0.5)
    ln1_g = 1.0 + nrm(ks[24], (DEPTH, D_MODEL), 0.02)
    ln1_b = nrm(ks[25], (DEPTH, D_MODEL), 0.02)
    w_ff1 = nrm(ks[26], (DEPTH, D_MODEL, D_FF), DEEPNORM_BETA * D_MODEL ** -0.5)
    w_ff2 = nrm(ks[27], (DEPTH, D_FF, D_MODEL), DEEPNORM_BETA * D_FF ** -0.5)
    ln2_g = 1.0 + nrm(ks[28], (DEPTH, D_MODEL), 0.02)
    ln2_b = nrm(ks[29], (DEPTH, D_MODEL), 0.02)
    return {
        "x_prompt": x_prompt, "x_sample": x_sample,
        "cache_k": cache_k, "cache_v": cache_v,
        "state_ssm_re": state_ssm_re, "state_ssm_im": state_ssm_im,
        "w_in": w_in,
        "lambda_q1": lambda_q1, "lambda_k1": lambda_k1, "lambda_q2": lambda_q2, "lambda_k2": lambda_k2,
        "subln_gain": subln_gain,
        "ssm_a_re": ssm_a_re, "ssm_a_im": ssm_a_im,
        "ssm_b_re": ssm_b_re, "ssm_b_im": ssm_b_im,
        "ssm_c_re": ssm_c_re, "ssm_c_im": ssm_c_im,
        "ssm_d": ssm_d, "ssm_log_dt": ssm_log_dt,
        "w_attn_proj": w_attn_proj, "w_glu_a": w_glu_a, "w_glu_b": w_glu_b, "w_out": w_out,
        "ln1_g": ln1_g, "ln1_b": ln1_b, "w_ff1": w_ff1, "w_ff2": w_ff2,
        "ln2_g": ln2_g, "ln2_b": ln2_b,
    }


def reference(x_prompt, x_sample, cache_k, cache_v, state_ssm_re, state_ssm_im, w_in,
              lambda_q1, lambda_k1, lambda_q2, lambda_k2, subln_gain,
              ssm_a_re, ssm_a_im, ssm_b_re, ssm_b_im, ssm_c_re, ssm_c_im, ssm_d, ssm_log_dt,
              w_attn_proj, w_glu_a, w_glu_b, w_out, ln1_g, ln1_b, w_ff1, w_ff2, ln2_g, ln2_b):
    xp, xs = x_prompt, x_sample
    bp, n_p = xp.shape[0], xp.shape[1]
    bs, n_s = xs.shape[0], xs.shape[1]
    past = cache_k.shape[2]
    pos_p = jnp.arange(n_p)
    pos_s = past + jnp.arange(n_s)
    kp_l, vp_l, srp_l, sip_l, ks_l, vs_l, srs_l, sis_l = [], [], [], [], [], [], [], []
    for l in range(DEPTH):
        lam_init = 0.8 - 0.6 * math.exp(-0.3 * l)
        lam = diff_lambda(lambda_q1[l], lambda_k1[l], lambda_q2[l], lambda_k2[l], lam_init)
        ssm_p = (ssm_a_re[l], ssm_a_im[l], ssm_b_re[l], ssm_b_im[l],
                 ssm_c_re[l], ssm_c_im[l], ssm_d[l], ssm_log_dt[l])
        tail = (w_attn_proj[l], w_glu_a[l], w_glu_b[l], w_out[l],
                ln1_g[l], ln1_b[l], w_ff1[l], w_ff2[l], ln2_g[l], ln2_b[l])

        q, k, v, u, g = split_projection(xp, w_in[l])
        q, k = rope(q, pos_p), rope(k, pos_p)
        ao = diff_post(prompt_attention(q, k, v, lam), subln_gain[l], lam_init)
        zero_state = jnp.zeros((bp, N_GROUPS, STATE_DIM), jnp.float32)
        sy, sr, si = s5_ssm(u, zero_state, zero_state, *ssm_p)
        kp_l.append(k.reshape(bp, n_p, N_HEADS, V_DIM))
        vp_l.append(v)
        srp_l.append(sr)
        sip_l.append(si)
        xp = merge_and_mlp(xp, ao, sy, g, *tail)

        q, k, v, u, g = split_projection(xs, w_in[l])
        q, k = rope(q, pos_s), rope(k, pos_s)
        k_all = jnp.concatenate(
            [cache_k[l].reshape(bs, past, N_HEADS, 2, HEAD_DIM).astype(k.dtype), k], axis=1)
        v_all = jnp.concatenate([cache_v[l].astype(v.dtype), v], axis=1)
        ao = diff_post(diff_attend(q, k_all, v_all, None, lam), subln_gain[l], lam_init)
        sy, sr, si = s5_ssm(u, state_ssm_re[l], state_ssm_im[l], *ssm_p)
        ks_l.append(k.reshape(bs, n_s, N_HEADS, V_DIM))
        vs_l.append(v)
        srs_l.append(sr)
        sis_l.append(si)
        xs = merge_and_mlp(xs, ao, sy, g, *tail)

    return (xp, xs,
            jnp.stack(kp_l), jnp.stack(vp_l), jnp.stack(srp_l), jnp.stack(sip_l),
            jnp.stack(ks_l), jnp.stack(vs_l), jnp.stack(srs_l), jnp.stack(sis_l))
```

```python
import functools
import math

import jax
import jax.numpy as jnp
from jax import lax
from jax.experimental import pallas as pl
from jax.experimental.pallas import tpu as pltpu

D_MODEL = 1024
CHUNK = 64
N_HEADS = 4
HEAD_DIM = 64
V_DIM = 2 * HEAD_DIM
ATTN_WIDTH = N_HEADS * V_DIM
SSM_WIDTH = 512
GROUP_SIZE = 16
N_GROUPS = SSM_WIDTH // GROUP_SIZE
STATE_DIM = 64
D_FF = 4 * D_MODEL
ROPE_THETA = 10000.0
LN_EPS = 1e-5
RMS_EPS = 1e-5
NEG_INF = -1e30
DEPTH = 1
DEEPNORM_ALPHA = (2.0 * DEPTH) ** 0.25
QKVU_COLS = 3 * ATTN_WIDTH + SSM_WIDTH
LOG2E = 1.4426950408889634

LANES = 128
VT_ROWS = V_DIM + 16
VMEM_LIMIT = 56 * 1024 * 1024

F32 = jnp.float32
BF16 = jnp.bfloat16


def _cparams(*sem):
    return pltpu.CompilerParams(dimension_semantics=sem, vmem_limit_bytes=VMEM_LIMIT)


def _nt_dot(a, b):
    return lax.dot_general(a, b, (((1,), (1,)), ((), ())), preferred_element_type=F32)


def _dot(a, b):
    return jnp.dot(a, b, preferred_element_type=F32)


def _rotary(z, cos, sin_signed, first_half):
    swapped = jnp.where(first_half, pltpu.roll(z, 96, 1), pltpu.roll(z, 32, 1))
    return z * cos + swapped * sin_signed


def _proj_kernel(x_ref, w_ref, cos_ref, sin_ref, q_ref, k_ref, v_ref, u_ref, *rest, emit_t):
    xb = x_ref[...].astype(BF16)
    cos = cos_ref[...]
    sin = sin_ref[...]
    lane = lax.broadcasted_iota(jnp.int32, cos.shape, 1)
    first_half = (lane % HEAD_DIM) < (HEAD_DIM // 2)
    zq = _dot(xb, w_ref[:, 0:ATTN_WIDTH])
    zk = _dot(xb, w_ref[:, ATTN_WIDTH:2 * ATTN_WIDTH])
    for h in range(N_HEADS):
        sl = slice(h * V_DIM, (h + 1) * V_DIM)
        q_ref[:, sl] = (_rotary(zq[:, sl], cos, sin, first_half) * (LOG2E * HEAD_DIM ** -0.5)).astype(BF16)
        kr = _rotary(zk[:, sl], cos, sin, first_half)
        k_ref[:, sl] = kr
        if emit_t:
            rest[0][:, sl] = kr.astype(BF16)
    zv = _dot(xb, w_ref[:, 2 * ATTN_WIDTH:3 * ATTN_WIDTH])
    v_ref[...] = zv
    if emit_t:
        vt_ref = rest[1]
        zvt = zv.T.astype(BF16)
        ones = jnp.ones((VT_ROWS - V_DIM, zvt.shape[1]), BF16)
        for h in range(N_HEADS):
            vt_ref[h, 0:V_DIM, :] = zvt[h * V_DIM:(h + 1) * V_DIM, :]
            vt_ref[h, V_DIM:VT_ROWS, :] = ones
    u_ref[...] = _dot(xb, w_ref[:, 3 * ATTN_WIDTH:QKVU_COLS])


def _project(x2d, w_qkvu, cos_t, sin_t, seq_len, tm, emit_t):
    t_tokens = x2d.shape[0]
    n_tiles = t_tokens // tm
    n_pos_tiles = cos_t.shape[0] // tm
    tiles_per_seq = max(seq_len // tm, 1)
    row = lambda i: (i, 0)
    pos = lambda i: (i % n_pos_tiles, 0)
    out_shape = [jax.ShapeDtypeStruct((t_tokens, ATTN_WIDTH), BF16),
                 jax.ShapeDtypeStruct((t_tokens, ATTN_WIDTH), F32),
                 jax.ShapeDtypeStruct((t_tokens, ATTN_WIDTH), F32),
                 jax.ShapeDtypeStruct((t_tokens, SSM_WIDTH), F32)]
    out_specs = [pl.BlockSpec((tm, ATTN_WIDTH), row)] * 3 + [pl.BlockSpec((tm, SSM_WIDTH), row)]
    if emit_t:
        bsz = t_tokens // seq_len
        out_shape += [jax.ShapeDtypeStruct((t_tokens, ATTN_WIDTH), BF16),
                      jax.ShapeDtypeStruct((bsz, N_HEADS, VT_ROWS, seq_len), BF16)]
        out_specs += [pl.BlockSpec((tm, ATTN_WIDTH), row),
                      pl.BlockSpec((None, N_HEADS, VT_ROWS, tm),
                                   lambda i: (i // tiles_per_seq, 0, 0, i % tiles_per_seq))]
    return pl.pallas_call(
        functools.partial(_proj_kernel, emit_t=emit_t),
        grid=(n_tiles,),
        in_specs=[pl.BlockSpec((tm, D_MODEL), row),
                  pl.BlockSpec((D_MODEL, QKVU_COLS), lambda i: (0, 0)),
                  pl.BlockSpec((tm, LANES), pos),
                  pl.BlockSpec((tm, LANES), pos)],
        out_specs=out_specs,
        out_shape=out_shape,
        compiler_params=_cparams("parallel"),
        name="proj_t" if emit_t else "proj",
    )(x2d, w_qkvu, cos_t, sin_t)


def _diff_lambda(lq1, lk1, lq2, lk2, lam_init):
    return (jnp.exp(jnp.sum(lq1 * lk1, axis=1, keepdims=True))
            - jnp.exp(jnp.sum(lq2 * lk2, axis=1, keepdims=True)) + lam_init)


def _sub_norm(d, gain, lam_init):
    ms = jnp.mean(jnp.square(d), axis=1, keepdims=True)
    return d * lax.rsqrt(ms + RMS_EPS) * gain * (1.0 - lam_init)


def _stack_maps(q):
    lane = lax.broadcasted_iota(jnp.int32, q.shape, 1)
    zero = jnp.zeros_like(q)
    return jnp.concatenate([jnp.where(lane < HEAD_DIM, q, zero), jnp.where(lane >= HEAD_DIM, q, zero)], axis=0)


def _prompt_attn_kernel(lq1_ref, lk1_ref, lq2_ref, lk2_ref, gain_ref, q_ref, k_ref, vt_ref, o_ref,
                        m_scr, acc_scr, *, tq, lam_init):
    qi = pl.program_id(2)
    qx = _stack_maps(q_ref[...])
    m_scr[...] = jnp.full(m_scr.shape, NEG_INF, F32)
    acc_scr[...] = jnp.zeros(acc_scr.shape, F32)

    def step(j, mask):
        start = pl.multiple_of(j * tq, tq)
        st = _nt_dot(k_ref[pl.ds(start, tq), :], qx)
        if mask is not None:
            st = jnp.where(mask, st, NEG_INF)
        m_old = m_scr[...]
        m_new = jnp.maximum(m_old, jnp.max(st, axis=0, keepdims=True))
        alpha = jnp.exp2(m_old - m_new)
        p = jnp.exp2(st - m_new).astype(BF16)
        acc_scr[...] = acc_scr[...] * alpha + _dot(vt_ref[:, pl.ds(start, tq)], p)
        m_scr[...] = m_new

    def body(j, carry):
        step(j, None)
        return carry

    lax.fori_loop(0, qi, body, 0)
    key_chunk = lax.broadcasted_iota(jnp.int32, (tq, 2 * tq), 0) // CHUNK
    qry_chunk = (lax.broadcasted_iota(jnp.int32, (tq, 2 * tq), 1) % tq) // CHUNK
    step(qi, key_chunk <= qry_chunk)

    acc = acc_scr[...]
    o = acc[0:V_DIM, :] * (1.0 / acc[V_DIM:V_DIM + 1, :])
    lam = _diff_lambda(lq1_ref[...], lk1_ref[...], lq2_ref[...], lk2_ref[...], lam_init)
    d = (o[:, 0:tq] - lam * o[:, tq:2 * tq]).T
    o_ref[...] = _sub_norm(d, gain_ref[...], lam_init).astype(BF16)


def _prompt_attention(lams, gain, q, kb, vt, bsz, seq_len, tq, lam_init):
    nq = seq_len // tq
    small = lambda b, h, i: (0, 0)
    return pl.pallas_call(
        functools.partial(_prompt_attn_kernel, tq=tq, lam_init=lam_init),
        grid=(bsz, N_HEADS, nq),
        in_specs=[pl.BlockSpec((1, HEAD_DIM), small)] * 4 + [
            pl.BlockSpec((1, V_DIM), small),
            pl.BlockSpec((tq, V_DIM), lambda b, h, i: (b * nq + i, h)),
            pl.BlockSpec((seq_len, V_DIM), lambda b, h, i: (b, h)),
            pl.BlockSpec((None, None, VT_ROWS, seq_len), lambda b, h, i: (b, h, 0, 0))],
        out_specs=pl.BlockSpec((tq, V_DIM), lambda b, h, i: (b * nq + i, h)),
        out_shape=jax.ShapeDtypeStruct((bsz * seq_len, ATTN_WIDTH), BF16),
        scratch_shapes=[pltpu.VMEM((1, 2 * tq), F32), pltpu.VMEM((VT_ROWS, 2 * tq), F32)],
        compiler_params=_cparams("parallel", "parallel", "arbitrary"),
        name="prompt_attn",
    )(*lams, gain, q, kb, vt)


def _sample_attn_kernel(lq1_ref, lk1_ref, lq2_ref, lk2_ref, gain_ref, q_ref, k_ref, v_ref, ck_ref, cv_ref,
                        o_ref, *, lam_init):
    n_new = q_ref.shape[0]
    lam = _diff_lambda(lq1_ref[...], lk1_ref[...], lq2_ref[...], lk2_ref[...], lam_init)
    gain = gain_ref[...]
    for h in range(N_HEADS):
        sl = slice(h * V_DIM, (h + 1) * V_DIM)
        qx = _stack_maps(q_ref[:, sl])
        s_c = _nt_dot(qx, ck_ref[:, sl].astype(BF16))
        s_n = _nt_dot(qx, k_ref[:, sl].astype(BF16))
        m = jnp.maximum(jnp.max(s_c, axis=1, keepdims=True), jnp.max(s_n, axis=1, keepdims=True))
        p_c = jnp.exp2(s_c - m)
        p_n = jnp.exp2(s_n - m)
        denom = jnp.sum(p_c, axis=1, keepdims=True) + jnp.sum(p_n, axis=1, keepdims=True)
        o = (_dot(p_c.astype(BF16), cv_ref[:, sl].astype(BF16))
             + _dot(p_n.astype(BF16), v_ref[:, sl].astype(BF16))) * (1.0 / denom)
        d = o[0:n_new, :] - lam * o[n_new:2 * n_new, :]
        o_ref[:, sl] = _sub_norm(d, gain, lam_init).astype(BF16)


def _sample_attention(lams, gain, q, k, v, cache_k, cache_v, lam_init):
    bsz, past = cache_k.shape[0], cache_k.shape[1]
    n_new = q.shape[0] // bsz
    small = lambda b: (0, 0)
    row = lambda b: (b, 0)
    return pl.pallas_call(
        functools.partial(_sample_attn_kernel, lam_init=lam_init),
        grid=(bsz,),
        in_specs=[pl.BlockSpec((1, HEAD_DIM), small)] * 4 + [
            pl.BlockSpec((1, V_DIM), small),
            pl.BlockSpec((n_new, ATTN_WIDTH), row),
            pl.BlockSpec((n_new, ATTN_WIDTH), row),
            pl.BlockSpec((n_new, ATTN_WIDTH), row),
            pl.BlockSpec((None, past, ATTN_WIDTH), lambda b: (b, 0, 0)),
            pl.BlockSpec((None, past, ATTN_WIDTH), lambda b: (b, 0, 0))],
        out_specs=pl.BlockSpec((n_new, ATTN_WIDTH), row),
        out_shape=jax.ShapeDtypeStruct(q.shape, BF16),
        compiler_params=_cparams("parallel"),
        name="sample_attn",
    )(*lams, gain, q, k, v, cache_k, cache_v)


def _ssm_tables(a_re, a_im, b_re, b_im, c_re, c_im, log_dt):
    dt = jnp.exp(log_dt)[:, None]
    lam_re, lam_im = a_re * dt, a_im * dt
    mag = jnp.exp(lam_re)
    ar, ai = mag * jnp.cos(lam_im), mag * jnp.sin(lam_im)
    den = jnp.square(a_re) + jnp.square(a_im)
    cr = ((ar - 1.0) * a_re + ai * a_im) / den
    ci = (ai * a_re - (ar - 1.0) * a_im) / den
    bbr = cr[..., None] * b_re - ci[..., None] * b_im
    bbi = cr[..., None] * b_im + ci[..., None] * b_re
    tau = jnp.arange(CHUNK + 1, dtype=F32)
    pm = jnp.exp(lam_re[..., None] * tau)
    pr, pi = pm * jnp.cos(lam_im[..., None] * tau), pm * jnp.sin(lam_im[..., None] * tau)
    hp = lax.Precision.HIGHEST
    zr = c_re[:, :, None, :] * bbr.transpose(0, 2, 1)[:, None] - c_im[:, :, None, :] * bbi.transpose(0, 2, 1)[:, None]
    zi = c_re[:, :, None, :] * bbi.transpose(0, 2, 1)[:, None] + c_im[:, :, None, :] * bbr.transpose(0, 2, 1)[:, None]
    kv = (jnp.einsum('gcdp,gpt->gdct', zr, pr[..., :CHUNK], precision=hp)
          - jnp.einsum('gcdp,gpt->gdct', zi, pi[..., :CHUNK], precision=hp))
    kv = jnp.concatenate([kv, jnp.zeros_like(kv)], axis=-1)
    prr, pir = pr[..., CHUNK - 1::-1][..., :CHUNK], pi[..., CHUNK - 1::-1][..., :CHUNK]
    w_re = prr[:, :, None, :] * bbr[..., None] - pir[:, :, None, :] * bbi[..., None]
    w_im = prr[:, :, None, :] * bbi[..., None] + pir[:, :, None, :] * bbr[..., None]
    g = a_re.shape[0]
    w = jnp.concatenate([w_re, w_im], axis=1).reshape(g, 2 * STATE_DIM, GROUP_SIZE * CHUNK)
    p1r, p1i = pr[..., 1:], pi[..., 1:]
    v_re = c_re[:, :, None, :] * p1r.transpose(0, 2, 1)[:, None] - c_im[:, :, None, :] * p1i.transpose(0, 2, 1)[:, None]
    v_im = c_re[:, :, None, :] * p1i.transpose(0, 2, 1)[:, None] + c_im[:, :, None, :] * p1r.transpose(0, 2, 1)[:, None]
    vm = jnp.concatenate([v_re, -v_im], axis=-1).reshape(g, GROUP_SIZE * CHUNK, 2 * STATE_DIM)
    acr, aci = pr[..., CHUNK], pi[..., CHUNK]
    a_c = jnp.concatenate([acr, acr], axis=-1)
    a_s = jnp.concatenate([-aci, aci], axis=-1)
    return kv, w, vm, a_c, a_s


def _ssm_kernel(kv_ref, w_ref, vm_ref, ac_ref, as_ref, x_ref, s0_ref, y_ref, sfp_ref, sfs_ref, mt_scr, st_scr,
                *, n_seq, n_chunk):
    n_p = n_seq * n_chunk
    lane = lax.broadcasted_iota(jnp.int32, (8, LANES), 1)
    low = lane < CHUNK
    for cp in range(GROUP_SIZE):
        for c2 in range(GROUP_SIZE // 2):
            ka = jnp.broadcast_to(kv_ref[cp, 2 * c2:2 * c2 + 1, :], (8, LANES))
            kb = jnp.broadcast_to(kv_ref[cp, 2 * c2 + 1:2 * c2 + 2, :], (8, LANES))
            for r in range(CHUNK // 8):
                ta = pltpu.roll(ka, 8 * r, 1, stride=1, stride_axis=0)
                tb = pltpu.roll(kb, CHUNK + 8 * r, 1, stride=1, stride_axis=0)
                mt_scr[cp * CHUNK + 8 * r:cp * CHUNK + 8 * r + 8, c2 * LANES:(c2 + 1) * LANES] = (
                    jnp.where(low, ta, tb).astype(BF16))
    x = x_ref[...]
    s_loc = _nt_dot(x, w_ref[...])
    a_c, a_s = ac_ref[...], as_ref[...]

    def cmul(pc, ps, s):
        return pc * s + ps * pltpu.roll(s, STATE_DIM, 1)

    s_inc = s_loc[0:n_p, :]
    row = lax.broadcasted_iota(jnp.int32, (n_p, 2 * STATE_DIM), 0) % n_chunk
    pc, ps = a_c, a_s
    dist = 1
    while dist < n_chunk:
        shifted = jnp.where(row >= dist, pltpu.roll(s_inc, dist, 0), 0.0)
        s_inc = s_inc + cmul(pc, ps, shifted)
        pc, ps = pc * pc - ps * ps, 2.0 * pc * ps
        dist *= 2
    st_scr[0:n_p, :] = s_inc
    s_prev_p = jnp.where(row >= 1, pltpu.roll(s_inc, 1, 0), 0.0)
    s0 = s0_ref[...]
    s_prev = jnp.concatenate([s_prev_p, s0], axis=0)
    y_ref[...] = _dot(x, mt_scr[...]) + _nt_dot(s_prev.astype(BF16), vm_ref[...])
    sfp_ref[...] = st_scr[pl.ds(n_chunk - 1, n_seq, stride=n_chunk), :]
    sfs_ref[...] = cmul(a_c, a_s, s0) + s_loc[n_p:, :]


def _ssm(tables, x_rows, s0, n_seq, n_chunk):
    kv, w, vm, a_c, a_s = tables
    n_groups, rows = x_rows.shape[0], x_rows.shape[1]
    n_s = s0.shape[1]
    width = GROUP_SIZE * CHUNK
    per_g3 = lambda g: (g, 0, 0)
    return pl.pallas_call(
        functools.partial(_ssm_kernel, n_seq=n_seq, n_chunk=n_chunk),
        grid=(n_groups,),
        in_specs=[pl.BlockSpec((None, GROUP_SIZE, GROUP_SIZE, LANES), lambda g: (g, 0, 0, 0)),
                  pl.BlockSpec((None, 2 * STATE_DIM, width), per_g3),
                  pl.BlockSpec((None, width, 2 * STATE_DIM), per_g3),
                  pl.BlockSpec((None, 1, 2 * STATE_DIM), per_g3),
                  pl.BlockSpec((None, 1, 2 * STATE_DIM), per_g3),
                  pl.BlockSpec((None, rows, width), per_g3),
                  pl.BlockSpec((None, n_s, 2 * STATE_DIM), per_g3)],
        out_specs=[pl.BlockSpec((None, rows, width), per_g3),
                   pl.BlockSpec((None, n_seq, 2 * STATE_DIM), per_g3),
                   pl.BlockSpec((None, n_s, 2 * STATE_DIM), per_g3)],
        out_shape=[jax.ShapeDtypeStruct((n_groups, rows, width), F32),
                   jax.ShapeDtypeStruct((n_groups, n_seq, 2 * STATE_DIM), F32),
                   jax.ShapeDtypeStruct((n_groups, n_s, 2 * STATE_DIM), F32)],
        scratch_shapes=[pltpu.VMEM((width, width), BF16), pltpu.VMEM((n_seq * n_chunk, 2 * STATE_DIM), F32)],
        compiler_params=_cparams("parallel"),
        name="ssm",
    )(kv, w, vm, a_c, a_s, x_rows, s0)


def _layer_norm(x, g, b):
    mu = jnp.mean(x, axis=1, keepdims=True)
    xc = x - mu
    var = jnp.mean(jnp.square(xc), axis=1, keepdims=True)
    return xc * lax.rsqrt(var + LN_EPS) * g + b


def _gelu_tanh(x):
    return 0.5 * x * (1.0 + jnp.tanh(math.sqrt(2.0 / math.pi) * (x + 0.044715 * (x * x * x))))


def _merge_kernel(x_ref, ao_ref, ys_ref, u_ref, d_ref, wg_ref, wap_ref, wglu_ref, wout_ref, g1_ref, b1_ref, h_ref):
    x = x_ref[...]
    xb = x.astype(BF16)
    a_branch = _dot(ao_ref[...], wap_ref[...])
    s_act = _gelu_tanh(ys_ref[...] + d_ref[...] * u_ref[...]).astype(BF16)
    s_branch = _dot(s_act, wglu_ref[:, 0:D_MODEL]) * jax.nn.sigmoid(_dot(s_act, wglu_ref[:, D_MODEL:2 * D_MODEL]))
    m = (jax.nn.sigmoid(_dot(xb, wg_ref[:, 0:D_MODEL])) * a_branch
         + jax.nn.sigmoid(_dot(xb, wg_ref[:, D_MODEL:2 * D_MODEL])) * s_branch)
    h_ref[...] = _layer_norm(DEEPNORM_ALPHA * x + _dot(m.astype(BF16), wout_ref[...]), g1_ref[...], b1_ref[...])


def _merge(x2d, ao, ys, u, d, w_gate, w_ap, w_glu, w_out, ln_g, ln_b, tm):
    t_tokens = x2d.shape[0]
    row = lambda i: (i, 0)
    const = lambda i: (0, 0)
    return pl.pallas_call(
        _merge_kernel,
        grid=(t_tokens // tm,),
        in_specs=[pl.BlockSpec((tm, D_MODEL), row),
                  pl.BlockSpec((tm, ATTN_WIDTH), row),
                  pl.BlockSpec((tm, SSM_WIDTH), row),
                  pl.BlockSpec((tm, SSM_WIDTH), row),
                  pl.BlockSpec((1, SSM_WIDTH), const),
                  pl.BlockSpec((D_MODEL, 2 * D_MODEL), const),
                  pl.BlockSpec((ATTN_WIDTH, D_MODEL), const),
                  pl.BlockSpec((SSM_WIDTH, 2 * D_MODEL), const),
                  pl.BlockSpec((D_MODEL, D_MODEL), const),
                  pl.BlockSpec((1, D_MODEL), const),
                  pl.BlockSpec((1, D_MODEL), const)],
        out_specs=pl.BlockSpec((tm, D_MODEL), row),
        out_shape=jax.ShapeDtypeStruct((t_tokens, D_MODEL), F32),
        compiler_params=_cparams("parallel"),
        name="merge",
    )(x2d, ao, ys, u, d, w_gate, w_ap, w_glu, w_out, ln_g, ln_b)


def _mlp_kernel(h_ref, w1_ref, w2_ref, g2_ref, b2_ref, o_ref, *, ff_chunk):
    h = h_ref[...]
    hb = h.astype(BF16)
    f = jnp.zeros(h.shape, F32)
    for c in range(D_FF // ff_chunk):
        sl = slice(c * ff_chunk, (c + 1) * ff_chunk)
        t = jnp.maximum(_dot(hb, w1_ref[:, sl]), 0.0)
        f = f + _dot((t * t).astype(BF16), w2_ref[sl, :])
    o_ref[...] = _layer_norm(DEEPNORM_ALPHA * h + f, g2_ref[...], b2_ref[...])


def _mlp(h, w1, w2, ln_g, ln_b, tm, ff_chunk=1024):
    t_tokens = h.shape[0]
    row = lambda i: (i, 0)
    const = lambda i: (0, 0)
    return pl.pallas_call(
        functools.partial(_mlp_kernel, ff_chunk=ff_chunk),
        grid=(t_tokens // tm,),
        in_specs=[pl.BlockSpec((tm, D_MODEL), row),
                  pl.BlockSpec((D_MODEL, D_FF), const),
                  pl.BlockSpec((D_FF, D_MODEL), const),
                  pl.BlockSpec((1, D_MODEL), const),
                  pl.BlockSpec((1, D_MODEL), const)],
        out_specs=pl.BlockSpec((tm, D_MODEL), row),
        out_shape=jax.ShapeDtypeStruct((t_tokens, D_MODEL), F32),
        compiler_params=_cparams("parallel"),
        name="mlp",
    )(h, w1, w2, ln_g, ln_b)


def _rope_tables(pos):
    inv = 1.0 / (ROPE_THETA ** (jnp.arange(0, HEAD_DIM, 2, dtype=F32) / HEAD_DIM))
    ang = pos.astype(F32)[:, None] * inv[None, :]
    c, s = jnp.cos(ang), jnp.sin(ang)
    reps = LANES // HEAD_DIM
    return jnp.tile(jnp.concatenate([c, c], axis=1), (1, reps)), jnp.tile(jnp.concatenate([-s, s], axis=1), (1, reps))


def _chunk_rows(u, bsz, n_chunk):
    t = u.reshape(bsz * n_chunk, CHUNK, N_GROUPS, GROUP_SIZE).transpose(2, 0, 3, 1)
    return t.reshape(N_GROUPS, bsz * n_chunk, GROUP_SIZE * CHUNK)


def _unchunk_rows(y, bsz, n_chunk):
    t = y.reshape(N_GROUPS, bsz * n_chunk, GROUP_SIZE, CHUNK).transpose(1, 3, 0, 2)
    return t.reshape(bsz * n_chunk * CHUNK, SSM_WIDTH)


def kernel(x_prompt, x_sample, cache_k, cache_v, state_ssm_re, state_ssm_im, w_in, lambda_q1, lambda_k1, lambda_q2, lambda_k2, subln_gain, ssm_a_re, ssm_a_im, ssm_b_re, ssm_b_im, ssm_c_re, ssm_c_im, ssm_d, ssm_log_dt, w_attn_proj, w_glu_a, w_glu_b, w_out, ln1_g, ln1_b, w_ff1, w_ff2, ln2_g, ln2_b):
    bp, n_p = x_prompt.shape[0], x_prompt.shape[1]
    bs, n_s = x_sample.shape[0], x_sample.shape[1]
    past = cache_k.shape[2]
    assert w_in.shape[0] == DEPTH and n_s == CHUNK and n_p % CHUNK == 0
    n_chunk = n_p // CHUNK
    tm = min(512, n_p)
    tm_s = min(512, bs * n_s)
    tq = min(256, n_p)
    l = 0
    lam_init = 0.8 - 0.6 * math.exp(-0.3 * l)

    xp = x_prompt.reshape(bp * n_p, D_MODEL)
    xs = x_sample.reshape(bs * n_s, D_MODEL)
    w_qkvu = w_in[l, :, 0:QKVU_COLS].astype(BF16)
    w_gate = w_in[l, :, QKVU_COLS:].astype(BF16)
    w_ap = w_attn_proj[l].astype(BF16)
    w_glu = jnp.concatenate([w_glu_a[l], w_glu_b[l]], axis=1).astype(BF16)
    w_o = w_out[l].astype(BF16)
    w1, w2 = w_ff1[l].astype(BF16), w_ff2[l].astype(BF16)
    lams = [v[l].reshape(1, HEAD_DIM) for v in (lambda_q1, lambda_k1, lambda_q2, lambda_k2)]
    gain = subln_gain[l].reshape(1, V_DIM)
    d_skip = ssm_d[l].reshape(1, SSM_WIDTH)
    lng = [v[l].reshape(1, D_MODEL) for v in (ln1_g, ln1_b, ln2_g, ln2_b)]

    cos_p, sin_p = _rope_tables(jnp.arange(n_p))
    cos_s, sin_s = _rope_tables(jnp.tile(past + jnp.arange(n_s), tm_s // n_s))

    q_p, k_p, v_p, u_p, kb_p, vt_p = _project(xp, w_qkvu, cos_p, sin_p, n_p, tm, True)
    q_s, k_s, v_s, u_s = _project(xs, w_qkvu, cos_s, sin_s, n_s, tm_s, False)

    ao_p = _prompt_attention(lams, gain, q_p, kb_p, vt_p, bp, n_p, tq, lam_init)
    ao_s = _sample_attention(lams, gain, q_s, k_s, v_s,
                             cache_k[l].reshape(bs, past, ATTN_WIDTH), cache_v[l].reshape(bs, past, ATTN_WIDTH),
                             lam_init)

    kv, w_st, vm, a_c, a_s = _ssm_tables(ssm_a_re[l], ssm_a_im[l], ssm_b_re[l], ssm_b_im[l],
                                         ssm_c_re[l], ssm_c_im[l], ssm_log_dt[l])
    tables = (kv, w_st.astype(BF16), vm.astype(BF16), a_c[:, None, :], a_s[:, None, :])
    x_rows = jnp.concatenate([_chunk_rows(u_p, bp, n_chunk), _chunk_rows(u_s, bs, 1)], axis=1).astype(BF16)
    s0 = jnp.concatenate([state_ssm_re[l], state_ssm_im[l]], axis=-1).transpose(1, 0, 2)
    y_rows, sf_p, sf_s = _ssm(tables, x_rows, s0, bp, n_chunk)
    ys_p = _unchunk_rows(y_rows[:, 0:bp * n_chunk], bp, n_chunk)
    ys_s = _unchunk_rows(y_rows[:, bp * n_chunk:], bs, 1)

    outs = []
    for x2d, ao, ys, u, tile in ((xp, ao_p, ys_p, u_p, tm), (xs, ao_s, ys_s, u_s, tm_s)):
        h = _merge(x2d, ao, ys, u, d_skip, w_gate, w_ap, w_glu, w_o, lng[0], lng[1], tile)
        outs.append(_mlp(h, w1, w2, lng[2], lng[3], tile))

    def states(sf):
        t = sf.transpose(1, 0, 2)
        return t[None, :, :, 0:STATE_DIM], t[None, :, :, STATE_DIM:]

    srp, sip = states(sf_p)
    srs, sis = states(sf_s)
    return (outs[0].reshape(bp, n_p, D_MODEL), outs[1].reshape(bs, n_s, D_MODEL),
            k_p.reshape(1, bp, n_p, N_HEADS, V_DIM), v_p.reshape(1, bp, n_p, N_HEADS, V_DIM), srp, sip,
            k_s.reshape(1, bs, n_s, N_HEADS, V_DIM), v_s.reshape(1, bs, n_s, N_HEADS, V_DIM), srs, sis)
```

```python
import functools
import math

import jax
import jax.numpy as jnp
from jax import lax
from jax.experimental import pallas as pl
from jax.experimental.pallas import tpu as pltpu

D_MODEL = 1024
CHUNK = 64
N_HEADS = 4
HEAD_DIM = 64
V_DIM = 2 * HEAD_DIM
ATTN_WIDTH = N_HEADS * V_DIM
SSM_WIDTH = 512
GROUP_SIZE = 16
N_GROUPS = SSM_WIDTH // GROUP_SIZE
STATE_DIM = 64
D_FF = 4 * D_MODEL
ROPE_THETA = 10000.0
LN_EPS = 1e-5
RMS_EPS = 1e-5
NEG_INF = -1e30
DEPTH = 1
DEEPNORM_ALPHA = (2.0 * DEPTH) ** 0.25
QKVU_COLS = 3 * ATTN_WIDTH + SSM_WIDTH
LOG2E = 1.4426950408889634

LANES = 128
VT_ROWS = V_DIM + 16
QUERY_LANES = 256
VMEM_LIMIT = 56 * 1024 * 1024

F32 = jnp.float32
BF16 = jnp.bfloat16


def _cparams(*sem):
    return pltpu.CompilerParams(dimension_semantics=sem, vmem_limit_bytes=VMEM_LIMIT)


def _nt_dot(a, b):
    return lax.dot_general(a, b, (((1,), (1,)), ((), ())), preferred_element_type=F32)


def _dot(a, b):
    return jnp.dot(a, b, preferred_element_type=F32)


def _rotary(z, cos, sin_signed, first_half):
    swapped = jnp.where(first_half, pltpu.roll(z, 96, 1), pltpu.roll(z, 32, 1))
    return z * cos + swapped * sin_signed


def _proj_kernel(x_ref, w_ref, cos_ref, sin_ref, q_ref, k_ref, v_ref, u_ref, *rest, emit_t):
    xb = x_ref[...].astype(BF16)
    cos = cos_ref[...]
    sin = sin_ref[...]
    lane = lax.broadcasted_iota(jnp.int32, cos.shape, 1)
    first_half = (lane % HEAD_DIM) < (HEAD_DIM // 2)
    zq = _dot(xb, w_ref[:, 0:ATTN_WIDTH])
    zk = _dot(xb, w_ref[:, ATTN_WIDTH:2 * ATTN_WIDTH])
    for h in range(N_HEADS):
        sl = slice(h * V_DIM, (h + 1) * V_DIM)
        q_ref[:, sl] = (_rotary(zq[:, sl], cos, sin, first_half) * (LOG2E * HEAD_DIM ** -0.5)).astype(BF16)
        kr = _rotary(zk[:, sl], cos, sin, first_half)
        k_ref[:, sl] = kr
        if emit_t:
            rest[0][:, sl] = kr.astype(BF16)
    zv = _dot(xb, w_ref[:, 2 * ATTN_WIDTH:3 * ATTN_WIDTH])
    v_ref[...] = zv
    if emit_t:
        vt_ref = rest[1]
        zvt = zv.T.astype(BF16)
        ones = jnp.ones((VT_ROWS - V_DIM, zvt.shape[1]), BF16)
        for h in range(N_HEADS):
            vt_ref[h, 0:V_DIM, :] = zvt[h * V_DIM:(h + 1) * V_DIM, :]
            vt_ref[h, V_DIM:VT_ROWS, :] = ones
    u_ref[...] = _dot(xb, w_ref[:, 3 * ATTN_WIDTH:QKVU_COLS])


def _project(x2d, w_qkvu, cos_t, sin_t, seq_len, tm, emit_t):
    t_tokens = x2d.shape[0]
    n_tiles = t_tokens // tm
    n_pos_tiles = cos_t.shape[0] // tm
    tiles_per_seq = max(seq_len // tm, 1)
    row = lambda i: (i, 0)
    pos = lambda i: (i % n_pos_tiles, 0)
    out_shape = [jax.ShapeDtypeStruct((t_tokens, ATTN_WIDTH), BF16),
                 jax.ShapeDtypeStruct((t_tokens, ATTN_WIDTH), F32),
                 jax.ShapeDtypeStruct((t_tokens, ATTN_WIDTH), F32),
                 jax.ShapeDtypeStruct((t_tokens, SSM_WIDTH), F32)]
    out_specs = [pl.BlockSpec((tm, ATTN_WIDTH), row)] * 3 + [pl.BlockSpec((tm, SSM_WIDTH), row)]
    if emit_t:
        bsz = t_tokens // seq_len
        out_shape += [jax.ShapeDtypeStruct((t_tokens, ATTN_WIDTH), BF16),
                      jax.ShapeDtypeStruct((bsz, N_HEADS, VT_ROWS, seq_len), BF16)]
        out_specs += [pl.BlockSpec((tm, ATTN_WIDTH), row),
                      pl.BlockSpec((None, N_HEADS, VT_ROWS, tm),
                                   lambda i: (i // tiles_per_seq, 0, 0, i % tiles_per_seq))]
    return pl.pallas_call(
        functools.partial(_proj_kernel, emit_t=emit_t),
        grid=(n_tiles,),
        in_specs=[pl.BlockSpec((tm, D_MODEL), row),
                  pl.BlockSpec((D_MODEL, QKVU_COLS), lambda i: (0, 0)),
                  pl.BlockSpec((tm, LANES), pos),
                  pl.BlockSpec((tm, LANES), pos)],
        out_specs=out_specs,
        out_shape=out_shape,
        compiler_params=_cparams("parallel"),
        name="proj_t" if emit_t else "proj",
    )(x2d, w_qkvu, cos_t, sin_t)


def _diff_lambda(lq1, lk1, lq2, lk2, lam_init):
    return (jnp.exp(jnp.sum(lq1 * lk1, axis=1, keepdims=True))
            - jnp.exp(jnp.sum(lq2 * lk2, axis=1, keepdims=True)) + lam_init)


def _sub_norm(d, gain, lam_init):
    ms = jnp.mean(jnp.square(d), axis=1, keepdims=True)
    return d * lax.rsqrt(ms + RMS_EPS) * gain * (1.0 - lam_init)


def _stack_maps(q):
    lane = lax.broadcasted_iota(jnp.int32, q.shape, 1)
    zero = jnp.zeros_like(q)
    return jnp.concatenate([jnp.where(lane < HEAD_DIM, q, zero), jnp.where(lane >= HEAD_DIM, q, zero)], axis=0)


def _prompt_attn_kernel(lq1_ref, lk1_ref, lq2_ref, lk2_ref, gain_ref, q_ref, k_ref, vt_ref, o_ref,
                        m_scr, acc_scr, qx_scr, sa_scr, sb_scr, *, tq, lam_init):
    qi = pl.program_id(2)
    qx_scr[...] = _stack_maps(q_ref[...])
    m_scr[...] = jnp.full(m_scr.shape, NEG_INF, F32)
    acc_scr[...] = jnp.zeros(acc_scr.shape, F32)
    chains = [slice(c * QUERY_LANES, (c + 1) * QUERY_LANES) for c in range(2 * tq // QUERY_LANES)]

    def scores(j, s_scr):
        kt = k_ref[pl.ds(pl.multiple_of(j * tq, tq), tq), :]
        for cs in chains:
            s_scr[:, cs] = _nt_dot(kt, qx_scr[cs, :])

    def softmax_pv(j, s_scr, diagonal):
        vt = vt_ref[:, pl.ds(pl.multiple_of(j * tq, tq), tq)]
        for cs in chains:
            st = s_scr[:, cs]
            if diagonal:
                key_chunk = lax.broadcasted_iota(jnp.int32, st.shape, 0) // CHUNK
                qry_chunk = (cs.start % tq + lax.broadcasted_iota(jnp.int32, st.shape, 1)) // CHUNK
                st = jnp.where(key_chunk <= qry_chunk, st, NEG_INF)
            m_old = m_scr[:, cs]
            m_new = jnp.maximum(m_old, jnp.max(st, axis=0, keepdims=True))
            alpha = jnp.exp2(m_old - m_new)
            p = jnp.exp2(st - m_new).astype(BF16)
            acc_scr[:, cs] = acc_scr[:, cs] * alpha + _dot(vt, p)
            m_scr[:, cs] = m_new

    scores(0, sa_scr)

    def pair(i, carry):
        scores(2 * i + 1, sb_scr)
        softmax_pv(2 * i, sa_scr, False)
        scores(2 * i + 2, sa_scr)
        softmax_pv(2 * i + 1, sb_scr, False)
        return carry

    lax.fori_loop(0, qi // 2, pair, 0)

    @pl.when(qi % 2 == 0)
    def _():
        softmax_pv(qi, sa_scr, True)

    @pl.when(qi % 2 == 1)
    def _():
        scores(qi, sb_scr)
        softmax_pv(qi - 1, sa_scr, False)
        softmax_pv(qi, sb_scr, True)

    acc = acc_scr[...]
    o = acc[0:V_DIM, :] * (1.0 / acc[V_DIM:V_DIM + 1, :])
    lam = _diff_lambda(lq1_ref[...], lk1_ref[...], lq2_ref[...], lk2_ref[...], lam_init)
    d = (o[:, 0:tq] - lam * o[:, tq:2 * tq]).T
    o_ref[...] = _sub_norm(d, gain_ref[...], lam_init).astype(BF16)


def _prompt_attention(lams, gain, q, kb, vt, bsz, seq_len, tq, lam_init):
    nq = seq_len // tq
    small = lambda b, h, i: (0, 0)
    return pl.pallas_call(
        functools.partial(_prompt_attn_kernel, tq=tq, lam_init=lam_init),
        grid=(bsz, N_HEADS, nq),
        in_specs=[pl.BlockSpec((1, HEAD_DIM), small)] * 4 + [
            pl.BlockSpec((1, V_DIM), small),
            pl.BlockSpec((tq, V_DIM), lambda b, h, i: (b * nq + i, h)),
            pl.BlockSpec((seq_len, V_DIM), lambda b, h, i: (b, h)),
            pl.BlockSpec((None, None, VT_ROWS, seq_len), lambda b, h, i: (b, h, 0, 0))],
        out_specs=pl.BlockSpec((tq, V_DIM), lambda b, h, i: (b * nq + i, h)),
        out_shape=jax.ShapeDtypeStruct((bsz * seq_len, ATTN_WIDTH), BF16),
        scratch_shapes=[pltpu.VMEM((1, 2 * tq), F32), pltpu.VMEM((VT_ROWS, 2 * tq), F32),
                        pltpu.VMEM((2 * tq, V_DIM), BF16),
                        pltpu.VMEM((tq, 2 * tq), F32), pltpu.VMEM((tq, 2 * tq), F32)],
        compiler_params=_cparams("parallel", "parallel", "arbitrary"),
        name="prompt_attn",
    )(*lams, gain, q, kb, vt)


def _sample_attn_kernel(lq1_ref, lk1_ref, lq2_ref, lk2_ref, gain_ref, q_ref, k_ref, v_ref, ck_ref, cv_ref,
                        o_ref, *, lam_init):
    n_new = q_ref.shape[0]
    lam = _diff_lambda(lq1_ref[...], lk1_ref[...], lq2_ref[...], lk2_ref[...], lam_init)
    gain = gain_ref[...]
    for h in range(N_HEADS):
        sl = slice(h * V_DIM, (h + 1) * V_DIM)
        qx = _stack_maps(q_ref[:, sl])
        s_c = _nt_dot(qx, ck_ref[:, sl].astype(BF16))
        s_n = _nt_dot(qx, k_ref[:, sl].astype(BF16))
        m = jnp.maximum(jnp.max(s_c, axis=1, keepdims=True), jnp.max(s_n, axis=1, keepdims=True))
        p_c = jnp.exp2(s_c - m)
        p_n = jnp.exp2(s_n - m)
        denom = jnp.sum(p_c, axis=1, keepdims=True) + jnp.sum(p_n, axis=1, keepdims=True)
        o = (_dot(p_c.astype(BF16), cv_ref[:, sl].astype(BF16))
             + _dot(p_n.astype(BF16), v_ref[:, sl].astype(BF16))) * (1.0 / denom)
        d = o[0:n_new, :] - lam * o[n_new:2 * n_new, :]
        o_ref[:, sl] = _sub_norm(d, gain, lam_init).astype(BF16)


def _sample_attention(lams, gain, q, k, v, cache_k, cache_v, lam_init):
    bsz, past = cache_k.shape[0], cache_k.shape[1]
    n_new = q.shape[0] // bsz
    small = lambda b: (0, 0)
    row = lambda b: (b, 0)
    return pl.pallas_call(
        functools.partial(_sample_attn_kernel, lam_init=lam_init),
        grid=(bsz,),
        in_specs=[pl.BlockSpec((1, HEAD_DIM), small)] * 4 + [
            pl.BlockSpec((1, V_DIM), small),
            pl.BlockSpec((n_new, ATTN_WIDTH), row),
            pl.BlockSpec((n_new, ATTN_WIDTH), row),
            pl.BlockSpec((n_new, ATTN_WIDTH), row),
            pl.BlockSpec((None, past, ATTN_WIDTH), lambda b: (b, 0, 0)),
            pl.BlockSpec((None, past, ATTN_WIDTH), lambda b: (b, 0, 0))],
        out_specs=pl.BlockSpec((n_new, ATTN_WIDTH), row),
        out_shape=jax.ShapeDtypeStruct(q.shape, BF16),
        compiler_params=_cparams("parallel"),
        name="sample_attn",
    )(*lams, gain, q, k, v, cache_k, cache_v)


def _ssm_tables(a_re, a_im, b_re, b_im, c_re, c_im, log_dt):
    dt = jnp.exp(log_dt)[:, None]
    lam_re, lam_im = a_re * dt, a_im * dt
    mag = jnp.exp(lam_re)
    ar, ai = mag * jnp.cos(lam_im), mag * jnp.sin(lam_im)
    den = jnp.square(a_re) + jnp.square(a_im)
    cr = ((ar - 1.0) * a_re + ai * a_im) / den
    ci = (ai * a_re - (ar - 1.0) * a_im) / den
    bbr = cr[..., None] * b_re - ci[..., None] * b_im
    bbi = cr[..., None] * b_im + ci[..., None] * b_re
    tau = jnp.arange(CHUNK + 1, dtype=F32)
    pm = jnp.exp(lam_re[..., None] * tau)
    pr, pi = pm * jnp.cos(lam_im[..., None] * tau), pm * jnp.sin(lam_im[..., None] * tau)
    hp = lax.Precision.HIGHEST
    zr = c_re[:, :, None, :] * bbr.transpose(0, 2, 1)[:, None] - c_im[:, :, None, :] * bbi.transpose(0, 2, 1)[:, None]
    zi = c_re[:, :, None, :] * bbi.transpose(0, 2, 1)[:, None] + c_im[:, :, None, :] * bbr.transpose(0, 2, 1)[:, None]
    kv = (jnp.einsum('gcdp,gpt->gdct', zr, pr[..., :CHUNK], precision=hp)
          - jnp.einsum('gcdp,gpt->gdct', zi, pi[..., :CHUNK], precision=hp))
    kv = jnp.concatenate([kv, jnp.zeros_like(kv)], axis=-1)
    prr, pir = pr[..., CHUNK - 1::-1][..., :CHUNK], pi[..., CHUNK - 1::-1][..., :CHUNK]
    w_re = prr[:, :, None, :] * bbr[..., None] - pir[:, :, None, :] * bbi[..., None]
    w_im = prr[:, :, None, :] * bbi[..., None] + pir[:, :, None, :] * bbr[..., None]
    g = a_re.shape[0]
    w = jnp.concatenate([w_re, w_im], axis=1).reshape(g, 2 * STATE_DIM, GROUP_SIZE * CHUNK)
    p1r, p1i = pr[..., 1:], pi[..., 1:]
    v_re = c_re[:, :, None, :] * p1r.transpose(0, 2, 1)[:, None] - c_im[:, :, None, :] * p1i.transpose(0, 2, 1)[:, None]
    v_im = c_re[:, :, None, :] * p1i.transpose(0, 2, 1)[:, None] + c_im[:, :, None, :] * p1r.transpose(0, 2, 1)[:, None]
    vm = jnp.concatenate([v_re, -v_im], axis=-1).reshape(g, GROUP_SIZE * CHUNK, 2 * STATE_DIM)
    acr, aci = pr[..., CHUNK], pi[..., CHUNK]
    a_c = jnp.concatenate([acr, acr], axis=-1)
    a_s = jnp.concatenate([-aci, aci], axis=-1)
    return kv, w, vm, a_c, a_s


def _ssm_kernel(kv_ref, w_ref, vm_ref, ac_ref, as_ref, x_ref, s0_ref, y_ref, sfp_ref, sfs_ref, mt_scr, st_scr,
                *, n_seq, n_chunk):
    n_p = n_seq * n_chunk
    lane = lax.broadcasted_iota(jnp.int32, (8, LANES), 1)
    low = lane < CHUNK
    for cp in range(GROUP_SIZE):
        for c2 in range(GROUP_SIZE // 2):
            ka = jnp.broadcast_to(kv_ref[cp, 2 * c2:2 * c2 + 1, :], (8, LANES))
            kb = jnp.broadcast_to(kv_ref[cp, 2 * c2 + 1:2 * c2 + 2, :], (8, LANES))
            for r in range(CHUNK // 8):
                ta = pltpu.roll(ka, 8 * r, 1, stride=1, stride_axis=0)
                tb = pltpu.roll(kb, CHUNK + 8 * r, 1, stride=1, stride_axis=0)
                mt_scr[cp * CHUNK + 8 * r:cp * CHUNK + 8 * r + 8, c2 * LANES:(c2 + 1) * LANES] = (
                    jnp.where(low, ta, tb).astype(BF16))
    x = x_ref[...]
    s_loc = _nt_dot(x, w_ref[...])
    a_c, a_s = ac_ref[...], as_ref[...]

    def cmul(pc, ps, s):
        return pc * s + ps * pltpu.roll(s, STATE_DIM, 1)

    s_inc = s_loc[0:n_p, :]
    row = lax.broadcasted_iota(jnp.int32, (n_p, 2 * STATE_DIM), 0) % n_chunk
    pc, ps = a_c, a_s
    dist = 1
    while dist < n_chunk:
        shifted = jnp.where(row >= dist, pltpu.roll(s_inc, dist, 0), 0.0)
        s_inc = s_inc + cmul(pc, ps, shifted)
        pc, ps = pc * pc - ps * ps, 2.0 * pc * ps
        dist *= 2
    st_scr[0:n_p, :] = s_inc
    s_prev_p = jnp.where(row >= 1, pltpu.roll(s_inc, 1, 0), 0.0)
    s0 = s0_ref[...]
    s_prev = jnp.concatenate([s_prev_p, s0], axis=0)
    y_ref[...] = _dot(x, mt_scr[...]) + _nt_dot(s_prev.astype(BF16), vm_ref[...])
    sfp_ref[...] = st_scr[pl.ds(n_chunk - 1, n_seq, stride=n_chunk), :]
    sfs_ref[...] = cmul(a_c, a_s, s0) + s_loc[n_p:, :]


def _ssm(tables, x_rows, s0, n_seq, n_chunk):
    kv, w, vm, a_c, a_s = tables
    n_groups, rows = x_rows.shape[0], x_rows.shape[1]
    n_s = s0.shape[1]
    width = GROUP_SIZE * CHUNK
    per_g3 = lambda g: (g, 0, 0)
    return pl.pallas_call(
        functools.partial(_ssm_kernel, n_seq=n_seq, n_chunk=n_chunk),
        grid=(n_groups,),
        in_specs=[pl.BlockSpec((None, GROUP_SIZE, GROUP_SIZE, LANES), lambda g: (g, 0, 0, 0)),
                  pl.BlockSpec((None, 2 * STATE_DIM, width), per_g3),
                  pl.BlockSpec((None, width, 2 * STATE_DIM), per_g3),
                  pl.BlockSpec((None, 1, 2 * STATE_DIM), per_g3),
                  pl.BlockSpec((None, 1, 2 * STATE_DIM), per_g3),
                  pl.BlockSpec((None, rows, width), per_g3),
                  pl.BlockSpec((None, n_s, 2 * STATE_DIM), per_g3)],
        out_specs=[pl.BlockSpec((None, rows, width), per_g3),
                   pl.BlockSpec((None, n_seq, 2 * STATE_DIM), per_g3),
                   pl.BlockSpec((None, n_s, 2 * STATE_DIM), per_g3)],
        out_shape=[jax.ShapeDtypeStruct((n_groups, rows, width), F32),
                   jax.ShapeDtypeStruct((n_groups, n_seq, 2 * STATE_DIM), F32),
                   jax.ShapeDtypeStruct((n_groups, n_s, 2 * STATE_DIM), F32)],
        scratch_shapes=[pltpu.VMEM((width, width), BF16), pltpu.VMEM((n_seq * n_chunk, 2 * STATE_DIM), F32)],
        compiler_params=_cparams("parallel"),
        name="ssm",
    )(kv, w, vm, a_c, a_s, x_rows, s0)


def _layer_norm(x, g, b):
    mu = jnp.mean(x, axis=1, keepdims=True)
    xc = x - mu
    var = jnp.mean(jnp.square(xc), axis=1, keepdims=True)
    return xc * lax.rsqrt(var + LN_EPS) * g + b


def _gelu_tanh(x):
    return 0.5 * x * (1.0 + jnp.tanh(math.sqrt(2.0 / math.pi) * (x + 0.044715 * (x * x * x))))


def _merge_kernel(x_ref, ao_ref, ys_ref, u_ref, d_ref, wg_ref, wap_ref, wglu_ref, wout_ref, g1_ref, b1_ref, h_ref):
    x = x_ref[...]
    xb = x.astype(BF16)
    a_branch = _dot(ao_ref[...], wap_ref[...])
    s_act = _gelu_tanh(ys_ref[...] + d_ref[...] * u_ref[...]).astype(BF16)
    s_branch = _dot(s_act, wglu_ref[:, 0:D_MODEL]) * jax.nn.sigmoid(_dot(s_act, wglu_ref[:, D_MODEL:2 * D_MODEL]))
    m = (jax.nn.sigmoid(_dot(xb, wg_ref[:, 0:D_MODEL])) * a_branch
         + jax.nn.sigmoid(_dot(xb, wg_ref[:, D_MODEL:2 * D_MODEL])) * s_branch)
    h_ref[...] = _layer_norm(DEEPNORM_ALPHA * x + _dot(m.astype(BF16), wout_ref[...]), g1_ref[...], b1_ref[...])


def _merge(x2d, ao, ys, u, d, w_gate, w_ap, w_glu, w_out, ln_g, ln_b, tm):
    t_tokens = x2d.shape[0]
    row = lambda i: (i, 0)
    const = lambda i: (0, 0)
    return pl.pallas_call(
        _merge_kernel,
        grid=(t_tokens // tm,),
        in_specs=[pl.BlockSpec((tm, D_MODEL), row),
                  pl.BlockSpec((tm, ATTN_WIDTH), row),
                  pl.BlockSpec((tm, SSM_WIDTH), row),
                  pl.BlockSpec((tm, SSM_WIDTH), row),
                  pl.BlockSpec((1, SSM_WIDTH), const),
                  pl.BlockSpec((D_MODEL, 2 * D_MODEL), const),
                  pl.BlockSpec((ATTN_WIDTH, D_MODEL), const),
                  pl.BlockSpec((SSM_WIDTH, 2 * D_MODEL), const),
                  pl.BlockSpec((D_MODEL, D_MODEL), const),
                  pl.BlockSpec((1, D_MODEL), const),
                  pl.BlockSpec((1, D_MODEL), const)],
        out_specs=pl.BlockSpec((tm, D_MODEL), row),
        out_shape=jax.ShapeDtypeStruct((t_tokens, D_MODEL), F32),
        compiler_params=_cparams("parallel"),
        name="merge",
    )(x2d, ao, ys, u, d, w_gate, w_ap, w_glu, w_out, ln_g, ln_b)


def _mlp_kernel(h_ref, w1_ref, w2_ref, g2_ref, b2_ref, o_ref, *, ff_chunk):
    h = h_ref[...]
    hb = h.astype(BF16)
    f = jnp.zeros(h.shape, F32)
    for c in range(D_FF // ff_chunk):
        sl = slice(c * ff_chunk, (c + 1) * ff_chunk)
        t = jnp.maximum(_dot(hb, w1_ref[:, sl]), 0.0)
        f = f + _dot((t * t).astype(BF16), w2_ref[sl, :])
    o_ref[...] = _layer_norm(DEEPNORM_ALPHA * h + f, g2_ref[...], b2_ref[...])


def _mlp(h, w1, w2, ln_g, ln_b, tm, ff_chunk=1024):
    t_tokens = h.shape[0]
    row = lambda i: (i, 0)
    const = lambda i: (0, 0)
    return pl.pallas_call(
        functools.partial(_mlp_kernel, ff_chunk=ff_chunk),
        grid=(t_tokens // tm,),
        in_specs=[pl.BlockSpec((tm, D_MODEL), row),
                  pl.BlockSpec((D_MODEL, D_FF), const),
                  pl.BlockSpec((D_FF, D_MODEL), const),
                  pl.BlockSpec((1, D_MODEL), const),
                  pl.BlockSpec((1, D_MODEL), const)],
        out_specs=pl.BlockSpec((tm, D_MODEL), row),
        out_shape=jax.ShapeDtypeStruct((t_tokens, D_MODEL), F32),
        compiler_params=_cparams("parallel"),
        name="mlp",
    )(h, w1, w2, ln_g, ln_b)


def _rope_tables(pos):
    inv = 1.0 / (ROPE_THETA ** (jnp.arange(0, HEAD_DIM, 2, dtype=F32) / HEAD_DIM))
    ang = pos.astype(F32)[:, None] * inv[None, :]
    c, s = jnp.cos(ang), jnp.sin(ang)
    reps = LANES // HEAD_DIM
    return jnp.tile(jnp.concatenate([c, c], axis=1), (1, reps)), jnp.tile(jnp.concatenate([-s, s], axis=1), (1, reps))


def _chunk_rows(u, bsz, n_chunk):
    t = u.reshape(bsz * n_chunk, CHUNK, N_GROUPS, GROUP_SIZE).transpose(2, 0, 3, 1)
    return t.reshape(N_GROUPS, bsz * n_chunk, GROUP_SIZE * CHUNK)


def _unchunk_rows(y, bsz, n_chunk):
    t = y.reshape(N_GROUPS, bsz * n_chunk, GROUP_SIZE, CHUNK).transpose(1, 3, 0, 2)
    return t.reshape(bsz * n_chunk * CHUNK, SSM_WIDTH)


def kernel(x_prompt, x_sample, cache_k, cache_v, state_ssm_re, state_ssm_im, w_in, lambda_q1, lambda_k1, lambda_q2, lambda_k2, subln_gain, ssm_a_re, ssm_a_im, ssm_b_re, ssm_b_im, ssm_c_re, ssm_c_im, ssm_d, ssm_log_dt, w_attn_proj, w_glu_a, w_glu_b, w_out, ln1_g, ln1_b, w_ff1, w_ff2, ln2_g, ln2_b):
    bp, n_p = x_prompt.shape[0], x_prompt.shape[1]
    bs, n_s = x_sample.shape[0], x_sample.shape[1]
    past = cache_k.shape[2]
    assert w_in.shape[0] == DEPTH and n_s == CHUNK and n_p % CHUNK == 0
    n_chunk = n_p // CHUNK
    tm = min(512, n_p)
    tm_s = min(512, bs * n_s)
    tq = min(512, n_p)
    l = 0
    lam_init = 0.8 - 0.6 * math.exp(-0.3 * l)

    xp = x_prompt.reshape(bp * n_p, D_MODEL)
    xs = x_sample.reshape(bs * n_s, D_MODEL)
    w_qkvu = w_in[l, :, 0:QKVU_COLS].astype(BF16)
    w_gate = w_in[l, :, QKVU_COLS:].astype(BF16)
    w_ap = w_attn_proj[l].astype(BF16)
    w_glu = jnp.concatenate([w_glu_a[l], w_glu_b[l]], axis=1).astype(BF16)
    w_o = w_out[l].astype(BF16)
    w1, w2 = w_ff1[l].astype(BF16), w_ff2[l].astype(BF16)
    lams = [v[l].reshape(1, HEAD_DIM) for v in (lambda_q1, lambda_k1, lambda_q2, lambda_k2)]
    gain = subln_gain[l].reshape(1, V_DIM)
    d_skip = ssm_d[l].reshape(1, SSM_WIDTH)
    lng = [v[l].reshape(1, D_MODEL) for v in (ln1_g, ln1_b, ln2_g, ln2_b)]

    cos_p, sin_p = _rope_tables(jnp.arange(n_p))
    cos_s, sin_s = _rope_tables(jnp.tile(past + jnp.arange(n_s), tm_s // n_s))

    q_p, k_p, v_p, u_p, kb_p, vt_p = _project(xp, w_qkvu, cos_p, sin_p, n_p, tm, True)
    q_s, k_s, v_s, u_s = _project(xs, w_qkvu, cos_s, sin_s, n_s, tm_s, False)

    ao_p = _prompt_attention(lams, gain, q_p, kb_p, vt_p, bp, n_p, tq, lam_init)
    ao_s = _sample_attention(lams, gain, q_s, k_s, v_s,
                             cache_k[l].reshape(bs, past, ATTN_WIDTH), cache_v[l].reshape(bs, past, ATTN_WIDTH),
                             lam_init)

    kv, w_st, vm, a_c, a_s = _ssm_tables(ssm_a_re[l], ssm_a_im[l], ssm_b_re[l], ssm_b_im[l],
                                         ssm_c_re[l], ssm_c_im[l], ssm_log_dt[l])
    tables = (kv, w_st.astype(BF16), vm.astype(BF16), a_c[:, None, :], a_s[:, None, :])
    x_rows = jnp.concatenate([_chunk_rows(u_p, bp, n_chunk), _chunk_rows(u_s, bs, 1)], axis=1).astype(BF16)
    s0 = jnp.concatenate([state_ssm_re[l], state_ssm_im[l]], axis=-1).transpose(1, 0, 2)
    y_rows, sf_p, sf_s = _ssm(tables, x_rows, s0, bp, n_chunk)
    ys_p = _unchunk_rows(y_rows[:, 0:bp * n_chunk], bp, n_chunk)
    ys_s = _unchunk_rows(y_rows[:, bp * n_chunk:], bs, 1)

    outs = []
    for x2d, ao, ys, u, tile in ((xp, ao_p, ys_p, u_p, tm), (xs, ao_s, ys_s, u_s, tm_s)):
        h = _merge(x2d, ao, ys, u, d_skip, w_gate, w_ap, w_glu, w_o, lng[0], lng[1], tile)
        outs.append(_mlp(h, w1, w2, lng[2], lng[3], tile))

    def states(sf):
        t = sf.transpose(1, 0, 2)
        return t[None, :, :, 0:STATE_DIM], t[None, :, :, STATE_DIM:]

    srp, sip = states(sf_p)
    srs, sis = states(sf_s)
    return (outs[0].reshape(bp, n_p, D_MODEL), outs[1].reshape(bs, n_s, D_MODEL),
            k_p.reshape(1, bp, n_p, N_HEADS, V_DIM), v_p.reshape(1, bp, n_p, N_HEADS, V_DIM), srp, sip,
            k_s.reshape(1, bs, n_s, N_HEADS, V_DIM), v_s.reshape(1, bs, n_s, N_HEADS, V_DIM), srs, sis)
```

```python
import functools
import math

import jax
import jax.numpy as jnp
from jax import lax
from jax.experimental import pallas as pl
from jax.experimental.pallas import tpu as pltpu

D_MODEL = 1024
CHUNK = 64
N_HEADS = 4
HEAD_DIM = 64
V_DIM = 2 * HEAD_DIM
ATTN_WIDTH = N_HEADS * V_DIM
SSM_WIDTH = 512
GROUP_SIZE = 16
N_GROUPS = SSM_WIDTH // GROUP_SIZE
STATE_DIM = 64
D_FF = 4 * D_MODEL
ROPE_THETA = 10000.0
LN_EPS = 1e-5
RMS_EPS = 1e-5
NEG_INF = -1e30
DEPTH = 1
DEEPNORM_ALPHA = (2.0 * DEPTH) ** 0.25
QKVU_COLS = 3 * ATTN_WIDTH + SSM_WIDTH
LOG2E = 1.4426950408889634

LANES = 128
VT_ROWS = V_DIM + 16
QUERY_LANES = 256
VMEM_LIMIT = 56 * 1024 * 1024

F32 = jnp.float32
BF16 = jnp.bfloat16


def _cparams(*sem):
    return pltpu.CompilerParams(dimension_semantics=sem, vmem_limit_bytes=VMEM_LIMIT)


def _nt_dot(a, b):
    return lax.dot_general(a, b, (((1,), (1,)), ((), ())), preferred_element_type=F32)


def _dot(a, b):
    return jnp.dot(a, b, preferred_element_type=F32)


def _rotary(z, cos, sin_signed, first_half):
    swapped = jnp.where(first_half, pltpu.roll(z, 96, 1), pltpu.roll(z, 32, 1))
    return z * cos + swapped * sin_signed


def _proj_kernel(x_ref, w_ref, cos_ref, sin_ref, q_ref, k_ref, v_ref, u_ref, *rest, emit_t):
    xb = x_ref[...].astype(BF16)
    cos = cos_ref[...]
    sin = sin_ref[...]
    lane = lax.broadcasted_iota(jnp.int32, cos.shape, 1)
    first_half = (lane % HEAD_DIM) < (HEAD_DIM // 2)
    tm = xb.shape[0]
    zq = _dot(xb, w_ref[:, 0:ATTN_WIDTH])
    zk = _dot(xb, w_ref[:, ATTN_WIDTH:2 * ATTN_WIDTH])
    zv = _dot(xb, w_ref[:, 2 * ATTN_WIDTH:3 * ATTN_WIDTH])
    for h in range(N_HEADS):
        sl = slice(h * V_DIM, (h + 1) * V_DIM)
        q_ref[:, sl] = (_rotary(zq[:, sl], cos, sin, first_half) * (LOG2E * HEAD_DIM ** -0.5)).astype(BF16)
        kr = _rotary(zk[:, sl], cos, sin, first_half)
        k_ref[pl.ds(h, tm, stride=N_HEADS), :] = kr
        v_ref[pl.ds(h, tm, stride=N_HEADS), :] = zv[:, sl]
        if emit_t:
            rest[0][:, sl] = kr.astype(BF16)
    if emit_t:
        vt_ref = rest[1]
        zvt = zv.T.astype(BF16)
        ones = jnp.ones((VT_ROWS - V_DIM, zvt.shape[1]), BF16)
        for h in range(N_HEADS):
            vt_ref[h, 0:V_DIM, :] = zvt[h * V_DIM:(h + 1) * V_DIM, :]
            vt_ref[h, V_DIM:VT_ROWS, :] = ones
    u_ref[...] = _dot(xb, w_ref[:, 3 * ATTN_WIDTH:QKVU_COLS])


def _project(x2d, w_qkvu, cos_t, sin_t, seq_len, tm, emit_t):
    t_tokens = x2d.shape[0]
    n_tiles = t_tokens // tm
    n_pos_tiles = cos_t.shape[0] // tm
    tiles_per_seq = max(seq_len // tm, 1)
    row = lambda i: (i, 0)
    pos = lambda i: (i % n_pos_tiles, 0)
    out_shape = [jax.ShapeDtypeStruct((t_tokens, ATTN_WIDTH), BF16),
                 jax.ShapeDtypeStruct((t_tokens * N_HEADS, V_DIM), F32),
                 jax.ShapeDtypeStruct((t_tokens * N_HEADS, V_DIM), F32),
                 jax.ShapeDtypeStruct((t_tokens, SSM_WIDTH), F32)]
    out_specs = ([pl.BlockSpec((tm, ATTN_WIDTH), row)] + [pl.BlockSpec((tm * N_HEADS, V_DIM), row)] * 2
                 + [pl.BlockSpec((tm, SSM_WIDTH), row)])
    if emit_t:
        bsz = t_tokens // seq_len
        out_shape += [jax.ShapeDtypeStruct((t_tokens, ATTN_WIDTH), BF16),
                      jax.ShapeDtypeStruct((bsz, N_HEADS, VT_ROWS, seq_len), BF16)]
        out_specs += [pl.BlockSpec((tm, ATTN_WIDTH), row),
                      pl.BlockSpec((None, N_HEADS, VT_ROWS, tm),
                                   lambda i: (i // tiles_per_seq, 0, 0, i % tiles_per_seq))]
    return pl.pallas_call(
        functools.partial(_proj_kernel, emit_t=emit_t),
        grid=(n_tiles,),
        in_specs=[pl.BlockSpec((tm, D_MODEL), row),
                  pl.BlockSpec((D_MODEL, QKVU_COLS), lambda i: (0, 0)),
                  pl.BlockSpec((tm, LANES), pos),
                  pl.BlockSpec((tm, LANES), pos)],
        out_specs=out_specs,
        out_shape=out_shape,
        compiler_params=_cparams("parallel"),
        name="proj_t" if emit_t else "proj",
    )(x2d, w_qkvu, cos_t, sin_t)


def _diff_lambda(lq1, lk1, lq2, lk2, lam_init):
    return (jnp.exp(jnp.sum(lq1 * lk1, axis=1, keepdims=True))
            - jnp.exp(jnp.sum(lq2 * lk2, axis=1, keepdims=True)) + lam_init)


def _sub_norm(d, gain, lam_init):
    ms = jnp.mean(jnp.square(d), axis=1, keepdims=True)
    return d * lax.rsqrt(ms + RMS_EPS) * gain * (1.0 - lam_init)


def _stack_maps(q):
    lane = lax.broadcasted_iota(jnp.int32, q.shape, 1)
    zero = jnp.zeros_like(q)
    return jnp.concatenate([jnp.where(lane < HEAD_DIM, q, zero), jnp.where(lane >= HEAD_DIM, q, zero)], axis=0)


def _prompt_attn_kernel(lq1_ref, lk1_ref, lq2_ref, lk2_ref, gain_ref, q_ref, k_ref, vt_ref, o_ref,
                        m_scr, acc_scr, qx_scr, sa_scr, sb_scr, *, tq, lam_init):
    qi = pl.program_id(2)
    qx_scr[...] = _stack_maps(q_ref[...])
    m_scr[...] = jnp.full(m_scr.shape, NEG_INF, F32)
    acc_scr[...] = jnp.zeros(acc_scr.shape, F32)
    chains = [slice(c * QUERY_LANES, (c + 1) * QUERY_LANES) for c in range(2 * tq // QUERY_LANES)]

    def scores(j, s_scr):
        kt = k_ref[pl.ds(pl.multiple_of(j * tq, tq), tq), :]
        for cs in chains:
            s_scr[:, cs] = _nt_dot(kt, qx_scr[cs, :])

    def softmax_pv(j, s_scr, diagonal):
        vt = vt_ref[:, pl.ds(pl.multiple_of(j * tq, tq), tq)]
        for cs in chains:
            st = s_scr[:, cs]
            if diagonal:
                key_chunk = lax.broadcasted_iota(jnp.int32, st.shape, 0) // CHUNK
                qry_chunk = (cs.start % tq + lax.broadcasted_iota(jnp.int32, st.shape, 1)) // CHUNK
                st = jnp.where(key_chunk <= qry_chunk, st, NEG_INF)
            m_old = m_scr[:, cs]
            m_new = jnp.maximum(m_old, jnp.max(st, axis=0, keepdims=True))
            alpha = jnp.exp2(m_old - m_new)
            p = jnp.exp2(st - m_new).astype(BF16)
            acc_scr[:, cs] = acc_scr[:, cs] * alpha + _dot(vt, p)
            m_scr[:, cs] = m_new

    scores(0, sa_scr)

    def pair(i, carry):
        scores(2 * i + 1, sb_scr)
        softmax_pv(2 * i, sa_scr, False)
        scores(2 * i + 2, sa_scr)
        softmax_pv(2 * i + 1, sb_scr, False)
        return carry

    lax.fori_loop(0, qi // 2, pair, 0)

    @pl.when(qi % 2 == 0)
    def _():
        softmax_pv(qi, sa_scr, True)

    @pl.when(qi % 2 == 1)
    def _():
        scores(qi, sb_scr)
        softmax_pv(qi - 1, sa_scr, False)
        softmax_pv(qi, sb_scr, True)

    acc = acc_scr[...]
    o = acc[0:V_DIM, :] * (1.0 / acc[V_DIM:V_DIM + 1, :])
    lam = _diff_lambda(lq1_ref[...], lk1_ref[...], lq2_ref[...], lk2_ref[...], lam_init)
    d = (o[:, 0:tq] - lam * o[:, tq:2 * tq]).T
    o_ref[...] = _sub_norm(d, gain_ref[...], lam_init).astype(BF16)


def _prompt_attention(lams, gain, q, kb, vt, bsz, seq_len, tq, lam_init):
    nq = seq_len // tq
    small = lambda b, h, i: (0, 0)
    return pl.pallas_call(
        functools.partial(_prompt_attn_kernel, tq=tq, lam_init=lam_init),
        grid=(bsz, N_HEADS, nq),
        in_specs=[pl.BlockSpec((1, HEAD_DIM), small)] * 4 + [
            pl.BlockSpec((1, V_DIM), small),
            pl.BlockSpec((tq, V_DIM), lambda b, h, i: (b * nq + i, h)),
            pl.BlockSpec((seq_len, V_DIM), lambda b, h, i: (b, h)),
            pl.BlockSpec((None, None, VT_ROWS, seq_len), lambda b, h, i: (b, h, 0, 0))],
        out_specs=pl.BlockSpec((tq, V_DIM), lambda b, h, i: (b * nq + i, h)),
        out_shape=jax.ShapeDtypeStruct((bsz * seq_len, ATTN_WIDTH), BF16),
        scratch_shapes=[pltpu.VMEM((1, 2 * tq), F32), pltpu.VMEM((VT_ROWS, 2 * tq), F32),
                        pltpu.VMEM((2 * tq, V_DIM), BF16),
                        pltpu.VMEM((tq, 2 * tq), F32), pltpu.VMEM((tq, 2 * tq), F32)],
        compiler_params=_cparams("parallel", "parallel", "arbitrary"),
        name="prompt_attn",
    )(*lams, gain, q, kb, vt)


def _sample_attn_kernel(lq1_ref, lk1_ref, lq2_ref, lk2_ref, gain_ref, q_ref, k_ref, v_ref, ck_ref, cv_ref,
                        o_ref, *, lam_init):
    n_new = q_ref.shape[0]
    past = ck_ref.shape[0] // N_HEADS
    lam = _diff_lambda(lq1_ref[...], lk1_ref[...], lq2_ref[...], lk2_ref[...], lam_init)
    gain = gain_ref[...]
    for h in range(N_HEADS):
        sl = slice(h * V_DIM, (h + 1) * V_DIM)
        old = pl.ds(h, past, stride=N_HEADS)
        new = pl.ds(h, n_new, stride=N_HEADS)
        qx = _stack_maps(q_ref[:, sl])
        s_c = _nt_dot(qx, ck_ref[old, :].astype(BF16))
        s_n = _nt_dot(qx, k_ref[new, :].astype(BF16))
        m = jnp.maximum(jnp.max(s_c, axis=1, keepdims=True), jnp.max(s_n, axis=1, keepdims=True))
        p_c = jnp.exp2(s_c - m)
        p_n = jnp.exp2(s_n - m)
        denom = jnp.sum(p_c, axis=1, keepdims=True) + jnp.sum(p_n, axis=1, keepdims=True)
        o = (_dot(p_c.astype(BF16), cv_ref[old, :].astype(BF16))
             + _dot(p_n.astype(BF16), v_ref[new, :].astype(BF16))) * (1.0 / denom)
        d = o[0:n_new, :] - lam * o[n_new:2 * n_new, :]
        o_ref[:, sl] = _sub_norm(d, gain, lam_init).astype(BF16)


def _sample_attention(lams, gain, q, k, v, cache_k, cache_v, lam_init):
    bsz, past_rows = cache_k.shape[0], cache_k.shape[1]
    n_new = q.shape[0] // bsz
    small = lambda b: (0, 0)
    row = lambda b: (b, 0)
    return pl.pallas_call(
        functools.partial(_sample_attn_kernel, lam_init=lam_init),
        grid=(bsz,),
        in_specs=[pl.BlockSpec((1, HEAD_DIM), small)] * 4 + [
            pl.BlockSpec((1, V_DIM), small),
            pl.BlockSpec((n_new, ATTN_WIDTH), row),
            pl.BlockSpec((n_new * N_HEADS, V_DIM), row),
            pl.BlockSpec((n_new * N_HEADS, V_DIM), row),
            pl.BlockSpec((None, past_rows, V_DIM), lambda b: (b, 0, 0)),
            pl.BlockSpec((None, past_rows, V_DIM), lambda b: (b, 0, 0))],
        out_specs=pl.BlockSpec((n_new, ATTN_WIDTH), row),
        out_shape=jax.ShapeDtypeStruct(q.shape, BF16),
        compiler_params=_cparams("parallel"),
        name="sample_attn",
    )(*lams, gain, q, k, v, cache_k, cache_v)


def _ssm_tables(a_re, a_im, b_re, b_im, c_re, c_im, log_dt):
    dt = jnp.exp(log_dt)[:, None]
    lam_re, lam_im = a_re * dt, a_im * dt
    mag = jnp.exp(lam_re)
    ar, ai = mag * jnp.cos(lam_im), mag * jnp.sin(lam_im)
    den = jnp.square(a_re) + jnp.square(a_im)
    cr = ((ar - 1.0) * a_re + ai * a_im) / den
    ci = (ai * a_re - (ar - 1.0) * a_im) / den
    bbr = cr[..., None] * b_re - ci[..., None] * b_im
    bbi = cr[..., None] * b_im + ci[..., None] * b_re
    tau = jnp.arange(CHUNK + 1, dtype=F32)
    pm = jnp.exp(lam_re[..., None] * tau)
    pr, pi = pm * jnp.cos(lam_im[..., None] * tau), pm * jnp.sin(lam_im[..., None] * tau)
    hp = lax.Precision.HIGHEST
    zr = c_re[:, :, None, :] * bbr.transpose(0, 2, 1)[:, None] - c_im[:, :, None, :] * bbi.transpose(0, 2, 1)[:, None]
    zi = c_re[:, :, None, :] * bbi.transpose(0, 2, 1)[:, None] + c_im[:, :, None, :] * bbr.transpose(0, 2, 1)[:, None]
    kv = (jnp.einsum('gcdp,gpt->gdct', zr, pr[..., :CHUNK], precision=hp)
          - jnp.einsum('gcdp,gpt->gdct', zi, pi[..., :CHUNK], precision=hp))
    kv = jnp.concatenate([kv, jnp.zeros_like(kv)], axis=-1)
    prr, pir = pr[..., CHUNK - 1::-1][..., :CHUNK], pi[..., CHUNK - 1::-1][..., :CHUNK]
    w_re = prr[:, :, None, :] * bbr[..., None] - pir[:, :, None, :] * bbi[..., None]
    w_im = prr[:, :, None, :] * bbi[..., None] + pir[:, :, None, :] * bbr[..., None]
    g = a_re.shape[0]
    w = jnp.concatenate([w_re, w_im], axis=1).reshape(g, 2 * STATE_DIM, GROUP_SIZE * CHUNK)
    p1r, p1i = pr[..., 1:], pi[..., 1:]
    v_re = c_re[:, :, None, :] * p1r.transpose(0, 2, 1)[:, None] - c_im[:, :, None, :] * p1i.transpose(0, 2, 1)[:, None]
    v_im = c_re[:, :, None, :] * p1i.transpose(0, 2, 1)[:, None] + c_im[:, :, None, :] * p1r.transpose(0, 2, 1)[:, None]
    vm = jnp.concatenate([v_re, -v_im], axis=-1).reshape(g, GROUP_SIZE * CHUNK, 2 * STATE_DIM)
    acr, aci = pr[..., CHUNK], pi[..., CHUNK]
    a_c = jnp.concatenate([acr, acr], axis=-1)
    a_s = jnp.concatenate([-aci, aci], axis=-1)
    return kv, w, vm, a_c, a_s


def _ssm_kernel(kv_ref, w_ref, vm_ref, ac_ref, as_ref, x_ref, s0_ref, y_ref, sfp_ref, sfs_ref, mt_scr, st_scr,
                *, n_seq, n_chunk):
    n_p = n_seq * n_chunk
    lane = lax.broadcasted_iota(jnp.int32, (8, LANES), 1)
    low = lane < CHUNK
    for cp in range(GROUP_SIZE):
        for c2 in range(GROUP_SIZE // 2):
            ka = jnp.broadcast_to(kv_ref[cp, 2 * c2:2 * c2 + 1, :], (8, LANES))
            kb = jnp.broadcast_to(kv_ref[cp, 2 * c2 + 1:2 * c2 + 2, :], (8, LANES))
            for r in range(CHUNK // 8):
                ta = pltpu.roll(ka, 8 * r, 1, stride=1, stride_axis=0)
                tb = pltpu.roll(kb, CHUNK + 8 * r, 1, stride=1, stride_axis=0)
                mt_scr[cp * CHUNK + 8 * r:cp * CHUNK + 8 * r + 8, c2 * LANES:(c2 + 1) * LANES] = (
                    jnp.where(low, ta, tb).astype(BF16))
    x = x_ref[...]
    s_loc = _nt_dot(x, w_ref[...])
    a_c, a_s = ac_ref[...], as_ref[...]

    def cmul(pc, ps, s):
        return pc * s + ps * pltpu.roll(s, STATE_DIM, 1)

    s_inc = s_loc[0:n_p, :]
    row = lax.broadcasted_iota(jnp.int32, (n_p, 2 * STATE_DIM), 0) % n_chunk
    pc, ps = a_c, a_s
    dist = 1
    while dist < n_chunk:
        shifted = jnp.where(row >= dist, pltpu.roll(s_inc, dist, 0), 0.0)
        s_inc = s_inc + cmul(pc, ps, shifted)
        pc, ps = pc * pc - ps * ps, 2.0 * pc * ps
        dist *= 2
    st_scr[0:n_p, :] = s_inc
    s_prev_p = jnp.where(row >= 1, pltpu.roll(s_inc, 1, 0), 0.0)
    s0 = s0_ref[...]
    s_prev = jnp.concatenate([s_prev_p, s0], axis=0)
    y_ref[...] = _dot(x, mt_scr[...]) + _nt_dot(s_prev.astype(BF16), vm_ref[...])
    sfp_ref[...] = st_scr[pl.ds(n_chunk - 1, n_seq, stride=n_chunk), :]
    sfs_ref[...] = cmul(a_c, a_s, s0) + s_loc[n_p:, :]


def _ssm(tables, x_rows, s0, n_seq, n_chunk):
    kv, w, vm, a_c, a_s = tables
    n_groups, rows = x_rows.shape[0], x_rows.shape[1]
    n_s = s0.shape[1]
    width = GROUP_SIZE * CHUNK
    per_g3 = lambda g: (g, 0, 0)
    return pl.pallas_call(
        functools.partial(_ssm_kernel, n_seq=n_seq, n_chunk=n_chunk),
        grid=(n_groups,),
        in_specs=[pl.BlockSpec((None, GROUP_SIZE, GROUP_SIZE, LANES), lambda g: (g, 0, 0, 0)),
                  pl.BlockSpec((None, 2 * STATE_DIM, width), per_g3),
                  pl.BlockSpec((None, width, 2 * STATE_DIM), per_g3),
                  pl.BlockSpec((None, 1, 2 * STATE_DIM), per_g3),
                  pl.BlockSpec((None, 1, 2 * STATE_DIM), per_g3),
                  pl.BlockSpec((None, rows, width), per_g3),
                  pl.BlockSpec((None, n_s, 2 * STATE_DIM), per_g3)],
        out_specs=[pl.BlockSpec((None, rows, width), per_g3),
                   pl.BlockSpec((None, n_seq, 2 * STATE_DIM), per_g3),
                   pl.BlockSpec((None, n_s, 2 * STATE_DIM), per_g3)],
        out_shape=[jax.ShapeDtypeStruct((n_groups, rows, width), F32),
                   jax.ShapeDtypeStruct((n_groups, n_seq, 2 * STATE_DIM), F32),
                   jax.ShapeDtypeStruct((n_groups, n_s, 2 * STATE_DIM), F32)],
        scratch_shapes=[pltpu.VMEM((width, width), BF16), pltpu.VMEM((n_seq * n_chunk, 2 * STATE_DIM), F32)],
        compiler_params=_cparams("parallel"),
        name="ssm",
    )(kv, w, vm, a_c, a_s, x_rows, s0)


def _layer_norm(x, g, b):
    mu = jnp.mean(x, axis=1, keepdims=True)
    xc = x - mu
    var = jnp.mean(jnp.square(xc), axis=1, keepdims=True)
    return xc * lax.rsqrt(var + LN_EPS) * g + b


def _gelu_tanh(x):
    return 0.5 * x * (1.0 + jnp.tanh(math.sqrt(2.0 / math.pi) * (x + 0.044715 * (x * x * x))))


def _merge_kernel(x_ref, ao_ref, ys_ref, u_ref, d_ref, wg_ref, wap_ref, wglu_ref, wout_ref, g1_ref, b1_ref, h_ref):
    x = x_ref[...]
    xb = x.astype(BF16)
    a_branch = _dot(ao_ref[...], wap_ref[...])
    s_act = _gelu_tanh(ys_ref[...] + d_ref[...] * u_ref[...]).astype(BF16)
    s_branch = _dot(s_act, wglu_ref[:, 0:D_MODEL]) * jax.nn.sigmoid(_dot(s_act, wglu_ref[:, D_MODEL:2 * D_MODEL]))
    m = (jax.nn.sigmoid(_dot(xb, wg_ref[:, 0:D_MODEL])) * a_branch
         + jax.nn.sigmoid(_dot(xb, wg_ref[:, D_MODEL:2 * D_MODEL])) * s_branch)
    h_ref[...] = _layer_norm(DEEPNORM_ALPHA * x + _dot(m.astype(BF16), wout_ref[...]), g1_ref[...], b1_ref[...])


def _merge(x2d, ao, ys, u, d, w_gate, w_ap, w_glu, w_out, ln_g, ln_b, tm):
    t_tokens = x2d.shape[0]
    row = lambda i: (i, 0)
    const = lambda i: (0, 0)
    return pl.pallas_call(
        _merge_kernel,
        grid=(t_tokens // tm,),
        in_specs=[pl.BlockSpec((tm, D_MODEL), row),
                  pl.BlockSpec((tm, ATTN_WIDTH), row),
                  pl.BlockSpec((tm, SSM_WIDTH), row),
                  pl.BlockSpec((tm, SSM_WIDTH), row),
                  pl.BlockSpec((1, SSM_WIDTH), const),
                  pl.BlockSpec((D_MODEL, 2 * D_MODEL), const),
                  pl.BlockSpec((ATTN_WIDTH, D_MODEL), const),
                  pl.BlockSpec((SSM_WIDTH, 2 * D_MODEL), const),
                  pl.BlockSpec((D_MODEL, D_MODEL), const),
                  pl.BlockSpec((1, D_MODEL), const),
                  pl.BlockSpec((1, D_MODEL), const)],
        out_specs=pl.BlockSpec((tm, D_MODEL), row),
        out_shape=jax.ShapeDtypeStruct((t_tokens, D_MODEL), F32),
        compiler_params=_cparams("parallel"),
        name="merge",
    )(x2d, ao, ys, u, d, w_gate, w_ap, w_glu, w_out, ln_g, ln_b)


def _mlp_kernel(h_ref, w1_ref, w2_ref, g2_ref, b2_ref, o_ref, *, ff_chunk):
    h = h_ref[...]
    hb = h.astype(BF16)
    f = jnp.zeros(h.shape, F32)
    for c in range(D_FF // ff_chunk):
        sl = slice(c * ff_chunk, (c + 1) * ff_chunk)
        t = jnp.maximum(_dot(hb, w1_ref[:, sl]), 0.0)
        f = f + _dot((t * t).astype(BF16), w2_ref[sl, :])
    o_ref[...] = _layer_norm(DEEPNORM_ALPHA * h + f, g2_ref[...], b2_ref[...])


def _mlp(h, w1, w2, ln_g, ln_b, tm, ff_chunk=1024):
    t_tokens = h.shape[0]
    row = lambda i: (i, 0)
    const = lambda i: (0, 0)
    return pl.pallas_call(
        functools.partial(_mlp_kernel, ff_chunk=ff_chunk),
        grid=(t_tokens // tm,),
        in_specs=[pl.BlockSpec((tm, D_MODEL), row),
                  pl.BlockSpec((D_MODEL, D_FF), const),
                  pl.BlockSpec((D_FF, D_MODEL), const),
                  pl.BlockSpec((1, D_MODEL), const),
                  pl.BlockSpec((1, D_MODEL), const)],
        out_specs=pl.BlockSpec((tm, D_MODEL), row),
        out_shape=jax.ShapeDtypeStruct((t_tokens, D_MODEL), F32),
        compiler_params=_cparams("parallel"),
        name="mlp",
    )(h, w1, w2, ln_g, ln_b)


def _rope_tables(pos):
    inv = 1.0 / (ROPE_THETA ** (jnp.arange(0, HEAD_DIM, 2, dtype=F32) / HEAD_DIM))
    ang = pos.astype(F32)[:, None] * inv[None, :]
    c, s = jnp.cos(ang), jnp.sin(ang)
    reps = LANES // HEAD_DIM
    return jnp.tile(jnp.concatenate([c, c], axis=1), (1, reps)), jnp.tile(jnp.concatenate([-s, s], axis=1), (1, reps))


def _chunk_rows(u, bsz, n_chunk):
    t = u.reshape(bsz * n_chunk, CHUNK, N_GROUPS, GROUP_SIZE).transpose(2, 0, 3, 1)
    return t.reshape(N_GROUPS, bsz * n_chunk, GROUP_SIZE * CHUNK)


def _unchunk_rows(y, bsz, n_chunk):
    t = y.reshape(N_GROUPS, bsz * n_chunk, GROUP_SIZE, CHUNK).transpose(1, 3, 0, 2)
    return t.reshape(bsz * n_chunk * CHUNK, SSM_WIDTH)


def kernel(x_prompt, x_sample, cache_k, cache_v, state_ssm_re, state_ssm_im, w_in, lambda_q1, lambda_k1, lambda_q2, lambda_k2, subln_gain, ssm_a_re, ssm_a_im, ssm_b_re, ssm_b_im, ssm_c_re, ssm_c_im, ssm_d, ssm_log_dt, w_attn_proj, w_glu_a, w_glu_b, w_out, ln1_g, ln1_b, w_ff1, w_ff2, ln2_g, ln2_b):
    bp, n_p = x_prompt.shape[0], x_prompt.shape[1]
    bs, n_s = x_sample.shape[0], x_sample.shape[1]
    past = cache_k.shape[2]
    assert w_in.shape[0] == DEPTH and n_s == CHUNK and n_p % CHUNK == 0
    n_chunk = n_p // CHUNK
    tm = min(512, n_p)
    tm_s = min(512, bs * n_s)
    tq = min(512, n_p)
    l = 0
    lam_init = 0.8 - 0.6 * math.exp(-0.3 * l)

    xp = x_prompt.reshape(bp * n_p, D_MODEL)
    xs = x_sample.reshape(bs * n_s, D_MODEL)
    w_qkvu = w_in[l, :, 0:QKVU_COLS].astype(BF16)
    w_gate = w_in[l, :, QKVU_COLS:].astype(BF16)
    w_ap = w_attn_proj[l].astype(BF16)
    w_glu = jnp.concatenate([w_glu_a[l], w_glu_b[l]], axis=1).astype(BF16)
    w_o = w_out[l].astype(BF16)
    w1, w2 = w_ff1[l].astype(BF16), w_ff2[l].astype(BF16)
    lams = [v[l].reshape(1, HEAD_DIM) for v in (lambda_q1, lambda_k1, lambda_q2, lambda_k2)]
    gain = subln_gain[l].reshape(1, V_DIM)
    d_skip = ssm_d[l].reshape(1, SSM_WIDTH)
    lng = [v[l].reshape(1, D_MODEL) for v in (ln1_g, ln1_b, ln2_g, ln2_b)]

    cos_p, sin_p = _rope_tables(jnp.arange(n_p))
    cos_s, sin_s = _rope_tables(jnp.tile(past + jnp.arange(n_s), tm_s // n_s))

    q_p, k_p, v_p, u_p, kb_p, vt_p = _project(xp, w_qkvu, cos_p, sin_p, n_p, tm, True)
    q_s, k_s, v_s, u_s = _project(xs, w_qkvu, cos_s, sin_s, n_s, tm_s, False)

    ao_p = _prompt_attention(lams, gain, q_p, kb_p, vt_p, bp, n_p, tq, lam_init)
    ao_s = _sample_attention(lams, gain, q_s, k_s, v_s,
                             cache_k[l].reshape(bs, past * N_HEADS, V_DIM),
                             cache_v[l].reshape(bs, past * N_HEADS, V_DIM), lam_init)

    kv, w_st, vm, a_c, a_s = _ssm_tables(ssm_a_re[l], ssm_a_im[l], ssm_b_re[l], ssm_b_im[l],
                                         ssm_c_re[l], ssm_c_im[l], ssm_log_dt[l])
    tables = (kv, w_st.astype(BF16), vm.astype(BF16), a_c[:, None, :], a_s[:, None, :])
    x_rows = jnp.concatenate([_chunk_rows(u_p, bp, n_chunk), _chunk_rows(u_s, bs, 1)], axis=1).astype(BF16)
    s0 = jnp.concatenate([state_ssm_re[l], state_ssm_im[l]], axis=-1).transpose(1, 0, 2)
    y_rows, sf_p, sf_s = _ssm(tables, x_rows, s0, bp, n_chunk)
    ys_p = _unchunk_rows(y_rows[:, 0:bp * n_chunk], bp, n_chunk)
    ys_s = _unchunk_rows(y_rows[:, bp * n_chunk:], bs, 1)

    outs = []
    for x2d, ao, ys, u, tile in ((xp, ao_p, ys_p, u_p, tm), (xs, ao_s, ys_s, u_s, tm_s)):
        h = _merge(x2d, ao, ys, u, d_skip, w_gate, w_ap, w_glu, w_o, lng[0], lng[1], tile)
        outs.append(_mlp(h, w1, w2, lng[2], lng[3], tile))

    def states(sf):
        t = sf.transpose(1, 0, 2)
        return t[None, :, :, 0:STATE_DIM], t[None, :, :, STATE_DIM:]

    srp, sip = states(sf_p)
    srs, sis = states(sf_s)
    return (outs[0].reshape(bp, n_p, D_MODEL), outs[1].reshape(bs, n_s, D_MODEL),
            k_p.reshape(1, bp, n_p, N_HEADS, V_DIM), v_p.reshape(1, bp, n_p, N_HEADS, V_DIM), srp, sip,
            k_s.reshape(1, bs, n_s, N_HEADS, V_DIM), v_s.reshape(1, bs, n_s, N_HEADS, V_DIM), srs, sis)
```

```python
import functools
import math

import jax
import jax.numpy as jnp
from jax import lax
from jax.experimental import pallas as pl
from jax.experimental.pallas import tpu as pltpu

D_MODEL = 1024
CHUNK = 64
N_HEADS = 4
HEAD_DIM = 64
V_DIM = 2 * HEAD_DIM
ATTN_WIDTH = N_HEADS * V_DIM
SSM_WIDTH = 512
GROUP_SIZE = 16
N_GROUPS = SSM_WIDTH // GROUP_SIZE
STATE_DIM = 64
D_FF = 4 * D_MODEL
ROPE_THETA = 10000.0
LN_EPS = 1e-5
RMS_EPS = 1e-5
NEG_INF = -1e30
DEPTH = 1
DEEPNORM_ALPHA = (2.0 * DEPTH) ** 0.25
QKVU_COLS = 3 * ATTN_WIDTH + SSM_WIDTH
LOG2E = 1.4426950408889634

LANES = 128
VT_ROWS = V_DIM + 16
QUERY_LANES = 256
TOEPLITZ_ROWS = 256
VMEM_LIMIT = 56 * 1024 * 1024

F32 = jnp.float32
BF16 = jnp.bfloat16


def _cparams(*sem):
    return pltpu.CompilerParams(dimension_semantics=sem, vmem_limit_bytes=VMEM_LIMIT)


def _nt_dot(a, b):
    return lax.dot_general(a, b, (((1,), (1,)), ((), ())), preferred_element_type=F32)


def _dot(a, b):
    return jnp.dot(a, b, preferred_element_type=F32)


def _rotary(z, cos, sin_signed, first_half):
    swapped = jnp.where(first_half, pltpu.roll(z, 96, 1), pltpu.roll(z, 32, 1))
    return z * cos + swapped * sin_signed


def _proj_kernel(x_ref, w_ref, cos_ref, sin_ref, q_ref, k_ref, v_ref, u_ref, *rest, emit_t):
    xb = x_ref[...].astype(BF16)
    cos = cos_ref[...]
    sin = sin_ref[...]
    lane = lax.broadcasted_iota(jnp.int32, cos.shape, 1)
    first_half = (lane % HEAD_DIM) < (HEAD_DIM // 2)
    tm = xb.shape[0]
    zq = _dot(xb, w_ref[:, 0:ATTN_WIDTH])
    zk = _dot(xb, w_ref[:, ATTN_WIDTH:2 * ATTN_WIDTH])
    zv = _dot(xb, w_ref[:, 2 * ATTN_WIDTH:3 * ATTN_WIDTH])
    for h in range(N_HEADS):
        sl = slice(h * V_DIM, (h + 1) * V_DIM)
        q_ref[:, sl] = (_rotary(zq[:, sl], cos, sin, first_half) * (LOG2E * HEAD_DIM ** -0.5)).astype(BF16)
        kr = _rotary(zk[:, sl], cos, sin, first_half)
        k_ref[pl.ds(h, tm, stride=N_HEADS), :] = kr
        v_ref[pl.ds(h, tm, stride=N_HEADS), :] = zv[:, sl]
        if emit_t:
            rest[0][:, sl] = kr.astype(BF16)
    if emit_t:
        vt_ref = rest[1]
        zvt = zv.T.astype(BF16)
        ones = jnp.ones((VT_ROWS - V_DIM, zvt.shape[1]), BF16)
        for h in range(N_HEADS):
            vt_ref[h, 0:V_DIM, :] = zvt[h * V_DIM:(h + 1) * V_DIM, :]
            vt_ref[h, V_DIM:VT_ROWS, :] = ones
    u_ref[...] = _dot(xb, w_ref[:, 3 * ATTN_WIDTH:QKVU_COLS])


def _project(x2d, w_qkvu, cos_t, sin_t, seq_len, tm, emit_t):
    t_tokens = x2d.shape[0]
    n_tiles = t_tokens // tm
    n_pos_tiles = cos_t.shape[0] // tm
    tiles_per_seq = max(seq_len // tm, 1)
    row = lambda i: (i, 0)
    pos = lambda i: (i % n_pos_tiles, 0)
    out_shape = [jax.ShapeDtypeStruct((t_tokens, ATTN_WIDTH), BF16),
                 jax.ShapeDtypeStruct((t_tokens * N_HEADS, V_DIM), F32),
                 jax.ShapeDtypeStruct((t_tokens * N_HEADS, V_DIM), F32),
                 jax.ShapeDtypeStruct((t_tokens, SSM_WIDTH), F32)]
    out_specs = ([pl.BlockSpec((tm, ATTN_WIDTH), row)] + [pl.BlockSpec((tm * N_HEADS, V_DIM), row)] * 2
                 + [pl.BlockSpec((tm, SSM_WIDTH), row)])
    if emit_t:
        bsz = t_tokens // seq_len
        out_shape += [jax.ShapeDtypeStruct((t_tokens, ATTN_WIDTH), BF16),
                      jax.ShapeDtypeStruct((bsz, N_HEADS, VT_ROWS, seq_len), BF16)]
        out_specs += [pl.BlockSpec((tm, ATTN_WIDTH), row),
                      pl.BlockSpec((None, N_HEADS, VT_ROWS, tm),
                                   lambda i: (i // tiles_per_seq, 0, 0, i % tiles_per_seq))]
    return pl.pallas_call(
        functools.partial(_proj_kernel, emit_t=emit_t),
        grid=(n_tiles,),
        in_specs=[pl.BlockSpec((tm, D_MODEL), row),
                  pl.BlockSpec((D_MODEL, QKVU_COLS), lambda i: (0, 0)),
                  pl.BlockSpec((tm, LANES), pos),
                  pl.BlockSpec((tm, LANES), pos)],
        out_specs=out_specs,
        out_shape=out_shape,
        compiler_params=_cparams("parallel"),
        name="proj_t" if emit_t else "proj",
    )(x2d, w_qkvu, cos_t, sin_t)


def _diff_lambda(lq1, lk1, lq2, lk2, lam_init):
    return (jnp.exp(jnp.sum(lq1 * lk1, axis=1, keepdims=True))
            - jnp.exp(jnp.sum(lq2 * lk2, axis=1, keepdims=True)) + lam_init)


def _sub_norm(d, gain, lam_init):
    ms = jnp.mean(jnp.square(d), axis=1, keepdims=True)
    return d * lax.rsqrt(ms + RMS_EPS) * gain * (1.0 - lam_init)


def _stack_maps(q):
    lane = lax.broadcasted_iota(jnp.int32, q.shape, 1)
    zero = jnp.zeros_like(q)
    return jnp.concatenate([jnp.where(lane < HEAD_DIM, q, zero), jnp.where(lane >= HEAD_DIM, q, zero)], axis=0)


def _prompt_attn_kernel(lq1_ref, lk1_ref, lq2_ref, lk2_ref, gain_ref, q_ref, k_ref, vt_ref, o_ref,
                        m_scr, acc_scr, qx_scr, sa_scr, sb_scr, *, tq, lam_init):
    qi = pl.program_id(2)
    qx_scr[...] = _stack_maps(q_ref[...])
    m_scr[...] = jnp.full(m_scr.shape, NEG_INF, F32)
    acc_scr[...] = jnp.zeros(acc_scr.shape, F32)
    chains = [slice(c * QUERY_LANES, (c + 1) * QUERY_LANES) for c in range(2 * tq // QUERY_LANES)]

    def scores(j, s_scr):
        kt = k_ref[pl.ds(pl.multiple_of(j * tq, tq), tq), :]
        for cs in chains:
            s_scr[:, cs] = _nt_dot(kt, qx_scr[cs, :])

    def softmax_pv(j, s_scr, diagonal):
        vt = vt_ref[:, pl.ds(pl.multiple_of(j * tq, tq), tq)]
        for cs in chains:
            st = s_scr[:, cs]
            if diagonal:
                key_chunk = lax.broadcasted_iota(jnp.int32, st.shape, 0) // CHUNK
                qry_chunk = (cs.start % tq + lax.broadcasted_iota(jnp.int32, st.shape, 1)) // CHUNK
                st = jnp.where(key_chunk <= qry_chunk, st, NEG_INF)
            m_old = m_scr[:, cs]
            m_new = jnp.maximum(m_old, jnp.max(st, axis=0, keepdims=True))
            alpha = jnp.exp2(m_old - m_new)
            p = jnp.exp2(st - m_new).astype(BF16)
            acc_scr[:, cs] = acc_scr[:, cs] * alpha + _dot(vt, p)
            m_scr[:, cs] = m_new

    scores(0, sa_scr)

    def pair(i, carry):
        scores(2 * i + 1, sb_scr)
        softmax_pv(2 * i, sa_scr, False)
        scores(2 * i + 2, sa_scr)
        softmax_pv(2 * i + 1, sb_scr, False)
        return carry

    lax.fori_loop(0, qi // 2, pair, 0)

    @pl.when(qi % 2 == 0)
    def _():
        softmax_pv(qi, sa_scr, True)

    @pl.when(qi % 2 == 1)
    def _():
        scores(qi, sb_scr)
        softmax_pv(qi - 1, sa_scr, False)
        softmax_pv(qi, sb_scr, True)

    acc = acc_scr[...]
    o = acc[0:V_DIM, :] * (1.0 / acc[V_DIM:V_DIM + 1, :])
    lam = _diff_lambda(lq1_ref[...], lk1_ref[...], lq2_ref[...], lk2_ref[...], lam_init)
    d = (o[:, 0:tq] - lam * o[:, tq:2 * tq]).T
    o_ref[...] = _sub_norm(d, gain_ref[...], lam_init).astype(BF16)


def _prompt_attention(lams, gain, q, kb, vt, bsz, seq_len, tq, lam_init):
    nq = seq_len // tq
    small = lambda b, h, i: (0, 0)
    return pl.pallas_call(
        functools.partial(_prompt_attn_kernel, tq=tq, lam_init=lam_init),
        grid=(bsz, N_HEADS, nq),
        in_specs=[pl.BlockSpec((1, HEAD_DIM), small)] * 4 + [
            pl.BlockSpec((1, V_DIM), small),
            pl.BlockSpec((tq, V_DIM), lambda b, h, i: (b * nq + i, h)),
            pl.BlockSpec((seq_len, V_DIM), lambda b, h, i: (b, h)),
            pl.BlockSpec((None, None, VT_ROWS, seq_len), lambda b, h, i: (b, h, 0, 0))],
        out_specs=pl.BlockSpec((tq, V_DIM), lambda b, h, i: (b * nq + i, h)),
        out_shape=jax.ShapeDtypeStruct((bsz * seq_len, ATTN_WIDTH), BF16),
        scratch_shapes=[pltpu.VMEM((1, 2 * tq), F32), pltpu.VMEM((VT_ROWS, 2 * tq), F32),
                        pltpu.VMEM((2 * tq, V_DIM), BF16),
                        pltpu.VMEM((tq, 2 * tq), F32), pltpu.VMEM((tq, 2 * tq), F32)],
        compiler_params=_cparams("parallel", "parallel", "arbitrary"),
        name="prompt_attn",
    )(*lams, gain, q, kb, vt)


def _sample_attn_kernel(lq1_ref, lk1_ref, lq2_ref, lk2_ref, gain_ref, q_ref, k_ref, v_ref, ck_ref, cv_ref,
                        o_ref, *, lam_init):
    n_new = q_ref.shape[0]
    past = ck_ref.shape[0] // N_HEADS
    lam = _diff_lambda(lq1_ref[...], lk1_ref[...], lq2_ref[...], lk2_ref[...], lam_init)
    gain = gain_ref[...]
    for h in range(N_HEADS):
        sl = slice(h * V_DIM, (h + 1) * V_DIM)
        old = pl.ds(h, past, stride=N_HEADS)
        new = pl.ds(h, n_new, stride=N_HEADS)
        qx = _stack_maps(q_ref[:, sl])
        s_c = _nt_dot(qx, ck_ref[old, :].astype(BF16))
        s_n = _nt_dot(qx, k_ref[new, :].astype(BF16))
        m = jnp.maximum(jnp.max(s_c, axis=1, keepdims=True), jnp.max(s_n, axis=1, keepdims=True))
        p_c = jnp.exp2(s_c - m)
        p_n = jnp.exp2(s_n - m)
        denom = jnp.sum(p_c, axis=1, keepdims=True) + jnp.sum(p_n, axis=1, keepdims=True)
        o = (_dot(p_c.astype(BF16), cv_ref[old, :].astype(BF16))
             + _dot(p_n.astype(BF16), v_ref[new, :].astype(BF16))) * (1.0 / denom)
        d = o[0:n_new, :] - lam * o[n_new:2 * n_new, :]
        o_ref[:, sl] = _sub_norm(d, gain, lam_init).astype(BF16)


def _sample_attention(lams, gain, q, k, v, cache_k, cache_v, lam_init):
    bsz, past_rows = cache_k.shape[0], cache_k.shape[1]
    n_new = q.shape[0] // bsz
    small = lambda b: (0, 0)
    row = lambda b: (b, 0)
    return pl.pallas_call(
        functools.partial(_sample_attn_kernel, lam_init=lam_init),
        grid=(bsz,),
        in_specs=[pl.BlockSpec((1, HEAD_DIM), small)] * 4 + [
            pl.BlockSpec((1, V_DIM), small),
            pl.BlockSpec((n_new, ATTN_WIDTH), row),
            pl.BlockSpec((n_new * N_HEADS, V_DIM), row),
            pl.BlockSpec((n_new * N_HEADS, V_DIM), row),
            pl.BlockSpec((None, past_rows, V_DIM), lambda b: (b, 0, 0)),
            pl.BlockSpec((None, past_rows, V_DIM), lambda b: (b, 0, 0))],
        out_specs=pl.BlockSpec((n_new, ATTN_WIDTH), row),
        out_shape=jax.ShapeDtypeStruct(q.shape, BF16),
        compiler_params=_cparams("parallel"),
        name="sample_attn",
    )(*lams, gain, q, k, v, cache_k, cache_v)


def _ssm_tables(a_re, a_im, b_re, b_im, c_re, c_im, log_dt):
    dt = jnp.exp(log_dt)[:, None]
    lam_re, lam_im = a_re * dt, a_im * dt
    mag = jnp.exp(lam_re)
    ar, ai = mag * jnp.cos(lam_im), mag * jnp.sin(lam_im)
    den = jnp.square(a_re) + jnp.square(a_im)
    cr = ((ar - 1.0) * a_re + ai * a_im) / den
    ci = (ai * a_re - (ar - 1.0) * a_im) / den
    bbr = cr[..., None] * b_re - ci[..., None] * b_im
    bbi = cr[..., None] * b_im + ci[..., None] * b_re
    tau = jnp.arange(CHUNK + 1, dtype=F32)
    pm = jnp.exp(lam_re[..., None] * tau)
    pr, pi = pm * jnp.cos(lam_im[..., None] * tau), pm * jnp.sin(lam_im[..., None] * tau)
    hp = lax.Precision.HIGHEST
    zr = c_re[:, :, None, :] * bbr.transpose(0, 2, 1)[:, None] - c_im[:, :, None, :] * bbi.transpose(0, 2, 1)[:, None]
    zi = c_re[:, :, None, :] * bbi.transpose(0, 2, 1)[:, None] + c_im[:, :, None, :] * bbr.transpose(0, 2, 1)[:, None]
    kk = (jnp.einsum('gcdp,gpt->gctd', zr, pr[..., :CHUNK], precision=hp)
          - jnp.einsum('gcdp,gpt->gctd', zi, pi[..., :CHUNK], precision=hp))
    g = a_re.shape[0]
    width = GROUP_SIZE * CHUNK
    rrev = kk[:, :, ::-1, :].reshape(g, GROUP_SIZE, width)
    prr, pir = pr[..., CHUNK - 1::-1][..., :CHUNK], pi[..., CHUNK - 1::-1][..., :CHUNK]
    w_re = prr[..., None] * bbr[:, :, None, :] - pir[..., None] * bbi[:, :, None, :]
    w_im = prr[..., None] * bbi[:, :, None, :] + pir[..., None] * bbr[:, :, None, :]
    w = jnp.concatenate([w_re, w_im], axis=1).reshape(g, 2 * STATE_DIM, width)
    p1r, p1i = pr[..., 1:].transpose(0, 2, 1), pi[..., 1:].transpose(0, 2, 1)
    v_re = c_re[:, None] * p1r[:, :, None, :] - c_im[:, None] * p1i[:, :, None, :]
    v_im = c_re[:, None] * p1i[:, :, None, :] + c_im[:, None] * p1r[:, :, None, :]
    vm = jnp.concatenate([v_re, -v_im], axis=-1).reshape(g, width, 2 * STATE_DIM)
    acr, aci = pr[..., CHUNK], pi[..., CHUNK]
    a_c = jnp.broadcast_to(jnp.concatenate([acr, acr], axis=-1)[..., None], (g, 2 * STATE_DIM, LANES))
    a_s = jnp.broadcast_to(jnp.concatenate([-aci, aci], axis=-1)[..., None], (g, 2 * STATE_DIM, LANES))
    return rrev, w, vm, a_c, a_s


def _chunk_cols_kernel(u_ref, o_ref):
    n_s, n_r, n_j = u_ref.shape[0], u_ref.shape[1], u_ref.shape[2]
    pad = LANES - n_s * n_r
    for jj in range(n_j):
        rows = [u_ref[s, :, jj, :] for s in range(n_s)]
        if pad:
            rows.append(jnp.zeros((pad, SSM_WIDTH), F32))
        cols = jnp.concatenate(rows, axis=0).T
        o_ref[:, jj * GROUP_SIZE:(jj + 1) * GROUP_SIZE, :] = cols.reshape(N_GROUPS, GROUP_SIZE, LANES).astype(BF16)


def _chunk_cols(u4, frames_per_step=16):
    s_total, n_r = u4.shape[0], u4.shape[1]
    s_step = min(s_total, LANES // n_r)
    assert s_step >= 1 and s_total % s_step == 0
    n_lane_blocks = s_total // s_step
    return pl.pallas_call(
        _chunk_cols_kernel,
        grid=(n_lane_blocks, CHUNK // frames_per_step),
        in_specs=[pl.BlockSpec((s_step, n_r, frames_per_step, SSM_WIDTH), lambda a, j: (a, 0, j, 0))],
        out_specs=pl.BlockSpec((N_GROUPS, frames_per_step * GROUP_SIZE, LANES), lambda a, j: (0, j, a)),
        out_shape=jax.ShapeDtypeStruct((N_GROUPS, CHUNK * GROUP_SIZE, n_lane_blocks * LANES), BF16),
        compiler_params=_cparams("parallel", "parallel"),
        name="ssm_in",
    )(u4)


def _unchunk_cols_kernel(y_ref, o_ref):
    n_s, n_r, n_j = o_ref.shape[0], o_ref.shape[1], o_ref.shape[2]
    for jj in range(n_j):
        cols = y_ref[:, jj * GROUP_SIZE:(jj + 1) * GROUP_SIZE, :].reshape(SSM_WIDTH, LANES)
        rows = cols.T
        for s in range(n_s):
            o_ref[s, :, jj, :] = rows[s * n_r:(s + 1) * n_r, :]


def _unchunk_cols(y_cols, s_total, n_r, frames_per_step=16):
    s_step = min(s_total, LANES // n_r)
    n_lane_blocks = s_total // s_step
    return pl.pallas_call(
        _unchunk_cols_kernel,
        grid=(n_lane_blocks, CHUNK // frames_per_step),
        in_specs=[pl.BlockSpec((N_GROUPS, frames_per_step * GROUP_SIZE, LANES), lambda a, j: (0, j, a))],
        out_specs=pl.BlockSpec((s_step, n_r, frames_per_step, SSM_WIDTH), lambda a, j: (a, 0, j, 0)),
        out_shape=jax.ShapeDtypeStruct((s_total, n_r, CHUNK, SSM_WIDTH), F32),
        compiler_params=_cparams("parallel", "parallel"),
        name="ssm_out",
    )(y_cols)


def _ssm_kernel(rrev_ref, w_ref, vm_ref, ac_ref, as_ref, xp_ref, xs_ref, s0_ref, yp_ref, ys_ref, sp_ref, ss_ref,
                mt_scr, *, n_chunk):
    width = GROUP_SIZE * CHUNK
    n_piece = width // LANES
    lane16 = lax.broadcasted_iota(jnp.int32, (GROUP_SIZE, LANES), 1)
    pieces = [rrev_ref[:, k * LANES:(k + 1) * LANES] for k in range(n_piece)] + [jnp.zeros((GROUP_SIZE, LANES), F32)]
    rolled = {0: pieces}
    for b in range(GROUP_SIZE, LANES, GROUP_SIZE):
        rolled[b] = [pltpu.roll(p, LANES - b, 1) for p in pieces[:n_piece]] + [pieces[n_piece]]
    def toeplitz_rows(t0, t1, n_cols):
        for t in range(t0, t1):
            shift = GROUP_SIZE * (CHUNK - 1 - t)
            a, b = shift // LANES, shift % LANES
            for v in range(n_cols // LANES):
                k = v + a
                if k >= n_piece:
                    blk = pieces[n_piece]
                elif b == 0:
                    blk = pieces[k]
                else:
                    blk = jnp.where(lane16 < LANES - b, rolled[b][k], rolled[b][k + 1])
                mt_scr[t * GROUP_SIZE:(t + 1) * GROUP_SIZE, v * LANES:(v + 1) * LANES] = blk.astype(BF16)

    w = w_ref[...]
    xp, xs = xp_ref[...], xs_ref[...]
    sloc_p, sloc_s = _dot(w, xp), _dot(w, xs)
    a_c, a_s = ac_ref[...], as_ref[...]

    def cmul(pc, ps, s):
        return pc * s + ps * pltpu.roll(s, STATE_DIM, 0)

    lane = lax.broadcasted_iota(jnp.int32, (2 * STATE_DIM, LANES), 1) % n_chunk
    prev_cols = []
    for tile in range(xp.shape[1] // LANES):
        s_inc = sloc_p[:, tile * LANES:(tile + 1) * LANES]
        pc, ps = a_c, a_s
        dist = 1
        while dist < n_chunk:
            shifted = jnp.where(lane >= dist, pltpu.roll(s_inc, dist, 1), 0.0)
            s_inc = s_inc + cmul(pc, ps, shifted)
            pc, ps = pc * pc - ps * ps, 2.0 * pc * ps
            dist *= 2
        sp_ref[:, tile * LANES:(tile + 1) * LANES] = s_inc
        prev_cols.append(jnp.where(lane >= 1, pltpu.roll(s_inc, 1, 1), 0.0))
    s_prev_p = jnp.concatenate(prev_cols, axis=1).astype(BF16)
    s0 = s0_ref[...]
    s0b = s0.astype(BF16)
    frames = TOEPLITZ_ROWS // GROUP_SIZE
    for i in range(width // TOEPLITZ_ROWS):
        rows = slice(i * TOEPLITZ_ROWS, (i + 1) * TOEPLITZ_ROWS)
        n_cols = (i + 1) * TOEPLITZ_ROWS
        toeplitz_rows(i * frames, (i + 1) * frames, n_cols)
        mt = mt_scr[rows, 0:n_cols]
        vm = vm_ref[rows, :]
        yp_ref[rows, :] = _dot(mt, xp[0:n_cols, :]) + _dot(vm, s_prev_p)
        ys_ref[rows, :] = _dot(mt, xs[0:n_cols, :]) + _dot(vm, s0b)
    ss_ref[...] = cmul(a_c, a_s, s0) + sloc_s


def _ssm(tables, x_p, x_s, s0, n_chunk):
    rrev, w, vm, a_c, a_s = tables
    n_groups, lanes_p = x_p.shape[0], x_p.shape[2]
    width = GROUP_SIZE * CHUNK
    per_g = lambda g: (g, 0, 0)
    state_rows = 2 * STATE_DIM
    return pl.pallas_call(
        functools.partial(_ssm_kernel, n_chunk=n_chunk),
        grid=(n_groups,),
        in_specs=[pl.BlockSpec((None, GROUP_SIZE, width), per_g),
                  pl.BlockSpec((None, state_rows, width), per_g),
                  pl.BlockSpec((None, width, state_rows), per_g),
                  pl.BlockSpec((None, state_rows, LANES), per_g),
                  pl.BlockSpec((None, state_rows, LANES), per_g),
                  pl.BlockSpec((None, width, lanes_p), per_g),
                  pl.BlockSpec((None, width, LANES), per_g),
                  pl.BlockSpec((None, state_rows, LANES), per_g)],
        out_specs=[pl.BlockSpec((None, width, lanes_p), per_g),
                   pl.BlockSpec((None, width, LANES), per_g),
                   pl.BlockSpec((None, state_rows, lanes_p), per_g),
                   pl.BlockSpec((None, state_rows, LANES), per_g)],
        out_shape=[jax.ShapeDtypeStruct((n_groups, width, lanes_p), F32),
                   jax.ShapeDtypeStruct((n_groups, width, LANES), F32),
                   jax.ShapeDtypeStruct((n_groups, state_rows, lanes_p), F32),
                   jax.ShapeDtypeStruct((n_groups, state_rows, LANES), F32)],
        scratch_shapes=[pltpu.VMEM((width, width), BF16)],
        compiler_params=_cparams("parallel"),
        name="ssm",
    )(rrev, w, vm, a_c, a_s, x_p, x_s, s0)


def _layer_norm(x, g, b):
    mu = jnp.mean(x, axis=1, keepdims=True)
    xc = x - mu
    var = jnp.mean(jnp.square(xc), axis=1, keepdims=True)
    return xc * lax.rsqrt(var + LN_EPS) * g + b


def _gelu_tanh(x):
    return 0.5 * x * (1.0 + jnp.tanh(math.sqrt(2.0 / math.pi) * (x + 0.044715 * (x * x * x))))


def _merge_kernel(x_ref, ao_ref, ys_ref, u_ref, d_ref, wg_ref, wap_ref, wglu_ref, wout_ref, g1_ref, b1_ref, h_ref):
    x = x_ref[...]
    xb = x.astype(BF16)
    a_branch = _dot(ao_ref[...], wap_ref[...])
    s_act = _gelu_tanh(ys_ref[...] + d_ref[...] * u_ref[...]).astype(BF16)
    s_branch = _dot(s_act, wglu_ref[:, 0:D_MODEL]) * jax.nn.sigmoid(_dot(s_act, wglu_ref[:, D_MODEL:2 * D_MODEL]))
    m = (jax.nn.sigmoid(_dot(xb, wg_ref[:, 0:D_MODEL])) * a_branch
         + jax.nn.sigmoid(_dot(xb, wg_ref[:, D_MODEL:2 * D_MODEL])) * s_branch)
    h_ref[...] = _layer_norm(DEEPNORM_ALPHA * x + _dot(m.astype(BF16), wout_ref[...]), g1_ref[...], b1_ref[...])


def _merge(x2d, ao, ys, u, d, w_gate, w_ap, w_glu, w_out, ln_g, ln_b, tm):
    t_tokens = x2d.shape[0]
    row = lambda i: (i, 0)
    const = lambda i: (0, 0)
    return pl.pallas_call(
        _merge_kernel,
        grid=(t_tokens // tm,),
        in_specs=[pl.BlockSpec((tm, D_MODEL), row),
                  pl.BlockSpec((tm, ATTN_WIDTH), row),
                  pl.BlockSpec((tm, SSM_WIDTH), row),
                  pl.BlockSpec((tm, SSM_WIDTH), row),
                  pl.BlockSpec((1, SSM_WIDTH), const),
                  pl.BlockSpec((D_MODEL, 2 * D_MODEL), const),
                  pl.BlockSpec((ATTN_WIDTH, D_MODEL), const),
                  pl.BlockSpec((SSM_WIDTH, 2 * D_MODEL), const),
                  pl.BlockSpec((D_MODEL, D_MODEL), const),
                  pl.BlockSpec((1, D_MODEL), const),
                  pl.BlockSpec((1, D_MODEL), const)],
        out_specs=pl.BlockSpec((tm, D_MODEL), row),
        out_shape=jax.ShapeDtypeStruct((t_tokens, D_MODEL), F32),
        compiler_params=_cparams("parallel"),
        name="merge",
    )(x2d, ao, ys, u, d, w_gate, w_ap, w_glu, w_out, ln_g, ln_b)


def _mlp_kernel(h_ref, w1_ref, w2_ref, g2_ref, b2_ref, o_ref, *, ff_chunk):
    h = h_ref[...]
    hb = h.astype(BF16)
    f = jnp.zeros(h.shape, F32)
    for c in range(D_FF // ff_chunk):
        sl = slice(c * ff_chunk, (c + 1) * ff_chunk)
        t = jnp.maximum(_dot(hb, w1_ref[:, sl]), 0.0)
        f = f + _dot((t * t).astype(BF16), w2_ref[sl, :])
    o_ref[...] = _layer_norm(DEEPNORM_ALPHA * h + f, g2_ref[...], b2_ref[...])


def _mlp(h, w1, w2, ln_g, ln_b, tm, ff_chunk=1024):
    t_tokens = h.shape[0]
    row = lambda i: (i, 0)
    const = lambda i: (0, 0)
    return pl.pallas_call(
        functools.partial(_mlp_kernel, ff_chunk=ff_chunk),
        grid=(t_tokens // tm,),
        in_specs=[pl.BlockSpec((tm, D_MODEL), row),
                  pl.BlockSpec((D_MODEL, D_FF), const),
                  pl.BlockSpec((D_FF, D_MODEL), const),
                  pl.BlockSpec((1, D_MODEL), const),
                  pl.BlockSpec((1, D_MODEL), const)],
        out_specs=pl.BlockSpec((tm, D_MODEL), row),
        out_shape=jax.ShapeDtypeStruct((t_tokens, D_MODEL), F32),
        compiler_params=_cparams("parallel"),
        name="mlp",
    )(h, w1, w2, ln_g, ln_b)


def _rope_tables(pos):
    inv = 1.0 / (ROPE_THETA ** (jnp.arange(0, HEAD_DIM, 2, dtype=F32) / HEAD_DIM))
    ang = pos.astype(F32)[:, None] * inv[None, :]
    c, s = jnp.cos(ang), jnp.sin(ang)
    reps = LANES // HEAD_DIM
    return jnp.tile(jnp.concatenate([c, c], axis=1), (1, reps)), jnp.tile(jnp.concatenate([-s, s], axis=1), (1, reps))


def kernel(x_prompt, x_sample, cache_k, cache_v, state_ssm_re, state_ssm_im, w_in, lambda_q1, lambda_k1, lambda_q2, lambda_k2, subln_gain, ssm_a_re, ssm_a_im, ssm_b_re, ssm_b_im, ssm_c_re, ssm_c_im, ssm_d, ssm_log_dt, w_attn_proj, w_glu_a, w_glu_b, w_out, ln1_g, ln1_b, w_ff1, w_ff2, ln2_g, ln2_b):
    bp, n_p = x_prompt.shape[0], x_prompt.shape[1]
    bs, n_s = x_sample.shape[0], x_sample.shape[1]
    past = cache_k.shape[2]
    assert w_in.shape[0] == DEPTH and n_s == CHUNK and n_p % CHUNK == 0
    n_chunk = n_p // CHUNK
    tm = min(512, n_p)
    tm_s = min(512, bs * n_s)
    tq = min(512, n_p)
    l = 0
    lam_init = 0.8 - 0.6 * math.exp(-0.3 * l)

    xp = x_prompt.reshape(bp * n_p, D_MODEL)
    xs = x_sample.reshape(bs * n_s, D_MODEL)
    w_qkvu = w_in[l, :, 0:QKVU_COLS].astype(BF16)
    w_gate = w_in[l, :, QKVU_COLS:].astype(BF16)
    w_ap = w_attn_proj[l].astype(BF16)
    w_glu = jnp.concatenate([w_glu_a[l], w_glu_b[l]], axis=1).astype(BF16)
    w_o = w_out[l].astype(BF16)
    w1, w2 = w_ff1[l].astype(BF16), w_ff2[l].astype(BF16)
    lams = [v[l].reshape(1, HEAD_DIM) for v in (lambda_q1, lambda_k1, lambda_q2, lambda_k2)]
    gain = subln_gain[l].reshape(1, V_DIM)
    d_skip = ssm_d[l].reshape(1, SSM_WIDTH)
    lng = [v[l].reshape(1, D_MODEL) for v in (ln1_g, ln1_b, ln2_g, ln2_b)]

    cos_p, sin_p = _rope_tables(jnp.arange(n_p))
    cos_s, sin_s = _rope_tables(jnp.tile(past + jnp.arange(n_s), tm_s // n_s))

    q_p, k_p, v_p, u_p, kb_p, vt_p = _project(xp, w_qkvu, cos_p, sin_p, n_p, tm, True)
    q_s, k_s, v_s, u_s = _project(xs, w_qkvu, cos_s, sin_s, n_s, tm_s, False)

    ao_p = _prompt_attention(lams, gain, q_p, kb_p, vt_p, bp, n_p, tq, lam_init)
    ao_s = _sample_attention(lams, gain, q_s, k_s, v_s,
                             cache_k[l].reshape(bs, past * N_HEADS, V_DIM),
                             cache_v[l].reshape(bs, past * N_HEADS, V_DIM), lam_init)

    kv, w_st, vm, a_c, a_s = _ssm_tables(ssm_a_re[l], ssm_a_im[l], ssm_b_re[l], ssm_b_im[l],
                                         ssm_c_re[l], ssm_c_im[l], ssm_log_dt[l])
    tables = (kv, w_st.astype(BF16), vm.astype(BF16), a_c, a_s)
    x_cols_p = _chunk_cols(u_p.reshape(bp, n_chunk, CHUNK, SSM_WIDTH))
    x_cols_s = _chunk_cols(u_s.reshape(1, bs, CHUNK, SSM_WIDTH))
    s0 = jnp.concatenate([state_ssm_re[l], state_ssm_im[l]], axis=-1).transpose(1, 2, 0)
    s0 = jnp.pad(s0, ((0, 0), (0, 0), (0, LANES - bs)))
    y_cols_p, y_cols_s, st_p, st_s = _ssm(tables, x_cols_p, x_cols_s, s0, n_chunk)
    ys_p = _unchunk_cols(y_cols_p, bp, n_chunk).reshape(bp * n_p, SSM_WIDTH)
    ys_s = _unchunk_cols(y_cols_s, 1, bs).reshape(bs * n_s, SSM_WIDTH)
    sf_p = st_p[:, :, n_chunk - 1::n_chunk]
    sf_s = st_s[:, :, 0:bs]

    outs = []
    for x2d, ao, ys, u, tile in ((xp, ao_p, ys_p, u_p, tm), (xs, ao_s, ys_s, u_s, tm_s)):
        h = _merge(x2d, ao, ys, u, d_skip, w_gate, w_ap, w_glu, w_o, lng[0], lng[1], tile)
        outs.append(_mlp(h, w1, w2, lng[2], lng[3], tile))

    def states(sf):
        t = sf.transpose(2, 0, 1)
        return t[None, :, :, 0:STATE_DIM], t[None, :, :, STATE_DIM:]

    srp, sip = states(sf_p)
    srs, sis = states(sf_s)
    return (outs[0].reshape(bp, n_p, D_MODEL), outs[1].reshape(bs, n_s, D_MODEL),
            k_p.reshape(1, bp, n_p, N_HEADS, V_DIM), v_p.reshape(1, bp, n_p, N_HEADS, V_DIM), srp, sip,
            k_s.reshape(1, bs, n_s, N_HEADS, V_DIM), v_s.reshape(1, bs, n_s, N_HEADS, V_DIM), srs, sis)
```

```python
import functools
import math

import jax
import jax.numpy as jnp
from jax import lax
from jax.experimental import pallas as pl
from jax.experimental.pallas import tpu as pltpu

D_MODEL = 1024
CHUNK = 64
N_HEADS = 4
HEAD_DIM = 64
V_DIM = 2 * HEAD_DIM
ATTN_WIDTH = N_HEADS * V_DIM
SSM_WIDTH = 512
GROUP_SIZE = 16
N_GROUPS = SSM_WIDTH // GROUP_SIZE
STATE_DIM = 64
D_FF = 4 * D_MODEL
ROPE_THETA = 10000.0
LN_EPS = 1e-5
RMS_EPS = 1e-5
NEG_INF = -1e30
DEPTH = 1
DEEPNORM_ALPHA = (2.0 * DEPTH) ** 0.25
QKVU_COLS = 3 * ATTN_WIDTH + SSM_WIDTH
LOG2E = 1.4426950408889634

LANES = 128
VT_ROWS = V_DIM + 16
QUERY_LANES = 256
TOEPLITZ_ROWS = 256
VMEM_LIMIT = 56 * 1024 * 1024

F32 = jnp.float32
BF16 = jnp.bfloat16


def _cparams(*sem):
    return pltpu.CompilerParams(dimension_semantics=sem, vmem_limit_bytes=VMEM_LIMIT)


def _nt_dot(a, b):
    return lax.dot_general(a, b, (((1,), (1,)), ((), ())), preferred_element_type=F32)


def _dot(a, b):
    return jnp.dot(a, b, preferred_element_type=F32)


def _rotary(z, cos, sin_signed, first_half):
    swapped = jnp.where(first_half, pltpu.roll(z, 96, 1), pltpu.roll(z, 32, 1))
    return z * cos + swapped * sin_signed


def _proj_kernel(x_ref, w_ref, cos_ref, sin_ref, q_ref, k_ref, v_ref, u_ref, *rest, emit_t):
    xb = x_ref[...].astype(BF16)
    cos = cos_ref[...]
    sin = sin_ref[...]
    lane = lax.broadcasted_iota(jnp.int32, cos.shape, 1)
    first_half = (lane % HEAD_DIM) < (HEAD_DIM // 2)
    tm = xb.shape[0]
    zq = _dot(xb, w_ref[:, 0:ATTN_WIDTH])
    zk = _dot(xb, w_ref[:, ATTN_WIDTH:2 * ATTN_WIDTH])
    zv = _dot(xb, w_ref[:, 2 * ATTN_WIDTH:3 * ATTN_WIDTH])
    for h in range(N_HEADS):
        sl = slice(h * V_DIM, (h + 1) * V_DIM)
        q_ref[:, sl] = (_rotary(zq[:, sl], cos, sin, first_half) * (LOG2E * HEAD_DIM ** -0.5)).astype(BF16)
        kr = _rotary(zk[:, sl], cos, sin, first_half)
        k_ref[pl.ds(h, tm, stride=N_HEADS), :] = kr
        v_ref[pl.ds(h, tm, stride=N_HEADS), :] = zv[:, sl]
        if emit_t:
            rest[0][:, sl] = kr.astype(BF16)
    if emit_t:
        vt_ref = rest[1]
        zvt = zv.T.astype(BF16)
        ones = jnp.ones((VT_ROWS - V_DIM, zvt.shape[1]), BF16)
        for h in range(N_HEADS):
            vt_ref[h, 0:V_DIM, :] = zvt[h * V_DIM:(h + 1) * V_DIM, :]
            vt_ref[h, V_DIM:VT_ROWS, :] = ones
    u_ref[...] = _dot(xb, w_ref[:, 3 * ATTN_WIDTH:QKVU_COLS])


def _project(x2d, w_qkvu, cos_t, sin_t, seq_len, tm, emit_t):
    t_tokens = x2d.shape[0]
    n_tiles = t_tokens // tm
    n_pos_tiles = cos_t.shape[0] // tm
    tiles_per_seq = max(seq_len // tm, 1)
    row = lambda i: (i, 0)
    pos = lambda i: (i % n_pos_tiles, 0)
    out_shape = [jax.ShapeDtypeStruct((t_tokens, ATTN_WIDTH), BF16),
                 jax.ShapeDtypeStruct((t_tokens * N_HEADS, V_DIM), F32),
                 jax.ShapeDtypeStruct((t_tokens * N_HEADS, V_DIM), F32),
                 jax.ShapeDtypeStruct((t_tokens, SSM_WIDTH), F32)]
    out_specs = ([pl.BlockSpec((tm, ATTN_WIDTH), row)] + [pl.BlockSpec((tm * N_HEADS, V_DIM), row)] * 2
                 + [pl.BlockSpec((tm, SSM_WIDTH), row)])
    if emit_t:
        bsz = t_tokens // seq_len
        out_shape += [jax.ShapeDtypeStruct((t_tokens, ATTN_WIDTH), BF16),
                      jax.ShapeDtypeStruct((bsz, N_HEADS, VT_ROWS, seq_len), BF16)]
        out_specs += [pl.BlockSpec((tm, ATTN_WIDTH), row),
                      pl.BlockSpec((None, N_HEADS, VT_ROWS, tm),
                                   lambda i: (i // tiles_per_seq, 0, 0, i % tiles_per_seq))]
    return pl.pallas_call(
        functools.partial(_proj_kernel, emit_t=emit_t),
        grid=(n_tiles,),
        in_specs=[pl.BlockSpec((tm, D_MODEL), row),
                  pl.BlockSpec((D_MODEL, QKVU_COLS), lambda i: (0, 0)),
                  pl.BlockSpec((tm, LANES), pos),
                  pl.BlockSpec((tm, LANES), pos)],
        out_specs=out_specs,
        out_shape=out_shape,
        compiler_params=_cparams("parallel"),
        name="proj_t" if emit_t else "proj",
    )(x2d, w_qkvu, cos_t, sin_t)


def _diff_lambda(lq1, lk1, lq2, lk2, lam_init):
    return (jnp.exp(jnp.sum(lq1 * lk1, axis=1, keepdims=True))
            - jnp.exp(jnp.sum(lq2 * lk2, axis=1, keepdims=True)) + lam_init)


def _sub_norm(d, gain, lam_init):
    ms = jnp.mean(jnp.square(d), axis=1, keepdims=True)
    return d * lax.rsqrt(ms + RMS_EPS) * gain * (1.0 - lam_init)


def _stack_maps(q):
    lane = lax.broadcasted_iota(jnp.int32, q.shape, 1)
    zero = jnp.zeros_like(q)
    return jnp.concatenate([jnp.where(lane < HEAD_DIM, q, zero), jnp.where(lane >= HEAD_DIM, q, zero)], axis=0)


def _prompt_attn_kernel(item_q_ref, item_k_ref, lq1_ref, lk1_ref, lq2_ref, lk2_ref, gain_ref, q_ref, k_ref, vt_ref,
                        o_ref, m_scr, acc_scr, qx_scr, s0_scr, s1_scr, *, tq, n_items, lam_init):
    nq = q_ref.shape[0] // tq
    for i in range(nq):
        qx_scr[i * 2 * tq:(i + 1) * 2 * tq, :] = _stack_maps(q_ref[i * tq:(i + 1) * tq, :])
    acc_scr[...] = jnp.zeros(acc_scr.shape, F32)
    m_scr[...] = jnp.full(m_scr.shape, NEG_INF, F32)
    chains = [slice(c * QUERY_LANES, (c + 1) * QUERY_LANES) for c in range(2 * tq // QUERY_LANES)]

    def scores(w, s_scr):
        qi, j = item_q_ref[w], item_k_ref[w]
        kt = k_ref[pl.ds(pl.multiple_of(j * tq, tq), tq), :]
        for cs in chains:
            qx = qx_scr[pl.ds(pl.multiple_of(qi * 2 * tq + cs.start, QUERY_LANES), QUERY_LANES), :]
            s_scr[:, cs] = _nt_dot(kt, qx)

    def softmax_pv(j, s_scr, diagonal):
        vt = vt_ref[:, pl.ds(pl.multiple_of(j * tq, tq), tq)]
        for cs in chains:
            st = s_scr[:, cs]
            if diagonal:
                key_chunk = lax.broadcasted_iota(jnp.int32, st.shape, 0) // CHUNK
                qry_chunk = (cs.start % tq + lax.broadcasted_iota(jnp.int32, st.shape, 1)) // CHUNK
                st = jnp.where(key_chunk <= qry_chunk, st, NEG_INF)
            m_old = jnp.where(j == 0, NEG_INF, m_scr[:, cs])
            m_new = jnp.maximum(m_old, jnp.max(st, axis=0, keepdims=True))
            alpha = jnp.exp2(m_old - m_new)
            p = jnp.exp2(st - m_new).astype(BF16)
            acc_scr[:, cs] = acc_scr[:, cs] * alpha + _dot(vt, p)
            m_scr[:, cs] = m_new

    def finish(qi):
        acc = acc_scr[...]
        o = acc[0:V_DIM, :] * (1.0 / acc[V_DIM:V_DIM + 1, :])
        lam = _diff_lambda(lq1_ref[...], lk1_ref[...], lq2_ref[...], lk2_ref[...], lam_init)
        d = (o[:, 0:tq] - lam * o[:, tq:2 * tq]).T
        o_ref[pl.ds(pl.multiple_of(qi * tq, tq), tq), :] = _sub_norm(d, gain_ref[...], lam_init).astype(BF16)

    def pair_block(w, diag0, diag1):
        scores(w + 1, s1_scr)
        softmax_pv(item_k_ref[w], s0_scr, diag0)
        if diag0:
            finish(item_q_ref[w])
        scores(w + 2, s0_scr)
        softmax_pv(item_k_ref[w + 1], s1_scr, diag1)
        if diag1:
            finish(item_q_ref[w + 1])

    scores(0, s0_scr)

    def pair(i, carry):
        w = 2 * i
        d0 = item_q_ref[w] == item_k_ref[w]
        d1 = item_q_ref[w + 1] == item_k_ref[w + 1]
        for diag0 in (False, True):
            for diag1 in (False, True):
                pl.when(jnp.logical_and(d0 == diag0, d1 == diag1))(
                    functools.partial(pair_block, w, diag0, diag1))
        return carry

    lax.fori_loop(0, n_items // 2, pair, 0)


def _prompt_attention(lams, gain, q, kb, vt, bsz, seq_len, tq, lam_init):
    nq = seq_len // tq
    items = [(qi, j) for qi in range(nq) for j in range(qi + 1)]
    n_items = len(items)
    assert n_items % 2 == 0
    items.append(items[-1])
    items.append(items[-1])
    item_q = jnp.asarray([it[0] for it in items], jnp.int32)
    item_k = jnp.asarray([it[1] for it in items], jnp.int32)
    small = lambda b, h, iq, ik: (0, 0)
    grid_spec = pltpu.PrefetchScalarGridSpec(
        num_scalar_prefetch=2,
        grid=(bsz, N_HEADS),
        in_specs=[pl.BlockSpec((1, HEAD_DIM), small)] * 4 + [
            pl.BlockSpec((1, V_DIM), small),
            pl.BlockSpec((seq_len, V_DIM), lambda b, h, iq, ik: (b, h)),
            pl.BlockSpec((seq_len, V_DIM), lambda b, h, iq, ik: (b, h)),
            pl.BlockSpec((None, None, VT_ROWS, seq_len), lambda b, h, iq, ik: (b, h, 0, 0))],
        out_specs=pl.BlockSpec((seq_len, V_DIM), lambda b, h, iq, ik: (b, h)),
        scratch_shapes=[pltpu.VMEM((1, 2 * tq), F32), pltpu.VMEM((VT_ROWS, 2 * tq), F32),
                        pltpu.VMEM((nq * 2 * tq, V_DIM), BF16),
                        pltpu.VMEM((tq, 2 * tq), F32), pltpu.VMEM((tq, 2 * tq), F32)])
    return pl.pallas_call(
        functools.partial(_prompt_attn_kernel, tq=tq, n_items=n_items, lam_init=lam_init),
        grid_spec=grid_spec,
        out_shape=jax.ShapeDtypeStruct((bsz * seq_len, ATTN_WIDTH), BF16),
        compiler_params=_cparams("parallel", "parallel"),
        name="prompt_attn",
    )(item_q, item_k, *lams, gain, q, kb, vt)


def _sample_attn_kernel(lq1_ref, lk1_ref, lq2_ref, lk2_ref, gain_ref, q_ref, k_ref, v_ref, ck_ref, cv_ref,
                        o_ref, *, lam_init):
    n_new = q_ref.shape[0]
    past = ck_ref.shape[0] // N_HEADS
    lam = _diff_lambda(lq1_ref[...], lk1_ref[...], lq2_ref[...], lk2_ref[...], lam_init)
    gain = gain_ref[...]
    for h in range(N_HEADS):
        sl = slice(h * V_DIM, (h + 1) * V_DIM)
        old = pl.ds(h, past, stride=N_HEADS)
        new = pl.ds(h, n_new, stride=N_HEADS)
        qx = _stack_maps(q_ref[:, sl])
        s_c = _nt_dot(qx, ck_ref[old, :].astype(BF16))
        s_n = _nt_dot(qx, k_ref[new, :].astype(BF16))
        m = jnp.maximum(jnp.max(s_c, axis=1, keepdims=True), jnp.max(s_n, axis=1, keepdims=True))
        p_c = jnp.exp2(s_c - m)
        p_n = jnp.exp2(s_n - m)
        denom = jnp.sum(p_c, axis=1, keepdims=True) + jnp.sum(p_n, axis=1, keepdims=True)
        o = (_dot(p_c.astype(BF16), cv_ref[old, :].astype(BF16))
             + _dot(p_n.astype(BF16), v_ref[new, :].astype(BF16))) * (1.0 / denom)
        d = o[0:n_new, :] - lam * o[n_new:2 * n_new, :]
        o_ref[:, sl] = _sub_norm(d, gain, lam_init).astype(BF16)


def _sample_attention(lams, gain, q, k, v, cache_k, cache_v, lam_init):
    bsz, past_rows = cache_k.shape[0], cache_k.shape[1]
    n_new = q.shape[0] // bsz
    small = lambda b: (0, 0)
    row = lambda b: (b, 0)
    return pl.pallas_call(
        functools.partial(_sample_attn_kernel, lam_init=lam_init),
        grid=(bsz,),
        in_specs=[pl.BlockSpec((1, HEAD_DIM), small)] * 4 + [
            pl.BlockSpec((1, V_DIM), small),
            pl.BlockSpec((n_new, ATTN_WIDTH), row),
            pl.BlockSpec((n_new * N_HEADS, V_DIM), row),
            pl.BlockSpec((n_new * N_HEADS, V_DIM), row),
            pl.BlockSpec((None, past_rows, V_DIM), lambda b: (b, 0, 0)),
            pl.BlockSpec((None, past_rows, V_DIM), lambda b: (b, 0, 0))],
        out_specs=pl.BlockSpec((n_new, ATTN_WIDTH), row),
        out_shape=jax.ShapeDtypeStruct(q.shape, BF16),
        compiler_params=_cparams("parallel"),
        name="sample_attn",
    )(*lams, gain, q, k, v, cache_k, cache_v)


def _ssm_tables(a_re, a_im, b_re, b_im, c_re, c_im, log_dt):
    dt = jnp.exp(log_dt)[:, None]
    lam_re, lam_im = a_re * dt, a_im * dt
    mag = jnp.exp(lam_re)
    ar, ai = mag * jnp.cos(lam_im), mag * jnp.sin(lam_im)
    den = jnp.square(a_re) + jnp.square(a_im)
    cr = ((ar - 1.0) * a_re + ai * a_im) / den
    ci = (ai * a_re - (ar - 1.0) * a_im) / den
    bbr = cr[..., None] * b_re - ci[..., None] * b_im
    bbi = cr[..., None] * b_im + ci[..., None] * b_re
    tau = jnp.arange(CHUNK + 1, dtype=F32)
    pm = jnp.exp(lam_re[..., None] * tau)
    pr, pi = pm * jnp.cos(lam_im[..., None] * tau), pm * jnp.sin(lam_im[..., None] * tau)
    hp = lax.Precision.HIGHEST
    zr = c_re[:, :, None, :] * bbr.transpose(0, 2, 1)[:, None] - c_im[:, :, None, :] * bbi.transpose(0, 2, 1)[:, None]
    zi = c_re[:, :, None, :] * bbi.transpose(0, 2, 1)[:, None] + c_im[:, :, None, :] * bbr.transpose(0, 2, 1)[:, None]
    kk = (jnp.einsum('gcdp,gpt->gctd', zr, pr[..., :CHUNK], precision=hp)
          - jnp.einsum('gcdp,gpt->gctd', zi, pi[..., :CHUNK], precision=hp))
    g = a_re.shape[0]
    width = GROUP_SIZE * CHUNK
    rrev = kk[:, :, ::-1, :].reshape(g, GROUP_SIZE, width)
    prr, pir = pr[..., CHUNK - 1::-1][..., :CHUNK], pi[..., CHUNK - 1::-1][..., :CHUNK]
    w_re = prr[..., None] * bbr[:, :, None, :] - pir[..., None] * bbi[:, :, None, :]
    w_im = prr[..., None] * bbi[:, :, None, :] + pir[..., None] * bbr[:, :, None, :]
    w = jnp.concatenate([w_re, w_im], axis=1).reshape(g, 2 * STATE_DIM, width)
    p1r, p1i = pr[..., 1:].transpose(0, 2, 1), pi[..., 1:].transpose(0, 2, 1)
    v_re = c_re[:, None] * p1r[:, :, None, :] - c_im[:, None] * p1i[:, :, None, :]
    v_im = c_re[:, None] * p1i[:, :, None, :] + c_im[:, None] * p1r[:, :, None, :]
    vm = jnp.concatenate([v_re, -v_im], axis=-1).reshape(g, width, 2 * STATE_DIM)
    acr, aci = pr[..., CHUNK], pi[..., CHUNK]
    a_c = jnp.broadcast_to(jnp.concatenate([acr, acr], axis=-1)[..., None], (g, 2 * STATE_DIM, LANES))
    a_s = jnp.broadcast_to(jnp.concatenate([-aci, aci], axis=-1)[..., None], (g, 2 * STATE_DIM, LANES))
    return rrev, w, vm, a_c, a_s


def _chunk_cols_kernel(u_ref, o_ref):
    n_s, n_r, n_j = u_ref.shape[0], u_ref.shape[1], u_ref.shape[2]
    pad = LANES - n_s * n_r
    for jj in range(n_j):
        rows = [u_ref[s, :, jj, :] for s in range(n_s)]
        if pad:
            rows.append(jnp.zeros((pad, SSM_WIDTH), F32))
        cols = jnp.concatenate(rows, axis=0).T
        o_ref[:, jj * GROUP_SIZE:(jj + 1) * GROUP_SIZE, :] = cols.reshape(N_GROUPS, GROUP_SIZE, LANES).astype(BF16)


def _chunk_cols(u4, frames_per_step=16):
    s_total, n_r = u4.shape[0], u4.shape[1]
    s_step = min(s_total, LANES // n_r)
    assert s_step >= 1 and s_total % s_step == 0
    n_lane_blocks = s_total // s_step
    return pl.pallas_call(
        _chunk_cols_kernel,
        grid=(n_lane_blocks, CHUNK // frames_per_step),
        in_specs=[pl.BlockSpec((s_step, n_r, frames_per_step, SSM_WIDTH), lambda a, j: (a, 0, j, 0))],
        out_specs=pl.BlockSpec((N_GROUPS, frames_per_step * GROUP_SIZE, LANES), lambda a, j: (0, j, a)),
        out_shape=jax.ShapeDtypeStruct((N_GROUPS, CHUNK * GROUP_SIZE, n_lane_blocks * LANES), BF16),
        compiler_params=_cparams("parallel", "parallel"),
        name="ssm_in",
    )(u4)


def _unchunk_cols_kernel(y_ref, o_ref):
    n_s, n_r, n_j = o_ref.shape[0], o_ref.shape[1], o_ref.shape[2]
    for jj in range(n_j):
        cols = y_ref[:, jj * GROUP_SIZE:(jj + 1) * GROUP_SIZE, :].reshape(SSM_WIDTH, LANES)
        rows = cols.T
        for s in range(n_s):
            o_ref[s, :, jj, :] = rows[s * n_r:(s + 1) * n_r, :]


def _unchunk_cols(y_cols, s_total, n_r, frames_per_step=16):
    s_step = min(s_total, LANES // n_r)
    n_lane_blocks = s_total // s_step
    return pl.pallas_call(
        _unchunk_cols_kernel,
        grid=(n_lane_blocks, CHUNK // frames_per_step),
        in_specs=[pl.BlockSpec((N_GROUPS, frames_per_step * GROUP_SIZE, LANES), lambda a, j: (0, j, a))],
        out_specs=pl.BlockSpec((s_step, n_r, frames_per_step, SSM_WIDTH), lambda a, j: (a, 0, j, 0)),
        out_shape=jax.ShapeDtypeStruct((s_total, n_r, CHUNK, SSM_WIDTH), F32),
        compiler_params=_cparams("parallel", "parallel"),
        name="ssm_out",
    )(y_cols)


def _ssm_kernel(rrev_ref, w_ref, vm_ref, ac_ref, as_ref, xp_ref, xs_ref, s0_ref, yp_ref, ys_ref, sp_ref, ss_ref,
                mt_scr, *, n_chunk):
    width = GROUP_SIZE * CHUNK
    n_piece = width // LANES
    lane16 = lax.broadcasted_iota(jnp.int32, (GROUP_SIZE, LANES), 1)
    pieces = [rrev_ref[:, k * LANES:(k + 1) * LANES] for k in range(n_piece)] + [jnp.zeros((GROUP_SIZE, LANES), F32)]
    rolled = {0: pieces}
    for b in range(GROUP_SIZE, LANES, GROUP_SIZE):
        rolled[b] = [pltpu.roll(p, LANES - b, 1) for p in pieces[:n_piece]] + [pieces[n_piece]]
    def toeplitz_rows(t0, t1, n_cols):
        for t in range(t0, t1):
            shift = GROUP_SIZE * (CHUNK - 1 - t)
            a, b = shift // LANES, shift % LANES
            for v in range(n_cols // LANES):
                k = v + a
                if k >= n_piece:
                    blk = pieces[n_piece]
                elif b == 0:
                    blk = pieces[k]
                else:
                    blk = jnp.where(lane16 < LANES - b, rolled[b][k], rolled[b][k + 1])
                mt_scr[t * GROUP_SIZE:(t + 1) * GROUP_SIZE, v * LANES:(v + 1) * LANES] = blk.astype(BF16)

    w = w_ref[...]
    xp, xs = xp_ref[...], xs_ref[...]
    sloc_p, sloc_s = _dot(w, xp), _dot(w, xs)
    a_c, a_s = ac_ref[...], as_ref[...]

    def cmul(pc, ps, s):
        return pc * s + ps * pltpu.roll(s, STATE_DIM, 0)

    lane = lax.broadcasted_iota(jnp.int32, (2 * STATE_DIM, LANES), 1) % n_chunk
    prev_cols = []
    for tile in range(xp.shape[1] // LANES):
        s_inc = sloc_p[:, tile * LANES:(tile + 1) * LANES]
        pc, ps = a_c, a_s
        dist = 1
        while dist < n_chunk:
            shifted = jnp.where(lane >= dist, pltpu.roll(s_inc, dist, 1), 0.0)
            s_inc = s_inc + cmul(pc, ps, shifted)
            pc, ps = pc * pc - ps * ps, 2.0 * pc * ps
            dist *= 2
        sp_ref[:, tile * LANES:(tile + 1) * LANES] = s_inc
        prev_cols.append(jnp.where(lane >= 1, pltpu.roll(s_inc, 1, 1), 0.0))
    s_prev_p = jnp.concatenate(prev_cols, axis=1).astype(BF16)
    s0 = s0_ref[...]
    s0b = s0.astype(BF16)
    frames = TOEPLITZ_ROWS // GROUP_SIZE
    for i in range(width // TOEPLITZ_ROWS):
        rows = slice(i * TOEPLITZ_ROWS, (i + 1) * TOEPLITZ_ROWS)
        n_cols = (i + 1) * TOEPLITZ_ROWS
        toeplitz_rows(i * frames, (i + 1) * frames, n_cols)
        mt = mt_scr[rows, 0:n_cols]
        vm = vm_ref[rows, :]
        yp_ref[rows, :] = _dot(mt, xp[0:n_cols, :]) + _dot(vm, s_prev_p)
        ys_ref[rows, :] = _dot(mt, xs[0:n_cols, :]) + _dot(vm, s0b)
    ss_ref[...] = cmul(a_c, a_s, s0) + sloc_s


def _ssm(tables, x_p, x_s, s0, n_chunk):
    rrev, w, vm, a_c, a_s = tables
    n_groups, lanes_p = x_p.shape[0], x_p.shape[2]
    width = GROUP_SIZE * CHUNK
    per_g = lambda g: (g, 0, 0)
    state_rows = 2 * STATE_DIM
    return pl.pallas_call(
        functools.partial(_ssm_kernel, n_chunk=n_chunk),
        grid=(n_groups,),
        in_specs=[pl.BlockSpec((None, GROUP_SIZE, width), per_g),
                  pl.BlockSpec((None, state_rows, width), per_g),
                  pl.BlockSpec((None, width, state_rows), per_g),
                  pl.BlockSpec((None, state_rows, LANES), per_g),
                  pl.BlockSpec((None, state_rows, LANES), per_g),
                  pl.BlockSpec((None, width, lanes_p), per_g),
                  pl.BlockSpec((None, width, LANES), per_g),
                  pl.BlockSpec((None, state_rows, LANES), per_g)],
        out_specs=[pl.BlockSpec((None, width, lanes_p), per_g),
                   pl.BlockSpec((None, width, LANES), per_g),
                   pl.BlockSpec((None, state_rows, lanes_p), per_g),
                   pl.BlockSpec((None, state_rows, LANES), per_g)],
        out_shape=[jax.ShapeDtypeStruct((n_groups, width, lanes_p), F32),
                   jax.ShapeDtypeStruct((n_groups, width, LANES), F32),
                   jax.ShapeDtypeStruct((n_groups, state_rows, lanes_p), F32),
                   jax.ShapeDtypeStruct((n_groups, state_rows, LANES), F32)],
        scratch_shapes=[pltpu.VMEM((width, width), BF16)],
        compiler_params=_cparams("parallel"),
        name="ssm",
    )(rrev, w, vm, a_c, a_s, x_p, x_s, s0)


def _layer_norm(x, g, b):
    mu = jnp.mean(x, axis=1, keepdims=True)
    xc = x - mu
    var = jnp.mean(jnp.square(xc), axis=1, keepdims=True)
    return xc * lax.rsqrt(var + LN_EPS) * g + b


def _gelu_tanh(x):
    return 0.5 * x * (1.0 + jnp.tanh(math.sqrt(2.0 / math.pi) * (x + 0.044715 * (x * x * x))))


def _merge_kernel(x_ref, ao_ref, ys_ref, u_ref, d_ref, wg_ref, wap_ref, wglu_ref, wout_ref, g1_ref, b1_ref, h_ref):
    x = x_ref[...]
    xb = x.astype(BF16)
    a_branch = _dot(ao_ref[...], wap_ref[...])
    s_act = _gelu_tanh(ys_ref[...] + d_ref[...] * u_ref[...]).astype(BF16)
    s_branch = _dot(s_act, wglu_ref[:, 0:D_MODEL]) * jax.nn.sigmoid(_dot(s_act, wglu_ref[:, D_MODEL:2 * D_MODEL]))
    m = (jax.nn.sigmoid(_dot(xb, wg_ref[:, 0:D_MODEL])) * a_branch
         + jax.nn.sigmoid(_dot(xb, wg_ref[:, D_MODEL:2 * D_MODEL])) * s_branch)
    h_ref[...] = _layer_norm(DEEPNORM_ALPHA * x + _dot(m.astype(BF16), wout_ref[...]), g1_ref[...], b1_ref[...])


def _merge(x2d, ao, ys, u, d, w_gate, w_ap, w_glu, w_out, ln_g, ln_b, tm):
    t_tokens = x2d.shape[0]
    row = lambda i: (i, 0)
    const = lambda i: (0, 0)
    return pl.pallas_call(
        _merge_kernel,
        grid=(t_tokens // tm,),
        in_specs=[pl.BlockSpec((tm, D_MODEL), row),
                  pl.BlockSpec((tm, ATTN_WIDTH), row),
                  pl.BlockSpec((tm, SSM_WIDTH), row),
                  pl.BlockSpec((tm, SSM_WIDTH), row),
                  pl.BlockSpec((1, SSM_WIDTH), const),
                  pl.BlockSpec((D_MODEL, 2 * D_MODEL), const),
                  pl.BlockSpec((ATTN_WIDTH, D_MODEL), const),
                  pl.BlockSpec((SSM_WIDTH, 2 * D_MODEL), const),
                  pl.BlockSpec((D_MODEL, D_MODEL), const),
                  pl.BlockSpec((1, D_MODEL), const),
                  pl.BlockSpec((1, D_MODEL), const)],
        out_specs=pl.BlockSpec((tm, D_MODEL), row),
        out_shape=jax.ShapeDtypeStruct((t_tokens, D_MODEL), F32),
        compiler_params=_cparams("parallel"),
        name="merge",
    )(x2d, ao, ys, u, d, w_gate, w_ap, w_glu, w_out, ln_g, ln_b)


def _mlp_kernel(h_ref, w1_ref, w2_ref, g2_ref, b2_ref, o_ref, *, ff_chunk):
    h = h_ref[...]
    hb = h.astype(BF16)
    f = jnp.zeros(h.shape, F32)
    for c in range(D_FF // ff_chunk):
        sl = slice(c * ff_chunk, (c + 1) * ff_chunk)
        t = jnp.maximum(_dot(hb, w1_ref[:, sl]), 0.0)
        f = f + _dot((t * t).astype(BF16), w2_ref[sl, :])
    o_ref[...] = _layer_norm(DEEPNORM_ALPHA * h + f, g2_ref[...], b2_ref[...])


def _mlp(h, w1, w2, ln_g, ln_b, tm, ff_chunk=1024):
    t_tokens = h.shape[0]
    row = lambda i: (i, 0)
    const = lambda i: (0, 0)
    return pl.pallas_call(
        functools.partial(_mlp_kernel, ff_chunk=ff_chunk),
        grid=(t_tokens // tm,),
        in_specs=[pl.BlockSpec((tm, D_MODEL), row),
                  pl.BlockSpec((D_MODEL, D_FF), const),
                  pl.BlockSpec((D_FF, D_MODEL), const),
                  pl.BlockSpec((1, D_MODEL), const),
                  pl.BlockSpec((1, D_MODEL), const)],
        out_specs=pl.BlockSpec((tm, D_MODEL), row),
        out_shape=jax.ShapeDtypeStruct((t_tokens, D_MODEL), F32),
        compiler_params=_cparams("parallel"),
        name="mlp",
    )(h, w1, w2, ln_g, ln_b)


def _rope_tables(pos):
    inv = 1.0 / (ROPE_THETA ** (jnp.arange(0, HEAD_DIM, 2, dtype=F32) / HEAD_DIM))
    ang = pos.astype(F32)[:, None] * inv[None, :]
    c, s = jnp.cos(ang), jnp.sin(ang)
    reps = LANES // HEAD_DIM
    return jnp.tile(jnp.concatenate([c, c], axis=1), (1, reps)), jnp.tile(jnp.concatenate([-s, s], axis=1), (1, reps))


def kernel(x_prompt, x_sample, cache_k, cache_v, state_ssm_re, state_ssm_im, w_in, lambda_q1, lambda_k1, lambda_q2, lambda_k2, subln_gain, ssm_a_re, ssm_a_im, ssm_b_re, ssm_b_im, ssm_c_re, ssm_c_im, ssm_d, ssm_log_dt, w_attn_proj, w_glu_a, w_glu_b, w_out, ln1_g, ln1_b, w_ff1, w_ff2, ln2_g, ln2_b):
    bp, n_p = x_prompt.shape[0], x_prompt.shape[1]
    bs, n_s = x_sample.shape[0], x_sample.shape[1]
    past = cache_k.shape[2]
    assert w_in.shape[0] == DEPTH and n_s == CHUNK and n_p % CHUNK == 0
    n_chunk = n_p // CHUNK
    tm = min(512, n_p)
    tm_s = min(512, bs * n_s)
    tq = min(512, n_p)
    l = 0
    lam_init = 0.8 - 0.6 * math.exp(-0.3 * l)

    xp = x_prompt.reshape(bp * n_p, D_MODEL)
    xs = x_sample.reshape(bs * n_s, D_MODEL)
    w_qkvu = w_in[l, :, 0:QKVU_COLS].astype(BF16)
    w_gate = w_in[l, :, QKVU_COLS:].astype(BF16)
    w_ap = w_attn_proj[l].astype(BF16)
    w_glu = jnp.concatenate([w_glu_a[l], w_glu_b[l]], axis=1).astype(BF16)
    w_o = w_out[l].astype(BF16)
    w1, w2 = w_ff1[l].astype(BF16), w_ff2[l].astype(BF16)
    lams = [v[l].reshape(1, HEAD_DIM) for v in (lambda_q1, lambda_k1, lambda_q2, lambda_k2)]
    gain = subln_gain[l].reshape(1, V_DIM)
    d_skip = ssm_d[l].reshape(1, SSM_WIDTH)
    lng = [v[l].reshape(1, D_MODEL) for v in (ln1_g, ln1_b, ln2_g, ln2_b)]

    cos_p, sin_p = _rope_tables(jnp.arange(n_p))
    cos_s, sin_s = _rope_tables(jnp.tile(past + jnp.arange(n_s), tm_s // n_s))

    q_p, k_p, v_p, u_p, kb_p, vt_p = _project(xp, w_qkvu, cos_p, sin_p, n_p, tm, True)
    q_s, k_s, v_s, u_s = _project(xs, w_qkvu, cos_s, sin_s, n_s, tm_s, False)

    ao_p = _prompt_attention(lams, gain, q_p, kb_p, vt_p, bp, n_p, tq, lam_init)
    ao_s = _sample_attention(lams, gain, q_s, k_s, v_s,
                             cache_k[l].reshape(bs, past * N_HEADS, V_DIM),
                             cache_v[l].reshape(bs, past * N_HEADS, V_DIM), lam_init)

    kv, w_st, vm, a_c, a_s = _ssm_tables(ssm_a_re[l], ssm_a_im[l], ssm_b_re[l], ssm_b_im[l],
                                         ssm_c_re[l], ssm_c_im[l], ssm_log_dt[l])
    tables = (kv, w_st.astype(BF16), vm.astype(BF16), a_c, a_s)
    x_cols_p = _chunk_cols(u_p.reshape(bp, n_chunk, CHUNK, SSM_WIDTH))
    x_cols_s = _chunk_cols(u_s.reshape(1, bs, CHUNK, SSM_WIDTH))
    s0 = jnp.concatenate([state_ssm_re[l], state_ssm_im[l]], axis=-1).transpose(1, 2, 0)
    s0 = jnp.pad(s0, ((0, 0), (0, 0), (0, LANES - bs)))
    y_cols_p, y_cols_s, st_p, st_s = _ssm(tables, x_cols_p, x_cols_s, s0, n_chunk)
    ys_p = _unchunk_cols(y_cols_p, bp, n_chunk).reshape(bp * n_p, SSM_WIDTH)
    ys_s = _unchunk_cols(y_cols_s, 1, bs).reshape(bs * n_s, SSM_WIDTH)
    sf_p = st_p[:, :, n_chunk - 1::n_chunk]
    sf_s = st_s[:, :, 0:bs]

    outs = []
    for x2d, ao, ys, u, tile in ((xp, ao_p, ys_p, u_p, tm), (xs, ao_s, ys_s, u_s, tm_s)):
        h = _merge(x2d, ao, ys, u, d_skip, w_gate, w_ap, w_glu, w_o, lng[0], lng[1], tile)
        outs.append(_mlp(h, w1, w2, lng[2], lng[3], tile))

    def states(sf):
        t = sf.transpose(2, 0, 1)
        return t[None, :, :, 0:STATE_DIM], t[None, :, :, STATE_DIM:]

    srp, sip = states(sf_p)
    srs, sis = states(sf_s)
    return (outs[0].reshape(bp, n_p, D_MODEL), outs[1].reshape(bs, n_s, D_MODEL),
            k_p.reshape(1, bp, n_p, N_HEADS, V_DIM), v_p.reshape(1, bp, n_p, N_HEADS, V_DIM), srp, sip,
            k_s.reshape(1, bs, n_s, N_HEADS, V_DIM), v_s.reshape(1, bs, n_s, N_HEADS, V_DIM), srs, sis)
```

```python
import functools
import math

import jax
import jax.numpy as jnp
from jax import lax
from jax.experimental import pallas as pl
from jax.experimental.pallas import tpu as pltpu

D_MODEL = 1024
CHUNK = 64
N_HEADS = 4
HEAD_DIM = 64
V_DIM = 2 * HEAD_DIM
ATTN_WIDTH = N_HEADS * V_DIM
SSM_WIDTH = 512
GROUP_SIZE = 16
N_GROUPS = SSM_WIDTH // GROUP_SIZE
STATE_DIM = 64
D_FF = 4 * D_MODEL
ROPE_THETA = 10000.0
LN_EPS = 1e-5
RMS_EPS = 1e-5
NEG_INF = -1e30
DEPTH = 1
DEEPNORM_ALPHA = (2.0 * DEPTH) ** 0.25
QKVU_COLS = 3 * ATTN_WIDTH + SSM_WIDTH
LOG2E = 1.4426950408889634

LANES = 128
VT_ROWS = V_DIM + 16
QUERY_LANES = 256
TOEPLITZ_ROWS = 256
VMEM_LIMIT = 56 * 1024 * 1024

F32 = jnp.float32
BF16 = jnp.bfloat16


def _cparams(*sem):
    return pltpu.CompilerParams(dimension_semantics=sem, vmem_limit_bytes=VMEM_LIMIT)


def _nt_dot(a, b):
    return lax.dot_general(a, b, (((1,), (1,)), ((), ())), preferred_element_type=F32)


def _dot(a, b):
    return jnp.dot(a, b, preferred_element_type=F32)


def _rotary(z, cos, sin_signed, first_half):
    swapped = jnp.where(first_half, pltpu.roll(z, 96, 1), pltpu.roll(z, 32, 1))
    return z * cos + swapped * sin_signed


def _proj_kernel(x_ref, w_ref, cos_ref, sin_ref, q_ref, k_ref, v_ref, u_ref, *rest, emit_t):
    xb = x_ref[...].astype(BF16)
    cos = cos_ref[...]
    sin = sin_ref[...]
    lane = lax.broadcasted_iota(jnp.int32, cos.shape, 1)
    first_half = (lane % HEAD_DIM) < (HEAD_DIM // 2)
    tm = xb.shape[0]
    zq = _dot(xb, w_ref[:, 0:ATTN_WIDTH])
    zk = _dot(xb, w_ref[:, ATTN_WIDTH:2 * ATTN_WIDTH])
    zv = _dot(xb, w_ref[:, 2 * ATTN_WIDTH:3 * ATTN_WIDTH])
    for h in range(N_HEADS):
        sl = slice(h * V_DIM, (h + 1) * V_DIM)
        q_ref[:, sl] = (_rotary(zq[:, sl], cos, sin, first_half) * (LOG2E * HEAD_DIM ** -0.5)).astype(BF16)
        kr = _rotary(zk[:, sl], cos, sin, first_half)
        k_ref[pl.ds(h, tm, stride=N_HEADS), :] = kr
        v_ref[pl.ds(h, tm, stride=N_HEADS), :] = zv[:, sl]
        if emit_t:
            rest[0][:, sl] = kr.astype(BF16)
    if emit_t:
        vt_ref = rest[1]
        zvt = zv.T.astype(BF16)
        ones = jnp.ones((VT_ROWS - V_DIM, zvt.shape[1]), BF16)
        for h in range(N_HEADS):
            vt_ref[h, 0:V_DIM, :] = zvt[h * V_DIM:(h + 1) * V_DIM, :]
            vt_ref[h, V_DIM:VT_ROWS, :] = ones
    u_ref[...] = _dot(xb, w_ref[:, 3 * ATTN_WIDTH:QKVU_COLS])


def _project(x2d, w_qkvu, cos_t, sin_t, seq_len, tm, emit_t):
    t_tokens = x2d.shape[0]
    n_tiles = t_tokens // tm
    n_pos_tiles = cos_t.shape[0] // tm
    tiles_per_seq = max(seq_len // tm, 1)
    row = lambda i: (i, 0)
    pos = lambda i: (i % n_pos_tiles, 0)
    out_shape = [jax.ShapeDtypeStruct((t_tokens, ATTN_WIDTH), BF16),
                 jax.ShapeDtypeStruct((t_tokens * N_HEADS, V_DIM), F32),
                 jax.ShapeDtypeStruct((t_tokens * N_HEADS, V_DIM), F32),
                 jax.ShapeDtypeStruct((t_tokens, SSM_WIDTH), F32)]
    out_specs = ([pl.BlockSpec((tm, ATTN_WIDTH), row)] + [pl.BlockSpec((tm * N_HEADS, V_DIM), row)] * 2
                 + [pl.BlockSpec((tm, SSM_WIDTH), row)])
    if emit_t:
        bsz = t_tokens // seq_len
        out_shape += [jax.ShapeDtypeStruct((t_tokens, ATTN_WIDTH), BF16),
                      jax.ShapeDtypeStruct((bsz, N_HEADS, VT_ROWS, seq_len), BF16)]
        out_specs += [pl.BlockSpec((tm, ATTN_WIDTH), row),
                      pl.BlockSpec((None, N_HEADS, VT_ROWS, tm),
                                   lambda i: (i // tiles_per_seq, 0, 0, i % tiles_per_seq))]
    return pl.pallas_call(
        functools.partial(_proj_kernel, emit_t=emit_t),
        grid=(n_tiles,),
        in_specs=[pl.BlockSpec((tm, D_MODEL), row),
                  pl.BlockSpec((D_MODEL, QKVU_COLS), lambda i: (0, 0)),
                  pl.BlockSpec((tm, LANES), pos),
                  pl.BlockSpec((tm, LANES), pos)],
        out_specs=out_specs,
        out_shape=out_shape,
        compiler_params=_cparams("parallel"),
        name="proj_t" if emit_t else "proj",
    )(x2d, w_qkvu, cos_t, sin_t)


def _diff_lambda(lq1, lk1, lq2, lk2, lam_init):
    return (jnp.exp(jnp.sum(lq1 * lk1, axis=1, keepdims=True))
            - jnp.exp(jnp.sum(lq2 * lk2, axis=1, keepdims=True)) + lam_init)


def _sub_norm(d, gain, lam_init):
    ms = jnp.mean(jnp.square(d), axis=1, keepdims=True)
    return d * lax.rsqrt(ms + RMS_EPS) * gain * (1.0 - lam_init)


def _stack_maps(q):
    lane = lax.broadcasted_iota(jnp.int32, q.shape, 1)
    zero = jnp.zeros_like(q)
    return jnp.concatenate([jnp.where(lane < HEAD_DIM, q, zero), jnp.where(lane >= HEAD_DIM, q, zero)], axis=0)


def _prompt_attn_kernel(item_q_ref, item_k_ref, lq1_ref, lk1_ref, lq2_ref, lk2_ref, gain_ref, q_ref, k_ref, vt_ref,
                        o_ref, m_scr, acc_scr, qx_scr, s0_scr, s1_scr, *, tq, n_items, lam_init):
    nq = q_ref.shape[0] // tq
    for i in range(nq):
        qx_scr[i * 2 * tq:(i + 1) * 2 * tq, :] = _stack_maps(q_ref[i * tq:(i + 1) * tq, :])
    acc_scr[...] = jnp.zeros(acc_scr.shape, F32)
    m_scr[...] = jnp.full(m_scr.shape, NEG_INF, F32)
    chains = [slice(c * QUERY_LANES, (c + 1) * QUERY_LANES) for c in range(2 * tq // QUERY_LANES)]

    def scores(w, s_scr):
        qi, j = item_q_ref[w], item_k_ref[w]
        kt = k_ref[pl.ds(pl.multiple_of(j * tq, tq), tq), :]
        for cs in chains:
            qx = qx_scr[pl.ds(pl.multiple_of(qi * 2 * tq + cs.start, QUERY_LANES), QUERY_LANES), :]
            s_scr[:, cs] = _nt_dot(kt, qx)

    def softmax_pv(j, s_scr, diagonal):
        vt = vt_ref[:, pl.ds(pl.multiple_of(j * tq, tq), tq)]
        for cs in chains:
            st = s_scr[:, cs]
            if diagonal:
                key_chunk = lax.broadcasted_iota(jnp.int32, st.shape, 0) // CHUNK
                qry_chunk = (cs.start % tq + lax.broadcasted_iota(jnp.int32, st.shape, 1)) // CHUNK
                st = jnp.where(key_chunk <= qry_chunk, st, NEG_INF)
            m_old = jnp.where(j == 0, NEG_INF, m_scr[:, cs])
            m_new = jnp.maximum(m_old, jnp.max(st, axis=0, keepdims=True))
            alpha = jnp.exp2(m_old - m_new)
            p = jnp.exp2(st - m_new).astype(BF16)
            acc_scr[:, cs] = acc_scr[:, cs] * alpha + _dot(vt, p)
            m_scr[:, cs] = m_new

    def finish(qi):
        acc = acc_scr[...]
        o = acc[0:V_DIM, :] * (1.0 / acc[V_DIM:V_DIM + 1, :])
        lam = _diff_lambda(lq1_ref[...], lk1_ref[...], lq2_ref[...], lk2_ref[...], lam_init)
        d = (o[:, 0:tq] - lam * o[:, tq:2 * tq]).T
        o_ref[pl.ds(pl.multiple_of(qi * tq, tq), tq), :] = _sub_norm(d, gain_ref[...], lam_init).astype(BF16)

    def pair_block(w, diag0, diag1):
        scores(w + 1, s1_scr)
        softmax_pv(item_k_ref[w], s0_scr, diag0)
        if diag0:
            finish(item_q_ref[w])
        scores(w + 2, s0_scr)
        softmax_pv(item_k_ref[w + 1], s1_scr, diag1)
        if diag1:
            finish(item_q_ref[w + 1])

    scores(0, s0_scr)

    def pair(i, carry):
        w = 2 * i
        d0 = item_q_ref[w] == item_k_ref[w]
        d1 = item_q_ref[w + 1] == item_k_ref[w + 1]
        for diag0 in (False, True):
            for diag1 in (False, True):
                pl.when(jnp.logical_and(d0 == diag0, d1 == diag1))(
                    functools.partial(pair_block, w, diag0, diag1))
        return carry

    lax.fori_loop(0, n_items // 2, pair, 0)


def _prompt_attention(lams, gain, q, kb, vt, bsz, seq_len, tq, lam_init):
    nq = seq_len // tq
    items = [(qi, j) for qi in range(nq) for j in range(qi + 1)]
    n_items = len(items)
    assert n_items % 2 == 0
    items.append(items[-1])
    items.append(items[-1])
    item_q = jnp.asarray([it[0] for it in items], jnp.int32)
    item_k = jnp.asarray([it[1] for it in items], jnp.int32)
    small = lambda b, h, iq, ik: (0, 0)
    grid_spec = pltpu.PrefetchScalarGridSpec(
        num_scalar_prefetch=2,
        grid=(bsz, N_HEADS),
        in_specs=[pl.BlockSpec((1, HEAD_DIM), small)] * 4 + [
            pl.BlockSpec((1, V_DIM), small),
            pl.BlockSpec((seq_len, V_DIM), lambda b, h, iq, ik: (b, h)),
            pl.BlockSpec((seq_len, V_DIM), lambda b, h, iq, ik: (b, h)),
            pl.BlockSpec((None, None, VT_ROWS, seq_len), lambda b, h, iq, ik: (b, h, 0, 0))],
        out_specs=pl.BlockSpec((seq_len, V_DIM), lambda b, h, iq, ik: (b, h)),
        scratch_shapes=[pltpu.VMEM((1, 2 * tq), F32), pltpu.VMEM((VT_ROWS, 2 * tq), F32),
                        pltpu.VMEM((nq * 2 * tq, V_DIM), BF16),
                        pltpu.VMEM((tq, 2 * tq), F32), pltpu.VMEM((tq, 2 * tq), F32)])
    return pl.pallas_call(
        functools.partial(_prompt_attn_kernel, tq=tq, n_items=n_items, lam_init=lam_init),
        grid_spec=grid_spec,
        out_shape=jax.ShapeDtypeStruct((bsz * seq_len, ATTN_WIDTH), BF16),
        compiler_params=_cparams("parallel", "parallel"),
        name="prompt_attn",
    )(item_q, item_k, *lams, gain, q, kb, vt)


def _sample_attn_kernel(lq1_ref, lk1_ref, lq2_ref, lk2_ref, gain_ref, q_ref, k_ref, v_ref, ck_ref, cv_ref,
                        o_ref, *, lam_init):
    n_new = q_ref.shape[0]
    past = ck_ref.shape[0] // N_HEADS
    lam = _diff_lambda(lq1_ref[...], lk1_ref[...], lq2_ref[...], lk2_ref[...], lam_init)
    gain = gain_ref[...]
    for h in range(N_HEADS):
        sl = slice(h * V_DIM, (h + 1) * V_DIM)
        old = pl.ds(h, past, stride=N_HEADS)
        new = pl.ds(h, n_new, stride=N_HEADS)
        qx = _stack_maps(q_ref[:, sl])
        s_c = _nt_dot(qx, ck_ref[old, :].astype(BF16))
        s_n = _nt_dot(qx, k_ref[new, :].astype(BF16))
        m = jnp.maximum(jnp.max(s_c, axis=1, keepdims=True), jnp.max(s_n, axis=1, keepdims=True))
        p_c = jnp.exp2(s_c - m)
        p_n = jnp.exp2(s_n - m)
        denom = jnp.sum(p_c, axis=1, keepdims=True) + jnp.sum(p_n, axis=1, keepdims=True)
        o = (_dot(p_c.astype(BF16), cv_ref[old, :].astype(BF16))
             + _dot(p_n.astype(BF16), v_ref[new, :].astype(BF16))) * (1.0 / denom)
        d = o[0:n_new, :] - lam * o[n_new:2 * n_new, :]
        o_ref[:, sl] = _sub_norm(d, gain, lam_init).astype(BF16)


def _sample_attention(lams, gain, q, k, v, cache_k, cache_v, lam_init):
    bsz, past_rows = cache_k.shape[0], cache_k.shape[1]
    n_new = q.shape[0] // bsz
    small = lambda b: (0, 0)
    row = lambda b: (b, 0)
    return pl.pallas_call(
        functools.partial(_sample_attn_kernel, lam_init=lam_init),
        grid=(bsz,),
        in_specs=[pl.BlockSpec((1, HEAD_DIM), small)] * 4 + [
            pl.BlockSpec((1, V_DIM), small),
            pl.BlockSpec((n_new, ATTN_WIDTH), row),
            pl.BlockSpec((n_new * N_HEADS, V_DIM), row),
            pl.BlockSpec((n_new * N_HEADS, V_DIM), row),
            pl.BlockSpec((None, past_rows, V_DIM), lambda b: (b, 0, 0)),
            pl.BlockSpec((None, past_rows, V_DIM), lambda b: (b, 0, 0))],
        out_specs=pl.BlockSpec((n_new, ATTN_WIDTH), row),
        out_shape=jax.ShapeDtypeStruct(q.shape, BF16),
        compiler_params=_cparams("parallel"),
        name="sample_attn",
    )(*lams, gain, q, k, v, cache_k, cache_v)


def _ssm_tables(a_re, a_im, b_re, b_im, c_re, c_im, log_dt):
    g = a_re.shape[0]
    width = GROUP_SIZE * CHUNK
    state_rows = 2 * STATE_DIM
    twice = lambda v: jnp.concatenate([v, v], axis=-1)
    rows = jnp.stack([twice(a_re), twice(a_im), jnp.broadcast_to(log_dt[:, None], (g, state_rows))], axis=1)
    rows = jnp.pad(rows, ((0, 0), (0, 8 - rows.shape[1]), (0, 0)))
    per_g = lambda i: (i, 0, 0)
    small = pl.BlockSpec((None, GROUP_SIZE, state_rows), per_g)
    return pl.pallas_call(
        _ssm_tables_kernel,
        grid=(g,),
        in_specs=[pl.BlockSpec((None, 8, state_rows), per_g), small, small, small, small],
        out_specs=[pl.BlockSpec((None, GROUP_SIZE, width), per_g),
                   pl.BlockSpec((None, state_rows, width), per_g),
                   pl.BlockSpec((None, width, state_rows), per_g),
                   pl.BlockSpec((None, state_rows, LANES), per_g),
                   pl.BlockSpec((None, state_rows, LANES), per_g)],
        out_shape=[jax.ShapeDtypeStruct((g, GROUP_SIZE, width), F32),
                   jax.ShapeDtypeStruct((g, state_rows, width), BF16),
                   jax.ShapeDtypeStruct((g, width, state_rows), BF16),
                   jax.ShapeDtypeStruct((g, state_rows, LANES), F32),
                   jax.ShapeDtypeStruct((g, state_rows, LANES), F32)],
        compiler_params=_cparams("parallel"),
        name="ssm_tables",
    )(rows, twice(b_re.transpose(0, 2, 1)), twice(b_im.transpose(0, 2, 1)), twice(c_re), twice(c_im))


def _ssm_tables_kernel(rows_ref, bre_ref, bim_ref, cre_ref, cim_ref, rrev_ref, w_ref, vm_ref, ac_ref, as_ref):
    lane = lax.broadcasted_iota(jnp.int32, (1, 2 * STATE_DIM), 1)
    lo = lane < STATE_DIM
    a_re, a_im, log_dt = rows_ref[0:1, :], rows_ref[1:2, :], rows_ref[2:3, :]
    dt = jnp.exp(log_dt)
    lam_re, lam_im = a_re * dt, a_im * dt
    mag = jnp.exp(lam_re)
    ar, ai = mag * jnp.cos(lam_im), mag * jnp.sin(lam_im)
    den = jnp.square(a_re) + jnp.square(a_im)
    cr = ((ar - 1.0) * a_re + ai * a_im) / den
    ci = (ai * a_re - (ar - 1.0) * a_im) / den
    bre, bim = bre_ref[...], bim_ref[...]
    bbr = cr * bre - ci * bim
    bbi = cr * bim + ci * bre
    quarter_turn = jnp.where(lo, 0.0, 0.5 * math.pi)

    def powers(tau):
        return jnp.exp(tau * lam_re) * jnp.cos(tau * lam_im - quarter_turn)

    def outer(y, x1, x2):
        y_sw = pltpu.roll(y, STATE_DIM, 1)
        prod = y[:, None, :] * x1[None, :, :] + y_sw[:, None, :] * x2[None, :, :]
        return prod.reshape(y.shape[0] * x1.shape[0], 2 * STATE_DIM)

    frames = lax.broadcasted_iota(jnp.int32, (CHUNK, 1), 0).astype(F32)
    wt = outer(powers((CHUNK - 1.0) - frames), bbr, jnp.where(lo, -bbi, bbi))
    w = wt.T
    w_ref[...] = w.astype(BF16)
    cre, cim = cre_ref[...], cim_ref[...]
    vm_ref[...] = outer(powers(frames + 1.0), jnp.where(lo, cre, -cre), -cim).astype(BF16)
    rrev_ref[...] = jnp.dot(jnp.where(lo, cre, -cim), w, precision=lax.Precision.HIGHEST,
                            preferred_element_type=F32)
    a_col = powers(jnp.full((2 * STATE_DIM, 1), float(CHUNK), F32)).T
    a_swap = pltpu.roll(a_col, STATE_DIM, 0)
    top = lax.broadcasted_iota(jnp.int32, a_col.shape, 0) < STATE_DIM
    ac_ref[...] = jnp.where(top, a_col, a_swap)
    as_ref[...] = jnp.where(top, -a_swap, a_col)


def _chunk_cols_kernel(u_ref, o_ref):
    n_s, n_r, n_j = u_ref.shape[0], u_ref.shape[1], u_ref.shape[2]
    pad = LANES - n_s * n_r
    for jj in range(n_j):
        rows = [u_ref[s, :, jj, :] for s in range(n_s)]
        if pad:
            rows.append(jnp.zeros((pad, SSM_WIDTH), F32))
        cols = jnp.concatenate(rows, axis=0).T
        o_ref[:, jj * GROUP_SIZE:(jj + 1) * GROUP_SIZE, :] = cols.reshape(N_GROUPS, GROUP_SIZE, LANES).astype(BF16)


def _chunk_cols(u4, frames_per_step=16):
    s_total, n_r = u4.shape[0], u4.shape[1]
    s_step = min(s_total, LANES // n_r)
    assert s_step >= 1 and s_total % s_step == 0
    n_lane_blocks = s_total // s_step
    return pl.pallas_call(
        _chunk_cols_kernel,
        grid=(n_lane_blocks, CHUNK // frames_per_step),
        in_specs=[pl.BlockSpec((s_step, n_r, frames_per_step, SSM_WIDTH), lambda a, j: (a, 0, j, 0))],
        out_specs=pl.BlockSpec((N_GROUPS, frames_per_step * GROUP_SIZE, LANES), lambda a, j: (0, j, a)),
        out_shape=jax.ShapeDtypeStruct((N_GROUPS, CHUNK * GROUP_SIZE, n_lane_blocks * LANES), BF16),
        compiler_params=_cparams("parallel", "parallel"),
        name="ssm_in",
    )(u4)


def _unchunk_cols_kernel(y_ref, o_ref):
    n_s, n_r, n_j = o_ref.shape[0], o_ref.shape[1], o_ref.shape[2]
    for jj in range(n_j):
        cols = y_ref[:, jj * GROUP_SIZE:(jj + 1) * GROUP_SIZE, :].reshape(SSM_WIDTH, LANES)
        rows = cols.T
        for s in range(n_s):
            o_ref[s, :, jj, :] = rows[s * n_r:(s + 1) * n_r, :]


def _unchunk_cols(y_cols, s_total, n_r, frames_per_step=16):
    s_step = min(s_total, LANES // n_r)
    n_lane_blocks = s_total // s_step
    return pl.pallas_call(
        _unchunk_cols_kernel,
        grid=(n_lane_blocks, CHUNK // frames_per_step),
        in_specs=[pl.BlockSpec((N_GROUPS, frames_per_step * GROUP_SIZE, LANES), lambda a, j: (0, j, a))],
        out_specs=pl.BlockSpec((s_step, n_r, frames_per_step, SSM_WIDTH), lambda a, j: (a, 0, j, 0)),
        out_shape=jax.ShapeDtypeStruct((s_total, n_r, CHUNK, SSM_WIDTH), F32),
        compiler_params=_cparams("parallel", "parallel"),
        name="ssm_out",
    )(y_cols)


def _ssm_kernel(rrev_ref, w_ref, vm_ref, ac_ref, as_ref, xp_ref, xs_ref, s0_ref, yp_ref, ys_ref, sp_ref, ss_ref,
                mt_scr, *, n_chunk):
    width = GROUP_SIZE * CHUNK
    n_piece = width // LANES
    lane16 = lax.broadcasted_iota(jnp.int32, (GROUP_SIZE, LANES), 1)
    pieces = [rrev_ref[:, k * LANES:(k + 1) * LANES] for k in range(n_piece)] + [jnp.zeros((GROUP_SIZE, LANES), F32)]
    rolled = {0: pieces}
    for b in range(GROUP_SIZE, LANES, GROUP_SIZE):
        rolled[b] = [pltpu.roll(p, LANES - b, 1) for p in pieces[:n_piece]] + [pieces[n_piece]]
    def toeplitz_rows(t0, t1, n_cols):
        for t in range(t0, t1):
            shift = GROUP_SIZE * (CHUNK - 1 - t)
            a, b = shift // LANES, shift % LANES
            for v in range(n_cols // LANES):
                k = v + a
                if k >= n_piece:
                    blk = pieces[n_piece]
                elif b == 0:
                    blk = pieces[k]
                else:
                    blk = jnp.where(lane16 < LANES - b, rolled[b][k], rolled[b][k + 1])
                mt_scr[t * GROUP_SIZE:(t + 1) * GROUP_SIZE, v * LANES:(v + 1) * LANES] = blk.astype(BF16)

    w = w_ref[...]
    xp, xs = xp_ref[...], xs_ref[...]
    sloc_p, sloc_s = _dot(w, xp), _dot(w, xs)
    a_c, a_s = ac_ref[...], as_ref[...]

    def cmul(pc, ps, s):
        return pc * s + ps * pltpu.roll(s, STATE_DIM, 0)

    lane = lax.broadcasted_iota(jnp.int32, (2 * STATE_DIM, LANES), 1) % n_chunk
    prev_cols = []
    for tile in range(xp.shape[1] // LANES):
        s_inc = sloc_p[:, tile * LANES:(tile + 1) * LANES]
        pc, ps = a_c, a_s
        dist = 1
        while dist < n_chunk:
            shifted = jnp.where(lane >= dist, pltpu.roll(s_inc, dist, 1), 0.0)
            s_inc = s_inc + cmul(pc, ps, shifted)
            pc, ps = pc * pc - ps * ps, 2.0 * pc * ps
            dist *= 2
        sp_ref[:, tile * LANES:(tile + 1) * LANES] = s_inc
        prev_cols.append(jnp.where(lane >= 1, pltpu.roll(s_inc, 1, 1), 0.0))
    s_prev_p = jnp.concatenate(prev_cols, axis=1).astype(BF16)
    s0 = s0_ref[...]
    s0b = s0.astype(BF16)
    frames = TOEPLITZ_ROWS // GROUP_SIZE
    for i in range(width // TOEPLITZ_ROWS):
        rows = slice(i * TOEPLITZ_ROWS, (i + 1) * TOEPLITZ_ROWS)
        n_cols = (i + 1) * TOEPLITZ_ROWS
        toeplitz_rows(i * frames, (i + 1) * frames, n_cols)
        mt = mt_scr[rows, 0:n_cols]
        vm = vm_ref[rows, :]
        yp_ref[rows, :] = _dot(mt, xp[0:n_cols, :]) + _dot(vm, s_prev_p)
        ys_ref[rows, :] = _dot(mt, xs[0:n_cols, :]) + _dot(vm, s0b)
    ss_ref[...] = cmul(a_c, a_s, s0) + sloc_s


def _ssm(tables, x_p, x_s, s0, n_chunk):
    rrev, w, vm, a_c, a_s = tables
    n_groups, lanes_p = x_p.shape[0], x_p.shape[2]
    width = GROUP_SIZE * CHUNK
    per_g = lambda g: (g, 0, 0)
    state_rows = 2 * STATE_DIM
    return pl.pallas_call(
        functools.partial(_ssm_kernel, n_chunk=n_chunk),
        grid=(n_groups,),
        in_specs=[pl.BlockSpec((None, GROUP_SIZE, width), per_g),
                  pl.BlockSpec((None, state_rows, width), per_g),
                  pl.BlockSpec((None, width, state_rows), per_g),
                  pl.BlockSpec((None, state_rows, LANES), per_g),
                  pl.BlockSpec((None, state_rows, LANES), per_g),
                  pl.BlockSpec((None, width, lanes_p), per_g),
                  pl.BlockSpec((None, width, LANES), per_g),
                  pl.BlockSpec((None, state_rows, LANES), per_g)],
        out_specs=[pl.BlockSpec((None, width, lanes_p), per_g),
                   pl.BlockSpec((None, width, LANES), per_g),
                   pl.BlockSpec((None, state_rows, lanes_p), per_g),
                   pl.BlockSpec((None, state_rows, LANES), per_g)],
        out_shape=[jax.ShapeDtypeStruct((n_groups, width, lanes_p), F32),
                   jax.ShapeDtypeStruct((n_groups, width, LANES), F32),
                   jax.ShapeDtypeStruct((n_groups, state_rows, lanes_p), F32),
                   jax.ShapeDtypeStruct((n_groups, state_rows, LANES), F32)],
        scratch_shapes=[pltpu.VMEM((width, width), BF16)],
        compiler_params=_cparams("parallel"),
        name="ssm",
    )(rrev, w, vm, a_c, a_s, x_p, x_s, s0)


def _layer_norm(x, g, b):
    mu = jnp.mean(x, axis=1, keepdims=True)
    xc = x - mu
    var = jnp.mean(jnp.square(xc), axis=1, keepdims=True)
    return xc * lax.rsqrt(var + LN_EPS) * g + b


def _gelu_tanh(x):
    return 0.5 * x * (1.0 + jnp.tanh(math.sqrt(2.0 / math.pi) * (x + 0.044715 * (x * x * x))))


def _merge_kernel(x_ref, ao_ref, ys_ref, u_ref, d_ref, wg_ref, wap_ref, wglu_ref, wout_ref, g1_ref, b1_ref, h_ref):
    x = x_ref[...]
    xb = x.astype(BF16)
    a_branch = _dot(ao_ref[...], wap_ref[...])
    s_act = _gelu_tanh(ys_ref[...] + d_ref[...] * u_ref[...]).astype(BF16)
    s_branch = _dot(s_act, wglu_ref[:, 0:D_MODEL]) * jax.nn.sigmoid(_dot(s_act, wglu_ref[:, D_MODEL:2 * D_MODEL]))
    m = (jax.nn.sigmoid(_dot(xb, wg_ref[:, 0:D_MODEL])) * a_branch
         + jax.nn.sigmoid(_dot(xb, wg_ref[:, D_MODEL:2 * D_MODEL])) * s_branch)
    h_ref[...] = _layer_norm(DEEPNORM_ALPHA * x + _dot(m.astype(BF16), wout_ref[...]), g1_ref[...], b1_ref[...])


def _merge(x2d, ao, ys, u, d, w_gate, w_ap, w_glu, w_out, ln_g, ln_b, tm):
    t_tokens = x2d.shape[0]
    row = lambda i: (i, 0)
    const = lambda i: (0, 0)
    return pl.pallas_call(
        _merge_kernel,
        grid=(t_tokens // tm,),
        in_specs=[pl.BlockSpec((tm, D_MODEL), row),
                  pl.BlockSpec((tm, ATTN_WIDTH), row),
                  pl.BlockSpec((tm, SSM_WIDTH), row),
                  pl.BlockSpec((tm, SSM_WIDTH), row),
                  pl.BlockSpec((1, SSM_WIDTH), const),
                  pl.BlockSpec((D_MODEL, 2 * D_MODEL), const),
                  pl.BlockSpec((ATTN_WIDTH, D_MODEL), const),
                  pl.BlockSpec((SSM_WIDTH, 2 * D_MODEL), const),
                  pl.BlockSpec((D_MODEL, D_MODEL), const),
                  pl.BlockSpec((1, D_MODEL), const),
                  pl.BlockSpec((1, D_MODEL), const)],
        out_specs=pl.BlockSpec((tm, D_MODEL), row),
        out_shape=jax.ShapeDtypeStruct((t_tokens, D_MODEL), F32),
        compiler_params=_cparams("parallel"),
        name="merge",
    )(x2d, ao, ys, u, d, w_gate, w_ap, w_glu, w_out, ln_g, ln_b)


def _mlp_kernel(h_ref, w1_ref, w2_ref, g2_ref, b2_ref, o_ref, *, ff_chunk):
    h = h_ref[...]
    hb = h.astype(BF16)
    f = jnp.zeros(h.shape, F32)
    for c in range(D_FF // ff_chunk):
        sl = slice(c * ff_chunk, (c + 1) * ff_chunk)
        t = jnp.maximum(_dot(hb, w1_ref[:, sl]), 0.0)
        f = f + _dot((t * t).astype(BF16), w2_ref[sl, :])
    o_ref[...] = _layer_norm(DEEPNORM_ALPHA * h + f, g2_ref[...], b2_ref[...])


def _mlp(h, w1, w2, ln_g, ln_b, tm, ff_chunk=1024):
    t_tokens = h.shape[0]
    row = lambda i: (i, 0)
    const = lambda i: (0, 0)
    return pl.pallas_call(
        functools.partial(_mlp_kernel, ff_chunk=ff_chunk),
        grid=(t_tokens // tm,),
        in_specs=[pl.BlockSpec((tm, D_MODEL), row),
                  pl.BlockSpec((D_MODEL, D_FF), const),
                  pl.BlockSpec((D_FF, D_MODEL), const),
                  pl.BlockSpec((1, D_MODEL), const),
                  pl.BlockSpec((1, D_MODEL), const)],
        out_specs=pl.BlockSpec((tm, D_MODEL), row),
        out_shape=jax.ShapeDtypeStruct((t_tokens, D_MODEL), F32),
        compiler_params=_cparams("parallel"),
        name="mlp",
    )(h, w1, w2, ln_g, ln_b)


def _rope_tables(pos):
    inv = 1.0 / (ROPE_THETA ** (jnp.arange(0, HEAD_DIM, 2, dtype=F32) / HEAD_DIM))
    ang = pos.astype(F32)[:, None] * inv[None, :]
    c, s = jnp.cos(ang), jnp.sin(ang)
    reps = LANES // HEAD_DIM
    return jnp.tile(jnp.concatenate([c, c], axis=1), (1, reps)), jnp.tile(jnp.concatenate([-s, s], axis=1), (1, reps))


def kernel(x_prompt, x_sample, cache_k, cache_v, state_ssm_re, state_ssm_im, w_in, lambda_q1, lambda_k1, lambda_q2, lambda_k2, subln_gain, ssm_a_re, ssm_a_im, ssm_b_re, ssm_b_im, ssm_c_re, ssm_c_im, ssm_d, ssm_log_dt, w_attn_proj, w_glu_a, w_glu_b, w_out, ln1_g, ln1_b, w_ff1, w_ff2, ln2_g, ln2_b):
    bp, n_p = x_prompt.shape[0], x_prompt.shape[1]
    bs, n_s = x_sample.shape[0], x_sample.shape[1]
    past = cache_k.shape[2]
    assert w_in.shape[0] == DEPTH and n_s == CHUNK and n_p % CHUNK == 0
    n_chunk = n_p // CHUNK
    tm = min(512, n_p)
    tm_s = min(512, bs * n_s)
    tq = min(512, n_p)
    l = 0
    lam_init = 0.8 - 0.6 * math.exp(-0.3 * l)

    xp = x_prompt.reshape(bp * n_p, D_MODEL)
    xs = x_sample.reshape(bs * n_s, D_MODEL)
    w_qkvu = w_in[l, :, 0:QKVU_COLS].astype(BF16)
    w_gate = w_in[l, :, QKVU_COLS:].astype(BF16)
    w_ap = w_attn_proj[l].astype(BF16)
    w_glu = jnp.concatenate([w_glu_a[l], w_glu_b[l]], axis=1).astype(BF16)
    w_o = w_out[l].astype(BF16)
    w1, w2 = w_ff1[l].astype(BF16), w_ff2[l].astype(BF16)
    lams = [v[l].reshape(1, HEAD_DIM) for v in (lambda_q1, lambda_k1, lambda_q2, lambda_k2)]
    gain = subln_gain[l].reshape(1, V_DIM)
    d_skip = ssm_d[l].reshape(1, SSM_WIDTH)
    lng = [v[l].reshape(1, D_MODEL) for v in (ln1_g, ln1_b, ln2_g, ln2_b)]

    cos_p, sin_p = _rope_tables(jnp.arange(n_p))
    cos_s, sin_s = _rope_tables(jnp.tile(past + jnp.arange(n_s), tm_s // n_s))

    q_p, k_p, v_p, u_p, kb_p, vt_p = _project(xp, w_qkvu, cos_p, sin_p, n_p, tm, True)
    q_s, k_s, v_s, u_s = _project(xs, w_qkvu, cos_s, sin_s, n_s, tm_s, False)

    ao_p = _prompt_attention(lams, gain, q_p, kb_p, vt_p, bp, n_p, tq, lam_init)
    ao_s = _sample_attention(lams, gain, q_s, k_s, v_s,
                             cache_k[l].reshape(bs, past * N_HEADS, V_DIM),
                             cache_v[l].reshape(bs, past * N_HEADS, V_DIM), lam_init)

    tables = _ssm_tables(ssm_a_re[l], ssm_a_im[l], ssm_b_re[l], ssm_b_im[l],
                         ssm_c_re[l], ssm_c_im[l], ssm_log_dt[l])
    x_cols_p = _chunk_cols(u_p.reshape(bp, n_chunk, CHUNK, SSM_WIDTH))
    x_cols_s = _chunk_cols(u_s.reshape(1, bs, CHUNK, SSM_WIDTH))
    s0 = jnp.concatenate([state_ssm_re[l], state_ssm_im[l]], axis=-1).transpose(1, 2, 0)
    s0 = jnp.pad(s0, ((0, 0), (0, 0), (0, LANES - bs)))
    y_cols_p, y_cols_s, st_p, st_s = _ssm(tables, x_cols_p, x_cols_s, s0, n_chunk)
    ys_p = _unchunk_cols(y_cols_p, bp, n_chunk).reshape(bp * n_p, SSM_WIDTH)
    ys_s = _unchunk_cols(y_cols_s, 1, bs).reshape(bs * n_s, SSM_WIDTH)
    sf_p = st_p[:, :, n_chunk - 1::n_chunk]
    sf_s = st_s[:, :, 0:bs]

    outs = []
    for x2d, ao, ys, u, tile in ((xp, ao_p, ys_p, u_p, tm), (xs, ao_s, ys_s, u_s, tm_s)):
        h = _merge(x2d, ao, ys, u, d_skip, w_gate, w_ap, w_glu, w_o, lng[0], lng[1], tile)
        outs.append(_mlp(h, w1, w2, lng[2], lng[3], tile))

    def states(sf):
        t = sf.transpose(2, 0, 1)
        return t[None, :, :, 0:STATE_DIM], t[None, :, :, STATE_DIM:]

    srp, sip = states(sf_p)
    srs, sis = states(sf_s)
    return (outs[0].reshape(bp, n_p, D_MODEL), outs[1].reshape(bs, n_s, D_MODEL),
            k_p.reshape(1, bp, n_p, N_HEADS, V_DIM), v_p.reshape(1, bp, n_p, N_HEADS, V_DIM), srp, sip,
            k_s.reshape(1, bs, n_s, N_HEADS, V_DIM), v_s.reshape(1, bs, n_s, N_HEADS, V_DIM), srs, sis)
```

```python
import functools
import math

import jax
import jax.numpy as jnp
from jax import lax
from jax.experimental import pallas as pl
from jax.experimental.pallas import tpu as pltpu

D_MODEL = 1024
CHUNK = 64
N_HEADS = 4
HEAD_DIM = 64
V_DIM = 2 * HEAD_DIM
ATTN_WIDTH = N_HEADS * V_DIM
SSM_WIDTH = 512
GROUP_SIZE = 16
N_GROUPS = SSM_WIDTH // GROUP_SIZE
STATE_DIM = 64
D_FF = 4 * D_MODEL
ROPE_THETA = 10000.0
LN_EPS = 1e-5
RMS_EPS = 1e-5
NEG_INF = -1e30
DEPTH = 1
DEEPNORM_ALPHA = (2.0 * DEPTH) ** 0.25
QKVU_COLS = 3 * ATTN_WIDTH + SSM_WIDTH
LOG2E = 1.4426950408889634

LANES = 128
VT_ROWS = V_DIM + 16
QUERY_LANES = 256
TOEPLITZ_ROWS = 256
SUB_ROWS = 256
SSM_GROUPS_PER_STEP = 4
VMEM_LIMIT = 56 * 1024 * 1024

F32 = jnp.float32
BF16 = jnp.bfloat16


def _cparams(*sem):
    return pltpu.CompilerParams(dimension_semantics=sem, vmem_limit_bytes=VMEM_LIMIT)


def _nt_dot(a, b):
    return lax.dot_general(a, b, (((1,), (1,)), ((), ())), preferred_element_type=F32)


def _dot(a, b):
    return jnp.dot(a, b, preferred_element_type=F32)


def _rotary(z, cos, sin_signed, first_half):
    swapped = jnp.where(first_half, pltpu.roll(z, 96, 1), pltpu.roll(z, 32, 1))
    return z * cos + swapped * sin_signed


def _proj_kernel(x_ref, w_ref, cos_ref, sin_ref, q_ref, k_ref, v_ref, u_ref, *rest, emit_t):
    xb = x_ref[...].astype(BF16)
    cos = cos_ref[...]
    sin = sin_ref[...]
    lane = lax.broadcasted_iota(jnp.int32, cos.shape, 1)
    first_half = (lane % HEAD_DIM) < (HEAD_DIM // 2)
    tm = xb.shape[0]
    zq = _dot(xb, w_ref[:, 0:ATTN_WIDTH])
    zk = _dot(xb, w_ref[:, ATTN_WIDTH:2 * ATTN_WIDTH])
    zv = _dot(xb, w_ref[:, 2 * ATTN_WIDTH:3 * ATTN_WIDTH])
    for h in range(N_HEADS):
        sl = slice(h * V_DIM, (h + 1) * V_DIM)
        q_ref[:, sl] = (_rotary(zq[:, sl], cos, sin, first_half) * (LOG2E * HEAD_DIM ** -0.5)).astype(BF16)
        kr = _rotary(zk[:, sl], cos, sin, first_half)
        k_ref[pl.ds(h, tm, stride=N_HEADS), :] = kr
        v_ref[pl.ds(h, tm, stride=N_HEADS), :] = zv[:, sl]
        if emit_t:
            rest[0][:, sl] = kr.astype(BF16)
    if emit_t:
        vt_ref = rest[1]
        zvt = zv.T.astype(BF16)
        ones = jnp.ones((VT_ROWS - V_DIM, zvt.shape[1]), BF16)
        for h in range(N_HEADS):
            vt_ref[h, 0:V_DIM, :] = zvt[h * V_DIM:(h + 1) * V_DIM, :]
            vt_ref[h, V_DIM:VT_ROWS, :] = ones
    u_ref[...] = _dot(xb, w_ref[:, 3 * ATTN_WIDTH:QKVU_COLS])


def _project(x2d, w_qkvu, cos_t, sin_t, seq_len, tm, emit_t):
    t_tokens = x2d.shape[0]
    n_tiles = t_tokens // tm
    n_pos_tiles = cos_t.shape[0] // tm
    tiles_per_seq = max(seq_len // tm, 1)
    row = lambda i: (i, 0)
    pos = lambda i: (i % n_pos_tiles, 0)
    out_shape = [jax.ShapeDtypeStruct((t_tokens, ATTN_WIDTH), BF16),
                 jax.ShapeDtypeStruct((t_tokens * N_HEADS, V_DIM), F32),
                 jax.ShapeDtypeStruct((t_tokens * N_HEADS, V_DIM), F32),
                 jax.ShapeDtypeStruct((t_tokens, SSM_WIDTH), F32)]
    out_specs = ([pl.BlockSpec((tm, ATTN_WIDTH), row)] + [pl.BlockSpec((tm * N_HEADS, V_DIM), row)] * 2
                 + [pl.BlockSpec((tm, SSM_WIDTH), row)])
    if emit_t:
        bsz = t_tokens // seq_len
        out_shape += [jax.ShapeDtypeStruct((t_tokens, ATTN_WIDTH), BF16),
                      jax.ShapeDtypeStruct((bsz, N_HEADS, VT_ROWS, seq_len), BF16)]
        out_specs += [pl.BlockSpec((tm, ATTN_WIDTH), row),
                      pl.BlockSpec((None, N_HEADS, VT_ROWS, tm),
                                   lambda i: (i // tiles_per_seq, 0, 0, i % tiles_per_seq))]
    return pl.pallas_call(
        functools.partial(_proj_kernel, emit_t=emit_t),
        grid=(n_tiles,),
        in_specs=[pl.BlockSpec((tm, D_MODEL), row),
                  pl.BlockSpec((D_MODEL, QKVU_COLS), lambda i: (0, 0)),
                  pl.BlockSpec((tm, LANES), pos),
                  pl.BlockSpec((tm, LANES), pos)],
        out_specs=out_specs,
        out_shape=out_shape,
        compiler_params=_cparams("parallel"),
        name="proj_t" if emit_t else "proj",
    )(x2d, w_qkvu, cos_t, sin_t)


def _diff_lambda(lq1, lk1, lq2, lk2, lam_init):
    return (jnp.exp(jnp.sum(lq1 * lk1, axis=1, keepdims=True))
            - jnp.exp(jnp.sum(lq2 * lk2, axis=1, keepdims=True)) + lam_init)


def _sub_norm(d, gain, lam_init):
    ms = jnp.mean(jnp.square(d), axis=1, keepdims=True)
    return d * lax.rsqrt(ms + RMS_EPS) * gain * (1.0 - lam_init)


def _stack_maps(q):
    lane = lax.broadcasted_iota(jnp.int32, q.shape, 1)
    zero = jnp.zeros_like(q)
    return jnp.concatenate([jnp.where(lane < HEAD_DIM, q, zero), jnp.where(lane >= HEAD_DIM, q, zero)], axis=0)


def _prompt_attn_kernel(item_q_ref, item_k_ref, lq1_ref, lk1_ref, lq2_ref, lk2_ref, gain_ref, q_ref, k_ref, vt_ref,
                        o_ref, m_scr, acc_scr, qx_scr, s0_scr, s1_scr, *, tq, n_items, lam_init):
    nq = q_ref.shape[0] // tq
    for i in range(nq):
        qx_scr[i * 2 * tq:(i + 1) * 2 * tq, :] = _stack_maps(q_ref[i * tq:(i + 1) * tq, :])
    acc_scr[...] = jnp.zeros(acc_scr.shape, F32)
    m_scr[...] = jnp.full(m_scr.shape, NEG_INF, F32)
    chains = [slice(c * QUERY_LANES, (c + 1) * QUERY_LANES) for c in range(2 * tq // QUERY_LANES)]

    def scores(w, s_scr):
        qi, j = item_q_ref[w], item_k_ref[w]
        kt = k_ref[pl.ds(pl.multiple_of(j * tq, tq), tq), :]
        for cs in chains:
            qx = qx_scr[pl.ds(pl.multiple_of(qi * 2 * tq + cs.start, QUERY_LANES), QUERY_LANES), :]
            s_scr[:, cs] = _nt_dot(kt, qx)

    def softmax_pv(j, s_scr, diagonal):
        vt = vt_ref[:, pl.ds(pl.multiple_of(j * tq, tq), tq)]
        for cs in chains:
            st = s_scr[:, cs]
            if diagonal:
                key_chunk = lax.broadcasted_iota(jnp.int32, st.shape, 0) // CHUNK
                qry_chunk = (cs.start % tq + lax.broadcasted_iota(jnp.int32, st.shape, 1)) // CHUNK
                st = jnp.where(key_chunk <= qry_chunk, st, NEG_INF)
            m_old = jnp.where(j == 0, NEG_INF, m_scr[:, cs])
            m_new = jnp.maximum(m_old, jnp.max(st, axis=0, keepdims=True))
            alpha = jnp.exp2(m_old - m_new)
            p = jnp.exp2(st - m_new).astype(BF16)
            acc_scr[:, cs] = acc_scr[:, cs] * alpha + _dot(vt, p)
            m_scr[:, cs] = m_new

    def finish(qi):
        acc = acc_scr[...]
        o = acc[0:V_DIM, :] * (1.0 / acc[V_DIM:V_DIM + 1, :])
        lam = _diff_lambda(lq1_ref[...], lk1_ref[...], lq2_ref[...], lk2_ref[...], lam_init)
        d = (o[:, 0:tq] - lam * o[:, tq:2 * tq]).T
        o_ref[pl.ds(pl.multiple_of(qi * tq, tq), tq), :] = _sub_norm(d, gain_ref[...], lam_init).astype(BF16)

    def pair_block(w, diag0, diag1):
        scores(w + 1, s1_scr)
        softmax_pv(item_k_ref[w], s0_scr, diag0)
        if diag0:
            finish(item_q_ref[w])
        scores(w + 2, s0_scr)
        softmax_pv(item_k_ref[w + 1], s1_scr, diag1)
        if diag1:
            finish(item_q_ref[w + 1])

    scores(0, s0_scr)

    def pair(i, carry):
        w = 2 * i
        d0 = item_q_ref[w] == item_k_ref[w]
        d1 = item_q_ref[w + 1] == item_k_ref[w + 1]
        for diag0 in (False, True):
            for diag1 in (False, True):
                pl.when(jnp.logical_and(d0 == diag0, d1 == diag1))(
                    functools.partial(pair_block, w, diag0, diag1))
        return carry

    lax.fori_loop(0, n_items // 2, pair, 0)


def _prompt_attention(lams, gain, q, kb, vt, bsz, seq_len, tq, lam_init):
    nq = seq_len // tq
    items = [(qi, j) for qi in range(nq) for j in range(qi + 1)]
    n_items = len(items)
    assert n_items % 2 == 0
    items.append(items[-1])
    items.append(items[-1])
    item_q = jnp.asarray([it[0] for it in items], jnp.int32)
    item_k = jnp.asarray([it[1] for it in items], jnp.int32)
    small = lambda b, h, iq, ik: (0, 0)
    grid_spec = pltpu.PrefetchScalarGridSpec(
        num_scalar_prefetch=2,
        grid=(bsz, N_HEADS),
        in_specs=[pl.BlockSpec((1, HEAD_DIM), small)] * 4 + [
            pl.BlockSpec((1, V_DIM), small),
            pl.BlockSpec((seq_len, V_DIM), lambda b, h, iq, ik: (b, h)),
            pl.BlockSpec((seq_len, V_DIM), lambda b, h, iq, ik: (b, h)),
            pl.BlockSpec((None, None, VT_ROWS, seq_len), lambda b, h, iq, ik: (b, h, 0, 0))],
        out_specs=pl.BlockSpec((seq_len, V_DIM), lambda b, h, iq, ik: (b, h)),
        scratch_shapes=[pltpu.VMEM((1, 2 * tq), F32), pltpu.VMEM((VT_ROWS, 2 * tq), F32),
                        pltpu.VMEM((nq * 2 * tq, V_DIM), BF16),
                        pltpu.VMEM((tq, 2 * tq), F32), pltpu.VMEM((tq, 2 * tq), F32)])
    return pl.pallas_call(
        functools.partial(_prompt_attn_kernel, tq=tq, n_items=n_items, lam_init=lam_init),
        grid_spec=grid_spec,
        out_shape=jax.ShapeDtypeStruct((bsz * seq_len, ATTN_WIDTH), BF16),
        compiler_params=_cparams("parallel", "parallel"),
        name="prompt_attn",
    )(item_q, item_k, *lams, gain, q, kb, vt)


def _sample_attn_kernel(lq1_ref, lk1_ref, lq2_ref, lk2_ref, gain_ref, q_ref, k_ref, v_ref, ck_ref, cv_ref,
                        o_ref, *, lam_init):
    n_new = q_ref.shape[0]
    past = ck_ref.shape[0] // N_HEADS
    lam = _diff_lambda(lq1_ref[...], lk1_ref[...], lq2_ref[...], lk2_ref[...], lam_init)
    gain = gain_ref[...]
    for h in range(N_HEADS):
        sl = slice(h * V_DIM, (h + 1) * V_DIM)
        old = pl.ds(h, past, stride=N_HEADS)
        new = pl.ds(h, n_new, stride=N_HEADS)
        qx = _stack_maps(q_ref[:, sl])
        s_c = _nt_dot(qx, ck_ref[old, :].astype(BF16))
        s_n = _nt_dot(qx, k_ref[new, :].astype(BF16))
        m = jnp.maximum(jnp.max(s_c, axis=1, keepdims=True), jnp.max(s_n, axis=1, keepdims=True))
        p_c = jnp.exp2(s_c - m)
        p_n = jnp.exp2(s_n - m)
        denom = jnp.sum(p_c, axis=1, keepdims=True) + jnp.sum(p_n, axis=1, keepdims=True)
        o = (_dot(p_c.astype(BF16), cv_ref[old, :].astype(BF16))
             + _dot(p_n.astype(BF16), v_ref[new, :].astype(BF16))) * (1.0 / denom)
        d = o[0:n_new, :] - lam * o[n_new:2 * n_new, :]
        o_ref[:, sl] = _sub_norm(d, gain, lam_init).astype(BF16)


def _sample_attention(lams, gain, q, k, v, cache_k, cache_v, lam_init):
    bsz, past_rows = cache_k.shape[0], cache_k.shape[1]
    n_new = q.shape[0] // bsz
    small = lambda b: (0, 0)
    row = lambda b: (b, 0)
    return pl.pallas_call(
        functools.partial(_sample_attn_kernel, lam_init=lam_init),
        grid=(bsz,),
        in_specs=[pl.BlockSpec((1, HEAD_DIM), small)] * 4 + [
            pl.BlockSpec((1, V_DIM), small),
            pl.BlockSpec((n_new, ATTN_WIDTH), row),
            pl.BlockSpec((n_new * N_HEADS, V_DIM), row),
            pl.BlockSpec((n_new * N_HEADS, V_DIM), row),
            pl.BlockSpec((None, past_rows, V_DIM), lambda b: (b, 0, 0)),
            pl.BlockSpec((None, past_rows, V_DIM), lambda b: (b, 0, 0))],
        out_specs=pl.BlockSpec((n_new, ATTN_WIDTH), row),
        out_shape=jax.ShapeDtypeStruct(q.shape, BF16),
        compiler_params=_cparams("parallel"),
        name="sample_attn",
    )(*lams, gain, q, k, v, cache_k, cache_v)


def _ssm_tables(a_re, a_im, b_re, b_im, c_re, c_im, log_dt):
    g = a_re.shape[0]
    width = GROUP_SIZE * CHUNK
    state_rows = 2 * STATE_DIM
    twice = lambda v: jnp.concatenate([v, v], axis=-1)
    rows = jnp.stack([twice(a_re), twice(a_im), jnp.broadcast_to(log_dt[:, None], (g, state_rows))], axis=1)
    rows = jnp.pad(rows, ((0, 0), (0, 8 - rows.shape[1]), (0, 0)))
    per_g = lambda i: (i, 0, 0)
    small = pl.BlockSpec((None, GROUP_SIZE, state_rows), per_g)
    return pl.pallas_call(
        _ssm_tables_kernel,
        grid=(g,),
        in_specs=[pl.BlockSpec((None, 8, state_rows), per_g), small, small, small, small],
        out_specs=[pl.BlockSpec((None, GROUP_SIZE, width), per_g),
                   pl.BlockSpec((None, state_rows, width), per_g),
                   pl.BlockSpec((None, width, state_rows), per_g),
                   pl.BlockSpec((None, state_rows, LANES), per_g),
                   pl.BlockSpec((None, state_rows, LANES), per_g)],
        out_shape=[jax.ShapeDtypeStruct((g, GROUP_SIZE, width), F32),
                   jax.ShapeDtypeStruct((g, state_rows, width), BF16),
                   jax.ShapeDtypeStruct((g, width, state_rows), BF16),
                   jax.ShapeDtypeStruct((g, state_rows, LANES), F32),
                   jax.ShapeDtypeStruct((g, state_rows, LANES), F32)],
        compiler_params=_cparams("parallel"),
        name="ssm_tables",
    )(rows, twice(b_re.transpose(0, 2, 1)), twice(b_im.transpose(0, 2, 1)), twice(c_re), twice(c_im))


def _ssm_tables_kernel(rows_ref, bre_ref, bim_ref, cre_ref, cim_ref, rrev_ref, w_ref, vm_ref, ac_ref, as_ref):
    lane = lax.broadcasted_iota(jnp.int32, (1, 2 * STATE_DIM), 1)
    lo = lane < STATE_DIM
    a_re, a_im, log_dt = rows_ref[0:1, :], rows_ref[1:2, :], rows_ref[2:3, :]
    dt = jnp.exp(log_dt)
    lam_re, lam_im = a_re * dt, a_im * dt
    mag = jnp.exp(lam_re)
    ar, ai = mag * jnp.cos(lam_im), mag * jnp.sin(lam_im)
    den = jnp.square(a_re) + jnp.square(a_im)
    cr = ((ar - 1.0) * a_re + ai * a_im) / den
    ci = (ai * a_re - (ar - 1.0) * a_im) / den
    bre, bim = bre_ref[...], bim_ref[...]
    bbr = cr * bre - ci * bim
    bbi = cr * bim + ci * bre
    quarter_turn = jnp.where(lo, 0.0, 0.5 * math.pi)

    def powers(tau):
        return jnp.exp(tau * lam_re) * jnp.cos(tau * lam_im - quarter_turn)

    def outer(y, x1, x2):
        y_sw = pltpu.roll(y, STATE_DIM, 1)
        prod = y[:, None, :] * x1[None, :, :] + y_sw[:, None, :] * x2[None, :, :]
        return prod.reshape(y.shape[0] * x1.shape[0], 2 * STATE_DIM)

    frames = lax.broadcasted_iota(jnp.int32, (CHUNK, 1), 0).astype(F32)
    wt = outer(powers((CHUNK - 1.0) - frames), bbr, jnp.where(lo, -bbi, bbi))
    w = wt.T
    w_ref[...] = w.astype(BF16)
    cre, cim = cre_ref[...], cim_ref[...]
    vm_ref[...] = outer(powers(frames + 1.0), jnp.where(lo, cre, -cre), -cim).astype(BF16)
    rrev_ref[...] = jnp.dot(jnp.where(lo, cre, -cim), w, precision=lax.Precision.HIGHEST,
                            preferred_element_type=F32)
    a_row = powers(jnp.full((8, 1), float(CHUNK), F32))[0:1, :]
    a_col = jnp.broadcast_to(a_row, (2 * STATE_DIM, 2 * STATE_DIM)).T
    a_swap = pltpu.roll(a_col, STATE_DIM, 0)
    top = lax.broadcasted_iota(jnp.int32, a_col.shape, 0) < STATE_DIM
    ac_ref[...] = jnp.where(top, a_col, a_swap)
    as_ref[...] = jnp.where(top, -a_swap, a_col)


def _chunk_cols_kernel(u_ref, o_ref):
    n_s, n_r, n_j = u_ref.shape[0], u_ref.shape[1], u_ref.shape[2]
    pad = LANES - n_s * n_r
    for jj in range(n_j):
        rows = [u_ref[s, :, jj, :] for s in range(n_s)]
        if pad:
            rows.append(jnp.zeros((pad, SSM_WIDTH), F32))
        cols = jnp.concatenate(rows, axis=0).T
        o_ref[:, jj * GROUP_SIZE:(jj + 1) * GROUP_SIZE, :] = cols.reshape(N_GROUPS, GROUP_SIZE, LANES).astype(BF16)


def _chunk_cols(u4, frames_per_step=16):
    s_total, n_r = u4.shape[0], u4.shape[1]
    s_step = min(s_total, LANES // n_r)
    assert s_step >= 1 and s_total % s_step == 0
    n_lane_blocks = s_total // s_step
    return pl.pallas_call(
        _chunk_cols_kernel,
        grid=(n_lane_blocks, CHUNK // frames_per_step),
        in_specs=[pl.BlockSpec((s_step, n_r, frames_per_step, SSM_WIDTH), lambda a, j: (a, 0, j, 0))],
        out_specs=pl.BlockSpec((N_GROUPS, frames_per_step * GROUP_SIZE, LANES), lambda a, j: (0, j, a)),
        out_shape=jax.ShapeDtypeStruct((N_GROUPS, CHUNK * GROUP_SIZE, n_lane_blocks * LANES), BF16),
        compiler_params=_cparams("parallel", "parallel"),
        name="ssm_in",
    )(u4)


def _unchunk_cols_kernel(y_ref, o_ref):
    n_s, n_r, n_j = o_ref.shape[0], o_ref.shape[1], o_ref.shape[2]
    for jj in range(n_j):
        cols = y_ref[:, jj * GROUP_SIZE:(jj + 1) * GROUP_SIZE, :].reshape(SSM_WIDTH, LANES)
        rows = cols.T
        for s in range(n_s):
            o_ref[s, :, jj, :] = rows[s * n_r:(s + 1) * n_r, :]


def _unchunk_cols(y_cols, s_total, n_r, frames_per_step=16):
    s_step = min(s_total, LANES // n_r)
    n_lane_blocks = s_total // s_step
    return pl.pallas_call(
        _unchunk_cols_kernel,
        grid=(n_lane_blocks, CHUNK // frames_per_step),
        in_specs=[pl.BlockSpec((N_GROUPS, frames_per_step * GROUP_SIZE, LANES), lambda a, j: (0, j, a))],
        out_specs=pl.BlockSpec((s_step, n_r, frames_per_step, SSM_WIDTH), lambda a, j: (a, 0, j, 0)),
        out_shape=jax.ShapeDtypeStruct((s_total, n_r, CHUNK, SSM_WIDTH), F32),
        compiler_params=_cparams("parallel", "parallel"),
        name="ssm_out",
    )(y_cols)


def _ssm_kernel(*refs, n_chunk):
    for gi in range(SSM_GROUPS_PER_STEP):
        _ssm_group(*[r.at[gi] for r in refs], n_chunk=n_chunk)


def _ssm_group(rrev_ref, w_ref, vm_ref, ac_ref, as_ref, xp_ref, xs_ref, s0_ref, yp_ref, ys_ref, sp_ref, ss_ref,
               mt_scr, *, n_chunk):
    width = GROUP_SIZE * CHUNK
    n_piece = width // LANES
    lane16 = lax.broadcasted_iota(jnp.int32, (GROUP_SIZE, LANES), 1)
    pieces = [rrev_ref[:, k * LANES:(k + 1) * LANES] for k in range(n_piece)] + [jnp.zeros((GROUP_SIZE, LANES), F32)]
    rolled = {0: pieces}
    for b in range(GROUP_SIZE, LANES, GROUP_SIZE):
        rolled[b] = [pltpu.roll(p, LANES - b, 1) for p in pieces[:n_piece]] + [pieces[n_piece]]
    def toeplitz_rows(t0, t1, n_cols):
        for t in range(t0, t1):
            shift = GROUP_SIZE * (CHUNK - 1 - t)
            a, b = shift // LANES, shift % LANES
            for v in range(n_cols // LANES):
                k = v + a
                if k >= n_piece:
                    blk = pieces[n_piece]
                elif b == 0:
                    blk = pieces[k]
                else:
                    blk = jnp.where(lane16 < LANES - b, rolled[b][k], rolled[b][k + 1])
                mt_scr[t * GROUP_SIZE:(t + 1) * GROUP_SIZE, v * LANES:(v + 1) * LANES] = blk.astype(BF16)

    w = w_ref[...]
    xp, xs = xp_ref[...], xs_ref[...]
    sloc_p, sloc_s = _dot(w, xp), _dot(w, xs)
    a_c, a_s = ac_ref[...], as_ref[...]

    def cmul(pc, ps, s):
        return pc * s + ps * pltpu.roll(s, STATE_DIM, 0)

    lane = lax.broadcasted_iota(jnp.int32, (2 * STATE_DIM, LANES), 1) % n_chunk
    prev_cols = []
    lane_id = lax.broadcasted_iota(jnp.int32, (2 * STATE_DIM, LANES), 1)
    final = jnp.zeros((2 * STATE_DIM, LANES), F32)
    seqs_per_tile = LANES // n_chunk
    for tile in range(xp.shape[1] // LANES):
        s_inc = sloc_p[:, tile * LANES:(tile + 1) * LANES]
        pc, ps = a_c, a_s
        dist = 1
        while dist < n_chunk:
            shifted = jnp.where(lane >= dist, pltpu.roll(s_inc, dist, 1), 0.0)
            s_inc = s_inc + cmul(pc, ps, shifted)
            pc, ps = pc * pc - ps * ps, 2.0 * pc * ps
            dist *= 2
        for k in range(seqs_per_tile):
            src, dst = (k + 1) * n_chunk - 1, tile * seqs_per_tile + k
            final = jnp.where(lane_id == dst, pltpu.roll(s_inc, (dst - src) % LANES, 1), final)
        prev_cols.append(jnp.where(lane >= 1, pltpu.roll(s_inc, 1, 1), 0.0))
    sp_ref[...] = final
    s_prev_p = jnp.concatenate(prev_cols, axis=1).astype(BF16)
    s0 = s0_ref[...]
    s0b = s0.astype(BF16)
    frames = TOEPLITZ_ROWS // GROUP_SIZE
    for i in range(width // TOEPLITZ_ROWS):
        rows = slice(i * TOEPLITZ_ROWS, (i + 1) * TOEPLITZ_ROWS)
        n_cols = (i + 1) * TOEPLITZ_ROWS
        toeplitz_rows(i * frames, (i + 1) * frames, n_cols)
        mt = mt_scr[rows, 0:n_cols]
        vm = vm_ref[rows, :]
        yp_ref[rows, :] = _dot(mt, xp[0:n_cols, :]) + _dot(vm, s_prev_p)
        ys_ref[rows, :] = _dot(mt, xs[0:n_cols, :]) + _dot(vm, s0b)
    ss_ref[...] = cmul(a_c, a_s, s0) + sloc_s


def _ssm(tables, x_p, x_s, s0, n_chunk):
    rrev, w, vm, a_c, a_s = tables
    n_groups, lanes_p = x_p.shape[0], x_p.shape[2]
    width = GROUP_SIZE * CHUNK
    per_g = lambda g: (g, 0, 0)
    state_rows = 2 * STATE_DIM
    gb = SSM_GROUPS_PER_STEP
    assert n_groups % gb == 0
    spec = lambda rows, cols: pl.BlockSpec((gb, rows, cols), per_g)
    return pl.pallas_call(
        functools.partial(_ssm_kernel, n_chunk=n_chunk),
        grid=(n_groups // gb,),
        in_specs=[spec(GROUP_SIZE, width), spec(state_rows, width), spec(width, state_rows),
                  spec(state_rows, LANES), spec(state_rows, LANES),
                  spec(width, lanes_p), spec(width, LANES), spec(state_rows, LANES)],
        out_specs=[spec(width, lanes_p), spec(width, LANES), spec(state_rows, LANES), spec(state_rows, LANES)],
        out_shape=[jax.ShapeDtypeStruct((n_groups, width, lanes_p), F32),
                   jax.ShapeDtypeStruct((n_groups, width, LANES), F32),
                   jax.ShapeDtypeStruct((n_groups, state_rows, LANES), F32),
                   jax.ShapeDtypeStruct((n_groups, state_rows, LANES), F32)],
        scratch_shapes=[pltpu.VMEM((gb, width, width), BF16)],
        compiler_params=_cparams("parallel"),
        name="ssm",
    )(rrev, w, vm, a_c, a_s, x_p, x_s, s0)


def _layer_norm(x, g, b):
    mu = jnp.mean(x, axis=1, keepdims=True)
    xc = x - mu
    var = jnp.mean(jnp.square(xc), axis=1, keepdims=True)
    return xc * lax.rsqrt(var + LN_EPS) * g + b


def _gelu_tanh(x):
    return 0.5 * x * (1.0 + jnp.tanh(math.sqrt(2.0 / math.pi) * (x + 0.044715 * (x * x * x))))


def _merge_kernel(x_ref, ao_ref, ys_ref, u_ref, d_ref, wg_ref, wap_ref, wglu_ref, wout_ref, g1_ref, b1_ref, h_ref):
    for r in range(x_ref.shape[0] // SUB_ROWS):
        rows = slice(r * SUB_ROWS, (r + 1) * SUB_ROWS)
        x = x_ref[rows, :]
        xb = x.astype(BF16)
        a_branch = _dot(ao_ref[rows, :], wap_ref[...])
        s_act = _gelu_tanh(ys_ref[rows, :] + d_ref[...] * u_ref[rows, :]).astype(BF16)
        s_branch = (_dot(s_act, wglu_ref[:, 0:D_MODEL])
                    * jax.nn.sigmoid(_dot(s_act, wglu_ref[:, D_MODEL:2 * D_MODEL])))
        m = (jax.nn.sigmoid(_dot(xb, wg_ref[:, 0:D_MODEL])) * a_branch
             + jax.nn.sigmoid(_dot(xb, wg_ref[:, D_MODEL:2 * D_MODEL])) * s_branch)
        h_ref[rows, :] = _layer_norm(DEEPNORM_ALPHA * x + _dot(m.astype(BF16), wout_ref[...]),
                                     g1_ref[...], b1_ref[...])


def _merge(x2d, ao, ys, u, d, w_gate, w_ap, w_glu, w_out, ln_g, ln_b, tm):
    t_tokens = x2d.shape[0]
    row = lambda i: (i, 0)
    const = lambda i: (0, 0)
    return pl.pallas_call(
        _merge_kernel,
        grid=(t_tokens // tm,),
        in_specs=[pl.BlockSpec((tm, D_MODEL), row),
                  pl.BlockSpec((tm, ATTN_WIDTH), row),
                  pl.BlockSpec((tm, SSM_WIDTH), row),
                  pl.BlockSpec((tm, SSM_WIDTH), row),
                  pl.BlockSpec((1, SSM_WIDTH), const),
                  pl.BlockSpec((D_MODEL, 2 * D_MODEL), const),
                  pl.BlockSpec((ATTN_WIDTH, D_MODEL), const),
                  pl.BlockSpec((SSM_WIDTH, 2 * D_MODEL), const),
                  pl.BlockSpec((D_MODEL, D_MODEL), const),
                  pl.BlockSpec((1, D_MODEL), const),
                  pl.BlockSpec((1, D_MODEL), const)],
        out_specs=pl.BlockSpec((tm, D_MODEL), row),
        out_shape=jax.ShapeDtypeStruct((t_tokens, D_MODEL), F32),
        compiler_params=_cparams("parallel"),
        name="merge",
    )(x2d, ao, ys, u, d, w_gate, w_ap, w_glu, w_out, ln_g, ln_b)


def _mlp_kernel(h_ref, w1_ref, w2_ref, g2_ref, b2_ref, o_ref, *, ff_chunk):
    for r in range(h_ref.shape[0] // SUB_ROWS):
        rows = slice(r * SUB_ROWS, (r + 1) * SUB_ROWS)
        h = h_ref[rows, :]
        hb = h.astype(BF16)
        f = jnp.zeros(h.shape, F32)
        for c in range(D_FF // ff_chunk):
            sl = slice(c * ff_chunk, (c + 1) * ff_chunk)
            t = jnp.maximum(_dot(hb, w1_ref[:, sl]), 0.0)
            f = f + _dot((t * t).astype(BF16), w2_ref[sl, :])
        o_ref[rows, :] = _layer_norm(DEEPNORM_ALPHA * h + f, g2_ref[...], b2_ref[...])


def _mlp(h, w1, w2, ln_g, ln_b, tm, ff_chunk=1024):
    t_tokens = h.shape[0]
    row = lambda i: (i, 0)
    const = lambda i: (0, 0)
    return pl.pallas_call(
        functools.partial(_mlp_kernel, ff_chunk=ff_chunk),
        grid=(t_tokens // tm,),
        in_specs=[pl.BlockSpec((tm, D_MODEL), row),
                  pl.BlockSpec((D_MODEL, D_FF), const),
                  pl.BlockSpec((D_FF, D_MODEL), const),
                  pl.BlockSpec((1, D_MODEL), const),
                  pl.BlockSpec((1, D_MODEL), const)],
        out_specs=pl.BlockSpec((tm, D_MODEL), row),
        out_shape=jax.ShapeDtypeStruct((t_tokens, D_MODEL), F32),
        compiler_params=_cparams("parallel"),
        name="mlp",
    )(h, w1, w2, ln_g, ln_b)


def _rope_tables(pos):
    inv = 1.0 / (ROPE_THETA ** (jnp.arange(0, HEAD_DIM, 2, dtype=F32) / HEAD_DIM))
    ang = pos.astype(F32)[:, None] * inv[None, :]
    c, s = jnp.cos(ang), jnp.sin(ang)
    reps = LANES // HEAD_DIM
    return jnp.tile(jnp.concatenate([c, c], axis=1), (1, reps)), jnp.tile(jnp.concatenate([-s, s], axis=1), (1, reps))


def kernel(x_prompt, x_sample, cache_k, cache_v, state_ssm_re, state_ssm_im, w_in, lambda_q1, lambda_k1, lambda_q2, lambda_k2, subln_gain, ssm_a_re, ssm_a_im, ssm_b_re, ssm_b_im, ssm_c_re, ssm_c_im, ssm_d, ssm_log_dt, w_attn_proj, w_glu_a, w_glu_b, w_out, ln1_g, ln1_b, w_ff1, w_ff2, ln2_g, ln2_b):
    bp, n_p = x_prompt.shape[0], x_prompt.shape[1]
    bs, n_s = x_sample.shape[0], x_sample.shape[1]
    past = cache_k.shape[2]
    assert w_in.shape[0] == DEPTH and n_s == CHUNK and n_p % CHUNK == 0
    n_chunk = n_p // CHUNK
    tm = min(512, n_p)
    tm_s = min(512, bs * n_s)
    tq = min(512, n_p)
    l = 0
    lam_init = 0.8 - 0.6 * math.exp(-0.3 * l)

    xp = x_prompt.reshape(bp * n_p, D_MODEL)
    xs = x_sample.reshape(bs * n_s, D_MODEL)
    w_qkvu = w_in[l, :, 0:QKVU_COLS].astype(BF16)
    w_gate = w_in[l, :, QKVU_COLS:].astype(BF16)
    w_ap = w_attn_proj[l].astype(BF16)
    w_glu = jnp.concatenate([w_glu_a[l], w_glu_b[l]], axis=1).astype(BF16)
    w_o = w_out[l].astype(BF16)
    w1, w2 = w_ff1[l].astype(BF16), w_ff2[l].astype(BF16)
    lams = [v[l].reshape(1, HEAD_DIM) for v in (lambda_q1, lambda_k1, lambda_q2, lambda_k2)]
    gain = subln_gain[l].reshape(1, V_DIM)
    d_skip = ssm_d[l].reshape(1, SSM_WIDTH)
    lng = [v[l].reshape(1, D_MODEL) for v in (ln1_g, ln1_b, ln2_g, ln2_b)]

    cos_p, sin_p = _rope_tables(jnp.arange(n_p))
    cos_s, sin_s = _rope_tables(jnp.tile(past + jnp.arange(n_s), tm_s // n_s))

    q_p, k_p, v_p, u_p, kb_p, vt_p = _project(xp, w_qkvu, cos_p, sin_p, n_p, tm, True)
    q_s, k_s, v_s, u_s = _project(xs, w_qkvu, cos_s, sin_s, n_s, tm_s, False)

    ao_p = _prompt_attention(lams, gain, q_p, kb_p, vt_p, bp, n_p, tq, lam_init)
    ao_s = _sample_attention(lams, gain, q_s, k_s, v_s,
                             cache_k[l].reshape(bs, past * N_HEADS, V_DIM),
                             cache_v[l].reshape(bs, past * N_HEADS, V_DIM), lam_init)

    tables = _ssm_tables(ssm_a_re[l], ssm_a_im[l], ssm_b_re[l], ssm_b_im[l],
                         ssm_c_re[l], ssm_c_im[l], ssm_log_dt[l])
    x_cols_p = _chunk_cols(u_p.reshape(bp, n_chunk, CHUNK, SSM_WIDTH))
    x_cols_s = _chunk_cols(u_s.reshape(1, bs, CHUNK, SSM_WIDTH))
    s0 = jnp.concatenate([state_ssm_re[l], state_ssm_im[l]], axis=-1).transpose(1, 2, 0)
    s0 = jnp.pad(s0, ((0, 0), (0, 0), (0, LANES - bs)))
    y_cols_p, y_cols_s, st_p, st_s = _ssm(tables, x_cols_p, x_cols_s, s0, n_chunk)
    ys_p = _unchunk_cols(y_cols_p, bp, n_chunk).reshape(bp * n_p, SSM_WIDTH)
    ys_s = _unchunk_cols(y_cols_s, 1, bs).reshape(bs * n_s, SSM_WIDTH)
    sf_p = st_p[:, :, 0:bp]
    sf_s = st_s[:, :, 0:bs]

    outs = []
    for x2d, ao, ys, u, tile in ((xp, ao_p, ys_p, u_p, tm), (xs, ao_s, ys_s, u_s, tm_s)):
        h = _merge(x2d, ao, ys, u, d_skip, w_gate, w_ap, w_glu, w_o, lng[0], lng[1], tile)
        outs.append(_mlp(h, w1, w2, lng[2], lng[3], tile))

    def states(sf):
        t = sf.transpose(2, 0, 1)
        return t[None, :, :, 0:STATE_DIM], t[None, :, :, STATE_DIM:]

    srp, sip = states(sf_p)
    srs, sis = states(sf_s)
    return (outs[0].reshape(bp, n_p, D_MODEL), outs[1].reshape(bs, n_s, D_MODEL),
            k_p.reshape(1, bp, n_p, N_HEADS, V_DIM), v_p.reshape(1, bp, n_p, N_HEADS, V_DIM), srp, sip,
            k_s.reshape(1, bs, n_s, N_HEADS, V_DIM), v_s.reshape(1, bs, n_s, N_HEADS, V_DIM), srs, sis)
```

```python
import functools
import math

import jax
import jax.numpy as jnp
from jax import lax
from jax.experimental import pallas as pl
from jax.experimental.pallas import tpu as pltpu

D_MODEL = 1024
CHUNK = 64
N_HEADS = 4
HEAD_DIM = 64
V_DIM = 2 * HEAD_DIM
ATTN_WIDTH = N_HEADS * V_DIM
SSM_WIDTH = 512
GROUP_SIZE = 16
N_GROUPS = SSM_WIDTH // GROUP_SIZE
STATE_DIM = 64
D_FF = 4 * D_MODEL
ROPE_THETA = 10000.0
LN_EPS = 1e-5
RMS_EPS = 1e-5
NEG_INF = -1e30
DEPTH = 1
DEEPNORM_ALPHA = (2.0 * DEPTH) ** 0.25
QKVU_COLS = 3 * ATTN_WIDTH + SSM_WIDTH
LOG2E = 1.4426950408889634

LANES = 128
VT_ROWS = V_DIM + 16
QUERY_LANES = 256
TOEPLITZ_ROWS = 256
SUB_ROWS = 256
SSM_GROUPS_PER_STEP = 4
VMEM_LIMIT = 56 * 1024 * 1024

F32 = jnp.float32
BF16 = jnp.bfloat16


def _cparams(*sem):
    return pltpu.CompilerParams(dimension_semantics=sem, vmem_limit_bytes=VMEM_LIMIT)


def _nt_dot(a, b):
    return lax.dot_general(a, b, (((1,), (1,)), ((), ())), preferred_element_type=F32)


def _dot(a, b):
    return jnp.dot(a, b, preferred_element_type=F32)


def _rotary(z, cos, sin_signed, first_half):
    swapped = jnp.where(first_half, pltpu.roll(z, 96, 1), pltpu.roll(z, 32, 1))
    return z * cos + swapped * sin_signed


def _proj_kernel(x_ref, w_ref, cos_ref, sin_ref, q_ref, k_ref, v_ref, u_ref, *rest, emit_t):
    xb = x_ref[...].astype(BF16)
    cos = cos_ref[...]
    sin = sin_ref[...]
    lane = lax.broadcasted_iota(jnp.int32, cos.shape, 1)
    first_half = (lane % HEAD_DIM) < (HEAD_DIM // 2)
    tm = xb.shape[0]
    zq = _dot(xb, w_ref[:, 0:ATTN_WIDTH])
    zk = _dot(xb, w_ref[:, ATTN_WIDTH:2 * ATTN_WIDTH])
    zv = _dot(xb, w_ref[:, 2 * ATTN_WIDTH:3 * ATTN_WIDTH])
    for h in range(N_HEADS):
        sl = slice(h * V_DIM, (h + 1) * V_DIM)
        q_ref[:, sl] = (_rotary(zq[:, sl], cos, sin, first_half) * (LOG2E * HEAD_DIM ** -0.5)).astype(BF16)
        kr = _rotary(zk[:, sl], cos, sin, first_half)
        k_ref[pl.ds(h, tm, stride=N_HEADS), :] = kr
        v_ref[pl.ds(h, tm, stride=N_HEADS), :] = zv[:, sl]
        if emit_t:
            rest[0][:, sl] = kr.astype(BF16)
    if emit_t:
        vt_ref = rest[1]
        zvt = zv.T.astype(BF16)
        ones = jnp.ones((VT_ROWS - V_DIM, zvt.shape[1]), BF16)
        for h in range(N_HEADS):
            vt_ref[h, 0:V_DIM, :] = zvt[h * V_DIM:(h + 1) * V_DIM, :]
            vt_ref[h, V_DIM:VT_ROWS, :] = ones
    u_ref[...] = _dot(xb, w_ref[:, 3 * ATTN_WIDTH:QKVU_COLS])


def _project(x2d, w_qkvu, cos_t, sin_t, seq_len, tm, emit_t):
    t_tokens = x2d.shape[0]
    n_tiles = t_tokens // tm
    n_pos_tiles = cos_t.shape[0] // tm
    tiles_per_seq = max(seq_len // tm, 1)
    row = lambda i: (i, 0)
    pos = lambda i: (i % n_pos_tiles, 0)
    out_shape = [jax.ShapeDtypeStruct((t_tokens, ATTN_WIDTH), BF16),
                 jax.ShapeDtypeStruct((t_tokens * N_HEADS, V_DIM), F32),
                 jax.ShapeDtypeStruct((t_tokens * N_HEADS, V_DIM), F32),
                 jax.ShapeDtypeStruct((t_tokens, SSM_WIDTH), F32)]
    out_specs = ([pl.BlockSpec((tm, ATTN_WIDTH), row)] + [pl.BlockSpec((tm * N_HEADS, V_DIM), row)] * 2
                 + [pl.BlockSpec((tm, SSM_WIDTH), row)])
    if emit_t:
        bsz = t_tokens // seq_len
        out_shape += [jax.ShapeDtypeStruct((t_tokens, ATTN_WIDTH), BF16),
                      jax.ShapeDtypeStruct((bsz, N_HEADS, VT_ROWS, seq_len), BF16)]
        out_specs += [pl.BlockSpec((tm, ATTN_WIDTH), row),
                      pl.BlockSpec((None, N_HEADS, VT_ROWS, tm),
                                   lambda i: (i // tiles_per_seq, 0, 0, i % tiles_per_seq))]
    return pl.pallas_call(
        functools.partial(_proj_kernel, emit_t=emit_t),
        grid=(n_tiles,),
        in_specs=[pl.BlockSpec((tm, D_MODEL), row),
                  pl.BlockSpec((D_MODEL, QKVU_COLS), lambda i: (0, 0)),
                  pl.BlockSpec((tm, LANES), pos),
                  pl.BlockSpec((tm, LANES), pos)],
        out_specs=out_specs,
        out_shape=out_shape,
        compiler_params=_cparams("parallel"),
        name="proj_t" if emit_t else "proj",
    )(x2d, w_qkvu, cos_t, sin_t)


def _diff_lambda(lq1, lk1, lq2, lk2, lam_init):
    return (jnp.exp(jnp.sum(lq1 * lk1, axis=1, keepdims=True))
            - jnp.exp(jnp.sum(lq2 * lk2, axis=1, keepdims=True)) + lam_init)


def _sub_norm(d, gain, lam_init):
    ms = jnp.mean(jnp.square(d), axis=1, keepdims=True)
    return d * lax.rsqrt(ms + RMS_EPS) * gain * (1.0 - lam_init)


def _stack_maps(q):
    lane = lax.broadcasted_iota(jnp.int32, q.shape, 1)
    zero = jnp.zeros_like(q)
    return jnp.concatenate([jnp.where(lane < HEAD_DIM, q, zero), jnp.where(lane >= HEAD_DIM, q, zero)], axis=0)


def _prompt_attn_kernel(item_q_ref, item_k_ref, lq1_ref, lk1_ref, lq2_ref, lk2_ref, gain_ref, q_ref, k_ref, vt_ref,
                        o_ref, m_scr, acc_scr, qx_scr, s0_scr, s1_scr, mx0_scr, mx1_scr, *, tq, n_items, lam_init):
    nq = q_ref.shape[0] // tq
    for i in range(nq):
        qx_scr[i * 2 * tq:(i + 1) * 2 * tq, :] = _stack_maps(q_ref[i * tq:(i + 1) * tq, :])
    acc_scr[...] = jnp.zeros(acc_scr.shape, F32)
    m_scr[...] = jnp.full(m_scr.shape, NEG_INF, F32)
    chains = [slice(c * QUERY_LANES, (c + 1) * QUERY_LANES) for c in range(2 * tq // QUERY_LANES)]

    def scores(w, s_scr, mx_scr):
        qi, j = item_q_ref[w], item_k_ref[w]
        kt = k_ref[pl.ds(pl.multiple_of(j * tq, tq), tq), :]
        for cs in chains:
            qx = qx_scr[pl.ds(pl.multiple_of(qi * 2 * tq + cs.start, QUERY_LANES), QUERY_LANES), :]
            st = _nt_dot(kt, qx)
            s_scr[:, cs] = st
            mx_scr[:, cs] = jnp.max(st, axis=0, keepdims=True)

    def softmax_pv(j, s_scr, mx_scr, diagonal):
        vt = vt_ref[:, pl.ds(pl.multiple_of(j * tq, tq), tq)]
        for cs in chains:
            st = s_scr[:, cs]
            if diagonal:
                key_chunk = lax.broadcasted_iota(jnp.int32, st.shape, 0) // CHUNK
                qry_chunk = (cs.start % tq + lax.broadcasted_iota(jnp.int32, st.shape, 1)) // CHUNK
                st = jnp.where(key_chunk <= qry_chunk, st, NEG_INF)
                tile_max = jnp.max(st, axis=0, keepdims=True)
            else:
                tile_max = mx_scr[:, cs]
            m_old = jnp.where(j == 0, NEG_INF, m_scr[:, cs])
            m_new = jnp.maximum(m_old, tile_max)
            alpha = jnp.exp2(m_old - m_new)
            p = jnp.exp2(st - m_new).astype(BF16)
            acc_scr[:, cs] = acc_scr[:, cs] * alpha + _dot(vt, p)
            m_scr[:, cs] = m_new

    def finish(qi):
        acc = acc_scr[...]
        o = acc[0:V_DIM, :] * (1.0 / acc[V_DIM:V_DIM + 1, :])
        lam = _diff_lambda(lq1_ref[...], lk1_ref[...], lq2_ref[...], lk2_ref[...], lam_init)
        d = (o[:, 0:tq] - lam * o[:, tq:2 * tq]).T
        o_ref[pl.ds(pl.multiple_of(qi * tq, tq), tq), :] = _sub_norm(d, gain_ref[...], lam_init).astype(BF16)

    def pair_block(w, diag0, diag1):
        scores(w + 1, s1_scr, mx1_scr)
        softmax_pv(item_k_ref[w], s0_scr, mx0_scr, diag0)
        if diag0:
            finish(item_q_ref[w])
        scores(w + 2, s0_scr, mx0_scr)
        softmax_pv(item_k_ref[w + 1], s1_scr, mx1_scr, diag1)
        if diag1:
            finish(item_q_ref[w + 1])

    scores(0, s0_scr, mx0_scr)

    def pair(i, carry):
        w = 2 * i
        d0 = item_q_ref[w] == item_k_ref[w]
        d1 = item_q_ref[w + 1] == item_k_ref[w + 1]
        for diag0 in (False, True):
            for diag1 in (False, True):
                pl.when(jnp.logical_and(d0 == diag0, d1 == diag1))(
                    functools.partial(pair_block, w, diag0, diag1))
        return carry

    lax.fori_loop(0, n_items // 2, pair, 0)


def _prompt_attention(lams, gain, q, kb, vt, bsz, seq_len, tq, lam_init):
    nq = seq_len // tq
    items = [(qi, j) for qi in range(nq) for j in range(qi + 1)]
    n_items = len(items)
    assert n_items % 2 == 0
    items.append(items[-1])
    items.append(items[-1])
    item_q = jnp.asarray([it[0] for it in items], jnp.int32)
    item_k = jnp.asarray([it[1] for it in items], jnp.int32)
    small = lambda b, h, iq, ik: (0, 0)
    grid_spec = pltpu.PrefetchScalarGridSpec(
        num_scalar_prefetch=2,
        grid=(bsz, N_HEADS),
        in_specs=[pl.BlockSpec((1, HEAD_DIM), small)] * 4 + [
            pl.BlockSpec((1, V_DIM), small),
            pl.BlockSpec((seq_len, V_DIM), lambda b, h, iq, ik: (b, h)),
            pl.BlockSpec((seq_len, V_DIM), lambda b, h, iq, ik: (b, h)),
            pl.BlockSpec((None, None, VT_ROWS, seq_len), lambda b, h, iq, ik: (b, h, 0, 0))],
        out_specs=pl.BlockSpec((seq_len, V_DIM), lambda b, h, iq, ik: (b, h)),
        scratch_shapes=[pltpu.VMEM((1, 2 * tq), F32), pltpu.VMEM((VT_ROWS, 2 * tq), F32),
                        pltpu.VMEM((nq * 2 * tq, V_DIM), BF16),
                        pltpu.VMEM((tq, 2 * tq), F32), pltpu.VMEM((tq, 2 * tq), F32),
                        pltpu.VMEM((1, 2 * tq), F32), pltpu.VMEM((1, 2 * tq), F32)])
    return pl.pallas_call(
        functools.partial(_prompt_attn_kernel, tq=tq, n_items=n_items, lam_init=lam_init),
        grid_spec=grid_spec,
        out_shape=jax.ShapeDtypeStruct((bsz * seq_len, ATTN_WIDTH), BF16),
        compiler_params=_cparams("parallel", "parallel"),
        name="prompt_attn",
    )(item_q, item_k, *lams, gain, q, kb, vt)


def _sample_attn_kernel(lq1_ref, lk1_ref, lq2_ref, lk2_ref, gain_ref, q_ref, k_ref, v_ref, ck_ref, cv_ref,
                        o_ref, *, lam_init):
    n_new = q_ref.shape[0]
    past = ck_ref.shape[0] // N_HEADS
    lam = _diff_lambda(lq1_ref[...], lk1_ref[...], lq2_ref[...], lk2_ref[...], lam_init)
    gain = gain_ref[...]
    for h in range(N_HEADS):
        sl = slice(h * V_DIM, (h + 1) * V_DIM)
        old = pl.ds(h, past, stride=N_HEADS)
        new = pl.ds(h, n_new, stride=N_HEADS)
        qx = _stack_maps(q_ref[:, sl])
        s_c = _nt_dot(qx, ck_ref[old, :].astype(BF16))
        s_n = _nt_dot(qx, k_ref[new, :].astype(BF16))
        m = jnp.maximum(jnp.max(s_c, axis=1, keepdims=True), jnp.max(s_n, axis=1, keepdims=True))
        p_c = jnp.exp2(s_c - m)
        p_n = jnp.exp2(s_n - m)
        denom = jnp.sum(p_c, axis=1, keepdims=True) + jnp.sum(p_n, axis=1, keepdims=True)
        o = (_dot(p_c.astype(BF16), cv_ref[old, :].astype(BF16))
             + _dot(p_n.astype(BF16), v_ref[new, :].astype(BF16))) * (1.0 / denom)
        d = o[0:n_new, :] - lam * o[n_new:2 * n_new, :]
        o_ref[:, sl] = _sub_norm(d, gain, lam_init).astype(BF16)


def _sample_attention(lams, gain, q, k, v, cache_k, cache_v, lam_init):
    bsz, past_rows = cache_k.shape[0], cache_k.shape[1]
    n_new = q.shape[0] // bsz
    small = lambda b: (0, 0)
    row = lambda b: (b, 0)
    return pl.pallas_call(
        functools.partial(_sample_attn_kernel, lam_init=lam_init),
        grid=(bsz,),
        in_specs=[pl.BlockSpec((1, HEAD_DIM), small)] * 4 + [
            pl.BlockSpec((1, V_DIM), small),
            pl.BlockSpec((n_new, ATTN_WIDTH), row),
            pl.BlockSpec((n_new * N_HEADS, V_DIM), row),
            pl.BlockSpec((n_new * N_HEADS, V_DIM), row),
            pl.BlockSpec((None, past_rows, V_DIM), lambda b: (b, 0, 0)),
            pl.BlockSpec((None, past_rows, V_DIM), lambda b: (b, 0, 0))],
        out_specs=pl.BlockSpec((n_new, ATTN_WIDTH), row),
        out_shape=jax.ShapeDtypeStruct(q.shape, BF16),
        compiler_params=_cparams("parallel"),
        name="sample_attn",
    )(*lams, gain, q, k, v, cache_k, cache_v)


def _ssm_tables(a_re, a_im, b_re, b_im, c_re, c_im, log_dt):
    g = a_re.shape[0]
    width = GROUP_SIZE * CHUNK
    state_rows = 2 * STATE_DIM
    twice = lambda v: jnp.concatenate([v, v], axis=-1)
    rows = jnp.stack([twice(a_re), twice(a_im), jnp.broadcast_to(log_dt[:, None], (g, state_rows))], axis=1)
    rows = jnp.pad(rows, ((0, 0), (0, 8 - rows.shape[1]), (0, 0)))
    per_g = lambda i: (i, 0, 0)
    small = pl.BlockSpec((None, GROUP_SIZE, state_rows), per_g)
    return pl.pallas_call(
        _ssm_tables_kernel,
        grid=(g,),
        in_specs=[pl.BlockSpec((None, 8, state_rows), per_g), small, small, small, small],
        out_specs=[pl.BlockSpec((None, GROUP_SIZE, width), per_g),
                   pl.BlockSpec((None, state_rows, width), per_g),
                   pl.BlockSpec((None, width, state_rows), per_g),
                   pl.BlockSpec((None, state_rows, LANES), per_g),
                   pl.BlockSpec((None, state_rows, LANES), per_g)],
        out_shape=[jax.ShapeDtypeStruct((g, GROUP_SIZE, width), F32),
                   jax.ShapeDtypeStruct((g, state_rows, width), BF16),
                   jax.ShapeDtypeStruct((g, width, state_rows), BF16),
                   jax.ShapeDtypeStruct((g, state_rows, LANES), F32),
                   jax.ShapeDtypeStruct((g, state_rows, LANES), F32)],
        compiler_params=_cparams("parallel"),
        name="ssm_tables",
    )(rows, twice(b_re.transpose(0, 2, 1)), twice(b_im.transpose(0, 2, 1)), twice(c_re), twice(c_im))


def _ssm_tables_kernel(rows_ref, bre_ref, bim_ref, cre_ref, cim_ref, rrev_ref, w_ref, vm_ref, ac_ref, as_ref):
    lane = lax.broadcasted_iota(jnp.int32, (1, 2 * STATE_DIM), 1)
    lo = lane < STATE_DIM
    a_re, a_im, log_dt = rows_ref[0:1, :], rows_ref[1:2, :], rows_ref[2:3, :]
    dt = jnp.exp(log_dt)
    lam_re, lam_im = a_re * dt, a_im * dt
    mag = jnp.exp(lam_re)
    ar, ai = mag * jnp.cos(lam_im), mag * jnp.sin(lam_im)
    den = jnp.square(a_re) + jnp.square(a_im)
    cr = ((ar - 1.0) * a_re + ai * a_im) / den
    ci = (ai * a_re - (ar - 1.0) * a_im) / den
    bre, bim = bre_ref[...], bim_ref[...]
    bbr = cr * bre - ci * bim
    bbi = cr * bim + ci * bre
    quarter_turn = jnp.where(lo, 0.0, 0.5 * math.pi)

    def powers(tau):
        return jnp.exp(tau * lam_re) * jnp.cos(tau * lam_im - quarter_turn)

    def outer(y, x1, x2):
        y_sw = pltpu.roll(y, STATE_DIM, 1)
        prod = y[:, None, :] * x1[None, :, :] + y_sw[:, None, :] * x2[None, :, :]
        return prod.reshape(y.shape[0] * x1.shape[0], 2 * STATE_DIM)

    frames = lax.broadcasted_iota(jnp.int32, (CHUNK, 1), 0).astype(F32)
    wt = outer(powers((CHUNK - 1.0) - frames), bbr, jnp.where(lo, -bbi, bbi))
    w = wt.T
    w_ref[...] = w.astype(BF16)
    cre, cim = cre_ref[...], cim_ref[...]
    vm_ref[...] = outer(powers(frames + 1.0), jnp.where(lo, cre, -cre), -cim).astype(BF16)
    rrev_ref[...] = jnp.dot(jnp.where(lo, cre, -cim), w, precision=lax.Precision.HIGHEST,
                            preferred_element_type=F32)
    a_row = powers(jnp.full((8, 1), float(CHUNK), F32))[0:1, :]
    a_col = jnp.broadcast_to(a_row, (2 * STATE_DIM, 2 * STATE_DIM)).T
    a_swap = pltpu.roll(a_col, STATE_DIM, 0)
    top = lax.broadcasted_iota(jnp.int32, a_col.shape, 0) < STATE_DIM
    ac_ref[...] = jnp.where(top, a_col, a_swap)
    as_ref[...] = jnp.where(top, -a_swap, a_col)


def _chunk_cols_kernel(u_ref, o_ref):
    n_s, n_r, n_j = u_ref.shape[0], u_ref.shape[1], u_ref.shape[2]
    pad = LANES - n_s * n_r
    for jj in range(n_j):
        rows = [u_ref[s, :, jj, :] for s in range(n_s)]
        if pad:
            rows.append(jnp.zeros((pad, SSM_WIDTH), F32))
        cols = jnp.concatenate(rows, axis=0).T
        o_ref[:, jj * GROUP_SIZE:(jj + 1) * GROUP_SIZE, :] = cols.reshape(N_GROUPS, GROUP_SIZE, LANES).astype(BF16)


def _chunk_cols(u4, frames_per_step=16):
    s_total, n_r = u4.shape[0], u4.shape[1]
    s_step = min(s_total, LANES // n_r)
    assert s_step >= 1 and s_total % s_step == 0
    n_lane_blocks = s_total // s_step
    return pl.pallas_call(
        _chunk_cols_kernel,
        grid=(n_lane_blocks, CHUNK // frames_per_step),
        in_specs=[pl.BlockSpec((s_step, n_r, frames_per_step, SSM_WIDTH), lambda a, j: (a, 0, j, 0))],
        out_specs=pl.BlockSpec((N_GROUPS, frames_per_step * GROUP_SIZE, LANES), lambda a, j: (0, j, a)),
        out_shape=jax.ShapeDtypeStruct((N_GROUPS, CHUNK * GROUP_SIZE, n_lane_blocks * LANES), BF16),
        compiler_params=_cparams("parallel", "parallel"),
        name="ssm_in",
    )(u4)


def _unchunk_cols_kernel(y_ref, o_ref):
    n_s, n_r, n_j = o_ref.shape[0], o_ref.shape[1], o_ref.shape[2]
    for jj in range(n_j):
        cols = y_ref[:, jj * GROUP_SIZE:(jj + 1) * GROUP_SIZE, :].reshape(SSM_WIDTH, LANES)
        rows = cols.T
        for s in range(n_s):
            o_ref[s, :, jj, :] = rows[s * n_r:(s + 1) * n_r, :]


def _unchunk_cols(y_cols, s_total, n_r, frames_per_step=16):
    s_step = min(s_total, LANES // n_r)
    n_lane_blocks = s_total // s_step
    return pl.pallas_call(
        _unchunk_cols_kernel,
        grid=(n_lane_blocks, CHUNK // frames_per_step),
        in_specs=[pl.BlockSpec((N_GROUPS, frames_per_step * GROUP_SIZE, LANES), lambda a, j: (0, j, a))],
        out_specs=pl.BlockSpec((s_step, n_r, frames_per_step, SSM_WIDTH), lambda a, j: (a, 0, j, 0)),
        out_shape=jax.ShapeDtypeStruct((s_total, n_r, CHUNK, SSM_WIDTH), F32),
        compiler_params=_cparams("parallel", "parallel"),
        name="ssm_out",
    )(y_cols)


def _ssm_kernel(*refs, n_chunk):
    for gi in range(SSM_GROUPS_PER_STEP):
        _ssm_group(*[r.at[gi] for r in refs], n_chunk=n_chunk)


def _ssm_group(rrev_ref, w_ref, vm_ref, ac_ref, as_ref, xp_ref, xs_ref, s0_ref, yp_ref, ys_ref, sp_ref, ss_ref,
               mt_scr, *, n_chunk):
    width = GROUP_SIZE * CHUNK
    n_piece = width // LANES
    lane16 = lax.broadcasted_iota(jnp.int32, (GROUP_SIZE, LANES), 1)
    pieces = [rrev_ref[:, k * LANES:(k + 1) * LANES] for k in range(n_piece)] + [jnp.zeros((GROUP_SIZE, LANES), F32)]
    rolled = {0: pieces}
    for b in range(GROUP_SIZE, LANES, GROUP_SIZE):
        rolled[b] = [pltpu.roll(p, LANES - b, 1) for p in pieces[:n_piece]] + [pieces[n_piece]]
    def toeplitz_rows(t0, t1, n_cols):
        for t in range(t0, t1):
            shift = GROUP_SIZE * (CHUNK - 1 - t)
            a, b = shift // LANES, shift % LANES
            for v in range(n_cols // LANES):
                k = v + a
                if k >= n_piece:
                    blk = pieces[n_piece]
                elif b == 0:
                    blk = pieces[k]
                else:
                    blk = jnp.where(lane16 < LANES - b, rolled[b][k], rolled[b][k + 1])
                mt_scr[t * GROUP_SIZE:(t + 1) * GROUP_SIZE, v * LANES:(v + 1) * LANES] = blk.astype(BF16)

    w = w_ref[...]
    xp, xs = xp_ref[...], xs_ref[...]
    sloc_p, sloc_s = _dot(w, xp), _dot(w, xs)
    a_c, a_s = ac_ref[...], as_ref[...]

    def cmul(pc, ps, s):
        return pc * s + ps * pltpu.roll(s, STATE_DIM, 0)

    lane = lax.broadcasted_iota(jnp.int32, (2 * STATE_DIM, LANES), 1) % n_chunk
    prev_cols = []
    lane_id = lax.broadcasted_iota(jnp.int32, (2 * STATE_DIM, LANES), 1)
    final = jnp.zeros((2 * STATE_DIM, LANES), F32)
    seqs_per_tile = LANES // n_chunk
    for tile in range(xp.shape[1] // LANES):
        s_inc = sloc_p[:, tile * LANES:(tile + 1) * LANES]
        pc, ps = a_c, a_s
        dist = 1
        while dist < n_chunk:
            shifted = jnp.where(lane >= dist, pltpu.roll(s_inc, dist, 1), 0.0)
            s_inc = s_inc + cmul(pc, ps, shifted)
            pc, ps = pc * pc - ps * ps, 2.0 * pc * ps
            dist *= 2
        for k in range(seqs_per_tile):
            src, dst = (k + 1) * n_chunk - 1, tile * seqs_per_tile + k
            final = jnp.where(lane_id == dst, pltpu.roll(s_inc, (dst - src) % LANES, 1), final)
        prev_cols.append(jnp.where(lane >= 1, pltpu.roll(s_inc, 1, 1), 0.0))
    sp_ref[...] = final
    s_prev_p = jnp.concatenate(prev_cols, axis=1).astype(BF16)
    s0 = s0_ref[...]
    s0b = s0.astype(BF16)
    frames = TOEPLITZ_ROWS // GROUP_SIZE
    for i in range(width // TOEPLITZ_ROWS):
        rows = slice(i * TOEPLITZ_ROWS, (i + 1) * TOEPLITZ_ROWS)
        n_cols = (i + 1) * TOEPLITZ_ROWS
        toeplitz_rows(i * frames, (i + 1) * frames, n_cols)
        mt = mt_scr[rows, 0:n_cols]
        vm = vm_ref[rows, :]
        yp_ref[rows, :] = _dot(mt, xp[0:n_cols, :]) + _dot(vm, s_prev_p)
        ys_ref[rows, :] = _dot(mt, xs[0:n_cols, :]) + _dot(vm, s0b)
    ss_ref[...] = cmul(a_c, a_s, s0) + sloc_s


def _ssm(tables, x_p, x_s, s0, n_chunk):
    rrev, w, vm, a_c, a_s = tables
    n_groups, lanes_p = x_p.shape[0], x_p.shape[2]
    width = GROUP_SIZE * CHUNK
    per_g = lambda g: (g, 0, 0)
    state_rows = 2 * STATE_DIM
    gb = SSM_GROUPS_PER_STEP
    assert n_groups % gb == 0
    spec = lambda rows, cols: pl.BlockSpec((gb, rows, cols), per_g)
    return pl.pallas_call(
        functools.partial(_ssm_kernel, n_chunk=n_chunk),
        grid=(n_groups // gb,),
        in_specs=[spec(GROUP_SIZE, width), spec(state_rows, width), spec(width, state_rows),
                  spec(state_rows, LANES), spec(state_rows, LANES),
                  spec(width, lanes_p), spec(width, LANES), spec(state_rows, LANES)],
        out_specs=[spec(width, lanes_p), spec(width, LANES), spec(state_rows, LANES), spec(state_rows, LANES)],
        out_shape=[jax.ShapeDtypeStruct((n_groups, width, lanes_p), F32),
                   jax.ShapeDtypeStruct((n_groups, width, LANES), F32),
                   jax.ShapeDtypeStruct((n_groups, state_rows, LANES), F32),
                   jax.ShapeDtypeStruct((n_groups, state_rows, LANES), F32)],
        scratch_shapes=[pltpu.VMEM((gb, width, width), BF16)],
        compiler_params=_cparams("parallel"),
        name="ssm",
    )(rrev, w, vm, a_c, a_s, x_p, x_s, s0)


def _layer_norm(x, g, b):
    mu = jnp.mean(x, axis=1, keepdims=True)
    xc = x - mu
    var = jnp.mean(jnp.square(xc), axis=1, keepdims=True)
    return xc * lax.rsqrt(var + LN_EPS) * g + b


def _gelu_tanh(x):
    return 0.5 * x * (1.0 + jnp.tanh(math.sqrt(2.0 / math.pi) * (x + 0.044715 * (x * x * x))))


def _merge_kernel(x_ref, ao_ref, ys_ref, u_ref, d_ref, wg_ref, wap_ref, wglu_ref, wout_ref, g1_ref, b1_ref, h_ref):
    for r in range(x_ref.shape[0] // SUB_ROWS):
        rows = slice(r * SUB_ROWS, (r + 1) * SUB_ROWS)
        x = x_ref[rows, :]
        xb = x.astype(BF16)
        a_branch = _dot(ao_ref[rows, :], wap_ref[...])
        s_act = _gelu_tanh(ys_ref[rows, :] + d_ref[...] * u_ref[rows, :]).astype(BF16)
        s_branch = (_dot(s_act, wglu_ref[:, 0:D_MODEL])
                    * jax.nn.sigmoid(_dot(s_act, wglu_ref[:, D_MODEL:2 * D_MODEL])))
        m = (jax.nn.sigmoid(_dot(xb, wg_ref[:, 0:D_MODEL])) * a_branch
             + jax.nn.sigmoid(_dot(xb, wg_ref[:, D_MODEL:2 * D_MODEL])) * s_branch)
        h_ref[rows, :] = _layer_norm(DEEPNORM_ALPHA * x + _dot(m.astype(BF16), wout_ref[...]),
                                     g1_ref[...], b1_ref[...])


def _merge(x2d, ao, ys, u, d, w_gate, w_ap, w_glu, w_out, ln_g, ln_b, tm):
    t_tokens = x2d.shape[0]
    row = lambda i: (i, 0)
    const = lambda i: (0, 0)
    return pl.pallas_call(
        _merge_kernel,
        grid=(t_tokens // tm,),
        in_specs=[pl.BlockSpec((tm, D_MODEL), row),
                  pl.BlockSpec((tm, ATTN_WIDTH), row),
                  pl.BlockSpec((tm, SSM_WIDTH), row),
                  pl.BlockSpec((tm, SSM_WIDTH), row),
                  pl.BlockSpec((1, SSM_WIDTH), const),
                  pl.BlockSpec((D_MODEL, 2 * D_MODEL), const),
                  pl.BlockSpec((ATTN_WIDTH, D_MODEL), const),
                  pl.BlockSpec((SSM_WIDTH, 2 * D_MODEL), const),
                  pl.BlockSpec((D_MODEL, D_MODEL), const),
                  pl.BlockSpec((1, D_MODEL), const),
                  pl.BlockSpec((1, D_MODEL), const)],
        out_specs=pl.BlockSpec((tm, D_MODEL), row),
        out_shape=jax.ShapeDtypeStruct((t_tokens, D_MODEL), F32),
        compiler_params=_cparams("parallel"),
        name="merge",
    )(x2d, ao, ys, u, d, w_gate, w_ap, w_glu, w_out, ln_g, ln_b)


def _mlp_kernel(h_ref, w1_ref, w2_ref, g2_ref, b2_ref, o_ref, *, ff_chunk):
    for r in range(h_ref.shape[0] // SUB_ROWS):
        rows = slice(r * SUB_ROWS, (r + 1) * SUB_ROWS)
        h = h_ref[rows, :]
        hb = h.astype(BF16)
        f = jnp.zeros(h.shape, F32)
        for c in range(D_FF // ff_chunk):
            sl = slice(c * ff_chunk, (c + 1) * ff_chunk)
            t = jnp.maximum(_dot(hb, w1_ref[:, sl]), 0.0)
            f = f + _dot((t * t).astype(BF16), w2_ref[sl, :])
        o_ref[rows, :] = _layer_norm(DEEPNORM_ALPHA * h + f, g2_ref[...], b2_ref[...])


def _mlp(h, w1, w2, ln_g, ln_b, tm, ff_chunk=1024):
    t_tokens = h.shape[0]
    row = lambda i: (i, 0)
    const = lambda i: (0, 0)
    return pl.pallas_call(
        functools.partial(_mlp_kernel, ff_chunk=ff_chunk),
        grid=(t_tokens // tm,),
        in_specs=[pl.BlockSpec((tm, D_MODEL), row),
                  pl.BlockSpec((D_MODEL, D_FF), const),
                  pl.BlockSpec((D_FF, D_MODEL), const),
                  pl.BlockSpec((1, D_MODEL), const),
                  pl.BlockSpec((1, D_MODEL), const)],
        out_specs=pl.BlockSpec((tm, D_MODEL), row),
        out_shape=jax.ShapeDtypeStruct((t_tokens, D_MODEL), F32),
        compiler_params=_cparams("parallel"),
        name="mlp",
    )(h, w1, w2, ln_g, ln_b)


def _rope_tables(pos):
    inv = 1.0 / (ROPE_THETA ** (jnp.arange(0, HEAD_DIM, 2, dtype=F32) / HEAD_DIM))
    ang = pos.astype(F32)[:, None] * inv[None, :]
    c, s = jnp.cos(ang), jnp.sin(ang)
    reps = LANES // HEAD_DIM
    return jnp.tile(jnp.concatenate([c, c], axis=1), (1, reps)), jnp.tile(jnp.concatenate([-s, s], axis=1), (1, reps))


def kernel(x_prompt, x_sample, cache_k, cache_v, state_ssm_re, state_ssm_im, w_in, lambda_q1, lambda_k1, lambda_q2, lambda_k2, subln_gain, ssm_a_re, ssm_a_im, ssm_b_re, ssm_b_im, ssm_c_re, ssm_c_im, ssm_d, ssm_log_dt, w_attn_proj, w_glu_a, w_glu_b, w_out, ln1_g, ln1_b, w_ff1, w_ff2, ln2_g, ln2_b):
    bp, n_p = x_prompt.shape[0], x_prompt.shape[1]
    bs, n_s = x_sample.shape[0], x_sample.shape[1]
    past = cache_k.shape[2]
    assert w_in.shape[0] == DEPTH and n_s == CHUNK and n_p % CHUNK == 0
    n_chunk = n_p // CHUNK
    tm = min(512, n_p)
    tm_s = min(512, bs * n_s)
    tq = min(512, n_p)
    l = 0
    lam_init = 0.8 - 0.6 * math.exp(-0.3 * l)

    xp = x_prompt.reshape(bp * n_p, D_MODEL)
    xs = x_sample.reshape(bs * n_s, D_MODEL)
    w_qkvu = w_in[l, :, 0:QKVU_COLS].astype(BF16)
    w_gate = w_in[l, :, QKVU_COLS:].astype(BF16)
    w_ap = w_attn_proj[l].astype(BF16)
    w_glu = jnp.concatenate([w_glu_a[l], w_glu_b[l]], axis=1).astype(BF16)
    w_o = w_out[l].astype(BF16)
    w1, w2 = w_ff1[l].astype(BF16), w_ff2[l].astype(BF16)
    lams = [v[l].reshape(1, HEAD_DIM) for v in (lambda_q1, lambda_k1, lambda_q2, lambda_k2)]
    gain = subln_gain[l].reshape(1, V_DIM)
    d_skip = ssm_d[l].reshape(1, SSM_WIDTH)
    lng = [v[l].reshape(1, D_MODEL) for v in (ln1_g, ln1_b, ln2_g, ln2_b)]

    cos_p, sin_p = _rope_tables(jnp.arange(n_p))
    cos_s, sin_s = _rope_tables(jnp.tile(past + jnp.arange(n_s), tm_s // n_s))

    q_p, k_p, v_p, u_p, kb_p, vt_p = _project(xp, w_qkvu, cos_p, sin_p, n_p, tm, True)
    q_s, k_s, v_s, u_s = _project(xs, w_qkvu, cos_s, sin_s, n_s, tm_s, False)

    ao_p = _prompt_attention(lams, gain, q_p, kb_p, vt_p, bp, n_p, tq, lam_init)
    ao_s = _sample_attention(lams, gain, q_s, k_s, v_s,
                             cache_k[l].reshape(bs, past * N_HEADS, V_DIM),
                             cache_v[l].reshape(bs, past * N_HEADS, V_DIM), lam_init)

    tables = _ssm_tables(ssm_a_re[l], ssm_a_im[l], ssm_b_re[l], ssm_b_im[l],
                         ssm_c_re[l], ssm_c_im[l], ssm_log_dt[l])
    x_cols_p = _chunk_cols(u_p.reshape(bp, n_chunk, CHUNK, SSM_WIDTH))
    x_cols_s = _chunk_cols(u_s.reshape(1, bs, CHUNK, SSM_WIDTH))
    s0 = jnp.concatenate([state_ssm_re[l], state_ssm_im[l]], axis=-1).transpose(1, 2, 0)
    s0 = jnp.pad(s0, ((0, 0), (0, 0), (0, LANES - bs)))
    y_cols_p, y_cols_s, st_p, st_s = _ssm(tables, x_cols_p, x_cols_s, s0, n_chunk)
    ys_p = _unchunk_cols(y_cols_p, bp, n_chunk).reshape(bp * n_p, SSM_WIDTH)
    ys_s = _unchunk_cols(y_cols_s, 1, bs).reshape(bs * n_s, SSM_WIDTH)
    sf_p = st_p[:, :, 0:bp]
    sf_s = st_s[:, :, 0:bs]

    outs = []
    for x2d, ao, ys, u, tile in ((xp, ao_p, ys_p, u_p, tm), (xs, ao_s, ys_s, u_s, tm_s)):
        h = _merge(x2d, ao, ys, u, d_skip, w_gate, w_ap, w_glu, w_o, lng[0], lng[1], tile)
        outs.append(_mlp(h, w1, w2, lng[2], lng[3], tile))

    def states(sf):
        t = sf.transpose(2, 0, 1)
        return t[None, :, :, 0:STATE_DIM], t[None, :, :, STATE_DIM:]

    srp, sip = states(sf_p)
    srs, sis = states(sf_s)
    return (outs[0].reshape(bp, n_p, D_MODEL), outs[1].reshape(bs, n_s, D_MODEL),
            k_p.reshape(1, bp, n_p, N_HEADS, V_DIM), v_p.reshape(1, bp, n_p, N_HEADS, V_DIM), srp, sip,
            k_s.reshape(1, bs, n_s, N_HEADS, V_DIM), v_s.reshape(1, bs, n_s, N_HEADS, V_DIM), srs, sis)
```

```python
import functools
import math

import jax
import jax.numpy as jnp
from jax import lax
from jax.experimental import pallas as pl
from jax.experimental.pallas import tpu as pltpu

D_MODEL = 1024
CHUNK = 64
N_HEADS = 4
HEAD_DIM = 64
V_DIM = 2 * HEAD_DIM
ATTN_WIDTH = N_HEADS * V_DIM
SSM_WIDTH = 512
GROUP_SIZE = 16
N_GROUPS = SSM_WIDTH // GROUP_SIZE
STATE_DIM = 64
D_FF = 4 * D_MODEL
ROPE_THETA = 10000.0
LN_EPS = 1e-5
RMS_EPS = 1e-5
NEG_INF = -1e30
DEPTH = 1
DEEPNORM_ALPHA = (2.0 * DEPTH) ** 0.25
QKVU_COLS = 3 * ATTN_WIDTH + SSM_WIDTH
LOG2E = 1.4426950408889634

LANES = 128
VT_ROWS = V_DIM + 16
QUERY_LANES = 256
TOEPLITZ_ROWS = 256
SUB_ROWS = 256
SSM_GROUPS_PER_STEP = 4
VMEM_LIMIT = 56 * 1024 * 1024

F32 = jnp.float32
BF16 = jnp.bfloat16


def _cparams(*sem):
    return pltpu.CompilerParams(dimension_semantics=sem, vmem_limit_bytes=VMEM_LIMIT)


def _nt_dot(a, b):
    return lax.dot_general(a, b, (((1,), (1,)), ((), ())), preferred_element_type=F32)


def _dot(a, b):
    return jnp.dot(a, b, preferred_element_type=F32)


def _rotary(z, cos, sin_signed, first_half):
    swapped = jnp.where(first_half, pltpu.roll(z, 96, 1), pltpu.roll(z, 32, 1))
    return z * cos + swapped * sin_signed


def _proj_kernel(x_ref, w_ref, cos_ref, sin_ref, q_ref, k_ref, v_ref, u_ref, *rest, emit_t):
    xb = x_ref[...].astype(BF16)
    cos = cos_ref[...]
    sin = sin_ref[...]
    lane = lax.broadcasted_iota(jnp.int32, cos.shape, 1)
    first_half = (lane % HEAD_DIM) < (HEAD_DIM // 2)
    tm = xb.shape[0]
    zq = _dot(xb, w_ref[:, 0:ATTN_WIDTH])
    zk = _dot(xb, w_ref[:, ATTN_WIDTH:2 * ATTN_WIDTH])
    zv = _dot(xb, w_ref[:, 2 * ATTN_WIDTH:3 * ATTN_WIDTH])
    for h in range(N_HEADS):
        sl = slice(h * V_DIM, (h + 1) * V_DIM)
        q_ref[:, sl] = (_rotary(zq[:, sl], cos, sin, first_half) * (LOG2E * HEAD_DIM ** -0.5)).astype(BF16)
        kr = _rotary(zk[:, sl], cos, sin, first_half)
        k_ref[pl.ds(h, tm, stride=N_HEADS), :] = kr
        v_ref[pl.ds(h, tm, stride=N_HEADS), :] = zv[:, sl]
        if emit_t:
            rest[0][:, sl] = kr.astype(BF16)
    if emit_t:
        vt_ref = rest[1]
        zvt = zv.T.astype(BF16)
        ones = jnp.ones((VT_ROWS - V_DIM, zvt.shape[1]), BF16)
        for h in range(N_HEADS):
            vt_ref[h, 0:V_DIM, :] = zvt[h * V_DIM:(h + 1) * V_DIM, :]
            vt_ref[h, V_DIM:VT_ROWS, :] = ones
    u_ref[...] = _dot(xb, w_ref[:, 3 * ATTN_WIDTH:QKVU_COLS])


def _project(x2d, w_qkvu, cos_t, sin_t, seq_len, tm, emit_t):
    t_tokens = x2d.shape[0]
    n_tiles = t_tokens // tm
    n_pos_tiles = cos_t.shape[0] // tm
    tiles_per_seq = max(seq_len // tm, 1)
    row = lambda i: (i, 0)
    pos = lambda i: (i % n_pos_tiles, 0)
    out_shape = [jax.ShapeDtypeStruct((t_tokens, ATTN_WIDTH), BF16),
                 jax.ShapeDtypeStruct((t_tokens * N_HEADS, V_DIM), F32),
                 jax.ShapeDtypeStruct((t_tokens * N_HEADS, V_DIM), F32),
                 jax.ShapeDtypeStruct((t_tokens, SSM_WIDTH), F32)]
    out_specs = ([pl.BlockSpec((tm, ATTN_WIDTH), row)] + [pl.BlockSpec((tm * N_HEADS, V_DIM), row)] * 2
                 + [pl.BlockSpec((tm, SSM_WIDTH), row)])
    if emit_t:
        bsz = t_tokens // seq_len
        out_shape += [jax.ShapeDtypeStruct((t_tokens, ATTN_WIDTH), BF16),
                      jax.ShapeDtypeStruct((bsz, N_HEADS, VT_ROWS, seq_len), BF16)]
        out_specs += [pl.BlockSpec((tm, ATTN_WIDTH), row),
                      pl.BlockSpec((None, N_HEADS, VT_ROWS, tm),
                                   lambda i: (i // tiles_per_seq, 0, 0, i % tiles_per_seq))]
    return pl.pallas_call(
        functools.partial(_proj_kernel, emit_t=emit_t),
        grid=(n_tiles,),
        in_specs=[pl.BlockSpec((tm, D_MODEL), row),
                  pl.BlockSpec((D_MODEL, QKVU_COLS), lambda i: (0, 0)),
                  pl.BlockSpec((tm, LANES), pos),
                  pl.BlockSpec((tm, LANES), pos)],
        out_specs=out_specs,
        out_shape=out_shape,
        compiler_params=_cparams("parallel"),
        name="proj_t" if emit_t else "proj",
    )(x2d, w_qkvu, cos_t, sin_t)


def _diff_lambda(lq1, lk1, lq2, lk2, lam_init):
    return (jnp.exp(jnp.sum(lq1 * lk1, axis=1, keepdims=True))
            - jnp.exp(jnp.sum(lq2 * lk2, axis=1, keepdims=True)) + lam_init)


def _sub_norm(d, gain, lam_init):
    ms = jnp.mean(jnp.square(d), axis=1, keepdims=True)
    return d * lax.rsqrt(ms + RMS_EPS) * gain * (1.0 - lam_init)


def _stack_maps(q):
    lane = lax.broadcasted_iota(jnp.int32, q.shape, 1)
    zero = jnp.zeros_like(q)
    return jnp.concatenate([jnp.where(lane < HEAD_DIM, q, zero), jnp.where(lane >= HEAD_DIM, q, zero)], axis=0)


def _prompt_attn_kernel(item_q_ref, item_k_ref, lq1_ref, lk1_ref, lq2_ref, lk2_ref, gain_ref, q_ref, k_ref, vt_ref,
                        o_ref, m_scr, acc_scr, qx_scr, s0_scr, s1_scr, mx0_scr, mx1_scr, *, tq, n_items, lam_init):
    nq = q_ref.shape[0] // tq
    for i in range(nq):
        qx_scr[i * 2 * tq:(i + 1) * 2 * tq, :] = _stack_maps(q_ref[i * tq:(i + 1) * tq, :])
    acc_scr[...] = jnp.zeros(acc_scr.shape, F32)
    m_scr[...] = jnp.full(m_scr.shape, NEG_INF, F32)
    chains = [slice(c * QUERY_LANES, (c + 1) * QUERY_LANES) for c in range(2 * tq // QUERY_LANES)]

    def scores(w, s_scr, mx_scr):
        qi, j = item_q_ref[w], item_k_ref[w]
        kt = k_ref[pl.ds(pl.multiple_of(j * tq, tq), tq), :]
        for cs in chains:
            qx = qx_scr[pl.ds(pl.multiple_of(qi * 2 * tq + cs.start, QUERY_LANES), QUERY_LANES), :]
            st = _nt_dot(kt, qx)
            s_scr[:, cs] = st
            mx_scr[:, cs] = jnp.max(st, axis=0, keepdims=True)

    def softmax_pv(j, s_scr, mx_scr, diagonal):
        start = pl.multiple_of(j * tq, tq)
        for cs in chains:
            if diagonal:
                n_keys = min(tq, cs.start % tq + QUERY_LANES)
                lane = lax.broadcasted_iota(jnp.int32, (CHUNK, QUERY_LANES), 1)
                blocks = []
                for kc in range(n_keys // CHUNK):
                    blk = s_scr[kc * CHUNK:(kc + 1) * CHUNK, cs]
                    first_visible = kc * CHUNK - cs.start % tq
                    if first_visible > 0:
                        blk = jnp.where(lane >= first_visible, blk, NEG_INF)
                    blocks.append(blk)
                st = jnp.concatenate(blocks, axis=0)
                tile_max = jnp.max(st, axis=0, keepdims=True)
            else:
                n_keys = tq
                st = s_scr[:, cs]
                tile_max = mx_scr[:, cs]
            vt = vt_ref[:, pl.ds(start, n_keys)]
            m_old = jnp.where(j == 0, NEG_INF, m_scr[:, cs])
            m_new = jnp.maximum(m_old, tile_max)
            alpha = jnp.exp2(m_old - m_new)
            p = jnp.exp2(st - m_new).astype(BF16)
            acc_scr[:, cs] = acc_scr[:, cs] * alpha + _dot(vt, p)
            m_scr[:, cs] = m_new

    def finish(qi):
        acc = acc_scr[...]
        o = acc[0:V_DIM, :] * (1.0 / acc[V_DIM:V_DIM + 1, :])
        lam = _diff_lambda(lq1_ref[...], lk1_ref[...], lq2_ref[...], lk2_ref[...], lam_init)
        d = (o[:, 0:tq] - lam * o[:, tq:2 * tq]).T
        o_ref[pl.ds(pl.multiple_of(qi * tq, tq), tq), :] = _sub_norm(d, gain_ref[...], lam_init).astype(BF16)

    def pair_block(w, diag0, diag1):
        scores(w + 1, s1_scr, mx1_scr)
        softmax_pv(item_k_ref[w], s0_scr, mx0_scr, diag0)
        if diag0:
            finish(item_q_ref[w])
        scores(w + 2, s0_scr, mx0_scr)
        softmax_pv(item_k_ref[w + 1], s1_scr, mx1_scr, diag1)
        if diag1:
            finish(item_q_ref[w + 1])

    scores(0, s0_scr, mx0_scr)

    def pair(i, carry):
        w = 2 * i
        d0 = item_q_ref[w] == item_k_ref[w]
        d1 = item_q_ref[w + 1] == item_k_ref[w + 1]
        for diag0 in (False, True):
            for diag1 in (False, True):
                pl.when(jnp.logical_and(d0 == diag0, d1 == diag1))(
                    functools.partial(pair_block, w, diag0, diag1))
        return carry

    lax.fori_loop(0, n_items // 2, pair, 0)


def _prompt_attention(lams, gain, q, kb, vt, bsz, seq_len, tq, lam_init):
    nq = seq_len // tq
    items = [(qi, j) for qi in range(nq) for j in range(qi + 1)]
    n_items = len(items)
    assert n_items % 2 == 0
    items.append(items[-1])
    items.append(items[-1])
    item_q = jnp.asarray([it[0] for it in items], jnp.int32)
    item_k = jnp.asarray([it[1] for it in items], jnp.int32)
    small = lambda b, h, iq, ik: (0, 0)
    grid_spec = pltpu.PrefetchScalarGridSpec(
        num_scalar_prefetch=2,
        grid=(bsz, N_HEADS),
        in_specs=[pl.BlockSpec((1, HEAD_DIM), small)] * 4 + [
            pl.BlockSpec((1, V_DIM), small),
            pl.BlockSpec((seq_len, V_DIM), lambda b, h, iq, ik: (b, h)),
            pl.BlockSpec((seq_len, V_DIM), lambda b, h, iq, ik: (b, h)),
            pl.BlockSpec((None, None, VT_ROWS, seq_len), lambda b, h, iq, ik: (b, h, 0, 0))],
        out_specs=pl.BlockSpec((seq_len, V_DIM), lambda b, h, iq, ik: (b, h)),
        scratch_shapes=[pltpu.VMEM((1, 2 * tq), F32), pltpu.VMEM((VT_ROWS, 2 * tq), F32),
                        pltpu.VMEM((nq * 2 * tq, V_DIM), BF16),
                        pltpu.VMEM((tq, 2 * tq), F32), pltpu.VMEM((tq, 2 * tq), F32),
                        pltpu.VMEM((1, 2 * tq), F32), pltpu.VMEM((1, 2 * tq), F32)])
    return pl.pallas_call(
        functools.partial(_prompt_attn_kernel, tq=tq, n_items=n_items, lam_init=lam_init),
        grid_spec=grid_spec,
        out_shape=jax.ShapeDtypeStruct((bsz * seq_len, ATTN_WIDTH), BF16),
        compiler_params=_cparams("parallel", "parallel"),
        name="prompt_attn",
    )(item_q, item_k, *lams, gain, q, kb, vt)


def _sample_attn_kernel(lq1_ref, lk1_ref, lq2_ref, lk2_ref, gain_ref, q_ref, k_ref, v_ref, ck_ref, cv_ref,
                        o_ref, *, lam_init):
    n_new = q_ref.shape[0]
    past = ck_ref.shape[0] // N_HEADS
    lam = _diff_lambda(lq1_ref[...], lk1_ref[...], lq2_ref[...], lk2_ref[...], lam_init)
    gain = gain_ref[...]
    for h in range(N_HEADS):
        sl = slice(h * V_DIM, (h + 1) * V_DIM)
        old = pl.ds(h, past, stride=N_HEADS)
        new = pl.ds(h, n_new, stride=N_HEADS)
        qx = _stack_maps(q_ref[:, sl])
        s_c = _nt_dot(qx, ck_ref[old, :].astype(BF16))
        s_n = _nt_dot(qx, k_ref[new, :].astype(BF16))
        m = jnp.maximum(jnp.max(s_c, axis=1, keepdims=True), jnp.max(s_n, axis=1, keepdims=True))
        p_c = jnp.exp2(s_c - m)
        p_n = jnp.exp2(s_n - m)
        denom = jnp.sum(p_c, axis=1, keepdims=True) + jnp.sum(p_n, axis=1, keepdims=True)
        o = (_dot(p_c.astype(BF16), cv_ref[old, :].astype(BF16))
             + _dot(p_n.astype(BF16), v_ref[new, :].astype(BF16))) * (1.0 / denom)
        d = o[0:n_new, :] - lam * o[n_new:2 * n_new, :]
        o_ref[:, sl] = _sub_norm(d, gain, lam_init).astype(BF16)


def _sample_attention(lams, gain, q, k, v, cache_k, cache_v, lam_init):
    bsz, past_rows = cache_k.shape[0], cache_k.shape[1]
    n_new = q.shape[0] // bsz
    small = lambda b: (0, 0)
    row = lambda b: (b, 0)
    return pl.pallas_call(
        functools.partial(_sample_attn_kernel, lam_init=lam_init),
        grid=(bsz,),
        in_specs=[pl.BlockSpec((1, HEAD_DIM), small)] * 4 + [
            pl.BlockSpec((1, V_DIM), small),
            pl.BlockSpec((n_new, ATTN_WIDTH), row),
            pl.BlockSpec((n_new * N_HEADS, V_DIM), row),
            pl.BlockSpec((n_new * N_HEADS, V_DIM), row),
            pl.BlockSpec((None, past_rows, V_DIM), lambda b: (b, 0, 0)),
            pl.BlockSpec((None, past_rows, V_DIM), lambda b: (b, 0, 0))],
        out_specs=pl.BlockSpec((n_new, ATTN_WIDTH), row),
        out_shape=jax.ShapeDtypeStruct(q.shape, BF16),
        compiler_params=_cparams("parallel"),
        name="sample_attn",
    )(*lams, gain, q, k, v, cache_k, cache_v)


def _ssm_tables(a_re, a_im, b_re, b_im, c_re, c_im, log_dt):
    g = a_re.shape[0]
    width = GROUP_SIZE * CHUNK
    state_rows = 2 * STATE_DIM
    twice = lambda v: jnp.concatenate([v, v], axis=-1)
    rows = jnp.stack([twice(a_re), twice(a_im), jnp.broadcast_to(log_dt[:, None], (g, state_rows))], axis=1)
    rows = jnp.pad(rows, ((0, 0), (0, 8 - rows.shape[1]), (0, 0)))
    per_g = lambda i: (i, 0, 0)
    small = pl.BlockSpec((None, GROUP_SIZE, state_rows), per_g)
    return pl.pallas_call(
        _ssm_tables_kernel,
        grid=(g,),
        in_specs=[pl.BlockSpec((None, 8, state_rows), per_g), small, small, small, small],
        out_specs=[pl.BlockSpec((None, GROUP_SIZE, width), per_g),
                   pl.BlockSpec((None, state_rows, width), per_g),
                   pl.BlockSpec((None, width, state_rows), per_g),
                   pl.BlockSpec((None, state_rows, LANES), per_g),
                   pl.BlockSpec((None, state_rows, LANES), per_g)],
        out_shape=[jax.ShapeDtypeStruct((g, GROUP_SIZE, width), F32),
                   jax.ShapeDtypeStruct((g, state_rows, width), BF16),
                   jax.ShapeDtypeStruct((g, width, state_rows), BF16),
                   jax.ShapeDtypeStruct((g, state_rows, LANES), F32),
                   jax.ShapeDtypeStruct((g, state_rows, LANES), F32)],
        compiler_params=_cparams("parallel"),
        name="ssm_tables",
    )(rows, twice(b_re.transpose(0, 2, 1)), twice(b_im.transpose(0, 2, 1)), twice(c_re), twice(c_im))


def _ssm_tables_kernel(rows_ref, bre_ref, bim_ref, cre_ref, cim_ref, rrev_ref, w_ref, vm_ref, ac_ref, as_ref):
    lane = lax.broadcasted_iota(jnp.int32, (1, 2 * STATE_DIM), 1)
    lo = lane < STATE_DIM
    a_re, a_im, log_dt = rows_ref[0:1, :], rows_ref[1:2, :], rows_ref[2:3, :]
    dt = jnp.exp(log_dt)
    lam_re, lam_im = a_re * dt, a_im * dt
    mag = jnp.exp(lam_re)
    ar, ai = mag * jnp.cos(lam_im), mag * jnp.sin(lam_im)
    den = jnp.square(a_re) + jnp.square(a_im)
    cr = ((ar - 1.0) * a_re + ai * a_im) / den
    ci = (ai * a_re - (ar - 1.0) * a_im) / den
    bre, bim = bre_ref[...], bim_ref[...]
    bbr = cr * bre - ci * bim
    bbi = cr * bim + ci * bre
    quarter_turn = jnp.where(lo, 0.0, 0.5 * math.pi)

    def powers(tau):
        return jnp.exp(tau * lam_re) * jnp.cos(tau * lam_im - quarter_turn)

    def outer(y, x1, x2):
        y_sw = pltpu.roll(y, STATE_DIM, 1)
        prod = y[:, None, :] * x1[None, :, :] + y_sw[:, None, :] * x2[None, :, :]
        return prod.reshape(y.shape[0] * x1.shape[0], 2 * STATE_DIM)

    frames = lax.broadcasted_iota(jnp.int32, (CHUNK, 1), 0).astype(F32)
    wt = outer(powers((CHUNK - 1.0) - frames), bbr, jnp.where(lo, -bbi, bbi))
    w = wt.T
    w_ref[...] = w.astype(BF16)
    cre, cim = cre_ref[...], cim_ref[...]
    vm_ref[...] = outer(powers(frames + 1.0), jnp.where(lo, cre, -cre), -cim).astype(BF16)
    rrev_ref[...] = jnp.dot(jnp.where(lo, cre, -cim), w, precision=lax.Precision.HIGHEST,
                            preferred_element_type=F32)
    a_row = powers(jnp.full((8, 1), float(CHUNK), F32))[0:1, :]
    a_col = jnp.broadcast_to(a_row, (2 * STATE_DIM, 2 * STATE_DIM)).T
    a_swap = pltpu.roll(a_col, STATE_DIM, 0)
    top = lax.broadcasted_iota(jnp.int32, a_col.shape, 0) < STATE_DIM
    ac_ref[...] = jnp.where(top, a_col, a_swap)
    as_ref[...] = jnp.where(top, -a_swap, a_col)


def _chunk_cols_kernel(u_ref, o_ref):
    n_s, n_r, n_j = u_ref.shape[0], u_ref.shape[1], u_ref.shape[2]
    pad = LANES - n_s * n_r
    for jj in range(n_j):
        rows = [u_ref[s, :, jj, :] for s in range(n_s)]
        if pad:
            rows.append(jnp.zeros((pad, SSM_WIDTH), F32))
        cols = jnp.concatenate(rows, axis=0).T
        o_ref[:, jj * GROUP_SIZE:(jj + 1) * GROUP_SIZE, :] = cols.reshape(N_GROUPS, GROUP_SIZE, LANES).astype(BF16)


def _chunk_cols(u4, frames_per_step=16):
    s_total, n_r = u4.shape[0], u4.shape[1]
    s_step = min(s_total, LANES // n_r)
    assert s_step >= 1 and s_total % s_step == 0
    n_lane_blocks = s_total // s_step
    return pl.pallas_call(
        _chunk_cols_kernel,
        grid=(n_lane_blocks, CHUNK // frames_per_step),
        in_specs=[pl.BlockSpec((s_step, n_r, frames_per_step, SSM_WIDTH), lambda a, j: (a, 0, j, 0))],
        out_specs=pl.BlockSpec((N_GROUPS, frames_per_step * GROUP_SIZE, LANES), lambda a, j: (0, j, a)),
        out_shape=jax.ShapeDtypeStruct((N_GROUPS, CHUNK * GROUP_SIZE, n_lane_blocks * LANES), BF16),
        compiler_params=_cparams("parallel", "parallel"),
        name="ssm_in",
    )(u4)


def _unchunk_cols_kernel(y_ref, o_ref):
    n_s, n_r, n_j = o_ref.shape[0], o_ref.shape[1], o_ref.shape[2]
    for jj in range(n_j):
        cols = y_ref[:, jj * GROUP_SIZE:(jj + 1) * GROUP_SIZE, :].reshape(SSM_WIDTH, LANES)
        rows = cols.T
        for s in range(n_s):
            o_ref[s, :, jj, :] = rows[s * n_r:(s + 1) * n_r, :]


def _unchunk_cols(y_cols, s_total, n_r, frames_per_step=16):
    s_step = min(s_total, LANES // n_r)
    n_lane_blocks = s_total // s_step
    return pl.pallas_call(
        _unchunk_cols_kernel,
        grid=(n_lane_blocks, CHUNK // frames_per_step),
        in_specs=[pl.BlockSpec((N_GROUPS, frames_per_step * GROUP_SIZE, LANES), lambda a, j: (0, j, a))],
        out_specs=pl.BlockSpec((s_step, n_r, frames_per_step, SSM_WIDTH), lambda a, j: (a, 0, j, 0)),
        out_shape=jax.ShapeDtypeStruct((s_total, n_r, CHUNK, SSM_WIDTH), F32),
        compiler_params=_cparams("parallel", "parallel"),
        name="ssm_out",
    )(y_cols)


def _ssm_kernel(*refs, n_chunk):
    for gi in range(SSM_GROUPS_PER_STEP):
        _ssm_group(*[r.at[gi] for r in refs], n_chunk=n_chunk)


def _ssm_group(rrev_ref, w_ref, vm_ref, ac_ref, as_ref, xp_ref, xs_ref, s0_ref, yp_ref, ys_ref, sp_ref, ss_ref,
               mt_scr, *, n_chunk):
    width = GROUP_SIZE * CHUNK
    n_piece = width // LANES
    lane16 = lax.broadcasted_iota(jnp.int32, (GROUP_SIZE, LANES), 1)
    pieces = [rrev_ref[:, k * LANES:(k + 1) * LANES] for k in range(n_piece)] + [jnp.zeros((GROUP_SIZE, LANES), F32)]
    rolled = {0: pieces}
    for b in range(GROUP_SIZE, LANES, GROUP_SIZE):
        rolled[b] = [pltpu.roll(p, LANES - b, 1) for p in pieces[:n_piece]] + [pieces[n_piece]]
    def toeplitz_rows(t0, t1, n_cols):
        for t in range(t0, t1):
            shift = GROUP_SIZE * (CHUNK - 1 - t)
            a, b = shift // LANES, shift % LANES
            for v in range(n_cols // LANES):
                k = v + a
                if k >= n_piece:
                    blk = pieces[n_piece]
                elif b == 0:
                    blk = pieces[k]
                else:
                    blk = jnp.where(lane16 < LANES - b, rolled[b][k], rolled[b][k + 1])
                mt_scr[t * GROUP_SIZE:(t + 1) * GROUP_SIZE, v * LANES:(v + 1) * LANES] = blk.astype(BF16)

    w = w_ref[...]
    xp, xs = xp_ref[...], xs_ref[...]
    sloc_p, sloc_s = _dot(w, xp), _dot(w, xs)
    a_c, a_s = ac_ref[...], as_ref[...]

    def cmul(pc, ps, s):
        return pc * s + ps * pltpu.roll(s, STATE_DIM, 0)

    lane = lax.broadcasted_iota(jnp.int32, (2 * STATE_DIM, LANES), 1) % n_chunk
    prev_cols = []
    lane_id = lax.broadcasted_iota(jnp.int32, (2 * STATE_DIM, LANES), 1)
    final = jnp.zeros((2 * STATE_DIM, LANES), F32)
    seqs_per_tile = LANES // n_chunk
    for tile in range(xp.shape[1] // LANES):
        s_inc = sloc_p[:, tile * LANES:(tile + 1) * LANES]
        pc, ps = a_c, a_s
        dist = 1
        while dist < n_chunk:
            shifted = jnp.where(lane >= dist, pltpu.roll(s_inc, dist, 1), 0.0)
            s_inc = s_inc + cmul(pc, ps, shifted)
            pc, ps = pc * pc - ps * ps, 2.0 * pc * ps
            dist *= 2
        for k in range(seqs_per_tile):
            src, dst = (k + 1) * n_chunk - 1, tile * seqs_per_tile + k
            final = jnp.where(lane_id == dst, pltpu.roll(s_inc, (dst - src) % LANES, 1), final)
        prev_cols.append(jnp.where(lane >= 1, pltpu.roll(s_inc, 1, 1), 0.0))
    sp_ref[...] = final
    s_prev_p = jnp.concatenate(prev_cols, axis=1).astype(BF16)
    s0 = s0_ref[...]
    s0b = s0.astype(BF16)
    frames = TOEPLITZ_ROWS // GROUP_SIZE
    for i in range(width // TOEPLITZ_ROWS):
        rows = slice(i * TOEPLITZ_ROWS, (i + 1) * TOEPLITZ_ROWS)
        n_cols = (i + 1) * TOEPLITZ_ROWS
        toeplitz_rows(i * frames, (i + 1) * frames, n_cols)
        mt = mt_scr[rows, 0:n_cols]
        vm = vm_ref[rows, :]
        yp_ref[rows, :] = _dot(mt, xp[0:n_cols, :]) + _dot(vm, s_prev_p)
        ys_ref[rows, :] = _dot(mt, xs[0:n_cols, :]) + _dot(vm, s0b)
    ss_ref[...] = cmul(a_c, a_s, s0) + sloc_s


def _ssm(tables, x_p, x_s, s0, n_chunk):
    rrev, w, vm, a_c, a_s = tables
    n_groups, lanes_p = x_p.shape[0], x_p.shape[2]
    width = GROUP_SIZE * CHUNK
    per_g = lambda g: (g, 0, 0)
    state_rows = 2 * STATE_DIM
    gb = SSM_GROUPS_PER_STEP
    assert n_groups % gb == 0
    spec = lambda rows, cols: pl.BlockSpec((gb, rows, cols), per_g)
    return pl.pallas_call(
        functools.partial(_ssm_kernel, n_chunk=n_chunk),
        grid=(n_groups // gb,),
        in_specs=[spec(GROUP_SIZE, width), spec(state_rows, width), spec(width, state_rows),
                  spec(state_rows, LANES), spec(state_rows, LANES),
                  spec(width, lanes_p), spec(width, LANES), spec(state_rows, LANES)],
        out_specs=[spec(width, lanes_p), spec(width, LANES), spec(state_rows, LANES), spec(state_rows, LANES)],
        out_shape=[jax.ShapeDtypeStruct((n_groups, width, lanes_p), F32),
                   jax.ShapeDtypeStruct((n_groups, width, LANES), F32),
                   jax.ShapeDtypeStruct((n_groups, state_rows, LANES), F32),
                   jax.ShapeDtypeStruct((n_groups, state_rows, LANES), F32)],
        scratch_shapes=[pltpu.VMEM((gb, width, width), BF16)],
        compiler_params=_cparams("parallel"),
        name="ssm",
    )(rrev, w, vm, a_c, a_s, x_p, x_s, s0)


def _layer_norm(x, g, b):
    mu = jnp.mean(x, axis=1, keepdims=True)
    xc = x - mu
    var = jnp.mean(jnp.square(xc), axis=1, keepdims=True)
    return xc * lax.rsqrt(var + LN_EPS) * g + b


def _gelu_tanh(x):
    return 0.5 * x * (1.0 + jnp.tanh(math.sqrt(2.0 / math.pi) * (x + 0.044715 * (x * x * x))))


def _merge_kernel(x_ref, ao_ref, ys_ref, u_ref, d_ref, wg_ref, wap_ref, wglu_ref, wout_ref, g1_ref, b1_ref, h_ref):
    for r in range(x_ref.shape[0] // SUB_ROWS):
        rows = slice(r * SUB_ROWS, (r + 1) * SUB_ROWS)
        x = x_ref[rows, :]
        xb = x.astype(BF16)
        a_branch = _dot(ao_ref[rows, :], wap_ref[...])
        s_act = _gelu_tanh(ys_ref[rows, :] + d_ref[...] * u_ref[rows, :]).astype(BF16)
        s_branch = (_dot(s_act, wglu_ref[:, 0:D_MODEL])
                    * jax.nn.sigmoid(_dot(s_act, wglu_ref[:, D_MODEL:2 * D_MODEL])))
        m = (jax.nn.sigmoid(_dot(xb, wg_ref[:, 0:D_MODEL])) * a_branch
             + jax.nn.sigmoid(_dot(xb, wg_ref[:, D_MODEL:2 * D_MODEL])) * s_branch)
        h_ref[rows, :] = _layer_norm(DEEPNORM_ALPHA * x + _dot(m.astype(BF16), wout_ref[...]),
                                     g1_ref[...], b1_ref[...])


def _merge(x2d, ao, ys, u, d, w_gate, w_ap, w_glu, w_out, ln_g, ln_b, tm):
    t_tokens = x2d.shape[0]
    row = lambda i: (i, 0)
    const = lambda i: (0, 0)
    return pl.pallas_call(
        _merge_kernel,
        grid=(t_tokens // tm,),
        in_specs=[pl.BlockSpec((tm, D_MODEL), row),
                  pl.BlockSpec((tm, ATTN_WIDTH), row),
                  pl.BlockSpec((tm, SSM_WIDTH), row),
                  pl.BlockSpec((tm, SSM_WIDTH), row),
                  pl.BlockSpec((1, SSM_WIDTH), const),
                  pl.BlockSpec((D_MODEL, 2 * D_MODEL), const),
                  pl.BlockSpec((ATTN_WIDTH, D_MODEL), const),
                  pl.BlockSpec((SSM_WIDTH, 2 * D_MODEL), const),
                  pl.BlockSpec((D_MODEL, D_MODEL), const),
                  pl.BlockSpec((1, D_MODEL), const),
                  pl.BlockSpec((1, D_MODEL), const)],
        out_specs=pl.BlockSpec((tm, D_MODEL), row),
        out_shape=jax.ShapeDtypeStruct((t_tokens, D_MODEL), F32),
        compiler_params=_cparams("parallel"),
        name="merge",
    )(x2d, ao, ys, u, d, w_gate, w_ap, w_glu, w_out, ln_g, ln_b)


def _mlp_kernel(h_ref, w1_ref, w2_ref, g2_ref, b2_ref, o_ref, *, ff_chunk):
    for r in range(h_ref.shape[0] // SUB_ROWS):
        rows = slice(r * SUB_ROWS, (r + 1) * SUB_ROWS)
        h = h_ref[rows, :]
        hb = h.astype(BF16)
        f = jnp.zeros(h.shape, F32)
        for c in range(D_FF // ff_chunk):
            sl = slice(c * ff_chunk, (c + 1) * ff_chunk)
            t = jnp.maximum(_dot(hb, w1_ref[:, sl]), 0.0)
            f = f + _dot((t * t).astype(BF16), w2_ref[sl, :])
        o_ref[rows, :] = _layer_norm(DEEPNORM_ALPHA * h + f, g2_ref[...], b2_ref[...])


def _mlp(h, w1, w2, ln_g, ln_b, tm, ff_chunk=1024):
    t_tokens = h.shape[0]
    row = lambda i: (i, 0)
    const = lambda i: (0, 0)
    return pl.pallas_call(
        functools.partial(_mlp_kernel, ff_chunk=ff_chunk),
        grid=(t_tokens // tm,),
        in_specs=[pl.BlockSpec((tm, D_MODEL), row),
                  pl.BlockSpec((D_MODEL, D_FF), const),
                  pl.BlockSpec((D_FF, D_MODEL), const),
                  pl.BlockSpec((1, D_MODEL), const),
                  pl.BlockSpec((1, D_MODEL), const)],
        out_specs=pl.BlockSpec((tm, D_MODEL), row),
        out_shape=jax.ShapeDtypeStruct((t_tokens, D_MODEL), F32),
        compiler_params=_cparams("parallel"),
        name="mlp",
    )(h, w1, w2, ln_g, ln_b)


def _rope_tables(pos):
    inv = 1.0 / (ROPE_THETA ** (jnp.arange(0, HEAD_DIM, 2, dtype=F32) / HEAD_DIM))
    ang = pos.astype(F32)[:, None] * inv[None, :]
    c, s = jnp.cos(ang), jnp.sin(ang)
    reps = LANES // HEAD_DIM
    return jnp.tile(jnp.concatenate([c, c], axis=1), (1, reps)), jnp.tile(jnp.concatenate([-s, s], axis=1), (1, reps))


def kernel(x_prompt, x_sample, cache_k, cache_v, state_ssm_re, state_ssm_im, w_in, lambda_q1, lambda_k1, lambda_q2, lambda_k2, subln_gain, ssm_a_re, ssm_a_im, ssm_b_re, ssm_b_im, ssm_c_re, ssm_c_im, ssm_d, ssm_log_dt, w_attn_proj, w_glu_a, w_glu_b, w_out, ln1_g, ln1_b, w_ff1, w_ff2, ln2_g, ln2_b):
    bp, n_p = x_prompt.shape[0], x_prompt.shape[1]
    bs, n_s = x_sample.shape[0], x_sample.shape[1]
    past = cache_k.shape[2]
    assert w_in.shape[0] == DEPTH and n_s == CHUNK and n_p % CHUNK == 0
    n_chunk = n_p // CHUNK
    tm = min(512, n_p)
    tm_s = min(512, bs * n_s)
    tq = min(512, n_p)
    l = 0
    lam_init = 0.8 - 0.6 * math.exp(-0.3 * l)

    xp = x_prompt.reshape(bp * n_p, D_MODEL)
    xs = x_sample.reshape(bs * n_s, D_MODEL)
    w_qkvu = w_in[l, :, 0:QKVU_COLS].astype(BF16)
    w_gate = w_in[l, :, QKVU_COLS:].astype(BF16)
    w_ap = w_attn_proj[l].astype(BF16)
    w_glu = jnp.concatenate([w_glu_a[l], w_glu_b[l]], axis=1).astype(BF16)
    w_o = w_out[l].astype(BF16)
    w1, w2 = w_ff1[l].astype(BF16), w_ff2[l].astype(BF16)
    lams = [v[l].reshape(1, HEAD_DIM) for v in (lambda_q1, lambda_k1, lambda_q2, lambda_k2)]
    gain = subln_gain[l].reshape(1, V_DIM)
    d_skip = ssm_d[l].reshape(1, SSM_WIDTH)
    lng = [v[l].reshape(1, D_MODEL) for v in (ln1_g, ln1_b, ln2_g, ln2_b)]

    cos_p, sin_p = _rope_tables(jnp.arange(n_p))
    cos_s, sin_s = _rope_tables(jnp.tile(past + jnp.arange(n_s), tm_s // n_s))

    q_p, k_p, v_p, u_p, kb_p, vt_p = _project(xp, w_qkvu, cos_p, sin_p, n_p, tm, True)
    q_s, k_s, v_s, u_s = _project(xs, w_qkvu, cos_s, sin_s, n_s, tm_s, False)

    ao_p = _prompt_attention(lams, gain, q_p, kb_p, vt_p, bp, n_p, tq, lam_init)
    ao_s = _sample_attention(lams, gain, q_s, k_s, v_s,
                             cache_k[l].reshape(bs, past * N_HEADS, V_DIM),
                             cache_v[l].reshape(bs, past * N_HEADS, V_DIM), lam_init)

    tables = _ssm_tables(ssm_a_re[l], ssm_a_im[l], ssm_b_re[l], ssm_b_im[l],
                         ssm_c_re[l], ssm_c_im[l], ssm_log_dt[l])
    x_cols_p = _chunk_cols(u_p.reshape(bp, n_chunk, CHUNK, SSM_WIDTH))
    x_cols_s = _chunk_cols(u_s.reshape(1, bs, CHUNK, SSM_WIDTH))
    s0 = jnp.concatenate([state_ssm_re[l], state_ssm_im[l]], axis=-1).transpose(1, 2, 0)
    s0 = jnp.pad(s0, ((0, 0), (0, 0), (0, LANES - bs)))
    y_cols_p, y_cols_s, st_p, st_s = _ssm(tables, x_cols_p, x_cols_s, s0, n_chunk)
    ys_p = _unchunk_cols(y_cols_p, bp, n_chunk).reshape(bp * n_p, SSM_WIDTH)
    ys_s = _unchunk_cols(y_cols_s, 1, bs).reshape(bs * n_s, SSM_WIDTH)
    sf_p = st_p[:, :, 0:bp]
    sf_s = st_s[:, :, 0:bs]

    outs = []
    for x2d, ao, ys, u, tile in ((xp, ao_p, ys_p, u_p, tm), (xs, ao_s, ys_s, u_s, tm_s)):
        h = _merge(x2d, ao, ys, u, d_skip, w_gate, w_ap, w_glu, w_o, lng[0], lng[1], tile)
        outs.append(_mlp(h, w1, w2, lng[2], lng[3], tile))

    def states(sf):
        t = sf.transpose(2, 0, 1)
        return t[None, :, :, 0:STATE_DIM], t[None, :, :, STATE_DIM:]

    srp, sip = states(sf_p)
    srs, sis = states(sf_s)
    return (outs[0].reshape(bp, n_p, D_MODEL), outs[1].reshape(bs, n_s, D_MODEL),
            k_p.reshape(1, bp, n_p, N_HEADS, V_DIM), v_p.reshape(1, bp, n_p, N_HEADS, V_DIM), srp, sip,
            k_s.reshape(1, bs, n_s, N_HEADS, V_DIM), v_s.reshape(1, bs, n_s, N_HEADS, V_DIM), srs, sis)
```

```python
import functools
import math

import jax
import jax.numpy as jnp
from jax import lax
from jax.experimental import pallas as pl
from jax.experimental.pallas import tpu as pltpu

D_MODEL = 1024
CHUNK = 64
N_HEADS = 4
HEAD_DIM = 64
V_DIM = 2 * HEAD_DIM
ATTN_WIDTH = N_HEADS * V_DIM
SSM_WIDTH = 512
GROUP_SIZE = 16
N_GROUPS = SSM_WIDTH // GROUP_SIZE
STATE_DIM = 64
D_FF = 4 * D_MODEL
ROPE_THETA = 10000.0
LN_EPS = 1e-5
RMS_EPS = 1e-5
NEG_INF = -1e30
DEPTH = 1
DEEPNORM_ALPHA = (2.0 * DEPTH) ** 0.25
QKVU_COLS = 3 * ATTN_WIDTH + SSM_WIDTH
LOG2E = 1.4426950408889634

LANES = 128
VT_ROWS = V_DIM + 16
QUERY_LANES = 256
TOEPLITZ_ROWS = 256
SUB_ROWS = 256
SSM_GROUPS_PER_STEP = 4
SAMPLE_STREAMS_PER_STEP = 2
VMEM_LIMIT = 56 * 1024 * 1024

F32 = jnp.float32
BF16 = jnp.bfloat16


def _cparams(*sem):
    return pltpu.CompilerParams(dimension_semantics=sem, vmem_limit_bytes=VMEM_LIMIT)


def _nt_dot(a, b):
    return lax.dot_general(a, b, (((1,), (1,)), ((), ())), preferred_element_type=F32)


def _dot(a, b):
    return jnp.dot(a, b, preferred_element_type=F32)


def _rotary(z, cos, sin_signed, first_half):
    swapped = jnp.where(first_half, pltpu.roll(z, 96, 1), pltpu.roll(z, 32, 1))
    return z * cos + swapped * sin_signed


def _proj_kernel(x_ref, w_ref, cos_ref, sin_ref, q_ref, k_ref, v_ref, u_ref, *rest, emit_t):
    xb = x_ref[...].astype(BF16)
    cos = cos_ref[...]
    sin = sin_ref[...]
    lane = lax.broadcasted_iota(jnp.int32, cos.shape, 1)
    first_half = (lane % HEAD_DIM) < (HEAD_DIM // 2)
    tm = xb.shape[0]
    zq = _dot(xb, w_ref[:, 0:ATTN_WIDTH])
    zk = _dot(xb, w_ref[:, ATTN_WIDTH:2 * ATTN_WIDTH])
    zv = _dot(xb, w_ref[:, 2 * ATTN_WIDTH:3 * ATTN_WIDTH])
    for h in range(N_HEADS):
        sl = slice(h * V_DIM, (h + 1) * V_DIM)
        q_ref[:, sl] = (_rotary(zq[:, sl], cos, sin, first_half) * (LOG2E * HEAD_DIM ** -0.5)).astype(BF16)
        kr = _rotary(zk[:, sl], cos, sin, first_half)
        k_ref[pl.ds(h, tm, stride=N_HEADS), :] = kr
        v_ref[pl.ds(h, tm, stride=N_HEADS), :] = zv[:, sl]
        if emit_t:
            rest[0][:, sl] = kr.astype(BF16)
    if emit_t:
        vt_ref = rest[1]
        zvt = zv.T.astype(BF16)
        ones = jnp.ones((VT_ROWS - V_DIM, zvt.shape[1]), BF16)
        for h in range(N_HEADS):
            vt_ref[h, 0:V_DIM, :] = zvt[h * V_DIM:(h + 1) * V_DIM, :]
            vt_ref[h, V_DIM:VT_ROWS, :] = ones
    u_ref[...] = _dot(xb, w_ref[:, 3 * ATTN_WIDTH:QKVU_COLS])


def _project(x2d, w_qkvu, cos_t, sin_t, seq_len, tm, emit_t):
    t_tokens = x2d.shape[0]
    n_tiles = t_tokens // tm
    n_pos_tiles = cos_t.shape[0] // tm
    tiles_per_seq = max(seq_len // tm, 1)
    row = lambda i: (i, 0)
    pos = lambda i: (i % n_pos_tiles, 0)
    out_shape = [jax.ShapeDtypeStruct((t_tokens, ATTN_WIDTH), BF16),
                 jax.ShapeDtypeStruct((t_tokens * N_HEADS, V_DIM), F32),
                 jax.ShapeDtypeStruct((t_tokens * N_HEADS, V_DIM), F32),
                 jax.ShapeDtypeStruct((t_tokens, SSM_WIDTH), F32)]
    out_specs = ([pl.BlockSpec((tm, ATTN_WIDTH), row)] + [pl.BlockSpec((tm * N_HEADS, V_DIM), row)] * 2
                 + [pl.BlockSpec((tm, SSM_WIDTH), row)])
    if emit_t:
        bsz = t_tokens // seq_len
        out_shape += [jax.ShapeDtypeStruct((t_tokens, ATTN_WIDTH), BF16),
                      jax.ShapeDtypeStruct((bsz, N_HEADS, VT_ROWS, seq_len), BF16)]
        out_specs += [pl.BlockSpec((tm, ATTN_WIDTH), row),
                      pl.BlockSpec((None, N_HEADS, VT_ROWS, tm),
                                   lambda i: (i // tiles_per_seq, 0, 0, i % tiles_per_seq))]
    return pl.pallas_call(
        functools.partial(_proj_kernel, emit_t=emit_t),
        grid=(n_tiles,),
        in_specs=[pl.BlockSpec((tm, D_MODEL), row),
                  pl.BlockSpec((D_MODEL, QKVU_COLS), lambda i: (0, 0)),
                  pl.BlockSpec((tm, LANES), pos),
                  pl.BlockSpec((tm, LANES), pos)],
        out_specs=out_specs,
        out_shape=out_shape,
        compiler_params=_cparams("parallel"),
        name="proj_t" if emit_t else "proj",
    )(x2d, w_qkvu, cos_t, sin_t)


def _diff_lambda(lq1, lk1, lq2, lk2, lam_init):
    return (jnp.exp(jnp.sum(lq1 * lk1, axis=1, keepdims=True))
            - jnp.exp(jnp.sum(lq2 * lk2, axis=1, keepdims=True)) + lam_init)


def _sub_norm(d, gain, lam_init):
    ms = jnp.mean(jnp.square(d), axis=1, keepdims=True)
    return d * lax.rsqrt(ms + RMS_EPS) * gain * (1.0 - lam_init)


def _stack_maps(q):
    lane = lax.broadcasted_iota(jnp.int32, q.shape, 1)
    zero = jnp.zeros_like(q)
    return jnp.concatenate([jnp.where(lane < HEAD_DIM, q, zero), jnp.where(lane >= HEAD_DIM, q, zero)], axis=0)


def _prompt_attn_kernel(item_q_ref, item_k_ref, lq1_ref, lk1_ref, lq2_ref, lk2_ref, gain_ref, q_ref, k_ref, vt_ref,
                        o_ref, m_scr, acc_scr, qx_scr, s0_scr, s1_scr, mx0_scr, mx1_scr, *, tq, n_items, lam_init):
    nq = q_ref.shape[0] // tq
    for i in range(nq):
        qx_scr[i * 2 * tq:(i + 1) * 2 * tq, :] = _stack_maps(q_ref[i * tq:(i + 1) * tq, :])
    acc_scr[...] = jnp.zeros(acc_scr.shape, F32)
    m_scr[...] = jnp.full(m_scr.shape, NEG_INF, F32)
    chains = [slice(c * QUERY_LANES, (c + 1) * QUERY_LANES) for c in range(2 * tq // QUERY_LANES)]

    def scores(w, s_scr, mx_scr):
        qi, j = item_q_ref[w], item_k_ref[w]
        kt = k_ref[pl.ds(pl.multiple_of(j * tq, tq), tq), :]
        for cs in chains:
            qx = qx_scr[pl.ds(pl.multiple_of(qi * 2 * tq + cs.start, QUERY_LANES), QUERY_LANES), :]
            st = _nt_dot(kt, qx)
            s_scr[:, cs] = st
            mx_scr[:, cs] = jnp.max(st, axis=0, keepdims=True)

    def softmax_pv(j, s_scr, mx_scr, diagonal):
        start = pl.multiple_of(j * tq, tq)
        for cs in chains:
            if diagonal:
                n_keys = min(tq, cs.start % tq + QUERY_LANES)
                lane = lax.broadcasted_iota(jnp.int32, (CHUNK, QUERY_LANES), 1)
                blocks = []
                for kc in range(n_keys // CHUNK):
                    blk = s_scr[kc * CHUNK:(kc + 1) * CHUNK, cs]
                    first_visible = kc * CHUNK - cs.start % tq
                    if first_visible > 0:
                        blk = jnp.where(lane >= first_visible, blk, NEG_INF)
                    blocks.append(blk)
                st = jnp.concatenate(blocks, axis=0)
                tile_max = jnp.max(st, axis=0, keepdims=True)
            else:
                n_keys = tq
                st = s_scr[:, cs]
                tile_max = mx_scr[:, cs]
            vt = vt_ref[:, pl.ds(start, n_keys)]
            m_old = jnp.where(j == 0, NEG_INF, m_scr[:, cs])
            m_new = jnp.maximum(m_old, tile_max)
            alpha = jnp.exp2(m_old - m_new)
            p = jnp.exp2(st - m_new).astype(BF16)
            acc_scr[:, cs] = acc_scr[:, cs] * alpha + _dot(vt, p)
            m_scr[:, cs] = m_new

    def finish(qi):
        acc = acc_scr[...]
        o = acc[0:V_DIM, :] * (1.0 / acc[V_DIM:V_DIM + 1, :])
        lam = _diff_lambda(lq1_ref[...], lk1_ref[...], lq2_ref[...], lk2_ref[...], lam_init)
        d = (o[:, 0:tq] - lam * o[:, tq:2 * tq]).T
        o_ref[pl.ds(pl.multiple_of(qi * tq, tq), tq), :] = _sub_norm(d, gain_ref[...], lam_init).astype(BF16)

    def pair_block(w, diag0, diag1):
        scores(w + 1, s1_scr, mx1_scr)
        softmax_pv(item_k_ref[w], s0_scr, mx0_scr, diag0)
        if diag0:
            finish(item_q_ref[w])
        scores(w + 2, s0_scr, mx0_scr)
        softmax_pv(item_k_ref[w + 1], s1_scr, mx1_scr, diag1)
        if diag1:
            finish(item_q_ref[w + 1])

    scores(0, s0_scr, mx0_scr)

    def pair(i, carry):
        w = 2 * i
        d0 = item_q_ref[w] == item_k_ref[w]
        d1 = item_q_ref[w + 1] == item_k_ref[w + 1]
        for diag0 in (False, True):
            for diag1 in (False, True):
                pl.when(jnp.logical_and(d0 == diag0, d1 == diag1))(
                    functools.partial(pair_block, w, diag0, diag1))
        return carry

    lax.fori_loop(0, n_items // 2, pair, 0)


def _prompt_attention(lams, gain, q, kb, vt, bsz, seq_len, tq, lam_init):
    nq = seq_len // tq
    items = [(qi, j) for qi in range(nq) for j in range(qi + 1)]
    n_items = len(items)
    assert n_items % 2 == 0
    items.append(items[-1])
    items.append(items[-1])
    item_q = jnp.asarray([it[0] for it in items], jnp.int32)
    item_k = jnp.asarray([it[1] for it in items], jnp.int32)
    small = lambda b, h, iq, ik: (0, 0)
    grid_spec = pltpu.PrefetchScalarGridSpec(
        num_scalar_prefetch=2,
        grid=(bsz, N_HEADS),
        in_specs=[pl.BlockSpec((1, HEAD_DIM), small)] * 4 + [
            pl.BlockSpec((1, V_DIM), small),
            pl.BlockSpec((seq_len, V_DIM), lambda b, h, iq, ik: (b, h)),
            pl.BlockSpec((seq_len, V_DIM), lambda b, h, iq, ik: (b, h)),
            pl.BlockSpec((None, None, VT_ROWS, seq_len), lambda b, h, iq, ik: (b, h, 0, 0))],
        out_specs=pl.BlockSpec((seq_len, V_DIM), lambda b, h, iq, ik: (b, h)),
        scratch_shapes=[pltpu.VMEM((1, 2 * tq), F32), pltpu.VMEM((VT_ROWS, 2 * tq), F32),
                        pltpu.VMEM((nq * 2 * tq, V_DIM), BF16),
                        pltpu.VMEM((tq, 2 * tq), F32), pltpu.VMEM((tq, 2 * tq), F32),
                        pltpu.VMEM((1, 2 * tq), F32), pltpu.VMEM((1, 2 * tq), F32)])
    return pl.pallas_call(
        functools.partial(_prompt_attn_kernel, tq=tq, n_items=n_items, lam_init=lam_init),
        grid_spec=grid_spec,
        out_shape=jax.ShapeDtypeStruct((bsz * seq_len, ATTN_WIDTH), BF16),
        compiler_params=_cparams("parallel", "parallel"),
        name="prompt_attn",
    )(item_q, item_k, *lams, gain, q, kb, vt)


def _sample_attn_kernel(lq1_ref, lk1_ref, lq2_ref, lk2_ref, gain_ref, q_ref, k_ref, v_ref, ck_ref, cv_ref,
                        o_ref, *, lam_init):
    n_streams = ck_ref.shape[0]
    n_new = q_ref.shape[0] // n_streams
    past = ck_ref.shape[1] // N_HEADS
    lam = _diff_lambda(lq1_ref[...], lk1_ref[...], lq2_ref[...], lk2_ref[...], lam_init)
    gain = gain_ref[...]
    for s in range(n_streams):
        rows = slice(s * n_new, (s + 1) * n_new)
        for h in range(N_HEADS):
            sl = slice(h * V_DIM, (h + 1) * V_DIM)
            old = pl.ds(h, past, stride=N_HEADS)
            new = pl.ds(s * n_new * N_HEADS + h, n_new, stride=N_HEADS)
            qx = _stack_maps(q_ref[rows, sl])
            s_c = _nt_dot(qx, ck_ref[s, old, :].astype(BF16))
            s_n = _nt_dot(qx, k_ref[new, :].astype(BF16))
            m = jnp.maximum(jnp.max(s_c, axis=1, keepdims=True), jnp.max(s_n, axis=1, keepdims=True))
            p_c = jnp.exp2(s_c - m)
            p_n = jnp.exp2(s_n - m)
            denom = jnp.sum(p_c, axis=1, keepdims=True) + jnp.sum(p_n, axis=1, keepdims=True)
            o = (_dot(p_c.astype(BF16), cv_ref[s, old, :].astype(BF16))
                 + _dot(p_n.astype(BF16), v_ref[new, :].astype(BF16))) * (1.0 / denom)
            d = o[0:n_new, :] - lam * o[n_new:2 * n_new, :]
            o_ref[rows, sl] = _sub_norm(d, gain, lam_init).astype(BF16)


def _sample_attention(lams, gain, q, k, v, cache_k, cache_v, lam_init):
    bsz, past_rows = cache_k.shape[0], cache_k.shape[1]
    n_new = q.shape[0] // bsz
    ns = SAMPLE_STREAMS_PER_STEP if bsz % SAMPLE_STREAMS_PER_STEP == 0 else 1
    small = lambda b: (0, 0)
    row = lambda b: (b, 0)
    return pl.pallas_call(
        functools.partial(_sample_attn_kernel, lam_init=lam_init),
        grid=(bsz // ns,),
        in_specs=[pl.BlockSpec((1, HEAD_DIM), small)] * 4 + [
            pl.BlockSpec((1, V_DIM), small),
            pl.BlockSpec((ns * n_new, ATTN_WIDTH), row),
            pl.BlockSpec((ns * n_new * N_HEADS, V_DIM), row),
            pl.BlockSpec((ns * n_new * N_HEADS, V_DIM), row),
            pl.BlockSpec((ns, past_rows, V_DIM), lambda b: (b, 0, 0)),
            pl.BlockSpec((ns, past_rows, V_DIM), lambda b: (b, 0, 0))],
        out_specs=pl.BlockSpec((ns * n_new, ATTN_WIDTH), row),
        out_shape=jax.ShapeDtypeStruct(q.shape, BF16),
        compiler_params=_cparams("parallel"),
        name="sample_attn",
    )(*lams, gain, q, k, v, cache_k, cache_v)


def _ssm_tables(a_re, a_im, b_re, b_im, c_re, c_im, log_dt):
    g = a_re.shape[0]
    width = GROUP_SIZE * CHUNK
    state_rows = 2 * STATE_DIM
    twice = lambda v: jnp.concatenate([v, v], axis=-1)
    rows = jnp.stack([twice(a_re), twice(a_im), jnp.broadcast_to(log_dt[:, None], (g, state_rows))], axis=1)
    rows = jnp.pad(rows, ((0, 0), (0, 8 - rows.shape[1]), (0, 0)))
    per_g = lambda i: (i, 0, 0)
    small = pl.BlockSpec((None, GROUP_SIZE, state_rows), per_g)
    return pl.pallas_call(
        _ssm_tables_kernel,
        grid=(g,),
        in_specs=[pl.BlockSpec((None, 8, state_rows), per_g), small, small, small, small],
        out_specs=[pl.BlockSpec((None, GROUP_SIZE, width), per_g),
                   pl.BlockSpec((None, state_rows, width), per_g),
                   pl.BlockSpec((None, width, state_rows), per_g),
                   pl.BlockSpec((None, state_rows, LANES), per_g),
                   pl.BlockSpec((None, state_rows, LANES), per_g)],
        out_shape=[jax.ShapeDtypeStruct((g, GROUP_SIZE, width), F32),
                   jax.ShapeDtypeStruct((g, state_rows, width), BF16),
                   jax.ShapeDtypeStruct((g, width, state_rows), BF16),
                   jax.ShapeDtypeStruct((g, state_rows, LANES), F32),
                   jax.ShapeDtypeStruct((g, state_rows, LANES), F32)],
        compiler_params=_cparams("parallel"),
        name="ssm_tables",
    )(rows, twice(b_re.transpose(0, 2, 1)), twice(b_im.transpose(0, 2, 1)), twice(c_re), twice(c_im))


def _ssm_tables_kernel(rows_ref, bre_ref, bim_ref, cre_ref, cim_ref, rrev_ref, w_ref, vm_ref, ac_ref, as_ref):
    lane = lax.broadcasted_iota(jnp.int32, (1, 2 * STATE_DIM), 1)
    lo = lane < STATE_DIM
    a_re, a_im, log_dt = rows_ref[0:1, :], rows_ref[1:2, :], rows_ref[2:3, :]
    dt = jnp.exp(log_dt)
    lam_re, lam_im = a_re * dt, a_im * dt
    mag = jnp.exp(lam_re)
    ar, ai = mag * jnp.cos(lam_im), mag * jnp.sin(lam_im)
    den = jnp.square(a_re) + jnp.square(a_im)
    cr = ((ar - 1.0) * a_re + ai * a_im) / den
    ci = (ai * a_re - (ar - 1.0) * a_im) / den
    bre, bim = bre_ref[...], bim_ref[...]
    bbr = cr * bre - ci * bim
    bbi = cr * bim + ci * bre
    quarter_turn = jnp.where(lo, 0.0, 0.5 * math.pi)

    def powers(tau):
        return jnp.exp(tau * lam_re) * jnp.cos(tau * lam_im - quarter_turn)

    def outer(y, x1, x2):
        y_sw = pltpu.roll(y, STATE_DIM, 1)
        prod = y[:, None, :] * x1[None, :, :] + y_sw[:, None, :] * x2[None, :, :]
        return prod.reshape(y.shape[0] * x1.shape[0], 2 * STATE_DIM)

    frames = lax.broadcasted_iota(jnp.int32, (CHUNK, 1), 0).astype(F32)
    wt = outer(powers((CHUNK - 1.0) - frames), bbr, jnp.where(lo, -bbi, bbi))
    w = wt.T
    w_ref[...] = w.astype(BF16)
    cre, cim = cre_ref[...], cim_ref[...]
    vm_ref[...] = outer(powers(frames + 1.0), jnp.where(lo, cre, -cre), -cim).astype(BF16)
    rrev_ref[...] = jnp.dot(jnp.where(lo, cre, -cim), w, precision=lax.Precision.HIGHEST,
                            preferred_element_type=F32)
    a_row = powers(jnp.full((8, 1), float(CHUNK), F32))[0:1, :]
    a_col = jnp.broadcast_to(a_row, (2 * STATE_DIM, 2 * STATE_DIM)).T
    a_swap = pltpu.roll(a_col, STATE_DIM, 0)
    top = lax.broadcasted_iota(jnp.int32, a_col.shape, 0) < STATE_DIM
    ac_ref[...] = jnp.where(top, a_col, a_swap)
    as_ref[...] = jnp.where(top, -a_swap, a_col)


def _chunk_cols_kernel(u_ref, o_ref):
    n_s, n_r, n_j = u_ref.shape[0], u_ref.shape[1], u_ref.shape[2]
    pad = LANES - n_s * n_r
    for jj in range(n_j):
        rows = [u_ref[s, :, jj, :] for s in range(n_s)]
        if pad:
            rows.append(jnp.zeros((pad, SSM_WIDTH), F32))
        cols = jnp.concatenate(rows, axis=0).T
        o_ref[:, jj * GROUP_SIZE:(jj + 1) * GROUP_SIZE, :] = cols.reshape(N_GROUPS, GROUP_SIZE, LANES).astype(BF16)


def _chunk_cols(u4, frames_per_step=16):
    s_total, n_r = u4.shape[0], u4.shape[1]
    s_step = min(s_total, LANES // n_r)
    assert s_step >= 1 and s_total % s_step == 0
    n_lane_blocks = s_total // s_step
    return pl.pallas_call(
        _chunk_cols_kernel,
        grid=(n_lane_blocks, CHUNK // frames_per_step),
        in_specs=[pl.BlockSpec((s_step, n_r, frames_per_step, SSM_WIDTH), lambda a, j: (a, 0, j, 0))],
        out_specs=pl.BlockSpec((N_GROUPS, frames_per_step * GROUP_SIZE, LANES), lambda a, j: (0, j, a)),
        out_shape=jax.ShapeDtypeStruct((N_GROUPS, CHUNK * GROUP_SIZE, n_lane_blocks * LANES), BF16),
        compiler_params=_cparams("parallel", "parallel"),
        name="ssm_in",
    )(u4)


def _unchunk_cols_kernel(y_ref, o_ref):
    n_s, n_r, n_j = o_ref.shape[0], o_ref.shape[1], o_ref.shape[2]
    for jj in range(n_j):
        cols = y_ref[:, jj * GROUP_SIZE:(jj + 1) * GROUP_SIZE, :].reshape(SSM_WIDTH, LANES)
        rows = cols.T
        for s in range(n_s):
            o_ref[s, :, jj, :] = rows[s * n_r:(s + 1) * n_r, :]


def _unchunk_cols(y_cols, s_total, n_r, frames_per_step=16):
    s_step = min(s_total, LANES // n_r)
    n_lane_blocks = s_total // s_step
    return pl.pallas_call(
        _unchunk_cols_kernel,
        grid=(n_lane_blocks, CHUNK // frames_per_step),
        in_specs=[pl.BlockSpec((N_GROUPS, frames_per_step * GROUP_SIZE, LANES), lambda a, j: (0, j, a))],
        out_specs=pl.BlockSpec((s_step, n_r, frames_per_step, SSM_WIDTH), lambda a, j: (a, 0, j, 0)),
        out_shape=jax.ShapeDtypeStruct((s_total, n_r, CHUNK, SSM_WIDTH), F32),
        compiler_params=_cparams("parallel", "parallel"),
        name="ssm_out",
    )(y_cols)


def _ssm_kernel(*refs, n_chunk):
    for gi in range(SSM_GROUPS_PER_STEP):
        _ssm_group(*[r.at[gi] for r in refs], n_chunk=n_chunk)


def _ssm_group(rrev_ref, w_ref, vm_ref, ac_ref, as_ref, xp_ref, xs_ref, s0_ref, yp_ref, ys_ref, sp_ref, ss_ref,
               mt_scr, *, n_chunk):
    width = GROUP_SIZE * CHUNK
    n_piece = width // LANES
    lane16 = lax.broadcasted_iota(jnp.int32, (GROUP_SIZE, LANES), 1)
    pieces = [rrev_ref[:, k * LANES:(k + 1) * LANES] for k in range(n_piece)] + [jnp.zeros((GROUP_SIZE, LANES), F32)]
    rolled = {0: pieces}
    for b in range(GROUP_SIZE, LANES, GROUP_SIZE):
        rolled[b] = [pltpu.roll(p, LANES - b, 1) for p in pieces[:n_piece]] + [pieces[n_piece]]
    def toeplitz_rows(t0, t1, n_cols):
        for t in range(t0, t1):
            shift = GROUP_SIZE * (CHUNK - 1 - t)
            a, b = shift // LANES, shift % LANES
            for v in range(n_cols // LANES):
                k = v + a
                if k >= n_piece:
                    blk = pieces[n_piece]
                elif b == 0:
                    blk = pieces[k]
                else:
                    blk = jnp.where(lane16 < LANES - b, rolled[b][k], rolled[b][k + 1])
                mt_scr[t * GROUP_SIZE:(t + 1) * GROUP_SIZE, v * LANES:(v + 1) * LANES] = blk.astype(BF16)

    w = w_ref[...]
    xp, xs = xp_ref[...], xs_ref[...]
    sloc_p, sloc_s = _dot(w, xp), _dot(w, xs)
    a_c, a_s = ac_ref[...], as_ref[...]

    def cmul(pc, ps, s):
        return pc * s + ps * pltpu.roll(s, STATE_DIM, 0)

    lane = lax.broadcasted_iota(jnp.int32, (2 * STATE_DIM, LANES), 1) % n_chunk
    prev_cols = []
    lane_id = lax.broadcasted_iota(jnp.int32, (2 * STATE_DIM, LANES), 1)
    final = jnp.zeros((2 * STATE_DIM, LANES), F32)
    seqs_per_tile = LANES // n_chunk
    for tile in range(xp.shape[1] // LANES):
        s_inc = sloc_p[:, tile * LANES:(tile + 1) * LANES]
        pc, ps = a_c, a_s
        dist = 1
        while dist < n_chunk:
            shifted = jnp.where(lane >= dist, pltpu.roll(s_inc, dist, 1), 0.0)
            s_inc = s_inc + cmul(pc, ps, shifted)
            pc, ps = pc * pc - ps * ps, 2.0 * pc * ps
            dist *= 2
        for k in range(seqs_per_tile):
            src, dst = (k + 1) * n_chunk - 1, tile * seqs_per_tile + k
            final = jnp.where(lane_id == dst, pltpu.roll(s_inc, (dst - src) % LANES, 1), final)
        prev_cols.append(jnp.where(lane >= 1, pltpu.roll(s_inc, 1, 1), 0.0))
    sp_ref[...] = final
    s_prev_p = jnp.concatenate(prev_cols, axis=1).astype(BF16)
    s0 = s0_ref[...]
    s0b = s0.astype(BF16)
    frames = TOEPLITZ_ROWS // GROUP_SIZE
    for i in range(width // TOEPLITZ_ROWS):
        rows = slice(i * TOEPLITZ_ROWS, (i + 1) * TOEPLITZ_ROWS)
        n_cols = (i + 1) * TOEPLITZ_ROWS
        toeplitz_rows(i * frames, (i + 1) * frames, n_cols)
        mt = mt_scr[rows, 0:n_cols]
        vm = vm_ref[rows, :]
        yp_ref[rows, :] = _dot(mt, xp[0:n_cols, :]) + _dot(vm, s_prev_p)
        ys_ref[rows, :] = _dot(mt, xs[0:n_cols, :]) + _dot(vm, s0b)
    ss_ref[...] = cmul(a_c, a_s, s0) + sloc_s


def _ssm(tables, x_p, x_s, s0, n_chunk):
    rrev, w, vm, a_c, a_s = tables
    n_groups, lanes_p = x_p.shape[0], x_p.shape[2]
    width = GROUP_SIZE * CHUNK
    per_g = lambda g: (g, 0, 0)
    state_rows = 2 * STATE_DIM
    gb = SSM_GROUPS_PER_STEP
    assert n_groups % gb == 0
    spec = lambda rows, cols: pl.BlockSpec((gb, rows, cols), per_g)
    return pl.pallas_call(
        functools.partial(_ssm_kernel, n_chunk=n_chunk),
        grid=(n_groups // gb,),
        in_specs=[spec(GROUP_SIZE, width), spec(state_rows, width), spec(width, state_rows),
                  spec(state_rows, LANES), spec(state_rows, LANES),
                  spec(width, lanes_p), spec(width, LANES), spec(state_rows, LANES)],
        out_specs=[spec(width, lanes_p), spec(width, LANES), spec(state_rows, LANES), spec(state_rows, LANES)],
        out_shape=[jax.ShapeDtypeStruct((n_groups, width, lanes_p), F32),
                   jax.ShapeDtypeStruct((n_groups, width, LANES), F32),
                   jax.ShapeDtypeStruct((n_groups, state_rows, LANES), F32),
                   jax.ShapeDtypeStruct((n_groups, state_rows, LANES), F32)],
        scratch_shapes=[pltpu.VMEM((gb, width, width), BF16)],
        compiler_params=_cparams("parallel"),
        name="ssm",
    )(rrev, w, vm, a_c, a_s, x_p, x_s, s0)


def _layer_norm(x, g, b):
    mu = jnp.mean(x, axis=1, keepdims=True)
    xc = x - mu
    var = jnp.mean(jnp.square(xc), axis=1, keepdims=True)
    return xc * lax.rsqrt(var + LN_EPS) * g + b


def _gelu_tanh(x):
    return 0.5 * x * (1.0 + jnp.tanh(math.sqrt(2.0 / math.pi) * (x + 0.044715 * (x * x * x))))


def _merge_kernel(x_ref, ao_ref, ys_ref, u_ref, d_ref, wg_ref, wap_ref, wglu_ref, wout_ref, g1_ref, b1_ref, h_ref):
    for r in range(x_ref.shape[0] // SUB_ROWS):
        rows = slice(r * SUB_ROWS, (r + 1) * SUB_ROWS)
        x = x_ref[rows, :]
        xb = x.astype(BF16)
        a_branch = _dot(ao_ref[rows, :], wap_ref[...])
        s_act = _gelu_tanh(ys_ref[rows, :] + d_ref[...] * u_ref[rows, :]).astype(BF16)
        s_branch = (_dot(s_act, wglu_ref[:, 0:D_MODEL])
                    * jax.nn.sigmoid(_dot(s_act, wglu_ref[:, D_MODEL:2 * D_MODEL])))
        m = (jax.nn.sigmoid(_dot(xb, wg_ref[:, 0:D_MODEL])) * a_branch
             + jax.nn.sigmoid(_dot(xb, wg_ref[:, D_MODEL:2 * D_MODEL])) * s_branch)
        h_ref[rows, :] = _layer_norm(DEEPNORM_ALPHA * x + _dot(m.astype(BF16), wout_ref[...]),
                                     g1_ref[...], b1_ref[...])


def _merge(x2d, ao, ys, u, d, w_gate, w_ap, w_glu, w_out, ln_g, ln_b, tm):
    t_tokens = x2d.shape[0]
    row = lambda i: (i, 0)
    const = lambda i: (0, 0)
    return pl.pallas_call(
        _merge_kernel,
        grid=(t_tokens // tm,),
        in_specs=[pl.BlockSpec((tm, D_MODEL), row),
                  pl.BlockSpec((tm, ATTN_WIDTH), row),
                  pl.BlockSpec((tm, SSM_WIDTH), row),
                  pl.BlockSpec((tm, SSM_WIDTH), row),
                  pl.BlockSpec((1, SSM_WIDTH), const),
                  pl.BlockSpec((D_MODEL, 2 * D_MODEL), const, pipeline_mode=pl.Buffered(1)),
                  pl.BlockSpec((ATTN_WIDTH, D_MODEL), const, pipeline_mode=pl.Buffered(1)),
                  pl.BlockSpec((SSM_WIDTH, 2 * D_MODEL), const, pipeline_mode=pl.Buffered(1)),
                  pl.BlockSpec((D_MODEL, D_MODEL), const, pipeline_mode=pl.Buffered(1)),
                  pl.BlockSpec((1, D_MODEL), const),
                  pl.BlockSpec((1, D_MODEL), const)],
        out_specs=pl.BlockSpec((tm, D_MODEL), row),
        out_shape=jax.ShapeDtypeStruct((t_tokens, D_MODEL), F32),
        compiler_params=_cparams("parallel"),
        name="merge",
    )(x2d, ao, ys, u, d, w_gate, w_ap, w_glu, w_out, ln_g, ln_b)


def _mlp_kernel(h_ref, w1_ref, w2_ref, g2_ref, b2_ref, o_ref, *, ff_chunk):
    for r in range(h_ref.shape[0] // SUB_ROWS):
        rows = slice(r * SUB_ROWS, (r + 1) * SUB_ROWS)
        h = h_ref[rows, :]
        hb = h.astype(BF16)
        f = jnp.zeros(h.shape, F32)
        for c in range(D_FF // ff_chunk):
            sl = slice(c * ff_chunk, (c + 1) * ff_chunk)
            t = jnp.maximum(_dot(hb, w1_ref[:, sl]), 0.0)
            f = f + _dot((t * t).astype(BF16), w2_ref[sl, :])
        o_ref[rows, :] = _layer_norm(DEEPNORM_ALPHA * h + f, g2_ref[...], b2_ref[...])


def _mlp(h, w1, w2, ln_g, ln_b, tm, ff_chunk=1024):
    t_tokens = h.shape[0]
    row = lambda i: (i, 0)
    const = lambda i: (0, 0)
    return pl.pallas_call(
        functools.partial(_mlp_kernel, ff_chunk=ff_chunk),
        grid=(t_tokens // tm,),
        in_specs=[pl.BlockSpec((tm, D_MODEL), row),
                  pl.BlockSpec((D_MODEL, D_FF), const, pipeline_mode=pl.Buffered(1)),
                  pl.BlockSpec((D_FF, D_MODEL), const, pipeline_mode=pl.Buffered(1)),
                  pl.BlockSpec((1, D_MODEL), const),
                  pl.BlockSpec((1, D_MODEL), const)],
        out_specs=pl.BlockSpec((tm, D_MODEL), row),
        out_shape=jax.ShapeDtypeStruct((t_tokens, D_MODEL), F32),
        compiler_params=_cparams("parallel"),
        name="mlp",
    )(h, w1, w2, ln_g, ln_b)


def _rope_tables(pos):
    inv = 1.0 / (ROPE_THETA ** (jnp.arange(0, HEAD_DIM, 2, dtype=F32) / HEAD_DIM))
    ang = pos.astype(F32)[:, None] * inv[None, :]
    c, s = jnp.cos(ang), jnp.sin(ang)
    reps = LANES // HEAD_DIM
    return jnp.tile(jnp.concatenate([c, c], axis=1), (1, reps)), jnp.tile(jnp.concatenate([-s, s], axis=1), (1, reps))


def kernel(x_prompt, x_sample, cache_k, cache_v, state_ssm_re, state_ssm_im, w_in, lambda_q1, lambda_k1, lambda_q2, lambda_k2, subln_gain, ssm_a_re, ssm_a_im, ssm_b_re, ssm_b_im, ssm_c_re, ssm_c_im, ssm_d, ssm_log_dt, w_attn_proj, w_glu_a, w_glu_b, w_out, ln1_g, ln1_b, w_ff1, w_ff2, ln2_g, ln2_b):
    bp, n_p = x_prompt.shape[0], x_prompt.shape[1]
    bs, n_s = x_sample.shape[0], x_sample.shape[1]
    past = cache_k.shape[2]
    assert w_in.shape[0] == DEPTH and n_s == CHUNK and n_p % CHUNK == 0
    n_chunk = n_p // CHUNK
    tm = min(512, n_p)
    tm_s = min(512, bs * n_s)
    tq = min(512, n_p)
    l = 0
    lam_init = 0.8 - 0.6 * math.exp(-0.3 * l)

    xp = x_prompt.reshape(bp * n_p, D_MODEL)
    xs = x_sample.reshape(bs * n_s, D_MODEL)
    w_qkvu = w_in[l, :, 0:QKVU_COLS].astype(BF16)
    w_gate = w_in[l, :, QKVU_COLS:].astype(BF16)
    w_ap = w_attn_proj[l].astype(BF16)
    w_glu = jnp.concatenate([w_glu_a[l], w_glu_b[l]], axis=1).astype(BF16)
    w_o = w_out[l].astype(BF16)
    w1, w2 = w_ff1[l].astype(BF16), w_ff2[l].astype(BF16)
    lams = [v[l].reshape(1, HEAD_DIM) for v in (lambda_q1, lambda_k1, lambda_q2, lambda_k2)]
    gain = subln_gain[l].reshape(1, V_DIM)
    d_skip = ssm_d[l].reshape(1, SSM_WIDTH)
    lng = [v[l].reshape(1, D_MODEL) for v in (ln1_g, ln1_b, ln2_g, ln2_b)]

    cos_p, sin_p = _rope_tables(jnp.arange(n_p))
    cos_s, sin_s = _rope_tables(jnp.tile(past + jnp.arange(n_s), tm_s // n_s))

    q_p, k_p, v_p, u_p, kb_p, vt_p = _project(xp, w_qkvu, cos_p, sin_p, n_p, tm, True)
    q_s, k_s, v_s, u_s = _project(xs, w_qkvu, cos_s, sin_s, n_s, tm_s, False)

    ao_p = _prompt_attention(lams, gain, q_p, kb_p, vt_p, bp, n_p, tq, lam_init)
    ao_s = _sample_attention(lams, gain, q_s, k_s, v_s,
                             cache_k[l].reshape(bs, past * N_HEADS, V_DIM),
                             cache_v[l].reshape(bs, past * N_HEADS, V_DIM), lam_init)

    tables = _ssm_tables(ssm_a_re[l], ssm_a_im[l], ssm_b_re[l], ssm_b_im[l],
                         ssm_c_re[l], ssm_c_im[l], ssm_log_dt[l])
    x_cols_p = _chunk_cols(u_p.reshape(bp, n_chunk, CHUNK, SSM_WIDTH))
    x_cols_s = _chunk_cols(u_s.reshape(1, bs, CHUNK, SSM_WIDTH))
    s0 = jnp.concatenate([state_ssm_re[l], state_ssm_im[l]], axis=-1).transpose(1, 2, 0)
    s0 = jnp.pad(s0, ((0, 0), (0, 0), (0, LANES - bs)))
    y_cols_p, y_cols_s, st_p, st_s = _ssm(tables, x_cols_p, x_cols_s, s0, n_chunk)
    ys_p = _unchunk_cols(y_cols_p, bp, n_chunk).reshape(bp * n_p, SSM_WIDTH)
    ys_s = _unchunk_cols(y_cols_s, 1, bs).reshape(bs * n_s, SSM_WIDTH)
    sf_p = st_p[:, :, 0:bp]
    sf_s = st_s[:, :, 0:bs]

    outs = []
    for x2d, ao, ys, u, tile in ((xp, ao_p, ys_p, u_p, tm), (xs, ao_s, ys_s, u_s, tm_s)):
        big = min(2 * tile, x2d.shape[0])
        h = _merge(x2d, ao, ys, u, d_skip, w_gate, w_ap, w_glu, w_o, lng[0], lng[1], big)
        outs.append(_mlp(h, w1, w2, lng[2], lng[3], big))

    def states(sf):
        t = sf.transpose(2, 0, 1)
        return t[None, :, :, 0:STATE_DIM], t[None, :, :, STATE_DIM:]

    srp, sip = states(sf_p)
    srs, sis = states(sf_s)
    return (outs[0].reshape(bp, n_p, D_MODEL), outs[1].reshape(bs, n_s, D_MODEL),
            k_p.reshape(1, bp, n_p, N_HEADS, V_DIM), v_p.reshape(1, bp, n_p, N_HEADS, V_DIM), srp, sip,
            k_s.reshape(1, bs, n_s, N_HEADS, V_DIM), v_s.reshape(1, bs, n_s, N_HEADS, V_DIM), srs, sis)
```

```python
import functools
import math

import jax
import jax.numpy as jnp
from jax import lax
from jax.experimental import pallas as pl
from jax.experimental.pallas import tpu as pltpu

D_MODEL = 1024
CHUNK = 64
N_HEADS = 4
HEAD_DIM = 64
V_DIM = 2 * HEAD_DIM
ATTN_WIDTH = N_HEADS * V_DIM
SSM_WIDTH = 512
GROUP_SIZE = 16
N_GROUPS = SSM_WIDTH // GROUP_SIZE
STATE_DIM = 64
D_FF = 4 * D_MODEL
ROPE_THETA = 10000.0
LN_EPS = 1e-5
RMS_EPS = 1e-5
NEG_INF = -1e30
DEPTH = 1
DEEPNORM_ALPHA = (2.0 * DEPTH) ** 0.25
QKVU_COLS = 3 * ATTN_WIDTH + SSM_WIDTH
LOG2E = 1.4426950408889634

LANES = 128
VT_ROWS = V_DIM + 16
QUERY_LANES = 256
TOEPLITZ_ROWS = 256
SUB_ROWS = 256
SSM_GROUPS_PER_STEP = 4
SAMPLE_STREAMS_PER_STEP = 2
VMEM_LIMIT = 56 * 1024 * 1024

F32 = jnp.float32
BF16 = jnp.bfloat16


def _cparams(*sem):
    return pltpu.CompilerParams(dimension_semantics=sem, vmem_limit_bytes=VMEM_LIMIT)


def _nt_dot(a, b):
    return lax.dot_general(a, b, (((1,), (1,)), ((), ())), preferred_element_type=F32)


def _dot(a, b):
    return jnp.dot(a, b, preferred_element_type=F32)


def _rotary(z, cos, sin_signed, first_half):
    swapped = jnp.where(first_half, pltpu.roll(z, 96, 1), pltpu.roll(z, 32, 1))
    return z * cos + swapped * sin_signed


def _proj_kernel(x_ref, w_ref, cos_ref, sin_ref, q_ref, k_ref, v_ref, u_ref, *rest, emit_t):
    xb = x_ref[...].astype(BF16)
    cos = cos_ref[...]
    sin = sin_ref[...]
    lane = lax.broadcasted_iota(jnp.int32, cos.shape, 1)
    first_half = (lane % HEAD_DIM) < (HEAD_DIM // 2)
    tm = xb.shape[0]
    zq = _dot(xb, w_ref[:, 0:ATTN_WIDTH])
    zk = _dot(xb, w_ref[:, ATTN_WIDTH:2 * ATTN_WIDTH])
    zv = _dot(xb, w_ref[:, 2 * ATTN_WIDTH:3 * ATTN_WIDTH])
    for h in range(N_HEADS):
        sl = slice(h * V_DIM, (h + 1) * V_DIM)
        q_ref[:, sl] = (_rotary(zq[:, sl], cos, sin, first_half) * (LOG2E * HEAD_DIM ** -0.5)).astype(BF16)
        kr = _rotary(zk[:, sl], cos, sin, first_half)
        k_ref[pl.ds(h, tm, stride=N_HEADS), :] = kr
        v_ref[pl.ds(h, tm, stride=N_HEADS), :] = zv[:, sl]
        if emit_t:
            rest[0][:, sl] = kr.astype(BF16)
    if emit_t:
        vt_ref = rest[1]
        zvt = zv.T.astype(BF16)
        ones = jnp.ones((VT_ROWS - V_DIM, zvt.shape[1]), BF16)
        for h in range(N_HEADS):
            vt_ref[h, 0:V_DIM, :] = zvt[h * V_DIM:(h + 1) * V_DIM, :]
            vt_ref[h, V_DIM:VT_ROWS, :] = ones
    u_ref[...] = _dot(xb, w_ref[:, 3 * ATTN_WIDTH:QKVU_COLS])


def _project(x2d, w_qkvu, cos_t, sin_t, seq_len, tm, emit_t):
    t_tokens = x2d.shape[0]
    n_tiles = t_tokens // tm
    n_pos_tiles = cos_t.shape[0] // tm
    tiles_per_seq = max(seq_len // tm, 1)
    row = lambda i: (i, 0)
    pos = lambda i: (i % n_pos_tiles, 0)
    out_shape = [jax.ShapeDtypeStruct((t_tokens, ATTN_WIDTH), BF16),
                 jax.ShapeDtypeStruct((t_tokens * N_HEADS, V_DIM), F32),
                 jax.ShapeDtypeStruct((t_tokens * N_HEADS, V_DIM), F32),
                 jax.ShapeDtypeStruct((t_tokens, SSM_WIDTH), F32)]
    out_specs = ([pl.BlockSpec((tm, ATTN_WIDTH), row)] + [pl.BlockSpec((tm * N_HEADS, V_DIM), row)] * 2
                 + [pl.BlockSpec((tm, SSM_WIDTH), row)])
    if emit_t:
        bsz = t_tokens // seq_len
        out_shape += [jax.ShapeDtypeStruct((t_tokens, ATTN_WIDTH), BF16),
                      jax.ShapeDtypeStruct((bsz, N_HEADS, VT_ROWS, seq_len), BF16)]
        out_specs += [pl.BlockSpec((tm, ATTN_WIDTH), row),
                      pl.BlockSpec((None, N_HEADS, VT_ROWS, tm),
                                   lambda i: (i // tiles_per_seq, 0, 0, i % tiles_per_seq))]
    return pl.pallas_call(
        functools.partial(_proj_kernel, emit_t=emit_t),
        grid=(n_tiles,),
        in_specs=[pl.BlockSpec((tm, D_MODEL), row),
                  pl.BlockSpec((D_MODEL, QKVU_COLS), lambda i: (0, 0)),
                  pl.BlockSpec((tm, LANES), pos),
                  pl.BlockSpec((tm, LANES), pos)],
        out_specs=out_specs,
        out_shape=out_shape,
        compiler_params=_cparams("parallel"),
        name="proj_t" if emit_t else "proj",
    )(x2d, w_qkvu, cos_t, sin_t)


def _diff_lambda(lq1, lk1, lq2, lk2, lam_init):
    return (jnp.exp(jnp.sum(lq1 * lk1, axis=1, keepdims=True))
            - jnp.exp(jnp.sum(lq2 * lk2, axis=1, keepdims=True)) + lam_init)


def _sub_norm(d, gain, lam_init):
    ms = jnp.mean(jnp.square(d), axis=1, keepdims=True)
    return d * lax.rsqrt(ms + RMS_EPS) * gain * (1.0 - lam_init)


def _stack_maps(q):
    lane = lax.broadcasted_iota(jnp.int32, q.shape, 1)
    zero = jnp.zeros_like(q)
    return jnp.concatenate([jnp.where(lane < HEAD_DIM, q, zero), jnp.where(lane >= HEAD_DIM, q, zero)], axis=0)


def _prompt_attn_kernel(lq1_ref, lk1_ref, lq2_ref, lk2_ref, gain_ref, q_ref, k_ref, vt_ref,
                        o_ref, m_scr, acc_scr, qx_scr, s0_scr, s1_scr, mx0_scr, mx1_scr, *, tq, lam_init):
    nq = q_ref.shape[0] // tq
    items = [(qi, j) for qi in range(nq) for j in range(qi + 1)]
    for i in range(nq):
        qx_scr[i * 2 * tq:(i + 1) * 2 * tq, :] = _stack_maps(q_ref[i * tq:(i + 1) * tq, :])
    chains = [slice(c * QUERY_LANES, (c + 1) * QUERY_LANES) for c in range(2 * tq // QUERY_LANES)]

    def scores(qi, j, s_scr, mx_scr):
        kt = k_ref[j * tq:(j + 1) * tq, :]
        for cs in chains:
            qx = qx_scr[qi * 2 * tq + cs.start:qi * 2 * tq + cs.stop, :]
            st = _nt_dot(kt, qx)
            s_scr[:, cs] = st
            mx_scr[:, cs] = jnp.max(st, axis=0, keepdims=True)

    def softmax_pv(j, s_scr, mx_scr, diagonal):
        start = j * tq
        for cs in chains:
            if diagonal:
                n_keys = min(tq, cs.start % tq + QUERY_LANES)
                lane = lax.broadcasted_iota(jnp.int32, (CHUNK, QUERY_LANES), 1)
                blocks = []
                for kc in range(n_keys // CHUNK):
                    blk = s_scr[kc * CHUNK:(kc + 1) * CHUNK, cs]
                    first_visible = kc * CHUNK - cs.start % tq
                    if first_visible > 0:
                        blk = jnp.where(lane >= first_visible, blk, NEG_INF)
                    blocks.append(blk)
                st = jnp.concatenate(blocks, axis=0)
                tile_max = jnp.max(st, axis=0, keepdims=True)
            else:
                n_keys = tq
                st = s_scr[:, cs]
                tile_max = mx_scr[:, cs]
            vt = vt_ref[:, start:start + n_keys]
            p_scale = None
            if j == 0:
                m_new = tile_max
            else:
                m_old = m_scr[:, cs]
                m_new = jnp.maximum(m_old, tile_max)
                p_scale = jnp.exp2(m_old - m_new)
            p = jnp.exp2(st - m_new).astype(BF16)
            pv = _dot(vt, p)
            acc_scr[:, cs] = pv if p_scale is None else acc_scr[:, cs] * p_scale + pv
            m_scr[:, cs] = m_new

    def finish(qi):
        acc = acc_scr[...]
        o = acc[0:V_DIM, :] * (1.0 / acc[V_DIM:V_DIM + 1, :])
        lam = _diff_lambda(lq1_ref[...], lk1_ref[...], lq2_ref[...], lk2_ref[...], lam_init)
        d = (o[:, 0:tq] - lam * o[:, tq:2 * tq]).T
        o_ref[qi * tq:(qi + 1) * tq, :] = _sub_norm(d, gain_ref[...], lam_init).astype(BF16)

    bufs = ((s0_scr, mx0_scr), (s1_scr, mx1_scr))
    scores(*items[0], *bufs[0])
    for w, (qi, j) in enumerate(items):
        if w + 1 < len(items):
            scores(*items[w + 1], *bufs[(w + 1) % 2])
        softmax_pv(j, *bufs[w % 2], j == qi)
        if j == qi:
            finish(qi)


def _prompt_attention(lams, gain, q, kb, vt, bsz, seq_len, tq, lam_init):
    nq = seq_len // tq
    small = lambda b, h: (0, 0)
    return pl.pallas_call(
        functools.partial(_prompt_attn_kernel, tq=tq, lam_init=lam_init),
        grid=(bsz, N_HEADS),
        in_specs=[pl.BlockSpec((1, HEAD_DIM), small)] * 4 + [
            pl.BlockSpec((1, V_DIM), small),
            pl.BlockSpec((seq_len, V_DIM), lambda b, h: (b, h)),
            pl.BlockSpec((seq_len, V_DIM), lambda b, h: (b, h)),
            pl.BlockSpec((None, None, VT_ROWS, seq_len), lambda b, h: (b, h, 0, 0))],
        out_specs=pl.BlockSpec((seq_len, V_DIM), lambda b, h: (b, h)),
        out_shape=jax.ShapeDtypeStruct((bsz * seq_len, ATTN_WIDTH), BF16),
        scratch_shapes=[pltpu.VMEM((1, 2 * tq), F32), pltpu.VMEM((VT_ROWS, 2 * tq), F32),
                        pltpu.VMEM((nq * 2 * tq, V_DIM), BF16),
                        pltpu.VMEM((tq, 2 * tq), F32), pltpu.VMEM((tq, 2 * tq), F32),
                        pltpu.VMEM((1, 2 * tq), F32), pltpu.VMEM((1, 2 * tq), F32)],
        compiler_params=_cparams("parallel", "parallel"),
        name="prompt_attn",
    )(*lams, gain, q, kb, vt)


def _sample_attn_kernel(lq1_ref, lk1_ref, lq2_ref, lk2_ref, gain_ref, q_ref, k_ref, v_ref, ck_ref, cv_ref,
                        o_ref, *, lam_init):
    n_streams = ck_ref.shape[0]
    n_new = q_ref.shape[0] // n_streams
    past = ck_ref.shape[1] // N_HEADS
    lam = _diff_lambda(lq1_ref[...], lk1_ref[...], lq2_ref[...], lk2_ref[...], lam_init)
    gain = gain_ref[...]
    for s in range(n_streams):
        rows = slice(s * n_new, (s + 1) * n_new)
        for h in range(N_HEADS):
            sl = slice(h * V_DIM, (h + 1) * V_DIM)
            old = pl.ds(h, past, stride=N_HEADS)
            new = pl.ds(s * n_new * N_HEADS + h, n_new, stride=N_HEADS)
            qx = _stack_maps(q_ref[rows, sl])
            s_c = _nt_dot(qx, ck_ref[s, old, :].astype(BF16))
            s_n = _nt_dot(qx, k_ref[new, :].astype(BF16))
            m = jnp.maximum(jnp.max(s_c, axis=1, keepdims=True), jnp.max(s_n, axis=1, keepdims=True))
            p_c = jnp.exp2(s_c - m)
            p_n = jnp.exp2(s_n - m)
            denom = jnp.sum(p_c, axis=1, keepdims=True) + jnp.sum(p_n, axis=1, keepdims=True)
            o = (_dot(p_c.astype(BF16), cv_ref[s, old, :].astype(BF16))
                 + _dot(p_n.astype(BF16), v_ref[new, :].astype(BF16))) * (1.0 / denom)
            d = o[0:n_new, :] - lam * o[n_new:2 * n_new, :]
            o_ref[rows, sl] = _sub_norm(d, gain, lam_init).astype(BF16)


def _sample_attention(lams, gain, q, k, v, cache_k, cache_v, lam_init):
    bsz, past_rows = cache_k.shape[0], cache_k.shape[1]
    n_new = q.shape[0] // bsz
    ns = SAMPLE_STREAMS_PER_STEP if bsz % SAMPLE_STREAMS_PER_STEP == 0 else 1
    small = lambda b: (0, 0)
    row = lambda b: (b, 0)
    return pl.pallas_call(
        functools.partial(_sample_attn_kernel, lam_init=lam_init),
        grid=(bsz // ns,),
        in_specs=[pl.BlockSpec((1, HEAD_DIM), small)] * 4 + [
            pl.BlockSpec((1, V_DIM), small),
            pl.BlockSpec((ns * n_new, ATTN_WIDTH), row),
            pl.BlockSpec((ns * n_new * N_HEADS, V_DIM), row),
            pl.BlockSpec((ns * n_new * N_HEADS, V_DIM), row),
            pl.BlockSpec((ns, past_rows, V_DIM), lambda b: (b, 0, 0)),
            pl.BlockSpec((ns, past_rows, V_DIM), lambda b: (b, 0, 0))],
        out_specs=pl.BlockSpec((ns * n_new, ATTN_WIDTH), row),
        out_shape=jax.ShapeDtypeStruct(q.shape, BF16),
        compiler_params=_cparams("parallel"),
        name="sample_attn",
    )(*lams, gain, q, k, v, cache_k, cache_v)


def _ssm_tables(a_re, a_im, b_re, b_im, c_re, c_im, log_dt):
    g = a_re.shape[0]
    width = GROUP_SIZE * CHUNK
    state_rows = 2 * STATE_DIM
    twice = lambda v: jnp.concatenate([v, v], axis=-1)
    rows = jnp.stack([twice(a_re), twice(a_im), jnp.broadcast_to(log_dt[:, None], (g, state_rows))], axis=1)
    rows = jnp.pad(rows, ((0, 0), (0, 8 - rows.shape[1]), (0, 0)))
    per_g = lambda i: (i, 0, 0)
    small = pl.BlockSpec((None, GROUP_SIZE, state_rows), per_g)
    return pl.pallas_call(
        _ssm_tables_kernel,
        grid=(g,),
        in_specs=[pl.BlockSpec((None, 8, state_rows), per_g), small, small, small, small],
        out_specs=[pl.BlockSpec((None, GROUP_SIZE, width), per_g),
                   pl.BlockSpec((None, state_rows, width), per_g),
                   pl.BlockSpec((None, width, state_rows), per_g),
                   pl.BlockSpec((None, state_rows, LANES), per_g),
                   pl.BlockSpec((None, state_rows, LANES), per_g)],
        out_shape=[jax.ShapeDtypeStruct((g, GROUP_SIZE, width), F32),
                   jax.ShapeDtypeStruct((g, state_rows, width), BF16),
                   jax.ShapeDtypeStruct((g, width, state_rows), BF16),
                   jax.ShapeDtypeStruct((g, state_rows, LANES), F32),
                   jax.ShapeDtypeStruct((g, state_rows, LANES), F32)],
        compiler_params=_cparams("parallel"),
        name="ssm_tables",
    )(rows, twice(b_re.transpose(0, 2, 1)), twice(b_im.transpose(0, 2, 1)), twice(c_re), twice(c_im))


def _ssm_tables_kernel(rows_ref, bre_ref, bim_ref, cre_ref, cim_ref, rrev_ref, w_ref, vm_ref, ac_ref, as_ref):
    lane = lax.broadcasted_iota(jnp.int32, (1, 2 * STATE_DIM), 1)
    lo = lane < STATE_DIM
    a_re, a_im, log_dt = rows_ref[0:1, :], rows_ref[1:2, :], rows_ref[2:3, :]
    dt = jnp.exp(log_dt)
    lam_re, lam_im = a_re * dt, a_im * dt
    mag = jnp.exp(lam_re)
    ar, ai = mag * jnp.cos(lam_im), mag * jnp.sin(lam_im)
    den = jnp.square(a_re) + jnp.square(a_im)
    cr = ((ar - 1.0) * a_re + ai * a_im) / den
    ci = (ai * a_re - (ar - 1.0) * a_im) / den
    bre, bim = bre_ref[...], bim_ref[...]
    bbr = cr * bre - ci * bim
    bbi = cr * bim + ci * bre
    quarter_turn = jnp.where(lo, 0.0, 0.5 * math.pi)

    def powers(tau):
        return jnp.exp(tau * lam_re) * jnp.cos(tau * lam_im - quarter_turn)

    def outer(y, x1, x2):
        y_sw = pltpu.roll(y, STATE_DIM, 1)
        prod = y[:, None, :] * x1[None, :, :] + y_sw[:, None, :] * x2[None, :, :]
        return prod.reshape(y.shape[0] * x1.shape[0], 2 * STATE_DIM)

    frames = lax.broadcasted_iota(jnp.int32, (CHUNK, 1), 0).astype(F32)
    wt = outer(powers((CHUNK - 1.0) - frames), bbr, jnp.where(lo, -bbi, bbi))
    w = wt.T
    w_ref[...] = w.astype(BF16)
    cre, cim = cre_ref[...], cim_ref[...]
    vm_ref[...] = outer(powers(frames + 1.0), jnp.where(lo, cre, -cre), -cim).astype(BF16)
    rrev_ref[...] = jnp.dot(jnp.where(lo, cre, -cim), w, precision=lax.Precision.HIGHEST,
                            preferred_element_type=F32)
    a_row = powers(jnp.full((8, 1), float(CHUNK), F32))[0:1, :]
    a_col = jnp.broadcast_to(a_row, (2 * STATE_DIM, 2 * STATE_DIM)).T
    a_swap = pltpu.roll(a_col, STATE_DIM, 0)
    top = lax.broadcasted_iota(jnp.int32, a_col.shape, 0) < STATE_DIM
    ac_ref[...] = jnp.where(top, a_col, a_swap)
    as_ref[...] = jnp.where(top, -a_swap, a_col)


def _chunk_cols_kernel(u_ref, o_ref):
    n_s, n_r, n_j = u_ref.shape[0], u_ref.shape[1], u_ref.shape[2]
    pad = LANES - n_s * n_r
    for jj in range(n_j):
        rows = [u_ref[s, :, jj, :] for s in range(n_s)]
        if pad:
            rows.append(jnp.zeros((pad, SSM_WIDTH), F32))
        cols = jnp.concatenate(rows, axis=0).T
        o_ref[:, jj * GROUP_SIZE:(jj + 1) * GROUP_SIZE, :] = cols.reshape(N_GROUPS, GROUP_SIZE, LANES).astype(BF16)


def _chunk_cols(u4, frames_per_step=16):
    s_total, n_r = u4.shape[0], u4.shape[1]
    s_step = min(s_total, LANES // n_r)
    assert s_step >= 1 and s_total % s_step == 0
    n_lane_blocks = s_total // s_step
    return pl.pallas_call(
        _chunk_cols_kernel,
        grid=(n_lane_blocks, CHUNK // frames_per_step),
        in_specs=[pl.BlockSpec((s_step, n_r, frames_per_step, SSM_WIDTH), lambda a, j: (a, 0, j, 0))],
        out_specs=pl.BlockSpec((N_GROUPS, frames_per_step * GROUP_SIZE, LANES), lambda a, j: (0, j, a)),
        out_shape=jax.ShapeDtypeStruct((N_GROUPS, CHUNK * GROUP_SIZE, n_lane_blocks * LANES), BF16),
        compiler_params=_cparams("parallel", "parallel"),
        name="ssm_in",
    )(u4)


def _unchunk_cols_kernel(y_ref, o_ref):
    n_s, n_r, n_j = o_ref.shape[0], o_ref.shape[1], o_ref.shape[2]
    for jj in range(n_j):
        cols = y_ref[:, jj * GROUP_SIZE:(jj + 1) * GROUP_SIZE, :].reshape(SSM_WIDTH, LANES)
        rows = cols.T
        for s in range(n_s):
            o_ref[s, :, jj, :] = rows[s * n_r:(s + 1) * n_r, :]


def _unchunk_cols(y_cols, s_total, n_r, frames_per_step=16):
    s_step = min(s_total, LANES // n_r)
    n_lane_blocks = s_total // s_step
    return pl.pallas_call(
        _unchunk_cols_kernel,
        grid=(n_lane_blocks, CHUNK // frames_per_step),
        in_specs=[pl.BlockSpec((N_GROUPS, frames_per_step * GROUP_SIZE, LANES), lambda a, j: (0, j, a))],
        out_specs=pl.BlockSpec((s_step, n_r, frames_per_step, SSM_WIDTH), lambda a, j: (a, 0, j, 0)),
        out_shape=jax.ShapeDtypeStruct((s_total, n_r, CHUNK, SSM_WIDTH), F32),
        compiler_params=_cparams("parallel", "parallel"),
        name="ssm_out",
    )(y_cols)


def _ssm_kernel(*refs, n_chunk):
    for gi in range(SSM_GROUPS_PER_STEP):
        _ssm_group(*[r.at[gi] for r in refs], n_chunk=n_chunk)


def _ssm_group(rrev_ref, w_ref, vm_ref, ac_ref, as_ref, xp_ref, xs_ref, s0_ref, yp_ref, ys_ref, sp_ref, ss_ref,
               mt_scr, *, n_chunk):
    width = GROUP_SIZE * CHUNK
    n_piece = width // LANES
    lane16 = lax.broadcasted_iota(jnp.int32, (GROUP_SIZE, LANES), 1)
    pieces = [rrev_ref[:, k * LANES:(k + 1) * LANES] for k in range(n_piece)] + [jnp.zeros((GROUP_SIZE, LANES), F32)]
    rolled = {0: pieces}
    for b in range(GROUP_SIZE, LANES, GROUP_SIZE):
        rolled[b] = [pltpu.roll(p, LANES - b, 1) for p in pieces[:n_piece]] + [pieces[n_piece]]
    def toeplitz_rows(t0, t1, n_cols):
        for t in range(t0, t1):
            shift = GROUP_SIZE * (CHUNK - 1 - t)
            a, b = shift // LANES, shift % LANES
            for v in range(n_cols // LANES):
                k = v + a
                if k >= n_piece:
                    blk = pieces[n_piece]
                elif b == 0:
                    blk = pieces[k]
                else:
                    blk = jnp.where(lane16 < LANES - b, rolled[b][k], rolled[b][k + 1])
                mt_scr[t * GROUP_SIZE:(t + 1) * GROUP_SIZE, v * LANES:(v + 1) * LANES] = blk.astype(BF16)

    w = w_ref[...]
    xp, xs = xp_ref[...], xs_ref[...]
    sloc_p, sloc_s = _dot(w, xp), _dot(w, xs)
    a_c, a_s = ac_ref[...], as_ref[...]

    def cmul(pc, ps, s):
        return pc * s + ps * pltpu.roll(s, STATE_DIM, 0)

    lane = lax.broadcasted_iota(jnp.int32, (2 * STATE_DIM, LANES), 1) % n_chunk
    prev_cols = []
    lane_id = lax.broadcasted_iota(jnp.int32, (2 * STATE_DIM, LANES), 1)
    final = jnp.zeros((2 * STATE_DIM, LANES), F32)
    seqs_per_tile = LANES // n_chunk
    for tile in range(xp.shape[1] // LANES):
        s_inc = sloc_p[:, tile * LANES:(tile + 1) * LANES]
        pc, ps = a_c, a_s
        dist = 1
        while dist < n_chunk:
            shifted = jnp.where(lane >= dist, pltpu.roll(s_inc, dist, 1), 0.0)
            s_inc = s_inc + cmul(pc, ps, shifted)
            pc, ps = pc * pc - ps * ps, 2.0 * pc * ps
            dist *= 2
        for k in range(seqs_per_tile):
            src, dst = (k + 1) * n_chunk - 1, tile * seqs_per_tile + k
            final = jnp.where(lane_id == dst, pltpu.roll(s_inc, (dst - src) % LANES, 1), final)
        prev_cols.append(jnp.where(lane >= 1, pltpu.roll(s_inc, 1, 1), 0.0))
    sp_ref[...] = final
    s_prev_p = jnp.concatenate(prev_cols, axis=1).astype(BF16)
    s0 = s0_ref[...]
    s0b = s0.astype(BF16)
    frames = TOEPLITZ_ROWS // GROUP_SIZE
    for i in range(width // TOEPLITZ_ROWS):
        rows = slice(i * TOEPLITZ_ROWS, (i + 1) * TOEPLITZ_ROWS)
        n_cols = (i + 1) * TOEPLITZ_ROWS
        toeplitz_rows(i * frames, (i + 1) * frames, n_cols)
        mt = mt_scr[rows, 0:n_cols]
        vm = vm_ref[rows, :]
        yp_ref[rows, :] = _dot(mt, xp[0:n_cols, :]) + _dot(vm, s_prev_p)
        ys_ref[rows, :] = _dot(mt, xs[0:n_cols, :]) + _dot(vm, s0b)
    ss_ref[...] = cmul(a_c, a_s, s0) + sloc_s


def _ssm(tables, x_p, x_s, s0, n_chunk):
    rrev, w, vm, a_c, a_s = tables
    n_groups, lanes_p = x_p.shape[0], x_p.shape[2]
    width = GROUP_SIZE * CHUNK
    per_g = lambda g: (g, 0, 0)
    state_rows = 2 * STATE_DIM
    gb = SSM_GROUPS_PER_STEP
    assert n_groups % gb == 0
    spec = lambda rows, cols: pl.BlockSpec((gb, rows, cols), per_g)
    return pl.pallas_call(
        functools.partial(_ssm_kernel, n_chunk=n_chunk),
        grid=(n_groups // gb,),
        in_specs=[spec(GROUP_SIZE, width), spec(state_rows, width), spec(width, state_rows),
                  spec(state_rows, LANES), spec(state_rows, LANES),
                  spec(width, lanes_p), spec(width, LANES), spec(state_rows, LANES)],
        out_specs=[spec(width, lanes_p), spec(width, LANES), spec(state_rows, LANES), spec(state_rows, LANES)],
        out_shape=[jax.ShapeDtypeStruct((n_groups, width, lanes_p), F32),
                   jax.ShapeDtypeStruct((n_groups, width, LANES), F32),
                   jax.ShapeDtypeStruct((n_groups, state_rows, LANES), F32),
                   jax.ShapeDtypeStruct((n_groups, state_rows, LANES), F32)],
        scratch_shapes=[pltpu.VMEM((gb, width, width), BF16)],
        compiler_params=_cparams("parallel"),
        name="ssm",
    )(rrev, w, vm, a_c, a_s, x_p, x_s, s0)


def _layer_norm(x, g, b):
    mu = jnp.mean(x, axis=1, keepdims=True)
    xc = x - mu
    var = jnp.mean(jnp.square(xc), axis=1, keepdims=True)
    return xc * lax.rsqrt(var + LN_EPS) * g + b


def _gelu_tanh(x):
    return 0.5 * x * (1.0 + jnp.tanh(math.sqrt(2.0 / math.pi) * (x + 0.044715 * (x * x * x))))


def _merge_kernel(x_ref, ao_ref, ys_ref, u_ref, d_ref, wg_ref, wap_ref, wglu_ref, wout_ref, g1_ref, b1_ref, h_ref):
    for r in range(x_ref.shape[0] // SUB_ROWS):
        rows = slice(r * SUB_ROWS, (r + 1) * SUB_ROWS)
        x = x_ref[rows, :]
        xb = x.astype(BF16)
        a_branch = _dot(ao_ref[rows, :], wap_ref[...])
        s_act = _gelu_tanh(ys_ref[rows, :] + d_ref[...] * u_ref[rows, :]).astype(BF16)
        s_branch = (_dot(s_act, wglu_ref[:, 0:D_MODEL])
                    * jax.nn.sigmoid(_dot(s_act, wglu_ref[:, D_MODEL:2 * D_MODEL])))
        m = (jax.nn.sigmoid(_dot(xb, wg_ref[:, 0:D_MODEL])) * a_branch
             + jax.nn.sigmoid(_dot(xb, wg_ref[:, D_MODEL:2 * D_MODEL])) * s_branch)
        h_ref[rows, :] = _layer_norm(DEEPNORM_ALPHA * x + _dot(m.astype(BF16), wout_ref[...]),
                                     g1_ref[...], b1_ref[...])


def _merge(x2d, ao, ys, u, d, w_gate, w_ap, w_glu, w_out, ln_g, ln_b, tm):
    t_tokens = x2d.shape[0]
    row = lambda i: (i, 0)
    const = lambda i: (0, 0)
    return pl.pallas_call(
        _merge_kernel,
        grid=(t_tokens // tm,),
        in_specs=[pl.BlockSpec((tm, D_MODEL), row),
                  pl.BlockSpec((tm, ATTN_WIDTH), row),
                  pl.BlockSpec((tm, SSM_WIDTH), row),
                  pl.BlockSpec((tm, SSM_WIDTH), row),
                  pl.BlockSpec((1, SSM_WIDTH), const),
                  pl.BlockSpec((D_MODEL, 2 * D_MODEL), const, pipeline_mode=pl.Buffered(1)),
                  pl.BlockSpec((ATTN_WIDTH, D_MODEL), const, pipeline_mode=pl.Buffered(1)),
                  pl.BlockSpec((SSM_WIDTH, 2 * D_MODEL), const, pipeline_mode=pl.Buffered(1)),
                  pl.BlockSpec((D_MODEL, D_MODEL), const, pipeline_mode=pl.Buffered(1)),
                  pl.BlockSpec((1, D_MODEL), const),
                  pl.BlockSpec((1, D_MODEL), const)],
        out_specs=pl.BlockSpec((tm, D_MODEL), row),
        out_shape=jax.ShapeDtypeStruct((t_tokens, D_MODEL), F32),
        compiler_params=_cparams("parallel"),
        name="merge",
    )(x2d, ao, ys, u, d, w_gate, w_ap, w_glu, w_out, ln_g, ln_b)


def _mlp_kernel(h_ref, w1_ref, w2_ref, g2_ref, b2_ref, o_ref, *, ff_chunk):
    for r in range(h_ref.shape[0] // SUB_ROWS):
        rows = slice(r * SUB_ROWS, (r + 1) * SUB_ROWS)
        h = h_ref[rows, :]
        hb = h.astype(BF16)
        f = jnp.zeros(h.shape, F32)
        for c in range(D_FF // ff_chunk):
            sl = slice(c * ff_chunk, (c + 1) * ff_chunk)
            t = jnp.maximum(_dot(hb, w1_ref[:, sl]), 0.0)
            f = f + _dot((t * t).astype(BF16), w2_ref[sl, :])
        o_ref[rows, :] = _layer_norm(DEEPNORM_ALPHA * h + f, g2_ref[...], b2_ref[...])


def _mlp(h, w1, w2, ln_g, ln_b, tm, ff_chunk=1024):
    t_tokens = h.shape[0]
    row = lambda i: (i, 0)
    const = lambda i: (0, 0)
    return pl.pallas_call(
        functools.partial(_mlp_kernel, ff_chunk=ff_chunk),
        grid=(t_tokens // tm,),
        in_specs=[pl.BlockSpec((tm, D_MODEL), row),
                  pl.BlockSpec((D_MODEL, D_FF), const, pipeline_mode=pl.Buffered(1)),
                  pl.BlockSpec((D_FF, D_MODEL), const, pipeline_mode=pl.Buffered(1)),
                  pl.BlockSpec((1, D_MODEL), const),
                  pl.BlockSpec((1, D_MODEL), const)],
        out_specs=pl.BlockSpec((tm, D_MODEL), row),
        out_shape=jax.ShapeDtypeStruct((t_tokens, D_MODEL), F32),
        compiler_params=_cparams("parallel"),
        name="mlp",
    )(h, w1, w2, ln_g, ln_b)


def _rope_tables(pos):
    inv = 1.0 / (ROPE_THETA ** (jnp.arange(0, HEAD_DIM, 2, dtype=F32) / HEAD_DIM))
    ang = pos.astype(F32)[:, None] * inv[None, :]
    c, s = jnp.cos(ang), jnp.sin(ang)
    reps = LANES // HEAD_DIM
    return jnp.tile(jnp.concatenate([c, c], axis=1), (1, reps)), jnp.tile(jnp.concatenate([-s, s], axis=1), (1, reps))


def kernel(x_prompt, x_sample, cache_k, cache_v, state_ssm_re, state_ssm_im, w_in, lambda_q1, lambda_k1, lambda_q2, lambda_k2, subln_gain, ssm_a_re, ssm_a_im, ssm_b_re, ssm_b_im, ssm_c_re, ssm_c_im, ssm_d, ssm_log_dt, w_attn_proj, w_glu_a, w_glu_b, w_out, ln1_g, ln1_b, w_ff1, w_ff2, ln2_g, ln2_b):
    bp, n_p = x_prompt.shape[0], x_prompt.shape[1]
    bs, n_s = x_sample.shape[0], x_sample.shape[1]
    past = cache_k.shape[2]
    assert w_in.shape[0] == DEPTH and n_s == CHUNK and n_p % CHUNK == 0
    n_chunk = n_p // CHUNK
    tm = min(512, n_p)
    tm_s = min(512, bs * n_s)
    tq = min(512, n_p)
    l = 0
    lam_init = 0.8 - 0.6 * math.exp(-0.3 * l)

    xp = x_prompt.reshape(bp * n_p, D_MODEL)
    xs = x_sample.reshape(bs * n_s, D_MODEL)
    w_qkvu = w_in[l, :, 0:QKVU_COLS].astype(BF16)
    w_gate = w_in[l, :, QKVU_COLS:].astype(BF16)
    w_ap = w_attn_proj[l].astype(BF16)
    w_glu = jnp.concatenate([w_glu_a[l], w_glu_b[l]], axis=1).astype(BF16)
    w_o = w_out[l].astype(BF16)
    w1, w2 = w_ff1[l].astype(BF16), w_ff2[l].astype(BF16)
    lams = [v[l].reshape(1, HEAD_DIM) for v in (lambda_q1, lambda_k1, lambda_q2, lambda_k2)]
    gain = subln_gain[l].reshape(1, V_DIM)
    d_skip = ssm_d[l].reshape(1, SSM_WIDTH)
    lng = [v[l].reshape(1, D_MODEL) for v in (ln1_g, ln1_b, ln2_g, ln2_b)]

    cos_p, sin_p = _rope_tables(jnp.arange(n_p))
    cos_s, sin_s = _rope_tables(jnp.tile(past + jnp.arange(n_s), tm_s // n_s))

    q_p, k_p, v_p, u_p, kb_p, vt_p = _project(xp, w_qkvu, cos_p, sin_p, n_p, tm, True)
    q_s, k_s, v_s, u_s = _project(xs, w_qkvu, cos_s, sin_s, n_s, tm_s, False)

    ao_p = _prompt_attention(lams, gain, q_p, kb_p, vt_p, bp, n_p, tq, lam_init)
    ao_s = _sample_attention(lams, gain, q_s, k_s, v_s,
                             cache_k[l].reshape(bs, past * N_HEADS, V_DIM),
                             cache_v[l].reshape(bs, past * N_HEADS, V_DIM), lam_init)

    tables = _ssm_tables(ssm_a_re[l], ssm_a_im[l], ssm_b_re[l], ssm_b_im[l],
                         ssm_c_re[l], ssm_c_im[l], ssm_log_dt[l])
    x_cols_p = _chunk_cols(u_p.reshape(bp, n_chunk, CHUNK, SSM_WIDTH))
    x_cols_s = _chunk_cols(u_s.reshape(1, bs, CHUNK, SSM_WIDTH))
    s0 = jnp.concatenate([state_ssm_re[l], state_ssm_im[l]], axis=-1).transpose(1, 2, 0)
    s0 = jnp.pad(s0, ((0, 0), (0, 0), (0, LANES - bs)))
    y_cols_p, y_cols_s, st_p, st_s = _ssm(tables, x_cols_p, x_cols_s, s0, n_chunk)
    ys_p = _unchunk_cols(y_cols_p, bp, n_chunk).reshape(bp * n_p, SSM_WIDTH)
    ys_s = _unchunk_cols(y_cols_s, 1, bs).reshape(bs * n_s, SSM_WIDTH)
    sf_p = st_p[:, :, 0:bp]
    sf_s = st_s[:, :, 0:bs]

    outs = []
    for x2d, ao, ys, u, tile in ((xp, ao_p, ys_p, u_p, tm), (xs, ao_s, ys_s, u_s, tm_s)):
        big = 2 * tile if x2d.shape[0] >= 8 * tile else tile
        h = _merge(x2d, ao, ys, u, d_skip, w_gate, w_ap, w_glu, w_o, lng[0], lng[1], big)
        outs.append(_mlp(h, w1, w2, lng[2], lng[3], big))

    def states(sf):
        t = sf.transpose(2, 0, 1)
        return t[None, :, :, 0:STATE_DIM], t[None, :, :, STATE_DIM:]

    srp, sip = states(sf_p)
    srs, sis = states(sf_s)
    return (outs[0].reshape(bp, n_p, D_MODEL), outs[1].reshape(bs, n_s, D_MODEL),
            k_p.reshape(1, bp, n_p, N_HEADS, V_DIM), v_p.reshape(1, bp, n_p, N_HEADS, V_DIM), srp, sip,
            k_s.reshape(1, bs, n_s, N_HEADS, V_DIM), v_s.reshape(1, bs, n_s, N_HEADS, V_DIM), srs, sis)
```

```python
import functools
import math

import jax
import jax.numpy as jnp
from jax import lax
from jax.experimental import pallas as pl
from jax.experimental.pallas import tpu as pltpu

D_MODEL = 1024
CHUNK = 64
N_HEADS = 4
HEAD_DIM = 64
V_DIM = 2 * HEAD_DIM
ATTN_WIDTH = N_HEADS * V_DIM
SSM_WIDTH = 512
GROUP_SIZE = 16
N_GROUPS = SSM_WIDTH // GROUP_SIZE
STATE_DIM = 64
D_FF = 4 * D_MODEL
ROPE_THETA = 10000.0
LN_EPS = 1e-5
RMS_EPS = 1e-5
NEG_INF = -1e30
DEPTH = 1
DEEPNORM_ALPHA = (2.0 * DEPTH) ** 0.25
QKVU_COLS = 3 * ATTN_WIDTH + SSM_WIDTH
LOG2E = 1.4426950408889634

LANES = 128
VT_ROWS = V_DIM + 16
QUERY_LANES = 256
TOEPLITZ_ROWS = 256
SUB_ROWS = 256
SSM_GROUPS_PER_STEP = 4
SAMPLE_STREAMS_PER_STEP = 2
VMEM_LIMIT = 56 * 1024 * 1024

F32 = jnp.float32
BF16 = jnp.bfloat16


def _cparams(*sem):
    return pltpu.CompilerParams(dimension_semantics=sem, vmem_limit_bytes=VMEM_LIMIT)


def _nt_dot(a, b):
    return lax.dot_general(a, b, (((1,), (1,)), ((), ())), preferred_element_type=F32)


def _dot(a, b):
    return jnp.dot(a, b, preferred_element_type=F32)


def _rotary(z, cos, sin_signed, first_half):
    swapped = jnp.where(first_half, pltpu.roll(z, 96, 1), pltpu.roll(z, 32, 1))
    return z * cos + swapped * sin_signed


def _proj_kernel(x_ref, w_ref, cos_ref, sin_ref, q_ref, k_ref, v_ref, u_ref, *rest, emit_t):
    xb = x_ref[...].astype(BF16)
    cos = cos_ref[...]
    sin = sin_ref[...]
    lane = lax.broadcasted_iota(jnp.int32, cos.shape, 1)
    first_half = (lane % HEAD_DIM) < (HEAD_DIM // 2)
    tm = xb.shape[0]
    zq = _dot(xb, w_ref[:, 0:ATTN_WIDTH])
    zk = _dot(xb, w_ref[:, ATTN_WIDTH:2 * ATTN_WIDTH])
    zv = _dot(xb, w_ref[:, 2 * ATTN_WIDTH:3 * ATTN_WIDTH])
    for h in range(N_HEADS):
        sl = slice(h * V_DIM, (h + 1) * V_DIM)
        q_ref[:, sl] = (_rotary(zq[:, sl], cos, sin, first_half) * (LOG2E * HEAD_DIM ** -0.5)).astype(BF16)
        kr = _rotary(zk[:, sl], cos, sin, first_half)
        k_ref[pl.ds(h, tm, stride=N_HEADS), :] = kr
        v_ref[pl.ds(h, tm, stride=N_HEADS), :] = zv[:, sl]
        if emit_t:
            rest[0][:, sl] = kr.astype(BF16)
    if emit_t:
        vt_ref = rest[1]
        zvt = zv.T.astype(BF16)
        ones = jnp.ones((VT_ROWS - V_DIM, zvt.shape[1]), BF16)
        for h in range(N_HEADS):
            vt_ref[h, 0:V_DIM, :] = zvt[h * V_DIM:(h + 1) * V_DIM, :]
            vt_ref[h, V_DIM:VT_ROWS, :] = ones
    u_ref[...] = _dot(xb, w_ref[:, 3 * ATTN_WIDTH:QKVU_COLS])


def _project(x2d, w_qkvu, cos_t, sin_t, seq_len, tm, emit_t):
    t_tokens = x2d.shape[0]
    n_tiles = t_tokens // tm
    n_pos_tiles = cos_t.shape[0] // tm
    tiles_per_seq = max(seq_len // tm, 1)
    row = lambda i: (i, 0)
    pos = lambda i: (i % n_pos_tiles, 0)
    out_shape = [jax.ShapeDtypeStruct((t_tokens, ATTN_WIDTH), BF16),
                 jax.ShapeDtypeStruct((t_tokens * N_HEADS, V_DIM), F32),
                 jax.ShapeDtypeStruct((t_tokens * N_HEADS, V_DIM), F32),
                 jax.ShapeDtypeStruct((t_tokens, SSM_WIDTH), F32)]
    out_specs = ([pl.BlockSpec((tm, ATTN_WIDTH), row)] + [pl.BlockSpec((tm * N_HEADS, V_DIM), row)] * 2
                 + [pl.BlockSpec((tm, SSM_WIDTH), row)])
    if emit_t:
        bsz = t_tokens // seq_len
        out_shape += [jax.ShapeDtypeStruct((t_tokens, ATTN_WIDTH), BF16),
                      jax.ShapeDtypeStruct((bsz, N_HEADS, VT_ROWS, seq_len), BF16)]
        out_specs += [pl.BlockSpec((tm, ATTN_WIDTH), row),
                      pl.BlockSpec((None, N_HEADS, VT_ROWS, tm),
                                   lambda i: (i // tiles_per_seq, 0, 0, i % tiles_per_seq))]
    return pl.pallas_call(
        functools.partial(_proj_kernel, emit_t=emit_t),
        grid=(n_tiles,),
        in_specs=[pl.BlockSpec((tm, D_MODEL), row),
                  pl.BlockSpec((D_MODEL, QKVU_COLS), lambda i: (0, 0)),
                  pl.BlockSpec((tm, LANES), pos),
                  pl.BlockSpec((tm, LANES), pos)],
        out_specs=out_specs,
        out_shape=out_shape,
        compiler_params=_cparams("parallel"),
        name="proj_t" if emit_t else "proj",
    )(x2d, w_qkvu, cos_t, sin_t)


def _diff_lambda(lq1, lk1, lq2, lk2, lam_init):
    return (jnp.exp(jnp.sum(lq1 * lk1, axis=1, keepdims=True))
            - jnp.exp(jnp.sum(lq2 * lk2, axis=1, keepdims=True)) + lam_init)


def _sub_norm(d, gain, lam_init):
    ms = jnp.mean(jnp.square(d), axis=1, keepdims=True)
    return d * lax.rsqrt(ms + RMS_EPS) * gain * (1.0 - lam_init)


def _stack_maps(q):
    lane = lax.broadcasted_iota(jnp.int32, q.shape, 1)
    zero = jnp.zeros_like(q)
    return jnp.concatenate([jnp.where(lane < HEAD_DIM, q, zero), jnp.where(lane >= HEAD_DIM, q, zero)], axis=0)


def _prompt_attn_kernel(lq1_ref, lk1_ref, lq2_ref, lk2_ref, gain_ref, q_ref, k_ref, vt_ref,
                        o_ref, m_scr, acc_scr, qx_scr, s0_scr, s1_scr, mx0_scr, mx1_scr, *, tq, lam_init):
    nq = q_ref.shape[0] // tq
    items = [(qi, j) for qi in range(nq) for j in range(qi + 1)]
    for i in range(nq):
        qx_scr[i * 2 * tq:(i + 1) * 2 * tq, :] = _stack_maps(q_ref[i * tq:(i + 1) * tq, :])
    chains = [slice(c * QUERY_LANES, (c + 1) * QUERY_LANES) for c in range(2 * tq // QUERY_LANES)]

    def visible_keys(cs, diagonal):
        return min(tq, cs.start % tq + QUERY_LANES) if diagonal else tq

    def scores(qi, j, s_scr, mx_scr):
        kt = k_ref[j * tq:(j + 1) * tq, :]
        for cs in chains:
            qx = qx_scr[qi * 2 * tq + cs.start:qi * 2 * tq + cs.stop, :]
            st = _nt_dot(kt, qx)
            s_scr[:, cs] = st
            mx_scr[:, cs] = jnp.max(st, axis=0, keepdims=True)

    def softmax_pv(j, s_scr, mx_scr, diagonal):
        start = j * tq
        for cs in chains:
            n_keys = visible_keys(cs, diagonal)
            if diagonal:
                lane = lax.broadcasted_iota(jnp.int32, (CHUNK, QUERY_LANES), 1)
                blocks = []
                for kc in range(n_keys // CHUNK):
                    blk = s_scr[kc * CHUNK:(kc + 1) * CHUNK, cs]
                    first_visible = kc * CHUNK - cs.start % tq
                    if first_visible > 0:
                        blk = jnp.where(lane >= first_visible, blk, NEG_INF)
                    blocks.append(blk)
                st = jnp.concatenate(blocks, axis=0)
                tile_max = jnp.max(st, axis=0, keepdims=True)
            else:
                st = s_scr[:, cs]
                tile_max = mx_scr[:, cs]
            vt = vt_ref[:, start:start + n_keys]
            p_scale = None
            if j == 0:
                m_new = tile_max
            else:
                m_old = m_scr[:, cs]
                m_new = jnp.maximum(m_old, tile_max)
                p_scale = jnp.exp2(m_old - m_new)
            p = jnp.exp2(st - m_new).astype(BF16)
            pv = _dot(vt, p)
            acc_scr[:, cs] = pv if p_scale is None else acc_scr[:, cs] * p_scale + pv
            m_scr[:, cs] = m_new

    def finish(qi):
        acc = acc_scr[...]
        o = acc[0:V_DIM, :] * (1.0 / acc[V_DIM:V_DIM + 1, :])
        lam = _diff_lambda(lq1_ref[...], lk1_ref[...], lq2_ref[...], lk2_ref[...], lam_init)
        d = (o[:, 0:tq] - lam * o[:, tq:2 * tq]).T
        o_ref[qi * tq:(qi + 1) * tq, :] = _sub_norm(d, gain_ref[...], lam_init).astype(BF16)

    bufs = ((s0_scr, mx0_scr), (s1_scr, mx1_scr))
    scores(*items[0], *bufs[0])
    for w, (qi, j) in enumerate(items):
        if w + 1 < len(items):
            scores(*items[w + 1], *bufs[(w + 1) % 2])
        softmax_pv(j, *bufs[w % 2], j == qi)
        if j == qi:
            finish(qi)


def _prompt_attention(lams, gain, q, kb, vt, bsz, seq_len, tq, lam_init):
    nq = seq_len // tq
    small = lambda b, h: (0, 0)
    return pl.pallas_call(
        functools.partial(_prompt_attn_kernel, tq=tq, lam_init=lam_init),
        grid=(bsz, N_HEADS),
        in_specs=[pl.BlockSpec((1, HEAD_DIM), small)] * 4 + [
            pl.BlockSpec((1, V_DIM), small),
            pl.BlockSpec((seq_len, V_DIM), lambda b, h: (b, h)),
            pl.BlockSpec((seq_len, V_DIM), lambda b, h: (b, h)),
            pl.BlockSpec((None, None, VT_ROWS, seq_len), lambda b, h: (b, h, 0, 0))],
        out_specs=pl.BlockSpec((seq_len, V_DIM), lambda b, h: (b, h)),
        out_shape=jax.ShapeDtypeStruct((bsz * seq_len, ATTN_WIDTH), BF16),
        scratch_shapes=[pltpu.VMEM((1, 2 * tq), F32), pltpu.VMEM((VT_ROWS, 2 * tq), F32),
                        pltpu.VMEM((nq * 2 * tq, V_DIM), BF16),
                        pltpu.VMEM((tq, 2 * tq), F32), pltpu.VMEM((tq, 2 * tq), F32),
                        pltpu.VMEM((1, 2 * tq), F32), pltpu.VMEM((1, 2 * tq), F32)],
        compiler_params=_cparams("parallel", "parallel"),
        name="prompt_attn",
    )(*lams, gain, q, kb, vt)


def _sample_attn_kernel(lq1_ref, lk1_ref, lq2_ref, lk2_ref, gain_ref, q_ref, k_ref, v_ref, ck_ref, cv_ref,
                        o_ref, *, lam_init):
    n_streams = ck_ref.shape[0]
    n_new = q_ref.shape[0] // n_streams
    past = ck_ref.shape[1] // N_HEADS
    lam = _diff_lambda(lq1_ref[...], lk1_ref[...], lq2_ref[...], lk2_ref[...], lam_init)
    gain = gain_ref[...]
    for s in range(n_streams):
        rows = slice(s * n_new, (s + 1) * n_new)
        for h in range(N_HEADS):
            sl = slice(h * V_DIM, (h + 1) * V_DIM)
            old = pl.ds(h, past, stride=N_HEADS)
            new = pl.ds(s * n_new * N_HEADS + h, n_new, stride=N_HEADS)
            qx = _stack_maps(q_ref[rows, sl])
            s_c = _nt_dot(qx, ck_ref[s, old, :].astype(BF16))
            s_n = _nt_dot(qx, k_ref[new, :].astype(BF16))
            m = jnp.maximum(jnp.max(s_c, axis=1, keepdims=True), jnp.max(s_n, axis=1, keepdims=True))
            p_c = jnp.exp2(s_c - m)
            p_n = jnp.exp2(s_n - m)
            denom = jnp.sum(p_c, axis=1, keepdims=True) + jnp.sum(p_n, axis=1, keepdims=True)
            o = (_dot(p_c.astype(BF16), cv_ref[s, old, :].astype(BF16))
                 + _dot(p_n.astype(BF16), v_ref[new, :].astype(BF16))) * (1.0 / denom)
            d = o[0:n_new, :] - lam * o[n_new:2 * n_new, :]
            o_ref[rows, sl] = _sub_norm(d, gain, lam_init).astype(BF16)


def _sample_attention(lams, gain, q, k, v, cache_k, cache_v, lam_init):
    bsz, past_rows = cache_k.shape[0], cache_k.shape[1]
    n_new = q.shape[0] // bsz
    ns = SAMPLE_STREAMS_PER_STEP if bsz % SAMPLE_STREAMS_PER_STEP == 0 else 1
    small = lambda b: (0, 0)
    row = lambda b: (b, 0)
    return pl.pallas_call(
        functools.partial(_sample_attn_kernel, lam_init=lam_init),
        grid=(bsz // ns,),
        in_specs=[pl.BlockSpec((1, HEAD_DIM), small)] * 4 + [
            pl.BlockSpec((1, V_DIM), small),
            pl.BlockSpec((ns * n_new, ATTN_WIDTH), row),
            pl.BlockSpec((ns * n_new * N_HEADS, V_DIM), row),
            pl.BlockSpec((ns * n_new * N_HEADS, V_DIM), row),
            pl.BlockSpec((ns, past_rows, V_DIM), lambda b: (b, 0, 0)),
            pl.BlockSpec((ns, past_rows, V_DIM), lambda b: (b, 0, 0))],
        out_specs=pl.BlockSpec((ns * n_new, ATTN_WIDTH), row),
        out_shape=jax.ShapeDtypeStruct(q.shape, BF16),
        compiler_params=_cparams("parallel"),
        name="sample_attn",
    )(*lams, gain, q, k, v, cache_k, cache_v)


def _ssm_params(a_re, a_im, b_re, b_im, c_re, c_im, log_dt):
    g = a_re.shape[0]
    twice = lambda v: jnp.concatenate([v, v], axis=-1)
    rows = jnp.stack([twice(a_re), twice(a_im), jnp.broadcast_to(log_dt[:, None], (g, 2 * STATE_DIM))], axis=1)
    rows = jnp.pad(rows, ((0, 0), (0, 8 - rows.shape[1]), (0, 0)))
    return rows, twice(b_re.transpose(0, 2, 1)), twice(b_im.transpose(0, 2, 1)), twice(c_re), twice(c_im)


def _group_tables(rows_ref, bre_ref, bim_ref, cre_ref, cim_ref):
    lane = lax.broadcasted_iota(jnp.int32, (1, 2 * STATE_DIM), 1)
    lo = lane < STATE_DIM
    a_re, a_im, log_dt = rows_ref[0:1, :], rows_ref[1:2, :], rows_ref[2:3, :]
    dt = jnp.exp(log_dt)
    lam_re, lam_im = a_re * dt, a_im * dt
    mag = jnp.exp(lam_re)
    ar, ai = mag * jnp.cos(lam_im), mag * jnp.sin(lam_im)
    den = jnp.square(a_re) + jnp.square(a_im)
    cr = ((ar - 1.0) * a_re + ai * a_im) / den
    ci = (ai * a_re - (ar - 1.0) * a_im) / den
    bre, bim = bre_ref[...], bim_ref[...]
    bbr = cr * bre - ci * bim
    bbi = cr * bim + ci * bre
    quarter_turn = jnp.where(lo, 0.0, 0.5 * math.pi)

    def powers(tau):
        return jnp.exp(tau * lam_re) * jnp.cos(tau * lam_im - quarter_turn)

    def outer(y, x1, x2):
        y_sw = pltpu.roll(y, STATE_DIM, 1)
        prod = y[:, None, :] * x1[None, :, :] + y_sw[:, None, :] * x2[None, :, :]
        return prod.reshape(y.shape[0] * x1.shape[0], 2 * STATE_DIM)

    frames = lax.broadcasted_iota(jnp.int32, (CHUNK, 1), 0).astype(F32)
    wt = outer(powers((CHUNK - 1.0) - frames), bbr, jnp.where(lo, -bbi, bbi))
    w = wt.T
    cre, cim = cre_ref[...], cim_ref[...]
    vm = outer(powers(frames + 1.0), jnp.where(lo, cre, -cre), -cim)
    rrev = jnp.dot(jnp.where(lo, cre, -cim), w, precision=lax.Precision.HIGHEST, preferred_element_type=F32)
    a_row = powers(jnp.full((8, 1), float(CHUNK), F32))[0:1, :]
    a_col = jnp.broadcast_to(a_row, (2 * STATE_DIM, 2 * STATE_DIM)).T
    a_swap = pltpu.roll(a_col, STATE_DIM, 0)
    top = lax.broadcasted_iota(jnp.int32, a_col.shape, 0) < STATE_DIM
    return (rrev, w.astype(BF16), vm.astype(BF16),
            jnp.where(top, a_col, a_swap), jnp.where(top, -a_swap, a_col))


def _chunk_cols_kernel(u_ref, o_ref):
    n_s, n_r, n_j = u_ref.shape[0], u_ref.shape[1], u_ref.shape[2]
    pad = LANES - n_s * n_r
    for jj in range(n_j):
        rows = [u_ref[s, :, jj, :] for s in range(n_s)]
        if pad:
            rows.append(jnp.zeros((pad, SSM_WIDTH), F32))
        cols = jnp.concatenate(rows, axis=0).T
        o_ref[:, jj * GROUP_SIZE:(jj + 1) * GROUP_SIZE, :] = cols.reshape(N_GROUPS, GROUP_SIZE, LANES).astype(BF16)


def _chunk_cols(u4, frames_per_step=16):
    s_total, n_r = u4.shape[0], u4.shape[1]
    s_step = min(s_total, LANES // n_r)
    assert s_step >= 1 and s_total % s_step == 0
    n_lane_blocks = s_total // s_step
    return pl.pallas_call(
        _chunk_cols_kernel,
        grid=(n_lane_blocks, CHUNK // frames_per_step),
        in_specs=[pl.BlockSpec((s_step, n_r, frames_per_step, SSM_WIDTH), lambda a, j: (a, 0, j, 0))],
        out_specs=pl.BlockSpec((N_GROUPS, frames_per_step * GROUP_SIZE, LANES), lambda a, j: (0, j, a)),
        out_shape=jax.ShapeDtypeStruct((N_GROUPS, CHUNK * GROUP_SIZE, n_lane_blocks * LANES), BF16),
        compiler_params=_cparams("parallel", "parallel"),
        name="ssm_in",
    )(u4)


def _unchunk_cols_kernel(y_ref, o_ref):
    n_s, n_r, n_j = o_ref.shape[0], o_ref.shape[1], o_ref.shape[2]
    for jj in range(n_j):
        cols = y_ref[:, jj * GROUP_SIZE:(jj + 1) * GROUP_SIZE, :].reshape(SSM_WIDTH, LANES)
        rows = cols.T
        for s in range(n_s):
            o_ref[s, :, jj, :] = rows[s * n_r:(s + 1) * n_r, :]


def _unchunk_cols(y_cols, s_total, n_r, frames_per_step=16):
    s_step = min(s_total, LANES // n_r)
    n_lane_blocks = s_total // s_step
    return pl.pallas_call(
        _unchunk_cols_kernel,
        grid=(n_lane_blocks, CHUNK // frames_per_step),
        in_specs=[pl.BlockSpec((N_GROUPS, frames_per_step * GROUP_SIZE, LANES), lambda a, j: (0, j, a))],
        out_specs=pl.BlockSpec((s_step, n_r, frames_per_step, SSM_WIDTH), lambda a, j: (a, 0, j, 0)),
        out_shape=jax.ShapeDtypeStruct((s_total, n_r, CHUNK, SSM_WIDTH), F32),
        compiler_params=_cparams("parallel", "parallel"),
        name="ssm_out",
    )(y_cols)


def _ssm_kernel(*refs, n_chunk):
    for gi in range(SSM_GROUPS_PER_STEP):
        _ssm_group(*[r.at[gi] for r in refs], n_chunk=n_chunk)


def _ssm_group(rows_ref, bre_ref, bim_ref, cre_ref, cim_ref, xp_ref, xs_ref, s0_ref, yp_ref, ys_ref, sp_ref, ss_ref,
               mt_scr, *, n_chunk):
    width = GROUP_SIZE * CHUNK
    n_piece = width // LANES
    lane16 = lax.broadcasted_iota(jnp.int32, (GROUP_SIZE, LANES), 1)
    rrev, w, vm_all, a_c, a_s = _group_tables(rows_ref, bre_ref, bim_ref, cre_ref, cim_ref)
    pieces = [rrev[:, k * LANES:(k + 1) * LANES] for k in range(n_piece)] + [jnp.zeros((GROUP_SIZE, LANES), F32)]
    rolled = {0: pieces}
    for b in range(GROUP_SIZE, LANES, GROUP_SIZE):
        rolled[b] = [pltpu.roll(p, LANES - b, 1) for p in pieces[:n_piece]] + [pieces[n_piece]]
    def toeplitz_rows(t0, t1, n_cols):
        for t in range(t0, t1):
            shift = GROUP_SIZE * (CHUNK - 1 - t)
            a, b = shift // LANES, shift % LANES
            for v in range(n_cols // LANES):
                k = v + a
                if k >= n_piece:
                    blk = pieces[n_piece]
                elif b == 0:
                    blk = pieces[k]
                else:
                    blk = jnp.where(lane16 < LANES - b, rolled[b][k], rolled[b][k + 1])
                mt_scr[t * GROUP_SIZE:(t + 1) * GROUP_SIZE, v * LANES:(v + 1) * LANES] = blk.astype(BF16)

    xp, xs = xp_ref[...], xs_ref[...]
    sloc_p, sloc_s = _dot(w, xp), _dot(w, xs)

    def cmul(pc, ps, s):
        return pc * s + ps * pltpu.roll(s, STATE_DIM, 0)

    lane = lax.broadcasted_iota(jnp.int32, (2 * STATE_DIM, LANES), 1) % n_chunk
    prev_cols = []
    lane_id = lax.broadcasted_iota(jnp.int32, (2 * STATE_DIM, LANES), 1)
    final = jnp.zeros((2 * STATE_DIM, LANES), F32)
    seqs_per_tile = LANES // n_chunk
    for tile in range(xp.shape[1] // LANES):
        s_inc = sloc_p[:, tile * LANES:(tile + 1) * LANES]
        pc, ps = a_c, a_s
        dist = 1
        while dist < n_chunk:
            shifted = jnp.where(lane >= dist, pltpu.roll(s_inc, dist, 1), 0.0)
            s_inc = s_inc + cmul(pc, ps, shifted)
            pc, ps = pc * pc - ps * ps, 2.0 * pc * ps
            dist *= 2
        for k in range(seqs_per_tile):
            src, dst = (k + 1) * n_chunk - 1, tile * seqs_per_tile + k
            final = jnp.where(lane_id == dst, pltpu.roll(s_inc, (dst - src) % LANES, 1), final)
        prev_cols.append(jnp.where(lane >= 1, pltpu.roll(s_inc, 1, 1), 0.0))
    sp_ref[...] = final
    s_prev_p = jnp.concatenate(prev_cols, axis=1).astype(BF16)
    s0 = s0_ref[...]
    s0b = s0.astype(BF16)
    frames = TOEPLITZ_ROWS // GROUP_SIZE
    for i in range(width // TOEPLITZ_ROWS):
        rows = slice(i * TOEPLITZ_ROWS, (i + 1) * TOEPLITZ_ROWS)
        n_cols = (i + 1) * TOEPLITZ_ROWS
        toeplitz_rows(i * frames, (i + 1) * frames, n_cols)
        mt = mt_scr[rows, 0:n_cols]
        vm = vm_all[rows, :]
        yp_ref[rows, :] = _dot(mt, xp[0:n_cols, :]) + _dot(vm, s_prev_p)
        ys_ref[rows, :] = _dot(mt, xs[0:n_cols, :]) + _dot(vm, s0b)
    ss_ref[...] = cmul(a_c, a_s, s0) + sloc_s


def _ssm(params, x_p, x_s, s0, n_chunk):
    n_groups, lanes_p = x_p.shape[0], x_p.shape[2]
    width = GROUP_SIZE * CHUNK
    per_g = lambda g: (g, 0, 0)
    state_rows = 2 * STATE_DIM
    gb = SSM_GROUPS_PER_STEP
    assert n_groups % gb == 0
    spec = lambda rows, cols: pl.BlockSpec((gb, rows, cols), per_g)
    return pl.pallas_call(
        functools.partial(_ssm_kernel, n_chunk=n_chunk),
        grid=(n_groups // gb,),
        in_specs=[spec(8, state_rows)] + [spec(GROUP_SIZE, state_rows)] * 4 + [
                  spec(width, lanes_p), spec(width, LANES), spec(state_rows, LANES)],
        out_specs=[spec(width, lanes_p), spec(width, LANES), spec(state_rows, LANES), spec(state_rows, LANES)],
        out_shape=[jax.ShapeDtypeStruct((n_groups, width, lanes_p), F32),
                   jax.ShapeDtypeStruct((n_groups, width, LANES), F32),
                   jax.ShapeDtypeStruct((n_groups, state_rows, LANES), F32),
                   jax.ShapeDtypeStruct((n_groups, state_rows, LANES), F32)],
        scratch_shapes=[pltpu.VMEM((gb, width, width), BF16)],
        compiler_params=_cparams("parallel"),
        name="ssm",
    )(*params, x_p, x_s, s0)


def _layer_norm(x, g, b):
    mu = jnp.mean(x, axis=1, keepdims=True)
    xc = x - mu
    var = jnp.mean(jnp.square(xc), axis=1, keepdims=True)
    return xc * lax.rsqrt(var + LN_EPS) * g + b


def _gelu_tanh(x):
    return 0.5 * x * (1.0 + jnp.tanh(math.sqrt(2.0 / math.pi) * (x + 0.044715 * (x * x * x))))


def _merge_kernel(x_ref, ao_ref, ys_ref, u_ref, d_ref, wg_ref, wap_ref, wglu_ref, wout_ref, g1_ref, b1_ref, h_ref):
    for r in range(x_ref.shape[0] // SUB_ROWS):
        rows = slice(r * SUB_ROWS, (r + 1) * SUB_ROWS)
        x = x_ref[rows, :]
        xb = x.astype(BF16)
        a_branch = _dot(ao_ref[rows, :], wap_ref[...])
        s_act = _gelu_tanh(ys_ref[rows, :] + d_ref[...] * u_ref[rows, :]).astype(BF16)
        s_branch = (_dot(s_act, wglu_ref[:, 0:D_MODEL])
                    * jax.nn.sigmoid(_dot(s_act, wglu_ref[:, D_MODEL:2 * D_MODEL])))
        m = (jax.nn.sigmoid(_dot(xb, wg_ref[:, 0:D_MODEL])) * a_branch
             + jax.nn.sigmoid(_dot(xb, wg_ref[:, D_MODEL:2 * D_MODEL])) * s_branch)
        h_ref[rows, :] = _layer_norm(DEEPNORM_ALPHA * x + _dot(m.astype(BF16), wout_ref[...]),
                                     g1_ref[...], b1_ref[...])


def _merge(x2d, ao, ys, u, d, w_gate, w_ap, w_glu, w_out, ln_g, ln_b, tm):
    t_tokens = x2d.shape[0]
    row = lambda i: (i, 0)
    const = lambda i: (0, 0)
    return pl.pallas_call(
        _merge_kernel,
        grid=(t_tokens // tm,),
        in_specs=[pl.BlockSpec((tm, D_MODEL), row),
                  pl.BlockSpec((tm, ATTN_WIDTH), row),
                  pl.BlockSpec((tm, SSM_WIDTH), row),
                  pl.BlockSpec((tm, SSM_WIDTH), row),
                  pl.BlockSpec((1, SSM_WIDTH), const),
                  pl.BlockSpec((D_MODEL, 2 * D_MODEL), const, pipeline_mode=pl.Buffered(1)),
                  pl.BlockSpec((ATTN_WIDTH, D_MODEL), const, pipeline_mode=pl.Buffered(1)),
                  pl.BlockSpec((SSM_WIDTH, 2 * D_MODEL), const, pipeline_mode=pl.Buffered(1)),
                  pl.BlockSpec((D_MODEL, D_MODEL), const, pipeline_mode=pl.Buffered(1)),
                  pl.BlockSpec((1, D_MODEL), const),
                  pl.BlockSpec((1, D_MODEL), const)],
        out_specs=pl.BlockSpec((tm, D_MODEL), row),
        out_shape=jax.ShapeDtypeStruct((t_tokens, D_MODEL), F32),
        compiler_params=_cparams("parallel"),
        name="merge",
    )(x2d, ao, ys, u, d, w_gate, w_ap, w_glu, w_out, ln_g, ln_b)


def _mlp_kernel(h_ref, w1_ref, w2_ref, g2_ref, b2_ref, o_ref, *, ff_chunk):
    for r in range(h_ref.shape[0] // SUB_ROWS):
        rows = slice(r * SUB_ROWS, (r + 1) * SUB_ROWS)
        h = h_ref[rows, :]
        hb = h.astype(BF16)
        f = jnp.zeros(h.shape, F32)
        for c in range(D_FF // ff_chunk):
            sl = slice(c * ff_chunk, (c + 1) * ff_chunk)
            t = jnp.maximum(_dot(hb, w1_ref[:, sl]), 0.0)
            f = f + _dot((t * t).astype(BF16), w2_ref[sl, :])
        o_ref[rows, :] = _layer_norm(DEEPNORM_ALPHA * h + f, g2_ref[...], b2_ref[...])


def _mlp(h, w1, w2, ln_g, ln_b, tm, ff_chunk=1024):
    t_tokens = h.shape[0]
    row = lambda i: (i, 0)
    const = lambda i: (0, 0)
    return pl.pallas_call(
        functools.partial(_mlp_kernel, ff_chunk=ff_chunk),
        grid=(t_tokens // tm,),
        in_specs=[pl.BlockSpec((tm, D_MODEL), row),
                  pl.BlockSpec((D_MODEL, D_FF), const, pipeline_mode=pl.Buffered(1)),
                  pl.BlockSpec((D_FF, D_MODEL), const, pipeline_mode=pl.Buffered(1)),
                  pl.BlockSpec((1, D_MODEL), const),
                  pl.BlockSpec((1, D_MODEL), const)],
        out_specs=pl.BlockSpec((tm, D_MODEL), row),
        out_shape=jax.ShapeDtypeStruct((t_tokens, D_MODEL), F32),
        compiler_params=_cparams("parallel"),
        name="mlp",
    )(h, w1, w2, ln_g, ln_b)


def _rope_tables(pos):
    inv = 1.0 / (ROPE_THETA ** (jnp.arange(0, HEAD_DIM, 2, dtype=F32) / HEAD_DIM))
    ang = pos.astype(F32)[:, None] * inv[None, :]
    c, s = jnp.cos(ang), jnp.sin(ang)
    reps = LANES // HEAD_DIM
    return jnp.tile(jnp.concatenate([c, c], axis=1), (1, reps)), jnp.tile(jnp.concatenate([-s, s], axis=1), (1, reps))


def kernel(x_prompt, x_sample, cache_k, cache_v, state_ssm_re, state_ssm_im, w_in, lambda_q1, lambda_k1, lambda_q2, lambda_k2, subln_gain, ssm_a_re, ssm_a_im, ssm_b_re, ssm_b_im, ssm_c_re, ssm_c_im, ssm_d, ssm_log_dt, w_attn_proj, w_glu_a, w_glu_b, w_out, ln1_g, ln1_b, w_ff1, w_ff2, ln2_g, ln2_b):
    bp, n_p = x_prompt.shape[0], x_prompt.shape[1]
    bs, n_s = x_sample.shape[0], x_sample.shape[1]
    past = cache_k.shape[2]
    assert w_in.shape[0] == DEPTH and n_s == CHUNK and n_p % CHUNK == 0
    n_chunk = n_p // CHUNK
    tm = min(512, n_p)
    tm_s = min(512, bs * n_s)
    tq = min(512, n_p)
    l = 0
    lam_init = 0.8 - 0.6 * math.exp(-0.3 * l)

    xp = x_prompt.reshape(bp * n_p, D_MODEL)
    xs = x_sample.reshape(bs * n_s, D_MODEL)
    w_qkvu = w_in[l, :, 0:QKVU_COLS].astype(BF16)
    w_gate = w_in[l, :, QKVU_COLS:].astype(BF16)
    w_ap = w_attn_proj[l].astype(BF16)
    w_glu = jnp.concatenate([w_glu_a[l], w_glu_b[l]], axis=1).astype(BF16)
    w_o = w_out[l].astype(BF16)
    w1, w2 = w_ff1[l].astype(BF16), w_ff2[l].astype(BF16)
    lams = [v[l].reshape(1, HEAD_DIM) for v in (lambda_q1, lambda_k1, lambda_q2, lambda_k2)]
    gain = subln_gain[l].reshape(1, V_DIM)
    d_skip = ssm_d[l].reshape(1, SSM_WIDTH)
    lng = [v[l].reshape(1, D_MODEL) for v in (ln1_g, ln1_b, ln2_g, ln2_b)]

    cos_p, sin_p = _rope_tables(jnp.arange(n_p))
    cos_s, sin_s = _rope_tables(jnp.tile(past + jnp.arange(n_s), tm_s // n_s))

    q_p, k_p, v_p, u_p, kb_p, vt_p = _project(xp, w_qkvu, cos_p, sin_p, n_p, tm, True)
    q_s, k_s, v_s, u_s = _project(xs, w_qkvu, cos_s, sin_s, n_s, tm_s, False)

    ao_p = _prompt_attention(lams, gain, q_p, kb_p, vt_p, bp, n_p, tq, lam_init)
    ao_s = _sample_attention(lams, gain, q_s, k_s, v_s,
                             cache_k[l].reshape(bs, past * N_HEADS, V_DIM),
                             cache_v[l].reshape(bs, past * N_HEADS, V_DIM), lam_init)

    tables = _ssm_params(ssm_a_re[l], ssm_a_im[l], ssm_b_re[l], ssm_b_im[l],
                         ssm_c_re[l], ssm_c_im[l], ssm_log_dt[l])
    x_cols_p = _chunk_cols(u_p.reshape(bp, n_chunk, CHUNK, SSM_WIDTH))
    x_cols_s = _chunk_cols(u_s.reshape(1, bs, CHUNK, SSM_WIDTH))
    s0 = jnp.concatenate([state_ssm_re[l], state_ssm_im[l]], axis=-1).transpose(1, 2, 0)
    s0 = jnp.pad(s0, ((0, 0), (0, 0), (0, LANES - bs)))
    y_cols_p, y_cols_s, st_p, st_s = _ssm(tables, x_cols_p, x_cols_s, s0, n_chunk)
    ys_p = _unchunk_cols(y_cols_p, bp, n_chunk).reshape(bp * n_p, SSM_WIDTH)
    ys_s = _unchunk_cols(y_cols_s, 1, bs).reshape(bs * n_s, SSM_WIDTH)
    sf_p = st_p[:, :, 0:bp]
    sf_s = st_s[:, :, 0:bs]

    outs = []
    for x2d, ao, ys, u, tile in ((xp, ao_p, ys_p, u_p, tm), (xs, ao_s, ys_s, u_s, tm_s)):
        big = 2 * tile if x2d.shape[0] >= 8 * tile else tile
        h = _merge(x2d, ao, ys, u, d_skip, w_gate, w_ap, w_glu, w_o, lng[0], lng[1], big)
        outs.append(_mlp(h, w1, w2, lng[2], lng[3], big))

    def states(sf):
        t = sf.transpose(2, 0, 1)
        return t[None, :, :, 0:STATE_DIM], t[None, :, :, STATE_DIM:]

    srp, sip = states(sf_p)
    srs, sis = states(sf_s)
    return (outs[0].reshape(bp, n_p, D_MODEL), outs[1].reshape(bs, n_s, D_MODEL),
            k_p.reshape(1, bp, n_p, N_HEADS, V_DIM), v_p.reshape(1, bp, n_p, N_HEADS, V_DIM), srp, sip,
            k_s.reshape(1, bs, n_s, N_HEADS, V_DIM), v_s.reshape(1, bs, n_s, N_HEADS, V_DIM), srs, sis)
```

```python
import functools
import math

import jax
import jax.numpy as jnp
from jax import lax
from jax.experimental import pallas as pl
from jax.experimental.pallas import tpu as pltpu

D_MODEL = 1024
CHUNK = 64
N_HEADS = 4
HEAD_DIM = 64
V_DIM = 2 * HEAD_DIM
ATTN_WIDTH = N_HEADS * V_DIM
SSM_WIDTH = 512
GROUP_SIZE = 16
N_GROUPS = SSM_WIDTH // GROUP_SIZE
STATE_DIM = 64
D_FF = 4 * D_MODEL
ROPE_THETA = 10000.0
LN_EPS = 1e-5
RMS_EPS = 1e-5
NEG_INF = -1e30
DEPTH = 1
DEEPNORM_ALPHA = (2.0 * DEPTH) ** 0.25
QKVU_COLS = 3 * ATTN_WIDTH + SSM_WIDTH
LOG2E = 1.4426950408889634

LANES = 128
VT_ROWS = V_DIM + 16
QUERY_LANES = 256
TOEPLITZ_ROWS = 256
SUB_ROWS = 256
SSM_GROUPS_PER_STEP = 4
SAMPLE_STREAMS_PER_STEP = 2
VMEM_LIMIT = 56 * 1024 * 1024

F32 = jnp.float32
BF16 = jnp.bfloat16


def _cparams(*sem):
    return pltpu.CompilerParams(dimension_semantics=sem, vmem_limit_bytes=VMEM_LIMIT)


def _nt_dot(a, b):
    return lax.dot_general(a, b, (((1,), (1,)), ((), ())), preferred_element_type=F32)


def _dot(a, b):
    return jnp.dot(a, b, preferred_element_type=F32)


def _rotary(z, cos, sin_signed, first_half):
    swapped = jnp.where(first_half, pltpu.roll(z, 96, 1), pltpu.roll(z, 32, 1))
    return z * cos + swapped * sin_signed


def _proj_kernel(x_ref, w_ref, cos_ref, sin_ref, q_ref, k_ref, v_ref, u_ref, *rest, emit_t):
    xb = x_ref[...].astype(BF16)
    cos = cos_ref[...]
    sin = sin_ref[...]
    lane = lax.broadcasted_iota(jnp.int32, cos.shape, 1)
    first_half = (lane % HEAD_DIM) < (HEAD_DIM // 2)
    tm = xb.shape[0]
    zq = _dot(xb, w_ref[:, 0:ATTN_WIDTH])
    zk = _dot(xb, w_ref[:, ATTN_WIDTH:2 * ATTN_WIDTH])
    zv = _dot(xb, w_ref[:, 2 * ATTN_WIDTH:3 * ATTN_WIDTH])
    for h in range(N_HEADS):
        sl = slice(h * V_DIM, (h + 1) * V_DIM)
        q_ref[:, sl] = (_rotary(zq[:, sl], cos, sin, first_half) * (LOG2E * HEAD_DIM ** -0.5)).astype(BF16)
        kr = _rotary(zk[:, sl], cos, sin, first_half)
        k_ref[pl.ds(h, tm, stride=N_HEADS), :] = kr
        v_ref[pl.ds(h, tm, stride=N_HEADS), :] = zv[:, sl]
        if emit_t:
            rest[0][:, sl] = kr.astype(BF16)
    if emit_t:
        vt_ref = rest[1]
        zvt = zv.T.astype(BF16)
        ones = jnp.ones((VT_ROWS - V_DIM, zvt.shape[1]), BF16)
        for h in range(N_HEADS):
            vt_ref[h, 0:V_DIM, :] = zvt[h * V_DIM:(h + 1) * V_DIM, :]
            vt_ref[h, V_DIM:VT_ROWS, :] = ones
    u_ref[...] = _dot(xb, w_ref[:, 3 * ATTN_WIDTH:QKVU_COLS])


def _project(x2d, w_qkvu, cos_t, sin_t, seq_len, tm, emit_t):
    t_tokens = x2d.shape[0]
    n_tiles = t_tokens // tm
    n_pos_tiles = cos_t.shape[0] // tm
    tiles_per_seq = max(seq_len // tm, 1)
    row = lambda i: (i, 0)
    pos = lambda i: (i % n_pos_tiles, 0)
    out_shape = [jax.ShapeDtypeStruct((t_tokens, ATTN_WIDTH), BF16),
                 jax.ShapeDtypeStruct((t_tokens * N_HEADS, V_DIM), F32),
                 jax.ShapeDtypeStruct((t_tokens * N_HEADS, V_DIM), F32),
                 jax.ShapeDtypeStruct((t_tokens, SSM_WIDTH), F32)]
    out_specs = ([pl.BlockSpec((tm, ATTN_WIDTH), row)] + [pl.BlockSpec((tm * N_HEADS, V_DIM), row)] * 2
                 + [pl.BlockSpec((tm, SSM_WIDTH), row)])
    if emit_t:
        bsz = t_tokens // seq_len
        out_shape += [jax.ShapeDtypeStruct((t_tokens, ATTN_WIDTH), BF16),
                      jax.ShapeDtypeStruct((bsz, N_HEADS, VT_ROWS, seq_len), BF16)]
        out_specs += [pl.BlockSpec((tm, ATTN_WIDTH), row),
                      pl.BlockSpec((None, N_HEADS, VT_ROWS, tm),
                                   lambda i: (i // tiles_per_seq, 0, 0, i % tiles_per_seq))]
    return pl.pallas_call(
        functools.partial(_proj_kernel, emit_t=emit_t),
        grid=(n_tiles,),
        in_specs=[pl.BlockSpec((tm, D_MODEL), row),
                  pl.BlockSpec((D_MODEL, QKVU_COLS), lambda i: (0, 0)),
                  pl.BlockSpec((tm, LANES), pos),
                  pl.BlockSpec((tm, LANES), pos)],
        out_specs=out_specs,
        out_shape=out_shape,
        compiler_params=_cparams("parallel"),
        name="proj_t" if emit_t else "proj",
    )(x2d, w_qkvu, cos_t, sin_t)


def _diff_lambda(lq1, lk1, lq2, lk2, lam_init):
    return (jnp.exp(jnp.sum(lq1 * lk1, axis=1, keepdims=True))
            - jnp.exp(jnp.sum(lq2 * lk2, axis=1, keepdims=True)) + lam_init)


def _sub_norm(d, gain, lam_init):
    ms = jnp.mean(jnp.square(d), axis=1, keepdims=True)
    return d * lax.rsqrt(ms + RMS_EPS) * gain * (1.0 - lam_init)


def _stack_maps(q):
    lane = lax.broadcasted_iota(jnp.int32, q.shape, 1)
    zero = jnp.zeros_like(q)
    return jnp.concatenate([jnp.where(lane < HEAD_DIM, q, zero), jnp.where(lane >= HEAD_DIM, q, zero)], axis=0)


def _prompt_attn_kernel(lq1_ref, lk1_ref, lq2_ref, lk2_ref, gain_ref, q_ref, k_ref, vt_ref,
                        o_ref, m_scr, acc_scr, qx_scr, s0_scr, s1_scr, mx0_scr, mx1_scr, *, tq, lam_init):
    nq = q_ref.shape[0] // tq
    items = [(qi, j) for qi in range(nq) for j in range(qi + 1)]
    for i in range(nq):
        qx_scr[i * 2 * tq:(i + 1) * 2 * tq, :] = _stack_maps(q_ref[i * tq:(i + 1) * tq, :])
    chains = [slice(c * QUERY_LANES, (c + 1) * QUERY_LANES) for c in range(2 * tq // QUERY_LANES)]

    def visible_keys(cs, diagonal):
        return min(tq, cs.start % tq + QUERY_LANES) if diagonal else tq

    def scores(qi, j, s_scr, mx_scr):
        kt = k_ref[j * tq:(j + 1) * tq, :]
        for cs in chains:
            qx = qx_scr[qi * 2 * tq + cs.start:qi * 2 * tq + cs.stop, :]
            st = _nt_dot(kt, qx)
            s_scr[:, cs] = st
            mx_scr[:, cs] = jnp.max(st, axis=0, keepdims=True)

    def softmax_pv(j, s_scr, mx_scr, diagonal):
        start = j * tq
        for cs in chains:
            n_keys = visible_keys(cs, diagonal)
            if diagonal:
                lane = lax.broadcasted_iota(jnp.int32, (CHUNK, QUERY_LANES), 1)
                blocks = []
                for kc in range(n_keys // CHUNK):
                    blk = s_scr[kc * CHUNK:(kc + 1) * CHUNK, cs]
                    first_visible = kc * CHUNK - cs.start % tq
                    if first_visible > 0:
                        blk = jnp.where(lane >= first_visible, blk, NEG_INF)
                    blocks.append(blk)
                st = jnp.concatenate(blocks, axis=0)
                tile_max = jnp.max(st, axis=0, keepdims=True)
            else:
                st = s_scr[:, cs]
                tile_max = mx_scr[:, cs]
            vt = vt_ref[:, start:start + n_keys]
            p_scale = None
            if j == 0:
                m_new = tile_max
            else:
                m_old = m_scr[:, cs]
                m_new = jnp.maximum(m_old, tile_max)
                p_scale = jnp.exp2(m_old - m_new)
            p = jnp.exp2(st - m_new).astype(BF16)
            pv = _dot(vt, p)
            acc_scr[:, cs] = pv if p_scale is None else acc_scr[:, cs] * p_scale + pv
            m_scr[:, cs] = m_new

    def finish(qi):
        acc = acc_scr[...]
        o = acc[0:V_DIM, :] * (1.0 / acc[V_DIM:V_DIM + 1, :])
        lam = _diff_lambda(lq1_ref[...], lk1_ref[...], lq2_ref[...], lk2_ref[...], lam_init)
        d = (o[:, 0:tq] - lam * o[:, tq:2 * tq]).T
        o_ref[qi * tq:(qi + 1) * tq, :] = _sub_norm(d, gain_ref[...], lam_init).astype(BF16)

    bufs = ((s0_scr, mx0_scr), (s1_scr, mx1_scr))
    scores(*items[0], *bufs[0])
    for w, (qi, j) in enumerate(items):
        if w + 1 < len(items):
            scores(*items[w + 1], *bufs[(w + 1) % 2])
        softmax_pv(j, *bufs[w % 2], j == qi)
        if j == qi:
            finish(qi)


def _prompt_attention(lams, gain, q, kb, vt, bsz, seq_len, tq, lam_init):
    nq = seq_len // tq
    small = lambda b, h: (0, 0)
    return pl.pallas_call(
        functools.partial(_prompt_attn_kernel, tq=tq, lam_init=lam_init),
        grid=(bsz, N_HEADS),
        in_specs=[pl.BlockSpec((1, HEAD_DIM), small)] * 4 + [
            pl.BlockSpec((1, V_DIM), small),
            pl.BlockSpec((seq_len, V_DIM), lambda b, h: (b, h)),
            pl.BlockSpec((seq_len, V_DIM), lambda b, h: (b, h)),
            pl.BlockSpec((None, None, VT_ROWS, seq_len), lambda b, h: (b, h, 0, 0))],
        out_specs=pl.BlockSpec((seq_len, V_DIM), lambda b, h: (b, h)),
        out_shape=jax.ShapeDtypeStruct((bsz * seq_len, ATTN_WIDTH), BF16),
        scratch_shapes=[pltpu.VMEM((1, 2 * tq), F32), pltpu.VMEM((VT_ROWS, 2 * tq), F32),
                        pltpu.VMEM((nq * 2 * tq, V_DIM), BF16),
                        pltpu.VMEM((tq, 2 * tq), F32), pltpu.VMEM((tq, 2 * tq), F32),
                        pltpu.VMEM((1, 2 * tq), F32), pltpu.VMEM((1, 2 * tq), F32)],
        compiler_params=_cparams("parallel", "parallel"),
        name="prompt_attn",
    )(*lams, gain, q, kb, vt)


def _sample_attn_kernel(lq1_ref, lk1_ref, lq2_ref, lk2_ref, gain_ref, q_ref, k_ref, v_ref, ck_ref, cv_ref,
                        o_ref, *, lam_init):
    n_streams = ck_ref.shape[0]
    n_new = q_ref.shape[0] // n_streams
    past = ck_ref.shape[1] // N_HEADS
    lam = _diff_lambda(lq1_ref[...], lk1_ref[...], lq2_ref[...], lk2_ref[...], lam_init)
    gain = gain_ref[...]
    for s in range(n_streams):
        rows = slice(s * n_new, (s + 1) * n_new)
        for h in range(N_HEADS):
            sl = slice(h * V_DIM, (h + 1) * V_DIM)
            old = pl.ds(h, past, stride=N_HEADS)
            new = pl.ds(s * n_new * N_HEADS + h, n_new, stride=N_HEADS)
            qx = _stack_maps(q_ref[rows, sl])
            s_c = _nt_dot(qx, ck_ref[s, old, :].astype(BF16))
            s_n = _nt_dot(qx, k_ref[new, :].astype(BF16))
            m = jnp.maximum(jnp.max(s_c, axis=1, keepdims=True), jnp.max(s_n, axis=1, keepdims=True))
            p_c = jnp.exp2(s_c - m).astype(BF16)
            p_n = jnp.exp2(s_n - m).astype(BF16)
            vc = jnp.concatenate([cv_ref[s, old, :].astype(BF16), jnp.ones((past, V_DIM), BF16)], axis=1)
            vn = jnp.concatenate([v_ref[new, :].astype(BF16), jnp.ones((n_new, V_DIM), BF16)], axis=1)
            o_ext = _dot(p_c, vc) + _dot(p_n, vn)
            o = o_ext[:, 0:V_DIM] * (1.0 / o_ext[:, V_DIM:2 * V_DIM])
            d = o[0:n_new, :] - lam * o[n_new:2 * n_new, :]
            o_ref[rows, sl] = _sub_norm(d, gain, lam_init).astype(BF16)


def _sample_attention(lams, gain, q, k, v, cache_k, cache_v, lam_init):
    bsz, past_rows = cache_k.shape[0], cache_k.shape[1]
    n_new = q.shape[0] // bsz
    ns = SAMPLE_STREAMS_PER_STEP if bsz % SAMPLE_STREAMS_PER_STEP == 0 else 1
    small = lambda b: (0, 0)
    row = lambda b: (b, 0)
    return pl.pallas_call(
        functools.partial(_sample_attn_kernel, lam_init=lam_init),
        grid=(bsz // ns,),
        in_specs=[pl.BlockSpec((1, HEAD_DIM), small)] * 4 + [
            pl.BlockSpec((1, V_DIM), small),
            pl.BlockSpec((ns * n_new, ATTN_WIDTH), row),
            pl.BlockSpec((ns * n_new * N_HEADS, V_DIM), row),
            pl.BlockSpec((ns * n_new * N_HEADS, V_DIM), row),
            pl.BlockSpec((ns, past_rows, V_DIM), lambda b: (b, 0, 0)),
            pl.BlockSpec((ns, past_rows, V_DIM), lambda b: (b, 0, 0))],
        out_specs=pl.BlockSpec((ns * n_new, ATTN_WIDTH), row),
        out_shape=jax.ShapeDtypeStruct(q.shape, BF16),
        compiler_params=_cparams("parallel"),
        name="sample_attn",
    )(*lams, gain, q, k, v, cache_k, cache_v)


def _ssm_params(a_re, a_im, b_re, b_im, c_re, c_im, log_dt):
    g = a_re.shape[0]
    twice = lambda v: jnp.concatenate([v, v], axis=-1)
    rows = jnp.stack([twice(a_re), twice(a_im), jnp.broadcast_to(log_dt[:, None], (g, 2 * STATE_DIM))], axis=1)
    rows = jnp.pad(rows, ((0, 0), (0, 8 - rows.shape[1]), (0, 0)))
    return rows, twice(b_re.transpose(0, 2, 1)), twice(b_im.transpose(0, 2, 1)), twice(c_re), twice(c_im)


def _group_tables(rows_ref, bre_ref, bim_ref, cre_ref, cim_ref):
    lane = lax.broadcasted_iota(jnp.int32, (1, 2 * STATE_DIM), 1)
    lo = lane < STATE_DIM
    a_re, a_im, log_dt = rows_ref[0:1, :], rows_ref[1:2, :], rows_ref[2:3, :]
    dt = jnp.exp(log_dt)
    lam_re, lam_im = a_re * dt, a_im * dt
    mag = jnp.exp(lam_re)
    ar, ai = mag * jnp.cos(lam_im), mag * jnp.sin(lam_im)
    den = jnp.square(a_re) + jnp.square(a_im)
    cr = ((ar - 1.0) * a_re + ai * a_im) / den
    ci = (ai * a_re - (ar - 1.0) * a_im) / den
    bre, bim = bre_ref[...], bim_ref[...]
    bbr = cr * bre - ci * bim
    bbi = cr * bim + ci * bre
    quarter_turn = jnp.where(lo, 0.0, 0.5 * math.pi)

    def powers(tau):
        return jnp.exp(tau * lam_re) * jnp.cos(tau * lam_im - quarter_turn)

    def outer(y, x1, x2):
        y_sw = pltpu.roll(y, STATE_DIM, 1)
        prod = y[:, None, :] * x1[None, :, :] + y_sw[:, None, :] * x2[None, :, :]
        return prod.reshape(y.shape[0] * x1.shape[0], 2 * STATE_DIM)

    frames = lax.broadcasted_iota(jnp.int32, (CHUNK, 1), 0).astype(F32)
    wt = outer(powers((CHUNK - 1.0) - frames), bbr, jnp.where(lo, -bbi, bbi))
    w = wt.T
    cre, cim = cre_ref[...], cim_ref[...]
    vm = outer(powers(frames + 1.0), jnp.where(lo, cre, -cre), -cim)
    rrev = jnp.dot(jnp.where(lo, cre, -cim), w, precision=lax.Precision.HIGHEST, preferred_element_type=F32)
    a_row = powers(jnp.full((8, 1), float(CHUNK), F32))[0:1, :]
    a_col = jnp.broadcast_to(a_row, (2 * STATE_DIM, 2 * STATE_DIM)).T
    a_swap = pltpu.roll(a_col, STATE_DIM, 0)
    top = lax.broadcasted_iota(jnp.int32, a_col.shape, 0) < STATE_DIM
    return (rrev, w.astype(BF16), vm.astype(BF16),
            jnp.where(top, a_col, a_swap), jnp.where(top, -a_swap, a_col))


def _chunk_cols_kernel(u_ref, o_ref):
    n_s, n_r, n_j = u_ref.shape[0], u_ref.shape[1], u_ref.shape[2]
    pad = LANES - n_s * n_r
    for jj in range(n_j):
        rows = [u_ref[s, :, jj, :] for s in range(n_s)]
        if pad:
            rows.append(jnp.zeros((pad, SSM_WIDTH), F32))
        cols = jnp.concatenate(rows, axis=0).T
        o_ref[:, jj * GROUP_SIZE:(jj + 1) * GROUP_SIZE, :] = cols.reshape(N_GROUPS, GROUP_SIZE, LANES).astype(BF16)


def _chunk_cols(u4, frames_per_step=16):
    s_total, n_r = u4.shape[0], u4.shape[1]
    s_step = min(s_total, LANES // n_r)
    assert s_step >= 1 and s_total % s_step == 0
    n_lane_blocks = s_total // s_step
    return pl.pallas_call(
        _chunk_cols_kernel,
        grid=(n_lane_blocks, CHUNK // frames_per_step),
        in_specs=[pl.BlockSpec((s_step, n_r, frames_per_step, SSM_WIDTH), lambda a, j: (a, 0, j, 0))],
        out_specs=pl.BlockSpec((N_GROUPS, frames_per_step * GROUP_SIZE, LANES), lambda a, j: (0, j, a)),
        out_shape=jax.ShapeDtypeStruct((N_GROUPS, CHUNK * GROUP_SIZE, n_lane_blocks * LANES), BF16),
        compiler_params=_cparams("parallel", "parallel"),
        name="ssm_in",
    )(u4)


def _unchunk_cols_kernel(y_ref, o_ref):
    n_s, n_r, n_j = o_ref.shape[0], o_ref.shape[1], o_ref.shape[2]
    for jj in range(n_j):
        cols = y_ref[:, jj * GROUP_SIZE:(jj + 1) * GROUP_SIZE, :].reshape(SSM_WIDTH, LANES)
        rows = cols.T
        for s in range(n_s):
            o_ref[s, :, jj, :] = rows[s * n_r:(s + 1) * n_r, :]


def _unchunk_cols(y_cols, s_total, n_r, frames_per_step=16):
    s_step = min(s_total, LANES // n_r)
    n_lane_blocks = s_total // s_step
    return pl.pallas_call(
        _unchunk_cols_kernel,
        grid=(n_lane_blocks, CHUNK // frames_per_step),
        in_specs=[pl.BlockSpec((N_GROUPS, frames_per_step * GROUP_SIZE, LANES), lambda a, j: (0, j, a))],
        out_specs=pl.BlockSpec((s_step, n_r, frames_per_step, SSM_WIDTH), lambda a, j: (a, 0, j, 0)),
        out_shape=jax.ShapeDtypeStruct((s_total, n_r, CHUNK, SSM_WIDTH), F32),
        compiler_params=_cparams("parallel", "parallel"),
        name="ssm_out",
    )(y_cols)


def _ssm_kernel(*refs, n_chunk):
    for gi in range(SSM_GROUPS_PER_STEP):
        _ssm_group(*[r.at[gi] for r in refs], n_chunk=n_chunk)


def _ssm_group(rows_ref, bre_ref, bim_ref, cre_ref, cim_ref, xp_ref, xs_ref, s0_ref, yp_ref, ys_ref, sp_ref, ss_ref,
               mt_scr, *, n_chunk):
    width = GROUP_SIZE * CHUNK
    n_piece = width // LANES
    lane16 = lax.broadcasted_iota(jnp.int32, (GROUP_SIZE, LANES), 1)
    rrev, w, vm_all, a_c, a_s = _group_tables(rows_ref, bre_ref, bim_ref, cre_ref, cim_ref)
    pieces = [rrev[:, k * LANES:(k + 1) * LANES] for k in range(n_piece)] + [jnp.zeros((GROUP_SIZE, LANES), F32)]
    rolled = {0: pieces}
    for b in range(GROUP_SIZE, LANES, GROUP_SIZE):
        rolled[b] = [pltpu.roll(p, LANES - b, 1) for p in pieces[:n_piece]] + [pieces[n_piece]]
    def toeplitz_rows(t0, t1, n_cols):
        for t in range(t0, t1):
            shift = GROUP_SIZE * (CHUNK - 1 - t)
            a, b = shift // LANES, shift % LANES
            for v in range(n_cols // LANES):
                k = v + a
                if k >= n_piece:
                    blk = pieces[n_piece]
                elif b == 0:
                    blk = pieces[k]
                else:
                    blk = jnp.where(lane16 < LANES - b, rolled[b][k], rolled[b][k + 1])
                mt_scr[t * GROUP_SIZE:(t + 1) * GROUP_SIZE, v * LANES:(v + 1) * LANES] = blk.astype(BF16)

    xp, xs = xp_ref[...], xs_ref[...]
    sloc_p, sloc_s = _dot(w, xp), _dot(w, xs)

    def cmul(pc, ps, s):
        return pc * s + ps * pltpu.roll(s, STATE_DIM, 0)

    lane = lax.broadcasted_iota(jnp.int32, (2 * STATE_DIM, LANES), 1) % n_chunk
    prev_cols = []
    lane_id = lax.broadcasted_iota(jnp.int32, (2 * STATE_DIM, LANES), 1)
    final = jnp.zeros((2 * STATE_DIM, LANES), F32)
    seqs_per_tile = LANES // n_chunk
    for tile in range(xp.shape[1] // LANES):
        s_inc = sloc_p[:, tile * LANES:(tile + 1) * LANES]
        pc, ps = a_c, a_s
        dist = 1
        while dist < n_chunk:
            shifted = jnp.where(lane >= dist, pltpu.roll(s_inc, dist, 1), 0.0)
            s_inc = s_inc + cmul(pc, ps, shifted)
            pc, ps = pc * pc - ps * ps, 2.0 * pc * ps
            dist *= 2
        for k in range(seqs_per_tile):
            src, dst = (k + 1) * n_chunk - 1, tile * seqs_per_tile + k
            final = jnp.where(lane_id == dst, pltpu.roll(s_inc, (dst - src) % LANES, 1), final)
        prev_cols.append(jnp.where(lane >= 1, pltpu.roll(s_inc, 1, 1), 0.0))
    sp_ref[...] = final
    s_prev_p = jnp.concatenate(prev_cols, axis=1).astype(BF16)
    s0 = s0_ref[...]
    s0b = s0.astype(BF16)
    frames = TOEPLITZ_ROWS // GROUP_SIZE
    for i in range(width // TOEPLITZ_ROWS):
        rows = slice(i * TOEPLITZ_ROWS, (i + 1) * TOEPLITZ_ROWS)
        n_cols = (i + 1) * TOEPLITZ_ROWS
        toeplitz_rows(i * frames, (i + 1) * frames, n_cols)
        mt = mt_scr[rows, 0:n_cols]
        vm = vm_all[rows, :]
        yp_ref[rows, :] = _dot(mt, xp[0:n_cols, :]) + _dot(vm, s_prev_p)
        ys_ref[rows, :] = _dot(mt, xs[0:n_cols, :]) + _dot(vm, s0b)
    ss_ref[...] = cmul(a_c, a_s, s0) + sloc_s


def _ssm(params, x_p, x_s, s0, n_chunk):
    n_groups, lanes_p = x_p.shape[0], x_p.shape[2]
    width = GROUP_SIZE * CHUNK
    per_g = lambda g: (g, 0, 0)
    state_rows = 2 * STATE_DIM
    gb = SSM_GROUPS_PER_STEP
    assert n_groups % gb == 0
    spec = lambda rows, cols: pl.BlockSpec((gb, rows, cols), per_g)
    return pl.pallas_call(
        functools.partial(_ssm_kernel, n_chunk=n_chunk),
        grid=(n_groups // gb,),
        in_specs=[spec(8, state_rows)] + [spec(GROUP_SIZE, state_rows)] * 4 + [
                  spec(width, lanes_p), spec(width, LANES), spec(state_rows, LANES)],
        out_specs=[spec(width, lanes_p), spec(width, LANES), spec(state_rows, LANES), spec(state_rows, LANES)],
        out_shape=[jax.ShapeDtypeStruct((n_groups, width, lanes_p), F32),
                   jax.ShapeDtypeStruct((n_groups, width, LANES), F32),
                   jax.ShapeDtypeStruct((n_groups, state_rows, LANES), F32),
                   jax.ShapeDtypeStruct((n_groups, state_rows, LANES), F32)],
        scratch_shapes=[pltpu.VMEM((gb, width, width), BF16)],
        compiler_params=_cparams("parallel"),
        name="ssm",
    )(*params, x_p, x_s, s0)


def _layer_norm(x, g, b):
    mu = jnp.mean(x, axis=1, keepdims=True)
    xc = x - mu
    var = jnp.mean(jnp.square(xc), axis=1, keepdims=True)
    return xc * lax.rsqrt(var + LN_EPS) * g + b


def _gelu_tanh(x):
    return 0.5 * x * (1.0 + jnp.tanh(math.sqrt(2.0 / math.pi) * (x + 0.044715 * (x * x * x))))


def _merge_kernel(x_ref, ao_ref, ys_ref, u_ref, d_ref, wg_ref, wap_ref, wglu_ref, wout_ref, g1_ref, b1_ref, h_ref):
    for r in range(x_ref.shape[0] // SUB_ROWS):
        rows = slice(r * SUB_ROWS, (r + 1) * SUB_ROWS)
        x = x_ref[rows, :]
        xb = x.astype(BF16)
        a_branch = _dot(ao_ref[rows, :], wap_ref[...])
        s_act = _gelu_tanh(ys_ref[rows, :] + d_ref[...] * u_ref[rows, :]).astype(BF16)
        s_branch = (_dot(s_act, wglu_ref[:, 0:D_MODEL])
                    * jax.nn.sigmoid(_dot(s_act, wglu_ref[:, D_MODEL:2 * D_MODEL])))
        m = (jax.nn.sigmoid(_dot(xb, wg_ref[:, 0:D_MODEL])) * a_branch
             + jax.nn.sigmoid(_dot(xb, wg_ref[:, D_MODEL:2 * D_MODEL])) * s_branch)
        h_ref[rows, :] = _layer_norm(DEEPNORM_ALPHA * x + _dot(m.astype(BF16), wout_ref[...]),
                                     g1_ref[...], b1_ref[...])


def _merge(x2d, ao, ys, u, d, w_gate, w_ap, w_glu, w_out, ln_g, ln_b, tm):
    t_tokens = x2d.shape[0]
    row = lambda i: (i, 0)
    const = lambda i: (0, 0)
    return pl.pallas_call(
        _merge_kernel,
        grid=(t_tokens // tm,),
        in_specs=[pl.BlockSpec((tm, D_MODEL), row),
                  pl.BlockSpec((tm, ATTN_WIDTH), row),
                  pl.BlockSpec((tm, SSM_WIDTH), row),
                  pl.BlockSpec((tm, SSM_WIDTH), row),
                  pl.BlockSpec((1, SSM_WIDTH), const),
                  pl.BlockSpec((D_MODEL, 2 * D_MODEL), const, pipeline_mode=pl.Buffered(1)),
                  pl.BlockSpec((ATTN_WIDTH, D_MODEL), const, pipeline_mode=pl.Buffered(1)),
                  pl.BlockSpec((SSM_WIDTH, 2 * D_MODEL), const, pipeline_mode=pl.Buffered(1)),
                  pl.BlockSpec((D_MODEL, D_MODEL), const, pipeline_mode=pl.Buffered(1)),
                  pl.BlockSpec((1, D_MODEL), const),
                  pl.BlockSpec((1, D_MODEL), const)],
        out_specs=pl.BlockSpec((tm, D_MODEL), row),
        out_shape=jax.ShapeDtypeStruct((t_tokens, D_MODEL), F32),
        compiler_params=_cparams("parallel"),
        name="merge",
    )(x2d, ao, ys, u, d, w_gate, w_ap, w_glu, w_out, ln_g, ln_b)


def _mlp_kernel(h_ref, w1_ref, w2_ref, g2_ref, b2_ref, o_ref, *, ff_chunk):
    for r in range(h_ref.shape[0] // SUB_ROWS):
        rows = slice(r * SUB_ROWS, (r + 1) * SUB_ROWS)
        h = h_ref[rows, :]
        hb = h.astype(BF16)
        f = jnp.zeros(h.shape, F32)
        for c in range(D_FF // ff_chunk):
            sl = slice(c * ff_chunk, (c + 1) * ff_chunk)
            t = jnp.maximum(_dot(hb, w1_ref[:, sl]), 0.0)
            f = f + _dot((t * t).astype(BF16), w2_ref[sl, :])
        o_ref[rows, :] = _layer_norm(DEEPNORM_ALPHA * h + f, g2_ref[...], b2_ref[...])


def _mlp(h, w1, w2, ln_g, ln_b, tm, ff_chunk=1024):
    t_tokens = h.shape[0]
    row = lambda i: (i, 0)
    const = lambda i: (0, 0)
    return pl.pallas_call(
        functools.partial(_mlp_kernel, ff_chunk=ff_chunk),
        grid=(t_tokens // tm,),
        in_specs=[pl.BlockSpec((tm, D_MODEL), row),
                  pl.BlockSpec((D_MODEL, D_FF), const, pipeline_mode=pl.Buffered(1)),
                  pl.BlockSpec((D_FF, D_MODEL), const, pipeline_mode=pl.Buffered(1)),
                  pl.BlockSpec((1, D_MODEL), const),
                  pl.BlockSpec((1, D_MODEL), const)],
        out_specs=pl.BlockSpec((tm, D_MODEL), row),
        out_shape=jax.ShapeDtypeStruct((t_tokens, D_MODEL), F32),
        compiler_params=_cparams("parallel"),
        name="mlp",
    )(h, w1, w2, ln_g, ln_b)


def _rope_tables(pos):
    inv = 1.0 / (ROPE_THETA ** (jnp.arange(0, HEAD_DIM, 2, dtype=F32) / HEAD_DIM))
    ang = pos.astype(F32)[:, None] * inv[None, :]
    c, s = jnp.cos(ang), jnp.sin(ang)
    reps = LANES // HEAD_DIM
    return jnp.tile(jnp.concatenate([c, c], axis=1), (1, reps)), jnp.tile(jnp.concatenate([-s, s], axis=1), (1, reps))


def kernel(x_prompt, x_sample, cache_k, cache_v, state_ssm_re, state_ssm_im, w_in, lambda_q1, lambda_k1, lambda_q2, lambda_k2, subln_gain, ssm_a_re, ssm_a_im, ssm_b_re, ssm_b_im, ssm_c_re, ssm_c_im, ssm_d, ssm_log_dt, w_attn_proj, w_glu_a, w_glu_b, w_out, ln1_g, ln1_b, w_ff1, w_ff2, ln2_g, ln2_b):
    bp, n_p = x_prompt.shape[0], x_prompt.shape[1]
    bs, n_s = x_sample.shape[0], x_sample.shape[1]
    past = cache_k.shape[2]
    assert w_in.shape[0] == DEPTH and n_s == CHUNK and n_p % CHUNK == 0
    n_chunk = n_p // CHUNK
    tm = min(512, n_p)
    tm_s = min(512, bs * n_s)
    tq = min(512, n_p)
    l = 0
    lam_init = 0.8 - 0.6 * math.exp(-0.3 * l)

    xp = x_prompt.reshape(bp * n_p, D_MODEL)
    xs = x_sample.reshape(bs * n_s, D_MODEL)
    w_qkvu = w_in[l, :, 0:QKVU_COLS].astype(BF16)
    w_gate = w_in[l, :, QKVU_COLS:].astype(BF16)
    w_ap = w_attn_proj[l].astype(BF16)
    w_glu = jnp.concatenate([w_glu_a[l], w_glu_b[l]], axis=1).astype(BF16)
    w_o = w_out[l].astype(BF16)
    w1, w2 = w_ff1[l].astype(BF16), w_ff2[l].astype(BF16)
    lams = [v[l].reshape(1, HEAD_DIM) for v in (lambda_q1, lambda_k1, lambda_q2, lambda_k2)]
    gain = subln_gain[l].reshape(1, V_DIM)
    d_skip = ssm_d[l].reshape(1, SSM_WIDTH)
    lng = [v[l].reshape(1, D_MODEL) for v in (ln1_g, ln1_b, ln2_g, ln2_b)]

    cos_p, sin_p = _rope_tables(jnp.arange(n_p))
    cos_s, sin_s = _rope_tables(jnp.tile(past + jnp.arange(n_s), tm_s // n_s))

    q_p, k_p, v_p, u_p, kb_p, vt_p = _project(xp, w_qkvu, cos_p, sin_p, n_p, tm, True)
    q_s, k_s, v_s, u_s = _project(xs, w_qkvu, cos_s, sin_s, n_s, tm_s, False)

    ao_p = _prompt_attention(lams, gain, q_p, kb_p, vt_p, bp, n_p, tq, lam_init)
    ao_s = _sample_attention(lams, gain, q_s, k_s, v_s,
                             cache_k[l].reshape(bs, past * N_HEADS, V_DIM),
                             cache_v[l].reshape(bs, past * N_HEADS, V_DIM), lam_init)

    tables = _ssm_params(ssm_a_re[l], ssm_a_im[l], ssm_b_re[l], ssm_b_im[l],
                         ssm_c_re[l], ssm_c_im[l], ssm_log_dt[l])
    x_cols_p = _chunk_cols(u_p.reshape(bp, n_chunk, CHUNK, SSM_WIDTH))
    x_cols_s = _chunk_cols(u_s.reshape(1, bs, CHUNK, SSM_WIDTH))
    s0 = jnp.concatenate([state_ssm_re[l], state_ssm_im[l]], axis=-1).transpose(1, 2, 0)
    s0 = jnp.pad(s0, ((0, 0), (0, 0), (0, LANES - bs)))
    y_cols_p, y_cols_s, st_p, st_s = _ssm(tables, x_cols_p, x_cols_s, s0, n_chunk)
    ys_p = _unchunk_cols(y_cols_p, bp, n_chunk).reshape(bp * n_p, SSM_WIDTH)
    ys_s = _unchunk_cols(y_cols_s, 1, bs).reshape(bs * n_s, SSM_WIDTH)
    sf_p = st_p[:, :, 0:bp]
    sf_s = st_s[:, :, 0:bs]

    outs = []
    for x2d, ao, ys, u, tile in ((xp, ao_p, ys_p, u_p, tm), (xs, ao_s, ys_s, u_s, tm_s)):
        big = 2 * tile if x2d.shape[0] >= 8 * tile else tile
        h = _merge(x2d, ao, ys, u, d_skip, w_gate, w_ap, w_glu, w_o, lng[0], lng[1], big)
        outs.append(_mlp(h, w1, w2, lng[2], lng[3], big))

    def states(sf):
        t = sf.transpose(2, 0, 1)
        return t[None, :, :, 0:STATE_DIM], t[None, :, :, STATE_DIM:]

    srp, sip = states(sf_p)
    srs, sis = states(sf_s)
    return (outs[0].reshape(bp, n_p, D_MODEL), outs[1].reshape(bs, n_s, D_MODEL),
            k_p.reshape(1, bp, n_p, N_HEADS, V_DIM), v_p.reshape(1, bp, n_p, N_HEADS, V_DIM), srp, sip,
            k_s.reshape(1, bs, n_s, N_HEADS, V_DIM), v_s.reshape(1, bs, n_s, N_HEADS, V_DIM), srs, sis)
```

```python
import functools
import math

import jax
import jax.numpy as jnp
from jax import lax
from jax.experimental import pallas as pl
from jax.experimental.pallas import tpu as pltpu

D_MODEL = 1024
CHUNK = 64
N_HEADS = 4
HEAD_DIM = 64
V_DIM = 2 * HEAD_DIM
ATTN_WIDTH = N_HEADS * V_DIM
SSM_WIDTH = 512
GROUP_SIZE = 16
N_GROUPS = SSM_WIDTH // GROUP_SIZE
STATE_DIM = 64
D_FF = 4 * D_MODEL
ROPE_THETA = 10000.0
LN_EPS = 1e-5
RMS_EPS = 1e-5
NEG_INF = -1e30
DEPTH = 1
DEEPNORM_ALPHA = (2.0 * DEPTH) ** 0.25
QKVU_COLS = 3 * ATTN_WIDTH + SSM_WIDTH
LOG2E = 1.4426950408889634

LANES = 128
VT_ROWS = V_DIM + 16
QUERY_LANES = 256
TOEPLITZ_ROWS = 256
SUB_ROWS = 256
ATTN_REGIONS = 3
SSM_GROUPS_PER_STEP = 4
SAMPLE_STREAMS_PER_STEP = 2
VMEM_LIMIT = 56 * 1024 * 1024

F32 = jnp.float32
BF16 = jnp.bfloat16


def _cparams(*sem):
    return pltpu.CompilerParams(dimension_semantics=sem, vmem_limit_bytes=VMEM_LIMIT)


def _nt_dot(a, b):
    return lax.dot_general(a, b, (((1,), (1,)), ((), ())), preferred_element_type=F32)


def _dot(a, b):
    return jnp.dot(a, b, preferred_element_type=F32)


def _rotary(z, cos, sin_signed, first_half):
    swapped = jnp.where(first_half, pltpu.roll(z, 96, 1), pltpu.roll(z, 32, 1))
    return z * cos + swapped * sin_signed


def _proj_kernel(x_ref, w_ref, cos_ref, sin_ref, q_ref, k_ref, v_ref, u_ref, *rest, emit_t):
    xb = x_ref[...].astype(BF16)
    cos = cos_ref[...]
    sin = sin_ref[...]
    lane = lax.broadcasted_iota(jnp.int32, cos.shape, 1)
    first_half = (lane % HEAD_DIM) < (HEAD_DIM // 2)
    tm = xb.shape[0]
    zq = _dot(xb, w_ref[:, 0:ATTN_WIDTH])
    zk = _dot(xb, w_ref[:, ATTN_WIDTH:2 * ATTN_WIDTH])
    zv = _dot(xb, w_ref[:, 2 * ATTN_WIDTH:3 * ATTN_WIDTH])
    for h in range(N_HEADS):
        sl = slice(h * V_DIM, (h + 1) * V_DIM)
        q_ref[:, sl] = (_rotary(zq[:, sl], cos, sin, first_half) * (LOG2E * HEAD_DIM ** -0.5)).astype(BF16)
        kr = _rotary(zk[:, sl], cos, sin, first_half)
        k_ref[pl.ds(h, tm, stride=N_HEADS), :] = kr
        v_ref[pl.ds(h, tm, stride=N_HEADS), :] = zv[:, sl]
        if emit_t:
            rest[0][:, sl] = kr.astype(BF16)
    if emit_t:
        vt_ref = rest[1]
        zvt = zv.T.astype(BF16)
        ones = jnp.ones((VT_ROWS - V_DIM, zvt.shape[1]), BF16)
        for h in range(N_HEADS):
            vt_ref[h, 0:V_DIM, :] = zvt[h * V_DIM:(h + 1) * V_DIM, :]
            vt_ref[h, V_DIM:VT_ROWS, :] = ones
    u_ref[...] = _dot(xb, w_ref[:, 3 * ATTN_WIDTH:QKVU_COLS])


def _project(x2d, w_qkvu, cos_t, sin_t, seq_len, tm, emit_t):
    t_tokens = x2d.shape[0]
    n_tiles = t_tokens // tm
    n_pos_tiles = cos_t.shape[0] // tm
    tiles_per_seq = max(seq_len // tm, 1)
    row = lambda i: (i, 0)
    pos = lambda i: (i % n_pos_tiles, 0)
    out_shape = [jax.ShapeDtypeStruct((t_tokens, ATTN_WIDTH), BF16),
                 jax.ShapeDtypeStruct((t_tokens * N_HEADS, V_DIM), F32),
                 jax.ShapeDtypeStruct((t_tokens * N_HEADS, V_DIM), F32),
                 jax.ShapeDtypeStruct((t_tokens, SSM_WIDTH), F32)]
    out_specs = ([pl.BlockSpec((tm, ATTN_WIDTH), row)] + [pl.BlockSpec((tm * N_HEADS, V_DIM), row)] * 2
                 + [pl.BlockSpec((tm, SSM_WIDTH), row)])
    if emit_t:
        bsz = t_tokens // seq_len
        out_shape += [jax.ShapeDtypeStruct((t_tokens, ATTN_WIDTH), BF16),
                      jax.ShapeDtypeStruct((bsz, N_HEADS, VT_ROWS, seq_len), BF16)]
        out_specs += [pl.BlockSpec((tm, ATTN_WIDTH), row),
                      pl.BlockSpec((None, N_HEADS, VT_ROWS, tm),
                                   lambda i: (i // tiles_per_seq, 0, 0, i % tiles_per_seq))]
    return pl.pallas_call(
        functools.partial(_proj_kernel, emit_t=emit_t),
        grid=(n_tiles,),
        in_specs=[pl.BlockSpec((tm, D_MODEL), row),
                  pl.BlockSpec((D_MODEL, QKVU_COLS), lambda i: (0, 0)),
                  pl.BlockSpec((tm, LANES), pos),
                  pl.BlockSpec((tm, LANES), pos)],
        out_specs=out_specs,
        out_shape=out_shape,
        compiler_params=_cparams("parallel"),
        name="proj_t" if emit_t else "proj",
    )(x2d, w_qkvu, cos_t, sin_t)


def _diff_lambda(lq1, lk1, lq2, lk2, lam_init):
    return (jnp.exp(jnp.sum(lq1 * lk1, axis=1, keepdims=True))
            - jnp.exp(jnp.sum(lq2 * lk2, axis=1, keepdims=True)) + lam_init)


def _sub_norm(d, gain, lam_init):
    ms = jnp.mean(jnp.square(d), axis=1, keepdims=True)
    return d * lax.rsqrt(ms + RMS_EPS) * gain * (1.0 - lam_init)


def _stack_maps(q):
    lane = lax.broadcasted_iota(jnp.int32, q.shape, 1)
    zero = jnp.zeros_like(q)
    return jnp.concatenate([jnp.where(lane < HEAD_DIM, q, zero), jnp.where(lane >= HEAD_DIM, q, zero)], axis=0)


def _prompt_attn_kernel(lq1_ref, lk1_ref, lq2_ref, lk2_ref, gain_ref, q_ref, k_ref, vt_ref,
                        o_ref, m_scr, acc_scr, qx_scr, s0_scr, s1_scr, mx0_scr, mx1_scr, *, tq, lam_init):
    nq = q_ref.shape[0] // tq
    items = [(qi, j) for qi in range(nq) for j in range(qi + 1)]
    for i in range(nq):
        qx_scr[i * 2 * tq:(i + 1) * 2 * tq, :] = _stack_maps(q_ref[i * tq:(i + 1) * tq, :])
    chains = [slice(c * QUERY_LANES, (c + 1) * QUERY_LANES) for c in range(2 * tq // QUERY_LANES)]

    def visible_keys(cs, diagonal):
        return min(tq, cs.start % tq + QUERY_LANES) if diagonal else tq

    def scores(qi, j, s_scr, mx_scr):
        kt = k_ref[j * tq:(j + 1) * tq, :]
        for cs in chains:
            qx = qx_scr[qi * 2 * tq + cs.start:qi * 2 * tq + cs.stop, :]
            st = _nt_dot(kt, qx)
            s_scr[:, cs] = st
            mx_scr[:, cs] = jnp.max(st, axis=0, keepdims=True)

    def softmax_pv(j, s_scr, mx_scr, diagonal):
        start = j * tq
        for cs in chains:
            n_keys = visible_keys(cs, diagonal)
            if diagonal:
                lane = lax.broadcasted_iota(jnp.int32, (CHUNK, QUERY_LANES), 1)
                blocks = []
                for kc in range(n_keys // CHUNK):
                    blk = s_scr[kc * CHUNK:(kc + 1) * CHUNK, cs]
                    first_visible = kc * CHUNK - cs.start % tq
                    if first_visible > 0:
                        blk = jnp.where(lane >= first_visible, blk, NEG_INF)
                    blocks.append(blk)
                st = jnp.concatenate(blocks, axis=0)
                tile_max = jnp.max(st, axis=0, keepdims=True)
            else:
                st = s_scr[:, cs]
                tile_max = mx_scr[:, cs]
            vt = vt_ref[:, start:start + n_keys]
            p_scale = None
            if j == 0:
                m_new = tile_max
            else:
                m_old = m_scr[:, cs]
                m_new = jnp.maximum(m_old, tile_max)
                p_scale = jnp.exp2(m_old - m_new)
            p = jnp.exp2(st - m_new).astype(BF16)
            pv = _dot(vt, p)
            acc_scr[:, cs] = pv if p_scale is None else acc_scr[:, cs] * p_scale + pv
            m_scr[:, cs] = m_new

    def finish(qi):
        acc = acc_scr[...]
        o = acc[0:V_DIM, :] * (1.0 / acc[V_DIM:V_DIM + 1, :])
        lam = _diff_lambda(lq1_ref[...], lk1_ref[...], lq2_ref[...], lk2_ref[...], lam_init)
        d = (o[:, 0:tq] - lam * o[:, tq:2 * tq]).T
        o_ref[qi * tq:(qi + 1) * tq, :] = _sub_norm(d, gain_ref[...], lam_init).astype(BF16)

    bufs = ((s0_scr, mx0_scr), (s1_scr, mx1_scr))
    scores(*items[0], *bufs[0])

    def run_items(first, last):
        for w in range(first, last):
            qi, j = items[w]
            if w + 1 < len(items):
                scores(*items[w + 1], *bufs[(w + 1) % 2])
            softmax_pv(j, *bufs[w % 2], j == qi)
            if j == qi:
                finish(qi)

    bounds = [len(items) * r // ATTN_REGIONS for r in range(ATTN_REGIONS + 1)]
    once = jnp.minimum(pl.program_id(0) + 1, 1)
    for first, last in zip(bounds[:-1], bounds[1:]):
        def region(_, carry, first=first, last=last):
            run_items(first, last)
            return carry
        lax.fori_loop(0, once, region, 0)


def _prompt_attention(lams, gain, q, kb, vt, bsz, seq_len, tq, lam_init):
    nq = seq_len // tq
    small = lambda b, h: (0, 0)
    return pl.pallas_call(
        functools.partial(_prompt_attn_kernel, tq=tq, lam_init=lam_init),
        grid=(bsz, N_HEADS),
        in_specs=[pl.BlockSpec((1, HEAD_DIM), small)] * 4 + [
            pl.BlockSpec((1, V_DIM), small),
            pl.BlockSpec((seq_len, V_DIM), lambda b, h: (b, h)),
            pl.BlockSpec((seq_len, V_DIM), lambda b, h: (b, h)),
            pl.BlockSpec((None, None, VT_ROWS, seq_len), lambda b, h: (b, h, 0, 0))],
        out_specs=pl.BlockSpec((seq_len, V_DIM), lambda b, h: (b, h)),
        out_shape=jax.ShapeDtypeStruct((bsz * seq_len, ATTN_WIDTH), BF16),
        scratch_shapes=[pltpu.VMEM((1, 2 * tq), F32), pltpu.VMEM((VT_ROWS, 2 * tq), F32),
                        pltpu.VMEM((nq * 2 * tq, V_DIM), BF16),
                        pltpu.VMEM((tq, 2 * tq), F32), pltpu.VMEM((tq, 2 * tq), F32),
                        pltpu.VMEM((1, 2 * tq), F32), pltpu.VMEM((1, 2 * tq), F32)],
        compiler_params=_cparams("parallel", "parallel"),
        name="prompt_attn",
    )(*lams, gain, q, kb, vt)


def _sample_attn_kernel(lq1_ref, lk1_ref, lq2_ref, lk2_ref, gain_ref, q_ref, k_ref, v_ref, ck_ref, cv_ref,
                        o_ref, *, lam_init):
    n_streams = ck_ref.shape[0]
    n_new = q_ref.shape[0] // n_streams
    past = ck_ref.shape[1] // N_HEADS
    lam = _diff_lambda(lq1_ref[...], lk1_ref[...], lq2_ref[...], lk2_ref[...], lam_init)
    gain = gain_ref[...]
    for s in range(n_streams):
        rows = slice(s * n_new, (s + 1) * n_new)
        for h in range(N_HEADS):
            sl = slice(h * V_DIM, (h + 1) * V_DIM)
            old = pl.ds(h, past, stride=N_HEADS)
            new = pl.ds(s * n_new * N_HEADS + h, n_new, stride=N_HEADS)
            qx = _stack_maps(q_ref[rows, sl])
            s_c = _nt_dot(qx, ck_ref[s, old, :].astype(BF16))
            s_n = _nt_dot(qx, k_ref[new, :].astype(BF16))
            m = jnp.maximum(jnp.max(s_c, axis=1, keepdims=True), jnp.max(s_n, axis=1, keepdims=True))
            p_c = jnp.exp2(s_c - m).astype(BF16)
            p_n = jnp.exp2(s_n - m).astype(BF16)
            vc = jnp.concatenate([cv_ref[s, old, :].astype(BF16), jnp.ones((past, V_DIM), BF16)], axis=1)
            vn = jnp.concatenate([v_ref[new, :].astype(BF16), jnp.ones((n_new, V_DIM), BF16)], axis=1)
            o_ext = _dot(p_c, vc) + _dot(p_n, vn)
            o = o_ext[:, 0:V_DIM] * (1.0 / o_ext[:, V_DIM:2 * V_DIM])
            d = o[0:n_new, :] - lam * o[n_new:2 * n_new, :]
            o_ref[rows, sl] = _sub_norm(d, gain, lam_init).astype(BF16)


def _sample_attention(lams, gain, q, k, v, cache_k, cache_v, lam_init):
    bsz, past_rows = cache_k.shape[0], cache_k.shape[1]
    n_new = q.shape[0] // bsz
    ns = SAMPLE_STREAMS_PER_STEP if bsz % SAMPLE_STREAMS_PER_STEP == 0 else 1
    small = lambda b: (0, 0)
    row = lambda b: (b, 0)
    return pl.pallas_call(
        functools.partial(_sample_attn_kernel, lam_init=lam_init),
        grid=(bsz // ns,),
        in_specs=[pl.BlockSpec((1, HEAD_DIM), small)] * 4 + [
            pl.BlockSpec((1, V_DIM), small),
            pl.BlockSpec((ns * n_new, ATTN_WIDTH), row),
            pl.BlockSpec((ns * n_new * N_HEADS, V_DIM), row),
            pl.BlockSpec((ns * n_new * N_HEADS, V_DIM), row),
            pl.BlockSpec((ns, past_rows, V_DIM), lambda b: (b, 0, 0)),
            pl.BlockSpec((ns, past_rows, V_DIM), lambda b: (b, 0, 0))],
        out_specs=pl.BlockSpec((ns * n_new, ATTN_WIDTH), row),
        out_shape=jax.ShapeDtypeStruct(q.shape, BF16),
        compiler_params=_cparams("parallel"),
        name="sample_attn",
    )(*lams, gain, q, k, v, cache_k, cache_v)


def _ssm_params(a_re, a_im, b_re, b_im, c_re, c_im, log_dt):
    g = a_re.shape[0]
    twice = lambda v: jnp.concatenate([v, v], axis=-1)
    rows = jnp.stack([twice(a_re), twice(a_im), jnp.broadcast_to(log_dt[:, None], (g, 2 * STATE_DIM))], axis=1)
    rows = jnp.pad(rows, ((0, 0), (0, 8 - rows.shape[1]), (0, 0)))
    return rows, twice(b_re.transpose(0, 2, 1)), twice(b_im.transpose(0, 2, 1)), twice(c_re), twice(c_im)


def _group_tables(rows_ref, bre_ref, bim_ref, cre_ref, cim_ref):
    lane = lax.broadcasted_iota(jnp.int32, (1, 2 * STATE_DIM), 1)
    lo = lane < STATE_DIM
    a_re, a_im, log_dt = rows_ref[0:1, :], rows_ref[1:2, :], rows_ref[2:3, :]
    dt = jnp.exp(log_dt)
    lam_re, lam_im = a_re * dt, a_im * dt
    mag = jnp.exp(lam_re)
    ar, ai = mag * jnp.cos(lam_im), mag * jnp.sin(lam_im)
    den = jnp.square(a_re) + jnp.square(a_im)
    cr = ((ar - 1.0) * a_re + ai * a_im) / den
    ci = (ai * a_re - (ar - 1.0) * a_im) / den
    bre, bim = bre_ref[...], bim_ref[...]
    bbr = cr * bre - ci * bim
    bbi = cr * bim + ci * bre
    quarter_turn = jnp.where(lo, 0.0, 0.5 * math.pi)

    def powers(tau):
        return jnp.exp(tau * lam_re) * jnp.cos(tau * lam_im - quarter_turn)

    def outer(y, x1, x2):
        y_sw = pltpu.roll(y, STATE_DIM, 1)
        prod = y[:, None, :] * x1[None, :, :] + y_sw[:, None, :] * x2[None, :, :]
        return prod.reshape(y.shape[0] * x1.shape[0], 2 * STATE_DIM)

    frames = lax.broadcasted_iota(jnp.int32, (CHUNK, 1), 0).astype(F32)
    wt = outer(powers((CHUNK - 1.0) - frames), bbr, jnp.where(lo, -bbi, bbi))
    w = wt.T
    cre, cim = cre_ref[...], cim_ref[...]
    vm = outer(powers(frames + 1.0), jnp.where(lo, cre, -cre), -cim)
    rrev = jnp.dot(jnp.where(lo, cre, -cim), w, precision=lax.Precision.HIGHEST, preferred_element_type=F32)
    a_row = powers(jnp.full((8, 1), float(CHUNK), F32))[0:1, :]
    a_col = jnp.broadcast_to(a_row, (2 * STATE_DIM, 2 * STATE_DIM)).T
    a_swap = pltpu.roll(a_col, STATE_DIM, 0)
    top = lax.broadcasted_iota(jnp.int32, a_col.shape, 0) < STATE_DIM
    return (rrev, w.astype(BF16), vm.astype(BF16),
            jnp.where(top, a_col, a_swap), jnp.where(top, -a_swap, a_col))


def _chunk_cols_kernel(u_ref, o_ref):
    n_s, n_r, n_j = u_ref.shape[0], u_ref.shape[1], u_ref.shape[2]
    pad = LANES - n_s * n_r
    for jj in range(n_j):
        rows = [u_ref[s, :, jj, :] for s in range(n_s)]
        if pad:
            rows.append(jnp.zeros((pad, SSM_WIDTH), F32))
        cols = jnp.concatenate(rows, axis=0).T
        o_ref[:, jj * GROUP_SIZE:(jj + 1) * GROUP_SIZE, :] = cols.reshape(N_GROUPS, GROUP_SIZE, LANES).astype(BF16)


def _chunk_cols(u4, frames_per_step=16):
    s_total, n_r = u4.shape[0], u4.shape[1]
    s_step = min(s_total, LANES // n_r)
    assert s_step >= 1 and s_total % s_step == 0
    n_lane_blocks = s_total // s_step
    return pl.pallas_call(
        _chunk_cols_kernel,
        grid=(n_lane_blocks, CHUNK // frames_per_step),
        in_specs=[pl.BlockSpec((s_step, n_r, frames_per_step, SSM_WIDTH), lambda a, j: (a, 0, j, 0))],
        out_specs=pl.BlockSpec((N_GROUPS, frames_per_step * GROUP_SIZE, LANES), lambda a, j: (0, j, a)),
        out_shape=jax.ShapeDtypeStruct((N_GROUPS, CHUNK * GROUP_SIZE, n_lane_blocks * LANES), BF16),
        compiler_params=_cparams("parallel", "parallel"),
        name="ssm_in",
    )(u4)


def _unchunk_cols_kernel(y_ref, o_ref):
    n_s, n_r, n_j = o_ref.shape[0], o_ref.shape[1], o_ref.shape[2]
    for jj in range(n_j):
        cols = y_ref[:, jj * GROUP_SIZE:(jj + 1) * GROUP_SIZE, :].reshape(SSM_WIDTH, LANES)
        rows = cols.T
        for s in range(n_s):
            o_ref[s, :, jj, :] = rows[s * n_r:(s + 1) * n_r, :]


def _unchunk_cols(y_cols, s_total, n_r, frames_per_step=16):
    s_step = min(s_total, LANES // n_r)
    n_lane_blocks = s_total // s_step
    return pl.pallas_call(
        _unchunk_cols_kernel,
        grid=(n_lane_blocks, CHUNK // frames_per_step),
        in_specs=[pl.BlockSpec((N_GROUPS, frames_per_step * GROUP_SIZE, LANES), lambda a, j: (0, j, a))],
        out_specs=pl.BlockSpec((s_step, n_r, frames_per_step, SSM_WIDTH), lambda a, j: (a, 0, j, 0)),
        out_shape=jax.ShapeDtypeStruct((s_total, n_r, CHUNK, SSM_WIDTH), F32),
        compiler_params=_cparams("parallel", "parallel"),
        name="ssm_out",
    )(y_cols)


def _ssm_kernel(*refs, n_chunk):
    for gi in range(SSM_GROUPS_PER_STEP):
        _ssm_group(*[r.at[gi] for r in refs], n_chunk=n_chunk)


def _ssm_group(rows_ref, bre_ref, bim_ref, cre_ref, cim_ref, xp_ref, xs_ref, s0_ref, yp_ref, ys_ref, sp_ref, ss_ref,
               mt_scr, *, n_chunk):
    width = GROUP_SIZE * CHUNK
    n_piece = width // LANES
    lane16 = lax.broadcasted_iota(jnp.int32, (GROUP_SIZE, LANES), 1)
    rrev, w, vm_all, a_c, a_s = _group_tables(rows_ref, bre_ref, bim_ref, cre_ref, cim_ref)
    pieces = [rrev[:, k * LANES:(k + 1) * LANES] for k in range(n_piece)] + [jnp.zeros((GROUP_SIZE, LANES), F32)]
    rolled = {0: pieces}
    for b in range(GROUP_SIZE, LANES, GROUP_SIZE):
        rolled[b] = [pltpu.roll(p, LANES - b, 1) for p in pieces[:n_piece]] + [pieces[n_piece]]
    def toeplitz_rows(t0, t1, n_cols):
        for t in range(t0, t1):
            shift = GROUP_SIZE * (CHUNK - 1 - t)
            a, b = shift // LANES, shift % LANES
            for v in range(n_cols // LANES):
                k = v + a
                if k >= n_piece:
                    blk = pieces[n_piece]
                elif b == 0:
                    blk = pieces[k]
                else:
                    blk = jnp.where(lane16 < LANES - b, rolled[b][k], rolled[b][k + 1])
                mt_scr[t * GROUP_SIZE:(t + 1) * GROUP_SIZE, v * LANES:(v + 1) * LANES] = blk.astype(BF16)

    xp, xs = xp_ref[...], xs_ref[...]
    sloc_p, sloc_s = _dot(w, xp), _dot(w, xs)

    def cmul(pc, ps, s):
        return pc * s + ps * pltpu.roll(s, STATE_DIM, 0)

    lane = lax.broadcasted_iota(jnp.int32, (2 * STATE_DIM, LANES), 1) % n_chunk
    prev_cols = []
    lane_id = lax.broadcasted_iota(jnp.int32, (2 * STATE_DIM, LANES), 1)
    final = jnp.zeros((2 * STATE_DIM, LANES), F32)
    seqs_per_tile = LANES // n_chunk
    for tile in range(xp.shape[1] // LANES):
        s_inc = sloc_p[:, tile * LANES:(tile + 1) * LANES]
        pc, ps = a_c, a_s
        dist = 1
        while dist < n_chunk:
            shifted = jnp.where(lane >= dist, pltpu.roll(s_inc, dist, 1), 0.0)
            s_inc = s_inc + cmul(pc, ps, shifted)
            pc, ps = pc * pc - ps * ps, 2.0 * pc * ps
            dist *= 2
        for k in range(seqs_per_tile):
            src, dst = (k + 1) * n_chunk - 1, tile * seqs_per_tile + k
            final = jnp.where(lane_id == dst, pltpu.roll(s_inc, (dst - src) % LANES, 1), final)
        prev_cols.append(jnp.where(lane >= 1, pltpu.roll(s_inc, 1, 1), 0.0))
    sp_ref[...] = final
    s_prev_p = jnp.concatenate(prev_cols, axis=1).astype(BF16)
    s0 = s0_ref[...]
    s0b = s0.astype(BF16)
    frames = TOEPLITZ_ROWS // GROUP_SIZE
    for i in range(width // TOEPLITZ_ROWS):
        rows = slice(i * TOEPLITZ_ROWS, (i + 1) * TOEPLITZ_ROWS)
        n_cols = (i + 1) * TOEPLITZ_ROWS
        toeplitz_rows(i * frames, (i + 1) * frames, n_cols)
        mt = mt_scr[rows, 0:n_cols]
        vm = vm_all[rows, :]
        yp_ref[rows, :] = _dot(mt, xp[0:n_cols, :]) + _dot(vm, s_prev_p)
        ys_ref[rows, :] = _dot(mt, xs[0:n_cols, :]) + _dot(vm, s0b)
    ss_ref[...] = cmul(a_c, a_s, s0) + sloc_s


def _ssm(params, x_p, x_s, s0, n_chunk):
    n_groups, lanes_p = x_p.shape[0], x_p.shape[2]
    width = GROUP_SIZE * CHUNK
    per_g = lambda g: (g, 0, 0)
    state_rows = 2 * STATE_DIM
    gb = SSM_GROUPS_PER_STEP
    assert n_groups % gb == 0
    spec = lambda rows, cols: pl.BlockSpec((gb, rows, cols), per_g)
    return pl.pallas_call(
        functools.partial(_ssm_kernel, n_chunk=n_chunk),
        grid=(n_groups // gb,),
        in_specs=[spec(8, state_rows)] + [spec(GROUP_SIZE, state_rows)] * 4 + [
                  spec(width, lanes_p), spec(width, LANES), spec(state_rows, LANES)],
        out_specs=[spec(width, lanes_p), spec(width, LANES), spec(state_rows, LANES), spec(state_rows, LANES)],
        out_shape=[jax.ShapeDtypeStruct((n_groups, width, lanes_p), F32),
                   jax.ShapeDtypeStruct((n_groups, width, LANES), F32),
                   jax.ShapeDtypeStruct((n_groups, state_rows, LANES), F32),
                   jax.ShapeDtypeStruct((n_groups, state_rows, LANES), F32)],
        scratch_shapes=[pltpu.VMEM((gb, width, width), BF16)],
        compiler_params=_cparams("parallel"),
        name="ssm",
    )(*params, x_p, x_s, s0)


def _layer_norm(x, g, b):
    mu = jnp.mean(x, axis=1, keepdims=True)
    xc = x - mu
    var = jnp.mean(jnp.square(xc), axis=1, keepdims=True)
    return xc * lax.rsqrt(var + LN_EPS) * g + b


def _gelu_tanh(x):
    return 0.5 * x * (1.0 + jnp.tanh(math.sqrt(2.0 / math.pi) * (x + 0.044715 * (x * x * x))))


def _merge_kernel(x_ref, ao_ref, ys_ref, u_ref, d_ref, wg_ref, wap_ref, wglu_ref, wout_ref, g1_ref, b1_ref, h_ref):
    for r in range(x_ref.shape[0] // SUB_ROWS):
        rows = slice(r * SUB_ROWS, (r + 1) * SUB_ROWS)
        x = x_ref[rows, :]
        xb = x.astype(BF16)
        a_branch = _dot(ao_ref[rows, :], wap_ref[...])
        s_act = _gelu_tanh(ys_ref[rows, :] + d_ref[...] * u_ref[rows, :]).astype(BF16)
        s_branch = (_dot(s_act, wglu_ref[:, 0:D_MODEL])
                    * jax.nn.sigmoid(_dot(s_act, wglu_ref[:, D_MODEL:2 * D_MODEL])))
        m = (jax.nn.sigmoid(_dot(xb, wg_ref[:, 0:D_MODEL])) * a_branch
             + jax.nn.sigmoid(_dot(xb, wg_ref[:, D_MODEL:2 * D_MODEL])) * s_branch)
        h_ref[rows, :] = _layer_norm(DEEPNORM_ALPHA * x + _dot(m.astype(BF16), wout_ref[...]),
                                     g1_ref[...], b1_ref[...])


def _merge(x2d, ao, ys, u, d, w_gate, w_ap, w_glu, w_out, ln_g, ln_b, tm):
    t_tokens = x2d.shape[0]
    row = lambda i: (i, 0)
    const = lambda i: (0, 0)
    return pl.pallas_call(
        _merge_kernel,
        grid=(t_tokens // tm,),
        in_specs=[pl.BlockSpec((tm, D_MODEL), row),
                  pl.BlockSpec((tm, ATTN_WIDTH), row),
                  pl.BlockSpec((tm, SSM_WIDTH), row),
                  pl.BlockSpec((tm, SSM_WIDTH), row),
                  pl.BlockSpec((1, SSM_WIDTH), const),
                  pl.BlockSpec((D_MODEL, 2 * D_MODEL), const, pipeline_mode=pl.Buffered(1)),
                  pl.BlockSpec((ATTN_WIDTH, D_MODEL), const, pipeline_mode=pl.Buffered(1)),
                  pl.BlockSpec((SSM_WIDTH, 2 * D_MODEL), const, pipeline_mode=pl.Buffered(1)),
                  pl.BlockSpec((D_MODEL, D_MODEL), const, pipeline_mode=pl.Buffered(1)),
                  pl.BlockSpec((1, D_MODEL), const),
                  pl.BlockSpec((1, D_MODEL), const)],
        out_specs=pl.BlockSpec((tm, D_MODEL), row),
        out_shape=jax.ShapeDtypeStruct((t_tokens, D_MODEL), F32),
        compiler_params=_cparams("parallel"),
        name="merge",
    )(x2d, ao, ys, u, d, w_gate, w_ap, w_glu, w_out, ln_g, ln_b)


def _mlp_kernel(h_ref, w1_ref, w2_ref, g2_ref, b2_ref, o_ref, *, ff_chunk):
    for r in range(h_ref.shape[0] // SUB_ROWS):
        rows = slice(r * SUB_ROWS, (r + 1) * SUB_ROWS)
        h = h_ref[rows, :]
        hb = h.astype(BF16)
        f = jnp.zeros(h.shape, F32)
        for c in range(D_FF // ff_chunk):
            sl = slice(c * ff_chunk, (c + 1) * ff_chunk)
            t = jnp.maximum(_dot(hb, w1_ref[:, sl]), 0.0)
            f = f + _dot((t * t).astype(BF16), w2_ref[sl, :])
        o_ref[rows, :] = _layer_norm(DEEPNORM_ALPHA * h + f, g2_ref[...], b2_ref[...])


def _mlp(h, w1, w2, ln_g, ln_b, tm, ff_chunk=1024):
    t_tokens = h.shape[0]
    row = lambda i: (i, 0)
    const = lambda i: (0, 0)
    return pl.pallas_call(
        functools.partial(_mlp_kernel, ff_chunk=ff_chunk),
        grid=(t_tokens // tm,),
        in_specs=[pl.BlockSpec((tm, D_MODEL), row),
                  pl.BlockSpec((D_MODEL, D_FF), const, pipeline_mode=pl.Buffered(1)),
                  pl.BlockSpec((D_FF, D_MODEL), const, pipeline_mode=pl.Buffered(1)),
                  pl.BlockSpec((1, D_MODEL), const),
                  pl.BlockSpec((1, D_MODEL), const)],
        out_specs=pl.BlockSpec((tm, D_MODEL), row),
        out_shape=jax.ShapeDtypeStruct((t_tokens, D_MODEL), F32),
        compiler_params=_cparams("parallel"),
        name="mlp",
    )(h, w1, w2, ln_g, ln_b)


def _rope_tables(pos):
    inv = 1.0 / (ROPE_THETA ** (jnp.arange(0, HEAD_DIM, 2, dtype=F32) / HEAD_DIM))
    ang = pos.astype(F32)[:, None] * inv[None, :]
    c, s = jnp.cos(ang), jnp.sin(ang)
    reps = LANES // HEAD_DIM
    return jnp.tile(jnp.concatenate([c, c], axis=1), (1, reps)), jnp.tile(jnp.concatenate([-s, s], axis=1), (1, reps))


def kernel(x_prompt, x_sample, cache_k, cache_v, state_ssm_re, state_ssm_im, w_in, lambda_q1, lambda_k1, lambda_q2, lambda_k2, subln_gain, ssm_a_re, ssm_a_im, ssm_b_re, ssm_b_im, ssm_c_re, ssm_c_im, ssm_d, ssm_log_dt, w_attn_proj, w_glu_a, w_glu_b, w_out, ln1_g, ln1_b, w_ff1, w_ff2, ln2_g, ln2_b):
    bp, n_p = x_prompt.shape[0], x_prompt.shape[1]
    bs, n_s = x_sample.shape[0], x_sample.shape[1]
    past = cache_k.shape[2]
    assert w_in.shape[0] == DEPTH and n_s == CHUNK and n_p % CHUNK == 0
    n_chunk = n_p // CHUNK
    tm = min(512, n_p)
    tm_s = min(512, bs * n_s)
    tq = min(512, n_p)
    l = 0
    lam_init = 0.8 - 0.6 * math.exp(-0.3 * l)

    xp = x_prompt.reshape(bp * n_p, D_MODEL)
    xs = x_sample.reshape(bs * n_s, D_MODEL)
    w_qkvu = w_in[l, :, 0:QKVU_COLS].astype(BF16)
    w_gate = w_in[l, :, QKVU_COLS:].astype(BF16)
    w_ap = w_attn_proj[l].astype(BF16)
    w_glu = jnp.concatenate([w_glu_a[l], w_glu_b[l]], axis=1).astype(BF16)
    w_o = w_out[l].astype(BF16)
    w1, w2 = w_ff1[l].astype(BF16), w_ff2[l].astype(BF16)
    lams = [v[l].reshape(1, HEAD_DIM) for v in (lambda_q1, lambda_k1, lambda_q2, lambda_k2)]
    gain = subln_gain[l].reshape(1, V_DIM)
    d_skip = ssm_d[l].reshape(1, SSM_WIDTH)
    lng = [v[l].reshape(1, D_MODEL) for v in (ln1_g, ln1_b, ln2_g, ln2_b)]

    cos_p, sin_p = _rope_tables(jnp.arange(n_p))
    cos_s, sin_s = _rope_tables(jnp.tile(past + jnp.arange(n_s), tm_s // n_s))

    q_p, k_p, v_p, u_p, kb_p, vt_p = _project(xp, w_qkvu, cos_p, sin_p, n_p, tm, True)
    q_s, k_s, v_s, u_s = _project(xs, w_qkvu, cos_s, sin_s, n_s, tm_s, False)

    ao_p = _prompt_attention(lams, gain, q_p, kb_p, vt_p, bp, n_p, tq, lam_init)
    ao_s = _sample_attention(lams, gain, q_s, k_s, v_s,
                             cache_k[l].reshape(bs, past * N_HEADS, V_DIM),
                             cache_v[l].reshape(bs, past * N_HEADS, V_DIM), lam_init)

    tables = _ssm_params(ssm_a_re[l], ssm_a_im[l], ssm_b_re[l], ssm_b_im[l],
                         ssm_c_re[l], ssm_c_im[l], ssm_log_dt[l])
    x_cols_p = _chunk_cols(u_p.reshape(bp, n_chunk, CHUNK, SSM_WIDTH))
    x_cols_s = _chunk_cols(u_s.reshape(1, bs, CHUNK, SSM_WIDTH))
    s0 = jnp.concatenate([state_ssm_re[l], state_ssm_im[l]], axis=-1).transpose(1, 2, 0)
    s0 = jnp.pad(s0, ((0, 0), (0, 0), (0, LANES - bs)))
    y_cols_p, y_cols_s, st_p, st_s = _ssm(tables, x_cols_p, x_cols_s, s0, n_chunk)
    ys_p = _unchunk_cols(y_cols_p, bp, n_chunk).reshape(bp * n_p, SSM_WIDTH)
    ys_s = _unchunk_cols(y_cols_s, 1, bs).reshape(bs * n_s, SSM_WIDTH)
    sf_p = st_p[:, :, 0:bp]
    sf_s = st_s[:, :, 0:bs]

    outs = []
    for x2d, ao, ys, u, tile in ((xp, ao_p, ys_p, u_p, tm), (xs, ao_s, ys_s, u_s, tm_s)):
        big = 2 * tile if x2d.shape[0] >= 8 * tile else tile
        h = _merge(x2d, ao, ys, u, d_skip, w_gate, w_ap, w_glu, w_o, lng[0], lng[1], big)
        outs.append(_mlp(h, w1, w2, lng[2], lng[3], big))

    def states(sf):
        t = sf.transpose(2, 0, 1)
        return t[None, :, :, 0:STATE_DIM], t[None, :, :, STATE_DIM:]

    srp, sip = states(sf_p)
    srs, sis = states(sf_s)
    return (outs[0].reshape(bp, n_p, D_MODEL), outs[1].reshape(bs, n_s, D_MODEL),
            k_p.reshape(1, bp, n_p, N_HEADS, V_DIM), v_p.reshape(1, bp, n_p, N_HEADS, V_DIM), srp, sip,
            k_s.reshape(1, bs, n_s, N_HEADS, V_DIM), v_s.reshape(1, bs, n_s, N_HEADS, V_DIM), srs, sis)
```

```python
import functools
import math

import jax
import jax.numpy as jnp
from jax import lax
from jax.experimental import pallas as pl
from jax.experimental.pallas import tpu as pltpu

D_MODEL = 1024
CHUNK = 64
N_HEADS = 4
HEAD_DIM = 64
V_DIM = 2 * HEAD_DIM
ATTN_WIDTH = N_HEADS * V_DIM
SSM_WIDTH = 512
GROUP_SIZE = 16
N_GROUPS = SSM_WIDTH // GROUP_SIZE
STATE_DIM = 64
D_FF = 4 * D_MODEL
ROPE_THETA = 10000.0
LN_EPS = 1e-5
RMS_EPS = 1e-5
NEG_INF = -1e30
DEPTH = 1
DEEPNORM_ALPHA = (2.0 * DEPTH) ** 0.25
QKVU_COLS = 3 * ATTN_WIDTH + SSM_WIDTH
LOG2E = 1.4426950408889634

LANES = 128
VT_ROWS = V_DIM + 16
QUERY_LANES = 256
TOEPLITZ_ROWS = 256
SUB_ROWS = 256
ATTN_ITEMS_PER_REGION = 12
SSM_GROUPS_PER_STEP = 4
SAMPLE_STREAMS_PER_STEP = 2
VMEM_LIMIT = 56 * 1024 * 1024

F32 = jnp.float32
BF16 = jnp.bfloat16


def _cparams(*sem):
    return pltpu.CompilerParams(dimension_semantics=sem, vmem_limit_bytes=VMEM_LIMIT)


def _nt_dot(a, b):
    return lax.dot_general(a, b, (((1,), (1,)), ((), ())), preferred_element_type=F32)


def _dot(a, b):
    return jnp.dot(a, b, preferred_element_type=F32)


def _in_regions(n_steps, steps_per_region, step):
    once = jnp.minimum(pl.program_id(0) + 1, 1)
    for first in range(0, n_steps, steps_per_region):
        def region(_, carry, first=first):
            for i in range(first, min(first + steps_per_region, n_steps)):
                step(i)
            return carry
        lax.fori_loop(0, once, region, 0)


def _rotary(z, cos, sin_signed, first_half):
    swapped = jnp.where(first_half, pltpu.roll(z, 96, 1), pltpu.roll(z, 32, 1))
    return z * cos + swapped * sin_signed


def _proj_kernel(x_ref, w_ref, cos_ref, sin_ref, q_ref, k_ref, v_ref, u_ref, *rest, emit_t):
    xb = x_ref[...].astype(BF16)
    cos = cos_ref[...]
    sin = sin_ref[...]
    lane = lax.broadcasted_iota(jnp.int32, cos.shape, 1)
    first_half = (lane % HEAD_DIM) < (HEAD_DIM // 2)
    tm = xb.shape[0]
    zq = _dot(xb, w_ref[:, 0:ATTN_WIDTH])
    zk = _dot(xb, w_ref[:, ATTN_WIDTH:2 * ATTN_WIDTH])
    zv = _dot(xb, w_ref[:, 2 * ATTN_WIDTH:3 * ATTN_WIDTH])
    for h in range(N_HEADS):
        sl = slice(h * V_DIM, (h + 1) * V_DIM)
        q_ref[:, sl] = (_rotary(zq[:, sl], cos, sin, first_half) * (LOG2E * HEAD_DIM ** -0.5)).astype(BF16)
        kr = _rotary(zk[:, sl], cos, sin, first_half)
        k_ref[pl.ds(h, tm, stride=N_HEADS), :] = kr
        v_ref[pl.ds(h, tm, stride=N_HEADS), :] = zv[:, sl]
        if emit_t:
            rest[0][:, sl] = kr.astype(BF16)
    if emit_t:
        vt_ref = rest[1]
        zvt = zv.T.astype(BF16)
        ones = jnp.ones((VT_ROWS - V_DIM, zvt.shape[1]), BF16)
        for h in range(N_HEADS):
            vt_ref[h, 0:V_DIM, :] = zvt[h * V_DIM:(h + 1) * V_DIM, :]
            vt_ref[h, V_DIM:VT_ROWS, :] = ones
    u_ref[...] = _dot(xb, w_ref[:, 3 * ATTN_WIDTH:QKVU_COLS])


def _project(x2d, w_qkvu, cos_t, sin_t, seq_len, tm, emit_t):
    t_tokens = x2d.shape[0]
    n_tiles = t_tokens // tm
    n_pos_tiles = cos_t.shape[0] // tm
    tiles_per_seq = max(seq_len // tm, 1)
    row = lambda i: (i, 0)
    pos = lambda i: (i % n_pos_tiles, 0)
    out_shape = [jax.ShapeDtypeStruct((t_tokens, ATTN_WIDTH), BF16),
                 jax.ShapeDtypeStruct((t_tokens * N_HEADS, V_DIM), F32),
                 jax.ShapeDtypeStruct((t_tokens * N_HEADS, V_DIM), F32),
                 jax.ShapeDtypeStruct((t_tokens, SSM_WIDTH), F32)]
    out_specs = ([pl.BlockSpec((tm, ATTN_WIDTH), row)] + [pl.BlockSpec((tm * N_HEADS, V_DIM), row)] * 2
                 + [pl.BlockSpec((tm, SSM_WIDTH), row)])
    if emit_t:
        bsz = t_tokens // seq_len
        out_shape += [jax.ShapeDtypeStruct((t_tokens, ATTN_WIDTH), BF16),
                      jax.ShapeDtypeStruct((bsz, N_HEADS, VT_ROWS, seq_len), BF16)]
        out_specs += [pl.BlockSpec((tm, ATTN_WIDTH), row),
                      pl.BlockSpec((None, N_HEADS, VT_ROWS, tm),
                                   lambda i: (i // tiles_per_seq, 0, 0, i % tiles_per_seq))]
    return pl.pallas_call(
        functools.partial(_proj_kernel, emit_t=emit_t),
        grid=(n_tiles,),
        in_specs=[pl.BlockSpec((tm, D_MODEL), row),
                  pl.BlockSpec((D_MODEL, QKVU_COLS), lambda i: (0, 0)),
                  pl.BlockSpec((tm, LANES), pos),
                  pl.BlockSpec((tm, LANES), pos)],
        out_specs=out_specs,
        out_shape=out_shape,
        compiler_params=_cparams("parallel"),
        name="proj_t" if emit_t else "proj",
    )(x2d, w_qkvu, cos_t, sin_t)


def _diff_lambda(lq1, lk1, lq2, lk2, lam_init):
    return (jnp.exp(jnp.sum(lq1 * lk1, axis=1, keepdims=True))
            - jnp.exp(jnp.sum(lq2 * lk2, axis=1, keepdims=True)) + lam_init)


def _sub_norm(d, gain, lam_init):
    ms = jnp.mean(jnp.square(d), axis=1, keepdims=True)
    return d * lax.rsqrt(ms + RMS_EPS) * gain * (1.0 - lam_init)


def _stack_maps(q):
    lane = lax.broadcasted_iota(jnp.int32, q.shape, 1)
    zero = jnp.zeros_like(q)
    return jnp.concatenate([jnp.where(lane < HEAD_DIM, q, zero), jnp.where(lane >= HEAD_DIM, q, zero)], axis=0)


def _prompt_attn_kernel(lq1_ref, lk1_ref, lq2_ref, lk2_ref, gain_ref, q_ref, k_ref, vt_ref,
                        o_ref, m_scr, acc_scr, qx_scr, s0_scr, s1_scr, mx0_scr, mx1_scr, *, tq, lam_init):
    nq = q_ref.shape[0] // tq
    items = [(qi, j) for qi in range(nq) for j in range(qi + 1)]
    for i in range(nq):
        qx_scr[i * 2 * tq:(i + 1) * 2 * tq, :] = _stack_maps(q_ref[i * tq:(i + 1) * tq, :])
    chains = [slice(c * QUERY_LANES, (c + 1) * QUERY_LANES) for c in range(2 * tq // QUERY_LANES)]

    def visible_keys(cs, diagonal):
        return min(tq, cs.start % tq + QUERY_LANES) if diagonal else tq

    def scores(qi, j, s_scr, mx_scr):
        kt = k_ref[j * tq:(j + 1) * tq, :]
        for cs in chains:
            qx = qx_scr[qi * 2 * tq + cs.start:qi * 2 * tq + cs.stop, :]
            st = _nt_dot(kt, qx)
            s_scr[:, cs] = st
            mx_scr[:, cs] = jnp.max(st, axis=0, keepdims=True)

    def softmax_pv(j, s_scr, mx_scr, diagonal):
        start = j * tq
        for cs in chains:
            n_keys = visible_keys(cs, diagonal)
            if diagonal:
                lane = lax.broadcasted_iota(jnp.int32, (CHUNK, QUERY_LANES), 1)
                blocks = []
                for kc in range(n_keys // CHUNK):
                    blk = s_scr[kc * CHUNK:(kc + 1) * CHUNK, cs]
                    first_visible = kc * CHUNK - cs.start % tq
                    if first_visible > 0:
                        blk = jnp.where(lane >= first_visible, blk, NEG_INF)
                    blocks.append(blk)
                st = jnp.concatenate(blocks, axis=0)
                tile_max = jnp.max(st, axis=0, keepdims=True)
            else:
                st = s_scr[:, cs]
                tile_max = mx_scr[:, cs]
            vt = vt_ref[:, start:start + n_keys]
            p_scale = None
            if j == 0:
                m_new = tile_max
            else:
                m_old = m_scr[:, cs]
                m_new = jnp.maximum(m_old, tile_max)
                p_scale = jnp.exp2(m_old - m_new)
            p = jnp.exp2(st - m_new).astype(BF16)
            pv = _dot(vt, p)
            acc_scr[:, cs] = pv if p_scale is None else acc_scr[:, cs] * p_scale + pv
            m_scr[:, cs] = m_new

    def finish(qi):
        acc = acc_scr[...]
        o = acc[0:V_DIM, :] * (1.0 / acc[V_DIM:V_DIM + 1, :])
        lam = _diff_lambda(lq1_ref[...], lk1_ref[...], lq2_ref[...], lk2_ref[...], lam_init)
        d = (o[:, 0:tq] - lam * o[:, tq:2 * tq]).T
        o_ref[qi * tq:(qi + 1) * tq, :] = _sub_norm(d, gain_ref[...], lam_init).astype(BF16)

    bufs = ((s0_scr, mx0_scr), (s1_scr, mx1_scr))
    scores(*items[0], *bufs[0])

    def run_item(w):
        qi, j = items[w]
        if w + 1 < len(items):
            scores(*items[w + 1], *bufs[(w + 1) % 2])
        softmax_pv(j, *bufs[w % 2], j == qi)
        if j == qi:
            finish(qi)

    _in_regions(len(items), ATTN_ITEMS_PER_REGION, run_item)


def _prompt_attention(lams, gain, q, kb, vt, bsz, seq_len, tq, lam_init):
    nq = seq_len // tq
    small = lambda b, h: (0, 0)
    return pl.pallas_call(
        functools.partial(_prompt_attn_kernel, tq=tq, lam_init=lam_init),
        grid=(bsz, N_HEADS),
        in_specs=[pl.BlockSpec((1, HEAD_DIM), small)] * 4 + [
            pl.BlockSpec((1, V_DIM), small),
            pl.BlockSpec((seq_len, V_DIM), lambda b, h: (b, h)),
            pl.BlockSpec((seq_len, V_DIM), lambda b, h: (b, h)),
            pl.BlockSpec((None, None, VT_ROWS, seq_len), lambda b, h: (b, h, 0, 0))],
        out_specs=pl.BlockSpec((seq_len, V_DIM), lambda b, h: (b, h)),
        out_shape=jax.ShapeDtypeStruct((bsz * seq_len, ATTN_WIDTH), BF16),
        scratch_shapes=[pltpu.VMEM((1, 2 * tq), F32), pltpu.VMEM((VT_ROWS, 2 * tq), F32),
                        pltpu.VMEM((nq * 2 * tq, V_DIM), BF16),
                        pltpu.VMEM((tq, 2 * tq), F32), pltpu.VMEM((tq, 2 * tq), F32),
                        pltpu.VMEM((1, 2 * tq), F32), pltpu.VMEM((1, 2 * tq), F32)],
        compiler_params=_cparams("parallel", "parallel"),
        name="prompt_attn",
    )(*lams, gain, q, kb, vt)


def _sample_attn_kernel(lq1_ref, lk1_ref, lq2_ref, lk2_ref, gain_ref, q_ref, k_ref, v_ref, ck_ref, cv_ref,
                        o_ref, *, lam_init):
    n_streams = ck_ref.shape[0]
    n_new = q_ref.shape[0] // n_streams
    past = ck_ref.shape[1] // N_HEADS
    lam = _diff_lambda(lq1_ref[...], lk1_ref[...], lq2_ref[...], lk2_ref[...], lam_init)
    gain = gain_ref[...]
    for s in range(n_streams):
        rows = slice(s * n_new, (s + 1) * n_new)
        for h in range(N_HEADS):
            sl = slice(h * V_DIM, (h + 1) * V_DIM)
            old = pl.ds(h, past, stride=N_HEADS)
            new = pl.ds(s * n_new * N_HEADS + h, n_new, stride=N_HEADS)
            qx = _stack_maps(q_ref[rows, sl])
            s_c = _nt_dot(qx, ck_ref[s, old, :].astype(BF16))
            s_n = _nt_dot(qx, k_ref[new, :].astype(BF16))
            m = jnp.maximum(jnp.max(s_c, axis=1, keepdims=True), jnp.max(s_n, axis=1, keepdims=True))
            p_c = jnp.exp2(s_c - m).astype(BF16)
            p_n = jnp.exp2(s_n - m).astype(BF16)
            vc = jnp.concatenate([cv_ref[s, old, :].astype(BF16), jnp.ones((past, V_DIM), BF16)], axis=1)
            vn = jnp.concatenate([v_ref[new, :].astype(BF16), jnp.ones((n_new, V_DIM), BF16)], axis=1)
            o_ext = _dot(p_c, vc) + _dot(p_n, vn)
            o = o_ext[:, 0:V_DIM] * (1.0 / o_ext[:, V_DIM:2 * V_DIM])
            d = o[0:n_new, :] - lam * o[n_new:2 * n_new, :]
            o_ref[rows, sl] = _sub_norm(d, gain, lam_init).astype(BF16)


def _sample_attention(lams, gain, q, k, v, cache_k, cache_v, lam_init):
    bsz, past_rows = cache_k.shape[0], cache_k.shape[1]
    n_new = q.shape[0] // bsz
    ns = SAMPLE_STREAMS_PER_STEP if bsz % SAMPLE_STREAMS_PER_STEP == 0 else 1
    small = lambda b: (0, 0)
    row = lambda b: (b, 0)
    return pl.pallas_call(
        functools.partial(_sample_attn_kernel, lam_init=lam_init),
        grid=(bsz // ns,),
        in_specs=[pl.BlockSpec((1, HEAD_DIM), small)] * 4 + [
            pl.BlockSpec((1, V_DIM), small),
            pl.BlockSpec((ns * n_new, ATTN_WIDTH), row),
            pl.BlockSpec((ns * n_new * N_HEADS, V_DIM), row),
            pl.BlockSpec((ns * n_new * N_HEADS, V_DIM), row),
            pl.BlockSpec((ns, past_rows, V_DIM), lambda b: (b, 0, 0)),
            pl.BlockSpec((ns, past_rows, V_DIM), lambda b: (b, 0, 0))],
        out_specs=pl.BlockSpec((ns * n_new, ATTN_WIDTH), row),
        out_shape=jax.ShapeDtypeStruct(q.shape, BF16),
        compiler_params=_cparams("parallel"),
        name="sample_attn",
    )(*lams, gain, q, k, v, cache_k, cache_v)


def _ssm_params(a_re, a_im, b_re, b_im, c_re, c_im, log_dt):
    g = a_re.shape[0]
    twice = lambda v: jnp.concatenate([v, v], axis=-1)
    rows = jnp.stack([twice(a_re), twice(a_im), jnp.broadcast_to(log_dt[:, None], (g, 2 * STATE_DIM))], axis=1)
    rows = jnp.pad(rows, ((0, 0), (0, 8 - rows.shape[1]), (0, 0)))
    return rows, twice(b_re.transpose(0, 2, 1)), twice(b_im.transpose(0, 2, 1)), twice(c_re), twice(c_im)


def _group_tables(rows_ref, bre_ref, bim_ref, cre_ref, cim_ref):
    lane = lax.broadcasted_iota(jnp.int32, (1, 2 * STATE_DIM), 1)
    lo = lane < STATE_DIM
    a_re, a_im, log_dt = rows_ref[0:1, :], rows_ref[1:2, :], rows_ref[2:3, :]
    dt = jnp.exp(log_dt)
    lam_re, lam_im = a_re * dt, a_im * dt
    mag = jnp.exp(lam_re)
    ar, ai = mag * jnp.cos(lam_im), mag * jnp.sin(lam_im)
    den = jnp.square(a_re) + jnp.square(a_im)
    cr = ((ar - 1.0) * a_re + ai * a_im) / den
    ci = (ai * a_re - (ar - 1.0) * a_im) / den
    bre, bim = bre_ref[...], bim_ref[...]
    bbr = cr * bre - ci * bim
    bbi = cr * bim + ci * bre
    quarter_turn = jnp.where(lo, 0.0, 0.5 * math.pi)

    def powers(tau):
        return jnp.exp(tau * lam_re) * jnp.cos(tau * lam_im - quarter_turn)

    def outer(y, x1, x2):
        y_sw = pltpu.roll(y, STATE_DIM, 1)
        prod = y[:, None, :] * x1[None, :, :] + y_sw[:, None, :] * x2[None, :, :]
        return prod.reshape(y.shape[0] * x1.shape[0], 2 * STATE_DIM)

    frames = lax.broadcasted_iota(jnp.int32, (CHUNK, 1), 0).astype(F32)
    wt = outer(powers((CHUNK - 1.0) - frames), bbr, jnp.where(lo, -bbi, bbi))
    w = wt.T
    cre, cim = cre_ref[...], cim_ref[...]
    vm = outer(powers(frames + 1.0), jnp.where(lo, cre, -cre), -cim)
    rrev = jnp.dot(jnp.where(lo, cre, -cim), w, precision=lax.Precision.HIGHEST, preferred_element_type=F32)
    a_row = powers(jnp.full((8, 1), float(CHUNK), F32))[0:1, :]
    a_col = jnp.broadcast_to(a_row, (2 * STATE_DIM, 2 * STATE_DIM)).T
    a_swap = pltpu.roll(a_col, STATE_DIM, 0)
    top = lax.broadcasted_iota(jnp.int32, a_col.shape, 0) < STATE_DIM
    return (rrev, w.astype(BF16), vm.astype(BF16),
            jnp.where(top, a_col, a_swap), jnp.where(top, -a_swap, a_col))


def _chunk_cols_kernel(u_ref, o_ref):
    n_s, n_r, n_j = u_ref.shape[0], u_ref.shape[1], u_ref.shape[2]
    pad = LANES - n_s * n_r
    for jj in range(n_j):
        rows = [u_ref[s, :, jj, :] for s in range(n_s)]
        if pad:
            rows.append(jnp.zeros((pad, SSM_WIDTH), F32))
        cols = jnp.concatenate(rows, axis=0).T
        o_ref[:, jj * GROUP_SIZE:(jj + 1) * GROUP_SIZE, :] = cols.reshape(N_GROUPS, GROUP_SIZE, LANES).astype(BF16)


def _chunk_cols(u4, frames_per_step=16):
    s_total, n_r = u4.shape[0], u4.shape[1]
    s_step = min(s_total, LANES // n_r)
    assert s_step >= 1 and s_total % s_step == 0
    n_lane_blocks = s_total // s_step
    return pl.pallas_call(
        _chunk_cols_kernel,
        grid=(n_lane_blocks, CHUNK // frames_per_step),
        in_specs=[pl.BlockSpec((s_step, n_r, frames_per_step, SSM_WIDTH), lambda a, j: (a, 0, j, 0))],
        out_specs=pl.BlockSpec((N_GROUPS, frames_per_step * GROUP_SIZE, LANES), lambda a, j: (0, j, a)),
        out_shape=jax.ShapeDtypeStruct((N_GROUPS, CHUNK * GROUP_SIZE, n_lane_blocks * LANES), BF16),
        compiler_params=_cparams("parallel", "parallel"),
        name="ssm_in",
    )(u4)


def _unchunk_cols_kernel(y_ref, o_ref):
    n_s, n_r, n_j = o_ref.shape[0], o_ref.shape[1], o_ref.shape[2]
    for jj in range(n_j):
        cols = y_ref[:, jj * GROUP_SIZE:(jj + 1) * GROUP_SIZE, :].reshape(SSM_WIDTH, LANES)
        rows = cols.T
        for s in range(n_s):
            o_ref[s, :, jj, :] = rows[s * n_r:(s + 1) * n_r, :]


def _unchunk_cols(y_cols, s_total, n_r, frames_per_step=16):
    s_step = min(s_total, LANES // n_r)
    n_lane_blocks = s_total // s_step
    return pl.pallas_call(
        _unchunk_cols_kernel,
        grid=(n_lane_blocks, CHUNK // frames_per_step),
        in_specs=[pl.BlockSpec((N_GROUPS, frames_per_step * GROUP_SIZE, LANES), lambda a, j: (0, j, a))],
        out_specs=pl.BlockSpec((s_step, n_r, frames_per_step, SSM_WIDTH), lambda a, j: (a, 0, j, 0)),
        out_shape=jax.ShapeDtypeStruct((s_total, n_r, CHUNK, SSM_WIDTH), F32),
        compiler_params=_cparams("parallel", "parallel"),
        name="ssm_out",
    )(y_cols)


def _ssm_kernel(*refs, n_chunk):
    for gi in range(SSM_GROUPS_PER_STEP):
        _ssm_group(*[r.at[gi] for r in refs], n_chunk=n_chunk)


def _ssm_group(rows_ref, bre_ref, bim_ref, cre_ref, cim_ref, xp_ref, xs_ref, s0_ref, yp_ref, ys_ref, sp_ref, ss_ref,
               mt_scr, *, n_chunk):
    width = GROUP_SIZE * CHUNK
    n_piece = width // LANES
    lane16 = lax.broadcasted_iota(jnp.int32, (GROUP_SIZE, LANES), 1)
    rrev, w, vm_all, a_c, a_s = _group_tables(rows_ref, bre_ref, bim_ref, cre_ref, cim_ref)
    pieces = [rrev[:, k * LANES:(k + 1) * LANES] for k in range(n_piece)] + [jnp.zeros((GROUP_SIZE, LANES), F32)]
    rolled = {0: pieces}
    for b in range(GROUP_SIZE, LANES, GROUP_SIZE):
        rolled[b] = [pltpu.roll(p, LANES - b, 1) for p in pieces[:n_piece]] + [pieces[n_piece]]
    def toeplitz_rows(t0, t1, n_cols):
        for t in range(t0, t1):
            shift = GROUP_SIZE * (CHUNK - 1 - t)
            a, b = shift // LANES, shift % LANES
            for v in range(n_cols // LANES):
                k = v + a
                if k >= n_piece:
                    blk = pieces[n_piece]
                elif b == 0:
                    blk = pieces[k]
                else:
                    blk = jnp.where(lane16 < LANES - b, rolled[b][k], rolled[b][k + 1])
                mt_scr[t * GROUP_SIZE:(t + 1) * GROUP_SIZE, v * LANES:(v + 1) * LANES] = blk.astype(BF16)

    xp, xs = xp_ref[...], xs_ref[...]
    sloc_p, sloc_s = _dot(w, xp), _dot(w, xs)

    def cmul(pc, ps, s):
        return pc * s + ps * pltpu.roll(s, STATE_DIM, 0)

    lane = lax.broadcasted_iota(jnp.int32, (2 * STATE_DIM, LANES), 1) % n_chunk
    prev_cols = []
    lane_id = lax.broadcasted_iota(jnp.int32, (2 * STATE_DIM, LANES), 1)
    final = jnp.zeros((2 * STATE_DIM, LANES), F32)
    seqs_per_tile = LANES // n_chunk
    for tile in range(xp.shape[1] // LANES):
        s_inc = sloc_p[:, tile * LANES:(tile + 1) * LANES]
        pc, ps = a_c, a_s
        dist = 1
        while dist < n_chunk:
            shifted = jnp.where(lane >= dist, pltpu.roll(s_inc, dist, 1), 0.0)
            s_inc = s_inc + cmul(pc, ps, shifted)
            pc, ps = pc * pc - ps * ps, 2.0 * pc * ps
            dist *= 2
        for k in range(seqs_per_tile):
            src, dst = (k + 1) * n_chunk - 1, tile * seqs_per_tile + k
            final = jnp.where(lane_id == dst, pltpu.roll(s_inc, (dst - src) % LANES, 1), final)
        prev_cols.append(jnp.where(lane >= 1, pltpu.roll(s_inc, 1, 1), 0.0))
    sp_ref[...] = final
    s_prev_p = jnp.concatenate(prev_cols, axis=1).astype(BF16)
    s0 = s0_ref[...]
    s0b = s0.astype(BF16)
    frames = TOEPLITZ_ROWS // GROUP_SIZE
    for i in range(width // TOEPLITZ_ROWS):
        rows = slice(i * TOEPLITZ_ROWS, (i + 1) * TOEPLITZ_ROWS)
        n_cols = (i + 1) * TOEPLITZ_ROWS
        toeplitz_rows(i * frames, (i + 1) * frames, n_cols)
        mt = mt_scr[rows, 0:n_cols]
        vm = vm_all[rows, :]
        yp_ref[rows, :] = _dot(mt, xp[0:n_cols, :]) + _dot(vm, s_prev_p)
        ys_ref[rows, :] = _dot(mt, xs[0:n_cols, :]) + _dot(vm, s0b)
    ss_ref[...] = cmul(a_c, a_s, s0) + sloc_s


def _ssm(params, x_p, x_s, s0, n_chunk):
    n_groups, lanes_p = x_p.shape[0], x_p.shape[2]
    width = GROUP_SIZE * CHUNK
    per_g = lambda g: (g, 0, 0)
    state_rows = 2 * STATE_DIM
    gb = SSM_GROUPS_PER_STEP
    assert n_groups % gb == 0
    spec = lambda rows, cols: pl.BlockSpec((gb, rows, cols), per_g)
    return pl.pallas_call(
        functools.partial(_ssm_kernel, n_chunk=n_chunk),
        grid=(n_groups // gb,),
        in_specs=[spec(8, state_rows)] + [spec(GROUP_SIZE, state_rows)] * 4 + [
                  spec(width, lanes_p), spec(width, LANES), spec(state_rows, LANES)],
        out_specs=[spec(width, lanes_p), spec(width, LANES), spec(state_rows, LANES), spec(state_rows, LANES)],
        out_shape=[jax.ShapeDtypeStruct((n_groups, width, lanes_p), F32),
                   jax.ShapeDtypeStruct((n_groups, width, LANES), F32),
                   jax.ShapeDtypeStruct((n_groups, state_rows, LANES), F32),
                   jax.ShapeDtypeStruct((n_groups, state_rows, LANES), F32)],
        scratch_shapes=[pltpu.VMEM((gb, width, width), BF16)],
        compiler_params=_cparams("parallel"),
        name="ssm",
    )(*params, x_p, x_s, s0)


def _layer_norm(x, g, b):
    mu = jnp.mean(x, axis=1, keepdims=True)
    xc = x - mu
    var = jnp.mean(jnp.square(xc), axis=1, keepdims=True)
    return xc * lax.rsqrt(var + LN_EPS) * g + b


def _gelu_tanh(x):
    return 0.5 * x * (1.0 + jnp.tanh(math.sqrt(2.0 / math.pi) * (x + 0.044715 * (x * x * x))))


def _merge_kernel(x_ref, ao_ref, ys_ref, u_ref, d_ref, wg_ref, wap_ref, wglu_ref, wout_ref, g1_ref, b1_ref, h_ref):
    def sub_tile(r):
        rows = slice(r * SUB_ROWS, (r + 1) * SUB_ROWS)
        x = x_ref[rows, :]
        xb = x.astype(BF16)
        a_branch = _dot(ao_ref[rows, :], wap_ref[...])
        s_act = _gelu_tanh(ys_ref[rows, :] + d_ref[...] * u_ref[rows, :]).astype(BF16)
        s_branch = (_dot(s_act, wglu_ref[:, 0:D_MODEL])
                    * jax.nn.sigmoid(_dot(s_act, wglu_ref[:, D_MODEL:2 * D_MODEL])))
        m = (jax.nn.sigmoid(_dot(xb, wg_ref[:, 0:D_MODEL])) * a_branch
             + jax.nn.sigmoid(_dot(xb, wg_ref[:, D_MODEL:2 * D_MODEL])) * s_branch)
        h_ref[rows, :] = _layer_norm(DEEPNORM_ALPHA * x + _dot(m.astype(BF16), wout_ref[...]),
                                     g1_ref[...], b1_ref[...])

    for r in range(x_ref.shape[0] // SUB_ROWS):
        sub_tile(r)


def _merge(x2d, ao, ys, u, d, w_gate, w_ap, w_glu, w_out, ln_g, ln_b, tm):
    t_tokens = x2d.shape[0]
    row = lambda i: (i, 0)
    const = lambda i: (0, 0)
    return pl.pallas_call(
        _merge_kernel,
        grid=(t_tokens // tm,),
        in_specs=[pl.BlockSpec((tm, D_MODEL), row),
                  pl.BlockSpec((tm, ATTN_WIDTH), row),
                  pl.BlockSpec((tm, SSM_WIDTH), row),
                  pl.BlockSpec((tm, SSM_WIDTH), row),
                  pl.BlockSpec((1, SSM_WIDTH), const),
                  pl.BlockSpec((D_MODEL, 2 * D_MODEL), const, pipeline_mode=pl.Buffered(1)),
                  pl.BlockSpec((ATTN_WIDTH, D_MODEL), const, pipeline_mode=pl.Buffered(1)),
                  pl.BlockSpec((SSM_WIDTH, 2 * D_MODEL), const, pipeline_mode=pl.Buffered(1)),
                  pl.BlockSpec((D_MODEL, D_MODEL), const, pipeline_mode=pl.Buffered(1)),
                  pl.BlockSpec((1, D_MODEL), const),
                  pl.BlockSpec((1, D_MODEL), const)],
        out_specs=pl.BlockSpec((tm, D_MODEL), row),
        out_shape=jax.ShapeDtypeStruct((t_tokens, D_MODEL), F32),
        compiler_params=_cparams("parallel"),
        name="merge",
    )(x2d, ao, ys, u, d, w_gate, w_ap, w_glu, w_out, ln_g, ln_b)


def _mlp_kernel(h_ref, w1_ref, w2_ref, g2_ref, b2_ref, o_ref, *, ff_chunk):
    for r in range(h_ref.shape[0] // SUB_ROWS):
        rows = slice(r * SUB_ROWS, (r + 1) * SUB_ROWS)
        h = h_ref[rows, :]
        hb = h.astype(BF16)
        f = jnp.zeros(h.shape, F32)
        for c in range(D_FF // ff_chunk):
            sl = slice(c * ff_chunk, (c + 1) * ff_chunk)
            t = jnp.maximum(_dot(hb, w1_ref[:, sl]), 0.0)
            f = f + _dot((t * t).astype(BF16), w2_ref[sl, :])
        o_ref[rows, :] = _layer_norm(DEEPNORM_ALPHA * h + f, g2_ref[...], b2_ref[...])


def _mlp(h, w1, w2, ln_g, ln_b, tm, ff_chunk=1024):
    t_tokens = h.shape[0]
    row = lambda i: (i, 0)
    const = lambda i: (0, 0)
    return pl.pallas_call(
        functools.partial(_mlp_kernel, ff_chunk=ff_chunk),
        grid=(t_tokens // tm,),
        in_specs=[pl.BlockSpec((tm, D_MODEL), row),
                  pl.BlockSpec((D_MODEL, D_FF), const, pipeline_mode=pl.Buffered(1)),
                  pl.BlockSpec((D_FF, D_MODEL), const, pipeline_mode=pl.Buffered(1)),
                  pl.BlockSpec((1, D_MODEL), const),
                  pl.BlockSpec((1, D_MODEL), const)],
        out_specs=pl.BlockSpec((tm, D_MODEL), row),
        out_shape=jax.ShapeDtypeStruct((t_tokens, D_MODEL), F32),
        compiler_params=_cparams("parallel"),
        name="mlp",
    )(h, w1, w2, ln_g, ln_b)


def _rope_tables(pos):
    inv = 1.0 / (ROPE_THETA ** (jnp.arange(0, HEAD_DIM, 2, dtype=F32) / HEAD_DIM))
    ang = pos.astype(F32)[:, None] * inv[None, :]
    c, s = jnp.cos(ang), jnp.sin(ang)
    reps = LANES // HEAD_DIM
    return jnp.tile(jnp.concatenate([c, c], axis=1), (1, reps)), jnp.tile(jnp.concatenate([-s, s], axis=1), (1, reps))


def kernel(x_prompt, x_sample, cache_k, cache_v, state_ssm_re, state_ssm_im, w_in, lambda_q1, lambda_k1, lambda_q2, lambda_k2, subln_gain, ssm_a_re, ssm_a_im, ssm_b_re, ssm_b_im, ssm_c_re, ssm_c_im, ssm_d, ssm_log_dt, w_attn_proj, w_glu_a, w_glu_b, w_out, ln1_g, ln1_b, w_ff1, w_ff2, ln2_g, ln2_b):
    bp, n_p = x_prompt.shape[0], x_prompt.shape[1]
    bs, n_s = x_sample.shape[0], x_sample.shape[1]
    past = cache_k.shape[2]
    assert w_in.shape[0] == DEPTH and n_s == CHUNK and n_p % CHUNK == 0
    n_chunk = n_p // CHUNK
    tm = min(512, n_p)
    tm_s = min(512, bs * n_s)
    tq = min(512, n_p)
    l = 0
    lam_init = 0.8 - 0.6 * math.exp(-0.3 * l)

    xp = x_prompt.reshape(bp * n_p, D_MODEL)
    xs = x_sample.reshape(bs * n_s, D_MODEL)
    w_qkvu = w_in[l, :, 0:QKVU_COLS].astype(BF16)
    w_gate = w_in[l, :, QKVU_COLS:].astype(BF16)
    w_ap = w_attn_proj[l].astype(BF16)
    w_glu = jnp.concatenate([w_glu_a[l], w_glu_b[l]], axis=1).astype(BF16)
    w_o = w_out[l].astype(BF16)
    w1, w2 = w_ff1[l].astype(BF16), w_ff2[l].astype(BF16)
    lams = [v[l].reshape(1, HEAD_DIM) for v in (lambda_q1, lambda_k1, lambda_q2, lambda_k2)]
    gain = subln_gain[l].reshape(1, V_DIM)
    d_skip = ssm_d[l].reshape(1, SSM_WIDTH)
    lng = [v[l].reshape(1, D_MODEL) for v in (ln1_g, ln1_b, ln2_g, ln2_b)]

    cos_p, sin_p = _rope_tables(jnp.arange(n_p))
    cos_s, sin_s = _rope_tables(jnp.tile(past + jnp.arange(n_s), tm_s // n_s))

    q_p, k_p, v_p, u_p, kb_p, vt_p = _project(xp, w_qkvu, cos_p, sin_p, n_p, tm, True)
    q_s, k_s, v_s, u_s = _project(xs, w_qkvu, cos_s, sin_s, n_s, tm_s, False)

    ao_p = _prompt_attention(lams, gain, q_p, kb_p, vt_p, bp, n_p, tq, lam_init)
    ao_s = _sample_attention(lams, gain, q_s, k_s, v_s,
                             cache_k[l].reshape(bs, past * N_HEADS, V_DIM),
                             cache_v[l].reshape(bs, past * N_HEADS, V_DIM), lam_init)

    tables = _ssm_params(ssm_a_re[l], ssm_a_im[l], ssm_b_re[l], ssm_b_im[l],
                         ssm_c_re[l], ssm_c_im[l], ssm_log_dt[l])
    x_cols_p = _chunk_cols(u_p.reshape(bp, n_chunk, CHUNK, SSM_WIDTH))
    x_cols_s = _chunk_cols(u_s.reshape(1, bs, CHUNK, SSM_WIDTH))
    s0 = jnp.concatenate([state_ssm_re[l], state_ssm_im[l]], axis=-1).transpose(1, 2, 0)
    s0 = jnp.pad(s0, ((0, 0), (0, 0), (0, LANES - bs)))
    y_cols_p, y_cols_s, st_p, st_s = _ssm(tables, x_cols_p, x_cols_s, s0, n_chunk)
    ys_p = _unchunk_cols(y_cols_p, bp, n_chunk).reshape(bp * n_p, SSM_WIDTH)
    ys_s = _unchunk_cols(y_cols_s, 1, bs).reshape(bs * n_s, SSM_WIDTH)
    sf_p = st_p[:, :, 0:bp]
    sf_s = st_s[:, :, 0:bs]

    outs = []
    for x2d, ao, ys, u, tile in ((xp, ao_p, ys_p, u_p, tm), (xs, ao_s, ys_s, u_s, tm_s)):
        big = 2 * tile if x2d.shape[0] >= 8 * tile else tile
        h = _merge(x2d, ao, ys, u, d_skip, w_gate, w_ap, w_glu, w_o, lng[0], lng[1], big)
        outs.append(_mlp(h, w1, w2, lng[2], lng[3], big))

    def states(sf):
        t = sf.transpose(2, 0, 1)
        return t[None, :, :, 0:STATE_DIM], t[None, :, :, STATE_DIM:]

    srp, sip = states(sf_p)
    srs, sis = states(sf_s)
    return (outs[0].reshape(bp, n_p, D_MODEL), outs[1].reshape(bs, n_s, D_MODEL),
            k_p.reshape(1, bp, n_p, N_HEADS, V_DIM), v_p.reshape(1, bp, n_p, N_HEADS, V_DIM), srp, sip,
            k_s.reshape(1, bs, n_s, N_HEADS, V_DIM), v_s.reshape(1, bs, n_s, N_HEADS, V_DIM), srs, sis)
```

```python
import functools
import math

import jax
import jax.numpy as jnp
from jax import lax
from jax.experimental import pallas as pl
from jax.experimental.pallas import tpu as pltpu

D_MODEL = 1024
CHUNK = 64
N_HEADS = 4
HEAD_DIM = 64
V_DIM = 2 * HEAD_DIM
ATTN_WIDTH = N_HEADS * V_DIM
SSM_WIDTH = 512
GROUP_SIZE = 16
N_GROUPS = SSM_WIDTH // GROUP_SIZE
STATE_DIM = 64
D_FF = 4 * D_MODEL
ROPE_THETA = 10000.0
LN_EPS = 1e-5
RMS_EPS = 1e-5
NEG_INF = -1e30
DEPTH = 1
DEEPNORM_ALPHA = (2.0 * DEPTH) ** 0.25
QKVU_COLS = 3 * ATTN_WIDTH + SSM_WIDTH
LOG2E = 1.4426950408889634

LANES = 128
VT_ROWS = V_DIM + 16
QUERY_LANES = 256
TOEPLITZ_ROWS = 256
SUB_ROWS = 256
ATTN_ITEMS_PER_REGION = 12
SSM_GROUPS_PER_STEP = 4
SAMPLE_STREAMS_PER_STEP = 2
VMEM_LIMIT = 56 * 1024 * 1024

F32 = jnp.float32
BF16 = jnp.bfloat16


def _cparams(*sem):
    return pltpu.CompilerParams(dimension_semantics=sem, vmem_limit_bytes=VMEM_LIMIT)


def _nt_dot(a, b):
    return lax.dot_general(a, b, (((1,), (1,)), ((), ())), preferred_element_type=F32)


def _dot(a, b):
    return jnp.dot(a, b, preferred_element_type=F32)


def _in_regions(n_steps, steps_per_region, step):
    once = jnp.minimum(pl.program_id(0) + 1, 1)
    for first in range(0, n_steps, steps_per_region):
        def region(_, carry, first=first):
            for i in range(first, min(first + steps_per_region, n_steps)):
                step(i)
            return carry
        lax.fori_loop(0, once, region, 0)


def _rotary(z, cos, sin_signed, first_half):
    swapped = jnp.where(first_half, pltpu.roll(z, 96, 1), pltpu.roll(z, 32, 1))
    return z * cos + swapped * sin_signed


def _proj_kernel(x_ref, w_ref, cos_ref, sin_ref, q_ref, k_ref, v_ref, u_ref, *rest, emit_t):
    xb = x_ref[...].astype(BF16)
    cos = cos_ref[...]
    sin = sin_ref[...]
    lane = lax.broadcasted_iota(jnp.int32, cos.shape, 1)
    first_half = (lane % HEAD_DIM) < (HEAD_DIM // 2)
    tm = xb.shape[0]
    zq = _dot(xb, w_ref[:, 0:ATTN_WIDTH])
    zk = _dot(xb, w_ref[:, ATTN_WIDTH:2 * ATTN_WIDTH])
    zv = _dot(xb, w_ref[:, 2 * ATTN_WIDTH:3 * ATTN_WIDTH])
    for h in range(N_HEADS):
        sl = slice(h * V_DIM, (h + 1) * V_DIM)
        q_ref[:, sl] = (_rotary(zq[:, sl], cos, sin, first_half) * (LOG2E * HEAD_DIM ** -0.5)).astype(BF16)
        kr = _rotary(zk[:, sl], cos, sin, first_half)
        k_ref[pl.ds(h, tm, stride=N_HEADS), :] = kr
        v_ref[pl.ds(h, tm, stride=N_HEADS), :] = zv[:, sl]
        if emit_t:
            rest[0][:, sl] = kr.astype(BF16)
    if emit_t:
        vt_ref = rest[1]
        zvt = zv.T.astype(BF16)
        ones = jnp.ones((VT_ROWS - V_DIM, zvt.shape[1]), BF16)
        for h in range(N_HEADS):
            vt_ref[h, 0:V_DIM, :] = zvt[h * V_DIM:(h + 1) * V_DIM, :]
            vt_ref[h, V_DIM:VT_ROWS, :] = ones
    u_ref[...] = _dot(xb, w_ref[:, 3 * ATTN_WIDTH:QKVU_COLS])


def _project(x2d, w_qkvu, cos_t, sin_t, seq_len, tm, emit_t):
    t_tokens = x2d.shape[0]
    n_tiles = t_tokens // tm
    n_pos_tiles = cos_t.shape[0] // tm
    tiles_per_seq = max(seq_len // tm, 1)
    row = lambda i: (i, 0)
    pos = lambda i: (i % n_pos_tiles, 0)
    out_shape = [jax.ShapeDtypeStruct((t_tokens, ATTN_WIDTH), BF16),
                 jax.ShapeDtypeStruct((t_tokens * N_HEADS, V_DIM), F32),
                 jax.ShapeDtypeStruct((t_tokens * N_HEADS, V_DIM), F32),
                 jax.ShapeDtypeStruct((t_tokens, SSM_WIDTH), F32)]
    out_specs = ([pl.BlockSpec((tm, ATTN_WIDTH), row)] + [pl.BlockSpec((tm * N_HEADS, V_DIM), row)] * 2
                 + [pl.BlockSpec((tm, SSM_WIDTH), row)])
    if emit_t:
        bsz = t_tokens // seq_len
        out_shape += [jax.ShapeDtypeStruct((t_tokens, ATTN_WIDTH), BF16),
                      jax.ShapeDtypeStruct((bsz, N_HEADS, VT_ROWS, seq_len), BF16)]
        out_specs += [pl.BlockSpec((tm, ATTN_WIDTH), row),
                      pl.BlockSpec((None, N_HEADS, VT_ROWS, tm),
                                   lambda i: (i // tiles_per_seq, 0, 0, i % tiles_per_seq))]
    return pl.pallas_call(
        functools.partial(_proj_kernel, emit_t=emit_t),
        grid=(n_tiles,),
        in_specs=[pl.BlockSpec((tm, D_MODEL), row),
                  pl.BlockSpec((D_MODEL, QKVU_COLS), lambda i: (0, 0)),
                  pl.BlockSpec((tm, LANES), pos),
                  pl.BlockSpec((tm, LANES), pos)],
        out_specs=out_specs,
        out_shape=out_shape,
        compiler_params=_cparams("parallel"),
        name="proj_t" if emit_t else "proj",
    )(x2d, w_qkvu, cos_t, sin_t)


def _diff_lambda(lq1, lk1, lq2, lk2, lam_init):
    return (jnp.exp(jnp.sum(lq1 * lk1, axis=1, keepdims=True))
            - jnp.exp(jnp.sum(lq2 * lk2, axis=1, keepdims=True)) + lam_init)


def _sub_norm(d, gain, lam_init):
    ms = jnp.mean(jnp.square(d), axis=1, keepdims=True)
    return d * lax.rsqrt(ms + RMS_EPS) * gain * (1.0 - lam_init)


def _stack_maps(q):
    lane = lax.broadcasted_iota(jnp.int32, q.shape, 1)
    zero = jnp.zeros_like(q)
    return jnp.concatenate([jnp.where(lane < HEAD_DIM, q, zero), jnp.where(lane >= HEAD_DIM, q, zero)], axis=0)


def _prompt_attn_kernel(lq1_ref, lk1_ref, lq2_ref, lk2_ref, gain_ref, q_ref, k_ref, vt_ref,
                        o_ref, m_scr, acc_scr, qx_scr, s0_scr, s1_scr, mx0_scr, mx1_scr, *, tq, lam_init):
    nq = q_ref.shape[0] // tq
    items = [(qi, j) for qi in range(nq) for j in range(qi + 1)]
    for i in range(nq):
        qx_scr[i * 2 * tq:(i + 1) * 2 * tq, :] = _stack_maps(q_ref[i * tq:(i + 1) * tq, :])
    chains = [slice(c * QUERY_LANES, (c + 1) * QUERY_LANES) for c in range(2 * tq // QUERY_LANES)]

    def visible_keys(cs, diagonal):
        return min(tq, cs.start % tq + QUERY_LANES) if diagonal else tq

    def scores(qi, j, s_scr, mx_scr):
        kt = k_ref[j * tq:(j + 1) * tq, :]
        for cs in chains:
            qx = qx_scr[qi * 2 * tq + cs.start:qi * 2 * tq + cs.stop, :]
            st = _nt_dot(kt, qx)
            s_scr[:, cs] = st
            mx_scr[:, cs] = jnp.max(st, axis=0, keepdims=True)

    def softmax_pv(j, s_scr, mx_scr, diagonal):
        start = j * tq
        for cs in chains:
            n_keys = visible_keys(cs, diagonal)
            if diagonal:
                lane = lax.broadcasted_iota(jnp.int32, (CHUNK, QUERY_LANES), 1)
                blocks = []
                for kc in range(n_keys // CHUNK):
                    blk = s_scr[kc * CHUNK:(kc + 1) * CHUNK, cs]
                    first_visible = kc * CHUNK - cs.start % tq
                    if first_visible > 0:
                        blk = jnp.where(lane >= first_visible, blk, NEG_INF)
                    blocks.append(blk)
                st = jnp.concatenate(blocks, axis=0)
                tile_max = jnp.max(st, axis=0, keepdims=True)
            else:
                st = s_scr[:, cs]
                tile_max = mx_scr[:, cs]
            vt = vt_ref[:, start:start + n_keys]
            p_scale = None
            if j == 0:
                m_new = tile_max
            else:
                m_old = m_scr[:, cs]
                m_new = jnp.maximum(m_old, tile_max)
                p_scale = jnp.exp2(m_old - m_new)
            p = jnp.exp2(st - m_new).astype(BF16)
            pv = _dot(vt, p)
            acc_scr[:, cs] = pv if p_scale is None else acc_scr[:, cs] * p_scale + pv
            m_scr[:, cs] = m_new

    def finish(qi):
        acc = acc_scr[...]
        o = acc[0:V_DIM, :] * (1.0 / acc[V_DIM:V_DIM + 1, :])
        lam = _diff_lambda(lq1_ref[...], lk1_ref[...], lq2_ref[...], lk2_ref[...], lam_init)
        d = (o[:, 0:tq] - lam * o[:, tq:2 * tq]).T
        o_ref[qi * tq:(qi + 1) * tq, :] = _sub_norm(d, gain_ref[...], lam_init).astype(BF16)

    bufs = ((s0_scr, mx0_scr), (s1_scr, mx1_scr))
    scores(*items[0], *bufs[0])

    def run_item(w):
        qi, j = items[w]
        if w + 1 < len(items):
            scores(*items[w + 1], *bufs[(w + 1) % 2])
        softmax_pv(j, *bufs[w % 2], j == qi)
        if j == qi:
            finish(qi)

    _in_regions(len(items), ATTN_ITEMS_PER_REGION, run_item)


def _prompt_attention(lams, gain, q, kb, vt, bsz, seq_len, tq, lam_init):
    nq = seq_len // tq
    small = lambda b, h: (0, 0)
    return pl.pallas_call(
        functools.partial(_prompt_attn_kernel, tq=tq, lam_init=lam_init),
        grid=(bsz, N_HEADS),
        in_specs=[pl.BlockSpec((1, HEAD_DIM), small)] * 4 + [
            pl.BlockSpec((1, V_DIM), small),
            pl.BlockSpec((seq_len, V_DIM), lambda b, h: (b, h)),
            pl.BlockSpec((seq_len, V_DIM), lambda b, h: (b, h)),
            pl.BlockSpec((None, None, VT_ROWS, seq_len), lambda b, h: (b, h, 0, 0))],
        out_specs=pl.BlockSpec((seq_len, V_DIM), lambda b, h: (b, h)),
        out_shape=jax.ShapeDtypeStruct((bsz * seq_len, ATTN_WIDTH), BF16),
        scratch_shapes=[pltpu.VMEM((1, 2 * tq), F32), pltpu.VMEM((VT_ROWS, 2 * tq), F32),
                        pltpu.VMEM((nq * 2 * tq, V_DIM), BF16),
                        pltpu.VMEM((tq, 2 * tq), F32), pltpu.VMEM((tq, 2 * tq), F32),
                        pltpu.VMEM((1, 2 * tq), F32), pltpu.VMEM((1, 2 * tq), F32)],
        compiler_params=_cparams("parallel", "parallel"),
        name="prompt_attn",
    )(*lams, gain, q, kb, vt)


def _sample_attn_kernel(lq1_ref, lk1_ref, lq2_ref, lk2_ref, gain_ref, q_ref, k_ref, v_ref, ck_ref, cv_ref,
                        o_ref, *, lam_init):
    n_streams = ck_ref.shape[0]
    n_new = q_ref.shape[0] // n_streams
    past = ck_ref.shape[1] // N_HEADS
    lam = _diff_lambda(lq1_ref[...], lk1_ref[...], lq2_ref[...], lk2_ref[...], lam_init)
    gain = gain_ref[...]
    for s in range(n_streams):
        rows = slice(s * n_new, (s + 1) * n_new)
        for h in range(N_HEADS):
            sl = slice(h * V_DIM, (h + 1) * V_DIM)
            old = pl.ds(h, past, stride=N_HEADS)
            new = pl.ds(s * n_new * N_HEADS + h, n_new, stride=N_HEADS)
            qx = _stack_maps(q_ref[rows, sl])
            s_c = _nt_dot(qx, ck_ref[s, old, :].astype(BF16))
            s_n = _nt_dot(qx, k_ref[new, :].astype(BF16))
            m = jnp.maximum(jnp.max(s_c, axis=1, keepdims=True), jnp.max(s_n, axis=1, keepdims=True))
            p_c = jnp.exp2(s_c - m).astype(BF16)
            p_n = jnp.exp2(s_n - m).astype(BF16)
            vc = jnp.concatenate([cv_ref[s, old, :].astype(BF16), jnp.ones((past, V_DIM), BF16)], axis=1)
            vn = jnp.concatenate([v_ref[new, :].astype(BF16), jnp.ones((n_new, V_DIM), BF16)], axis=1)
            o_ext = _dot(p_c, vc) + _dot(p_n, vn)
            o = o_ext[:, 0:V_DIM] * (1.0 / o_ext[:, V_DIM:2 * V_DIM])
            d = o[0:n_new, :] - lam * o[n_new:2 * n_new, :]
            o_ref[rows, sl] = _sub_norm(d, gain, lam_init).astype(BF16)


def _sample_attention(lams, gain, q, k, v, cache_k, cache_v, lam_init):
    bsz, past_rows = cache_k.shape[0], cache_k.shape[1]
    n_new = q.shape[0] // bsz
    ns = SAMPLE_STREAMS_PER_STEP if bsz % SAMPLE_STREAMS_PER_STEP == 0 else 1
    small = lambda b: (0, 0)
    row = lambda b: (b, 0)
    return pl.pallas_call(
        functools.partial(_sample_attn_kernel, lam_init=lam_init),
        grid=(bsz // ns,),
        in_specs=[pl.BlockSpec((1, HEAD_DIM), small)] * 4 + [
            pl.BlockSpec((1, V_DIM), small),
            pl.BlockSpec((ns * n_new, ATTN_WIDTH), row),
            pl.BlockSpec((ns * n_new * N_HEADS, V_DIM), row),
            pl.BlockSpec((ns * n_new * N_HEADS, V_DIM), row),
            pl.BlockSpec((ns, past_rows, V_DIM), lambda b: (b, 0, 0)),
            pl.BlockSpec((ns, past_rows, V_DIM), lambda b: (b, 0, 0))],
        out_specs=pl.BlockSpec((ns * n_new, ATTN_WIDTH), row),
        out_shape=jax.ShapeDtypeStruct(q.shape, BF16),
        compiler_params=_cparams("parallel"),
        name="sample_attn",
    )(*lams, gain, q, k, v, cache_k, cache_v)


def _ssm_params(a_re, a_im, b_re, b_im, c_re, c_im, log_dt):
    g = a_re.shape[0]
    twice = lambda v: jnp.concatenate([v, v], axis=-1)
    rows = jnp.stack([twice(a_re), twice(a_im), jnp.broadcast_to(log_dt[:, None], (g, 2 * STATE_DIM))], axis=1)
    rows = jnp.pad(rows, ((0, 0), (0, 8 - rows.shape[1]), (0, 0)))
    return rows, twice(b_re.transpose(0, 2, 1)), twice(b_im.transpose(0, 2, 1)), twice(c_re), twice(c_im)


def _group_tables(rows_ref, bre_ref, bim_ref, cre_ref, cim_ref):
    lane = lax.broadcasted_iota(jnp.int32, (1, 2 * STATE_DIM), 1)
    lo = lane < STATE_DIM
    a_re, a_im, log_dt = rows_ref[0:1, :], rows_ref[1:2, :], rows_ref[2:3, :]
    dt = jnp.exp(log_dt)
    lam_re, lam_im = a_re * dt, a_im * dt
    mag = jnp.exp(lam_re)
    ar, ai = mag * jnp.cos(lam_im), mag * jnp.sin(lam_im)
    den = jnp.square(a_re) + jnp.square(a_im)
    cr = ((ar - 1.0) * a_re + ai * a_im) / den
    ci = (ai * a_re - (ar - 1.0) * a_im) / den
    bre, bim = bre_ref[...], bim_ref[...]
    bbr = cr * bre - ci * bim
    bbi = cr * bim + ci * bre
    quarter_turn = jnp.where(lo, 0.0, 0.5 * math.pi)

    def powers(tau):
        return jnp.exp(tau * lam_re) * jnp.cos(tau * lam_im - quarter_turn)

    def outer(y, x1, x2):
        y_sw = pltpu.roll(y, STATE_DIM, 1)
        prod = y[:, None, :] * x1[None, :, :] + y_sw[:, None, :] * x2[None, :, :]
        return prod.reshape(y.shape[0] * x1.shape[0], 2 * STATE_DIM)

    frames = lax.broadcasted_iota(jnp.int32, (CHUNK, 1), 0).astype(F32)
    wt = outer(powers((CHUNK - 1.0) - frames), bbr, jnp.where(lo, -bbi, bbi))
    w = wt.T
    cre, cim = cre_ref[...], cim_ref[...]
    vm = outer(powers(frames + 1.0), jnp.where(lo, cre, -cre), -cim)
    rrev = jnp.dot(jnp.where(lo, cre, -cim), w, precision=lax.Precision.HIGHEST, preferred_element_type=F32)
    a_row = powers(jnp.full((8, 1), float(CHUNK), F32))[0:1, :]
    a_col = jnp.broadcast_to(a_row, (2 * STATE_DIM, 2 * STATE_DIM)).T
    a_swap = pltpu.roll(a_col, STATE_DIM, 0)
    top = lax.broadcasted_iota(jnp.int32, a_col.shape, 0) < STATE_DIM
    return (rrev, w.astype(BF16), vm.astype(BF16),
            jnp.where(top, a_col, a_swap), jnp.where(top, -a_swap, a_col))


def _chunk_cols_kernel(u_ref, o_ref):
    n_s, n_r, n_j = u_ref.shape[0], u_ref.shape[1], u_ref.shape[2]
    pad = LANES - n_s * n_r
    for jj in range(n_j):
        rows = [u_ref[s, :, jj, :] for s in range(n_s)]
        if pad:
            rows.append(jnp.zeros((pad, SSM_WIDTH), F32))
        cols = jnp.concatenate(rows, axis=0).T
        o_ref[:, jj * GROUP_SIZE:(jj + 1) * GROUP_SIZE, :] = cols.reshape(N_GROUPS, GROUP_SIZE, LANES).astype(BF16)


def _chunk_cols(u4, frames_per_step=16):
    s_total, n_r = u4.shape[0], u4.shape[1]
    s_step = min(s_total, LANES // n_r)
    assert s_step >= 1 and s_total % s_step == 0
    n_lane_blocks = s_total // s_step
    return pl.pallas_call(
        _chunk_cols_kernel,
        grid=(n_lane_blocks, CHUNK // frames_per_step),
        in_specs=[pl.BlockSpec((s_step, n_r, frames_per_step, SSM_WIDTH), lambda a, j: (a, 0, j, 0))],
        out_specs=pl.BlockSpec((N_GROUPS, frames_per_step * GROUP_SIZE, LANES), lambda a, j: (0, j, a)),
        out_shape=jax.ShapeDtypeStruct((N_GROUPS, CHUNK * GROUP_SIZE, n_lane_blocks * LANES), BF16),
        compiler_params=_cparams("parallel", "parallel"),
        name="ssm_in",
    )(u4)


def _unchunk_cols_kernel(y_ref, o_ref):
    n_s, n_r, n_j = o_ref.shape[0], o_ref.shape[1], o_ref.shape[2]
    for jj in range(n_j):
        cols = y_ref[:, jj * GROUP_SIZE:(jj + 1) * GROUP_SIZE, :].reshape(SSM_WIDTH, LANES)
        rows = cols.T
        for s in range(n_s):
            o_ref[s, :, jj, :] = rows[s * n_r:(s + 1) * n_r, :]


def _unchunk_cols(y_cols, s_total, n_r, frames_per_step=16):
    s_step = min(s_total, LANES // n_r)
    n_lane_blocks = s_total // s_step
    return pl.pallas_call(
        _unchunk_cols_kernel,
        grid=(n_lane_blocks, CHUNK // frames_per_step),
        in_specs=[pl.BlockSpec((N_GROUPS, frames_per_step * GROUP_SIZE, LANES), lambda a, j: (0, j, a))],
        out_specs=pl.BlockSpec((s_step, n_r, frames_per_step, SSM_WIDTH), lambda a, j: (a, 0, j, 0)),
        out_shape=jax.ShapeDtypeStruct((s_total, n_r, CHUNK, SSM_WIDTH), F32),
        compiler_params=_cparams("parallel", "parallel"),
        name="ssm_out",
    )(y_cols)


def _ssm_kernel(*refs, n_chunk):
    for gi in range(SSM_GROUPS_PER_STEP):
        _ssm_group(*[r.at[gi] for r in refs], n_chunk=n_chunk)


def _ssm_group(rows_ref, bre_ref, bim_ref, cre_ref, cim_ref, xp_ref, xs_ref, s0_ref, yp_ref, ys_ref, sp_ref, ss_ref,
               mt_scr, *, n_chunk):
    width = GROUP_SIZE * CHUNK
    n_piece = width // LANES
    lane16 = lax.broadcasted_iota(jnp.int32, (GROUP_SIZE, LANES), 1)
    rrev, w, vm_all, a_c, a_s = _group_tables(rows_ref, bre_ref, bim_ref, cre_ref, cim_ref)
    pieces = [rrev[:, k * LANES:(k + 1) * LANES] for k in range(n_piece)] + [jnp.zeros((GROUP_SIZE, LANES), F32)]
    rolled = {0: pieces}
    for b in range(GROUP_SIZE, LANES, GROUP_SIZE):
        rolled[b] = [pltpu.roll(p, LANES - b, 1) for p in pieces[:n_piece]] + [pieces[n_piece]]
    def toeplitz_rows(t0, t1, n_cols):
        for t in range(t0, t1):
            shift = GROUP_SIZE * (CHUNK - 1 - t)
            a, b = shift // LANES, shift % LANES
            for v in range(n_cols // LANES):
                k = v + a
                if k >= n_piece:
                    blk = pieces[n_piece]
                elif b == 0:
                    blk = pieces[k]
                else:
                    blk = jnp.where(lane16 < LANES - b, rolled[b][k], rolled[b][k + 1])
                mt_scr[t * GROUP_SIZE:(t + 1) * GROUP_SIZE, v * LANES:(v + 1) * LANES] = blk.astype(BF16)

    xp, xs = xp_ref[...], xs_ref[...]
    sloc_p, sloc_s = _dot(w, xp), _dot(w, xs)

    def cmul(pc, ps, s):
        return pc * s + ps * pltpu.roll(s, STATE_DIM, 0)

    lane = lax.broadcasted_iota(jnp.int32, (2 * STATE_DIM, LANES), 1) % n_chunk
    prev_cols = []
    lane_id = lax.broadcasted_iota(jnp.int32, (2 * STATE_DIM, LANES), 1)
    final = jnp.zeros((2 * STATE_DIM, LANES), F32)
    seqs_per_tile = LANES // n_chunk
    for tile in range(xp.shape[1] // LANES):
        s_inc = sloc_p[:, tile * LANES:(tile + 1) * LANES]
        pc, ps = a_c, a_s
        dist = 1
        while dist < n_chunk:
            shifted = jnp.where(lane >= dist, pltpu.roll(s_inc, dist, 1), 0.0)
            s_inc = s_inc + cmul(pc, ps, shifted)
            pc, ps = pc * pc - ps * ps, 2.0 * pc * ps
            dist *= 2
        for k in range(seqs_per_tile):
            src, dst = (k + 1) * n_chunk - 1, tile * seqs_per_tile + k
            final = jnp.where(lane_id == dst, pltpu.roll(s_inc, (dst - src) % LANES, 1), final)
        prev_cols.append(jnp.where(lane >= 1, pltpu.roll(s_inc, 1, 1), 0.0))
    sp_ref[...] = final
    s_prev_p = jnp.concatenate(prev_cols, axis=1).astype(BF16)
    s0 = s0_ref[...]
    s0b = s0.astype(BF16)
    frames = TOEPLITZ_ROWS // GROUP_SIZE
    for i in range(width // TOEPLITZ_ROWS):
        rows = slice(i * TOEPLITZ_ROWS, (i + 1) * TOEPLITZ_ROWS)
        n_cols = (i + 1) * TOEPLITZ_ROWS
        toeplitz_rows(i * frames, (i + 1) * frames, n_cols)
        mt = mt_scr[rows, 0:n_cols]
        vm = vm_all[rows, :]
        yp_ref[rows, :] = _dot(mt, xp[0:n_cols, :]) + _dot(vm, s_prev_p)
        ys_ref[rows, :] = _dot(mt, xs[0:n_cols, :]) + _dot(vm, s0b)
    ss_ref[...] = cmul(a_c, a_s, s0) + sloc_s


def _ssm(params, x_p, x_s, s0, n_chunk):
    n_groups, lanes_p = x_p.shape[0], x_p.shape[2]
    width = GROUP_SIZE * CHUNK
    per_g = lambda g: (g, 0, 0)
    state_rows = 2 * STATE_DIM
    gb = SSM_GROUPS_PER_STEP
    assert n_groups % gb == 0
    spec = lambda rows, cols: pl.BlockSpec((gb, rows, cols), per_g)
    return pl.pallas_call(
        functools.partial(_ssm_kernel, n_chunk=n_chunk),
        grid=(n_groups // gb,),
        in_specs=[spec(8, state_rows)] + [spec(GROUP_SIZE, state_rows)] * 4 + [
                  spec(width, lanes_p), spec(width, LANES), spec(state_rows, LANES)],
        out_specs=[spec(width, lanes_p), spec(width, LANES), spec(state_rows, LANES), spec(state_rows, LANES)],
        out_shape=[jax.ShapeDtypeStruct((n_groups, width, lanes_p), F32),
                   jax.ShapeDtypeStruct((n_groups, width, LANES), F32),
                   jax.ShapeDtypeStruct((n_groups, state_rows, LANES), F32),
                   jax.ShapeDtypeStruct((n_groups, state_rows, LANES), F32)],
        scratch_shapes=[pltpu.VMEM((gb, width, width), BF16)],
        compiler_params=_cparams("parallel"),
        name="ssm",
    )(*params, x_p, x_s, s0)


def _layer_norm(x, g, b):
    mu = jnp.mean(x, axis=1, keepdims=True)
    xc = x - mu
    var = jnp.mean(jnp.square(xc), axis=1, keepdims=True)
    return xc * lax.rsqrt(var + LN_EPS) * g + b


def _gelu_tanh(x):
    return 0.5 * x * (1.0 + jnp.tanh(math.sqrt(2.0 / math.pi) * (x + 0.044715 * (x * x * x))))


def _merge_kernel(x_ref, ao_ref, ys_ref, u_ref, d_ref, wg_ref, wap_ref, wglu_ref, wout_ref, g1_ref, b1_ref, h_ref):
    def sub_tile(r):
        rows = slice(r * SUB_ROWS, (r + 1) * SUB_ROWS)
        x = x_ref[rows, :]
        xb = x.astype(BF16)
        a_branch = _dot(ao_ref[rows, :], wap_ref[...])
        s_act = _gelu_tanh(ys_ref[rows, :] + d_ref[...] * u_ref[rows, :]).astype(BF16)
        s_branch = (_dot(s_act, wglu_ref[:, 0:D_MODEL])
                    * jax.nn.sigmoid(_dot(s_act, wglu_ref[:, D_MODEL:2 * D_MODEL])))
        m = (jax.nn.sigmoid(_dot(xb, wg_ref[:, 0:D_MODEL])) * a_branch
             + jax.nn.sigmoid(_dot(xb, wg_ref[:, D_MODEL:2 * D_MODEL])) * s_branch)
        h_ref[rows, :] = _layer_norm(DEEPNORM_ALPHA * x + _dot(m.astype(BF16), wout_ref[...]),
                                     g1_ref[...], b1_ref[...])

    for r in range(x_ref.shape[0] // SUB_ROWS):
        sub_tile(r)


def _merge(x2d, ao, ys, u, d, w_gate, w_ap, w_glu, w_out, ln_g, ln_b, tm):
    t_tokens = x2d.shape[0]
    row = lambda i: (i, 0)
    const = lambda i: (0, 0)
    return pl.pallas_call(
        _merge_kernel,
        grid=(t_tokens // tm,),
        in_specs=[pl.BlockSpec((tm, D_MODEL), row),
                  pl.BlockSpec((tm, ATTN_WIDTH), row),
                  pl.BlockSpec((tm, SSM_WIDTH), row),
                  pl.BlockSpec((tm, SSM_WIDTH), row),
                  pl.BlockSpec((1, SSM_WIDTH), const),
                  pl.BlockSpec((D_MODEL, 2 * D_MODEL), const, pipeline_mode=pl.Buffered(1)),
                  pl.BlockSpec((ATTN_WIDTH, D_MODEL), const, pipeline_mode=pl.Buffered(1)),
                  pl.BlockSpec((SSM_WIDTH, 2 * D_MODEL), const, pipeline_mode=pl.Buffered(1)),
                  pl.BlockSpec((D_MODEL, D_MODEL), const, pipeline_mode=pl.Buffered(1)),
                  pl.BlockSpec((1, D_MODEL), const),
                  pl.BlockSpec((1, D_MODEL), const)],
        out_specs=pl.BlockSpec((tm, D_MODEL), row),
        out_shape=jax.ShapeDtypeStruct((t_tokens, D_MODEL), F32),
        compiler_params=_cparams("parallel"),
        name="merge",
    )(x2d, ao, ys, u, d, w_gate, w_ap, w_glu, w_out, ln_g, ln_b)


def _mlp_kernel(h_ref, w1_ref, w2_ref, g2_ref, b2_ref, o_ref, *, ff_chunk):
    for r in range(h_ref.shape[0] // SUB_ROWS):
        rows = slice(r * SUB_ROWS, (r + 1) * SUB_ROWS)
        h = h_ref[rows, :]
        hb = h.astype(BF16)
        f = jnp.zeros(h.shape, F32)
        for c in range(D_FF // ff_chunk):
            sl = slice(c * ff_chunk, (c + 1) * ff_chunk)
            t = jnp.maximum(_dot(hb, w1_ref[:, sl]), 0.0)
            f = f + _dot((t * t).astype(BF16), w2_ref[sl, :])
        o_ref[rows, :] = _layer_norm(DEEPNORM_ALPHA * h + f, g2_ref[...], b2_ref[...])


def _mlp(h, w1, w2, ln_g, ln_b, tm, ff_chunk=1024):
    t_tokens = h.shape[0]
    row = lambda i: (i, 0)
    const = lambda i: (0, 0)
    return pl.pallas_call(
        functools.partial(_mlp_kernel, ff_chunk=ff_chunk),
        grid=(t_tokens // tm,),
        in_specs=[pl.BlockSpec((tm, D_MODEL), row),
                  pl.BlockSpec((D_MODEL, D_FF), const, pipeline_mode=pl.Buffered(1)),
                  pl.BlockSpec((D_FF, D_MODEL), const, pipeline_mode=pl.Buffered(1)),
                  pl.BlockSpec((1, D_MODEL), const),
                  pl.BlockSpec((1, D_MODEL), const)],
        out_specs=pl.BlockSpec((tm, D_MODEL), row),
        out_shape=jax.ShapeDtypeStruct((t_tokens, D_MODEL), F32),
        compiler_params=_cparams("parallel"),
        name="mlp",
    )(h, w1, w2, ln_g, ln_b)


def _tail_kernel(x_ref, ao_ref, ys_ref, u_ref, d_ref, wg_ref, wap_ref, wglu_ref, wout_ref, g1_ref, b1_ref,
                 w1_ref, w2_ref, g2_ref, b2_ref, o_ref, *, ff_chunk):
    for r in range(x_ref.shape[0] // SUB_ROWS):
        rows = slice(r * SUB_ROWS, (r + 1) * SUB_ROWS)
        x = x_ref[rows, :]
        xb = x.astype(BF16)
        a_branch = _dot(ao_ref[rows, :], wap_ref[...])
        s_act = _gelu_tanh(ys_ref[rows, :] + d_ref[...] * u_ref[rows, :]).astype(BF16)
        s_branch = (_dot(s_act, wglu_ref[:, 0:D_MODEL])
                    * jax.nn.sigmoid(_dot(s_act, wglu_ref[:, D_MODEL:2 * D_MODEL])))
        m = (jax.nn.sigmoid(_dot(xb, wg_ref[:, 0:D_MODEL])) * a_branch
             + jax.nn.sigmoid(_dot(xb, wg_ref[:, D_MODEL:2 * D_MODEL])) * s_branch)
        h = _layer_norm(DEEPNORM_ALPHA * x + _dot(m.astype(BF16), wout_ref[...]), g1_ref[...], b1_ref[...])
        hb = h.astype(BF16)
        f = jnp.zeros(h.shape, F32)
        for c in range(D_FF // ff_chunk):
            sl = slice(c * ff_chunk, (c + 1) * ff_chunk)
            t = jnp.maximum(_dot(hb, w1_ref[:, sl]), 0.0)
            f = f + _dot((t * t).astype(BF16), w2_ref[sl, :])
        o_ref[rows, :] = _layer_norm(DEEPNORM_ALPHA * h + f, g2_ref[...], b2_ref[...])


def _tail(x2d, ao, ys, u, d, w_gate, w_ap, w_glu, w_out, ln1_g, ln1_b, w1, w2, ln2_g, ln2_b, tm, ff_chunk=1024):
    t_tokens = x2d.shape[0]
    row = lambda i: (i, 0)
    const = lambda i: (0, 0)
    resident = lambda rows, cols: pl.BlockSpec((rows, cols), const, pipeline_mode=pl.Buffered(1))
    vec = pl.BlockSpec((1, D_MODEL), const)
    return pl.pallas_call(
        functools.partial(_tail_kernel, ff_chunk=ff_chunk),
        grid=(t_tokens // tm,),
        in_specs=[pl.BlockSpec((tm, D_MODEL), row),
                  pl.BlockSpec((tm, ATTN_WIDTH), row),
                  pl.BlockSpec((tm, SSM_WIDTH), row),
                  pl.BlockSpec((tm, SSM_WIDTH), row),
                  pl.BlockSpec((1, SSM_WIDTH), const),
                  resident(D_MODEL, 2 * D_MODEL), resident(ATTN_WIDTH, D_MODEL),
                  resident(SSM_WIDTH, 2 * D_MODEL), resident(D_MODEL, D_MODEL), vec, vec,
                  resident(D_MODEL, D_FF), resident(D_FF, D_MODEL), vec, vec],
        out_specs=pl.BlockSpec((tm, D_MODEL), row),
        out_shape=jax.ShapeDtypeStruct((t_tokens, D_MODEL), F32),
        compiler_params=_cparams("parallel"),
        name="tail",
    )(x2d, ao, ys, u, d, w_gate, w_ap, w_glu, w_out, ln1_g, ln1_b, w1, w2, ln2_g, ln2_b)


def _rope_tables(pos):
    inv = 1.0 / (ROPE_THETA ** (jnp.arange(0, HEAD_DIM, 2, dtype=F32) / HEAD_DIM))
    ang = pos.astype(F32)[:, None] * inv[None, :]
    c, s = jnp.cos(ang), jnp.sin(ang)
    reps = LANES // HEAD_DIM
    return jnp.tile(jnp.concatenate([c, c], axis=1), (1, reps)), jnp.tile(jnp.concatenate([-s, s], axis=1), (1, reps))


def kernel(x_prompt, x_sample, cache_k, cache_v, state_ssm_re, state_ssm_im, w_in, lambda_q1, lambda_k1, lambda_q2, lambda_k2, subln_gain, ssm_a_re, ssm_a_im, ssm_b_re, ssm_b_im, ssm_c_re, ssm_c_im, ssm_d, ssm_log_dt, w_attn_proj, w_glu_a, w_glu_b, w_out, ln1_g, ln1_b, w_ff1, w_ff2, ln2_g, ln2_b):
    bp, n_p = x_prompt.shape[0], x_prompt.shape[1]
    bs, n_s = x_sample.shape[0], x_sample.shape[1]
    past = cache_k.shape[2]
    assert w_in.shape[0] == DEPTH and n_s == CHUNK and n_p % CHUNK == 0
    n_chunk = n_p // CHUNK
    tm = min(512, n_p)
    tm_s = min(512, bs * n_s)
    tq = min(512, n_p)
    l = 0
    lam_init = 0.8 - 0.6 * math.exp(-0.3 * l)

    xp = x_prompt.reshape(bp * n_p, D_MODEL)
    xs = x_sample.reshape(bs * n_s, D_MODEL)
    w_qkvu = w_in[l, :, 0:QKVU_COLS].astype(BF16)
    w_gate = w_in[l, :, QKVU_COLS:].astype(BF16)
    w_ap = w_attn_proj[l].astype(BF16)
    w_glu = jnp.concatenate([w_glu_a[l], w_glu_b[l]], axis=1).astype(BF16)
    w_o = w_out[l].astype(BF16)
    w1, w2 = w_ff1[l].astype(BF16), w_ff2[l].astype(BF16)
    lams = [v[l].reshape(1, HEAD_DIM) for v in (lambda_q1, lambda_k1, lambda_q2, lambda_k2)]
    gain = subln_gain[l].reshape(1, V_DIM)
    d_skip = ssm_d[l].reshape(1, SSM_WIDTH)
    lng = [v[l].reshape(1, D_MODEL) for v in (ln1_g, ln1_b, ln2_g, ln2_b)]

    cos_p, sin_p = _rope_tables(jnp.arange(n_p))
    cos_s, sin_s = _rope_tables(jnp.tile(past + jnp.arange(n_s), tm_s // n_s))

    q_p, k_p, v_p, u_p, kb_p, vt_p = _project(xp, w_qkvu, cos_p, sin_p, n_p, tm, True)
    q_s, k_s, v_s, u_s = _project(xs, w_qkvu, cos_s, sin_s, n_s, tm_s, False)

    ao_p = _prompt_attention(lams, gain, q_p, kb_p, vt_p, bp, n_p, tq, lam_init)
    ao_s = _sample_attention(lams, gain, q_s, k_s, v_s,
                             cache_k[l].reshape(bs, past * N_HEADS, V_DIM),
                             cache_v[l].reshape(bs, past * N_HEADS, V_DIM), lam_init)

    tables = _ssm_params(ssm_a_re[l], ssm_a_im[l], ssm_b_re[l], ssm_b_im[l],
                         ssm_c_re[l], ssm_c_im[l], ssm_log_dt[l])
    x_cols_p = _chunk_cols(u_p.reshape(bp, n_chunk, CHUNK, SSM_WIDTH))
    x_cols_s = _chunk_cols(u_s.reshape(1, bs, CHUNK, SSM_WIDTH))
    s0 = jnp.concatenate([state_ssm_re[l], state_ssm_im[l]], axis=-1).transpose(1, 2, 0)
    s0 = jnp.pad(s0, ((0, 0), (0, 0), (0, LANES - bs)))
    y_cols_p, y_cols_s, st_p, st_s = _ssm(tables, x_cols_p, x_cols_s, s0, n_chunk)
    ys_p = _unchunk_cols(y_cols_p, bp, n_chunk).reshape(bp * n_p, SSM_WIDTH)
    ys_s = _unchunk_cols(y_cols_s, 1, bs).reshape(bs * n_s, SSM_WIDTH)
    sf_p = st_p[:, :, 0:bp]
    sf_s = st_s[:, :, 0:bs]

    outs = []
    for x2d, ao, ys, u, tile in ((xp, ao_p, ys_p, u_p, tm), (xs, ao_s, ys_s, u_s, tm_s)):
        outs.append(_tail(x2d, ao, ys, u, d_skip, w_gate, w_ap, w_glu, w_o, lng[0], lng[1],
                          w1, w2, lng[2], lng[3], tile))

    def states(sf):
        t = sf.transpose(2, 0, 1)
        return t[None, :, :, 0:STATE_DIM], t[None, :, :, STATE_DIM:]

    srp, sip = states(sf_p)
    srs, sis = states(sf_s)
    return (outs[0].reshape(bp, n_p, D_MODEL), outs[1].reshape(bs, n_s, D_MODEL),
            k_p.reshape(1, bp, n_p, N_HEADS, V_DIM), v_p.reshape(1, bp, n_p, N_HEADS, V_DIM), srp, sip,
            k_s.reshape(1, bs, n_s, N_HEADS, V_DIM), v_s.reshape(1, bs, n_s, N_HEADS, V_DIM), srs, sis)
```

```python
import functools
import math

import jax
import jax.numpy as jnp
from jax import lax
from jax.experimental import pallas as pl
from jax.experimental.pallas import tpu as pltpu

D_MODEL = 1024
CHUNK = 64
N_HEADS = 4
HEAD_DIM = 64
V_DIM = 2 * HEAD_DIM
ATTN_WIDTH = N_HEADS * V_DIM
SSM_WIDTH = 512
GROUP_SIZE = 16
N_GROUPS = SSM_WIDTH // GROUP_SIZE
STATE_DIM = 64
D_FF = 4 * D_MODEL
ROPE_THETA = 10000.0
LN_EPS = 1e-5
RMS_EPS = 1e-5
NEG_INF = -1e30
DEPTH = 1
DEEPNORM_ALPHA = (2.0 * DEPTH) ** 0.25
QKVU_COLS = 3 * ATTN_WIDTH + SSM_WIDTH
LOG2E = 1.4426950408889634

LANES = 128
VT_ROWS = V_DIM + 16
QUERY_LANES = 256
TOEPLITZ_ROWS = 256
SUB_ROWS = 256
ATTN_ITEMS_PER_REGION = 12
SSM_GROUPS_PER_STEP = 4
SAMPLE_STREAMS_PER_STEP = 4
VMEM_LIMIT = 56 * 1024 * 1024

F32 = jnp.float32
BF16 = jnp.bfloat16


def _cparams(*sem):
    return pltpu.CompilerParams(dimension_semantics=sem, vmem_limit_bytes=VMEM_LIMIT)


def _nt_dot(a, b):
    return lax.dot_general(a, b, (((1,), (1,)), ((), ())), preferred_element_type=F32)


def _dot(a, b):
    return jnp.dot(a, b, preferred_element_type=F32)


def _in_regions(n_steps, steps_per_region, step):
    once = jnp.minimum(pl.program_id(0) + 1, 1)
    for first in range(0, n_steps, steps_per_region):
        def region(_, carry, first=first):
            for i in range(first, min(first + steps_per_region, n_steps)):
                step(i)
            return carry
        lax.fori_loop(0, once, region, 0)


def _rotary(z, cos, sin_signed, first_half):
    swapped = jnp.where(first_half, pltpu.roll(z, 96, 1), pltpu.roll(z, 32, 1))
    return z * cos + swapped * sin_signed


def _proj_kernel(x_ref, w_ref, cos_ref, sin_ref, q_ref, k_ref, v_ref, u_ref, *rest, emit_t):
    xb = x_ref[...].astype(BF16)
    cos = cos_ref[...]
    sin = sin_ref[...]
    lane = lax.broadcasted_iota(jnp.int32, cos.shape, 1)
    first_half = (lane % HEAD_DIM) < (HEAD_DIM // 2)
    tm = xb.shape[0]
    zq = _dot(xb, w_ref[:, 0:ATTN_WIDTH])
    zk = _dot(xb, w_ref[:, ATTN_WIDTH:2 * ATTN_WIDTH])
    zv = _dot(xb, w_ref[:, 2 * ATTN_WIDTH:3 * ATTN_WIDTH])
    for h in range(N_HEADS):
        sl = slice(h * V_DIM, (h + 1) * V_DIM)
        q_ref[:, sl] = (_rotary(zq[:, sl], cos, sin, first_half) * (LOG2E * HEAD_DIM ** -0.5)).astype(BF16)
        kr = _rotary(zk[:, sl], cos, sin, first_half)
        k_ref[pl.ds(h, tm, stride=N_HEADS), :] = kr
        v_ref[pl.ds(h, tm, stride=N_HEADS), :] = zv[:, sl]
        if emit_t:
            rest[0][:, sl] = kr.astype(BF16)
    if emit_t:
        vt_ref = rest[1]
        zvt = zv.T.astype(BF16)
        ones = jnp.ones((VT_ROWS - V_DIM, zvt.shape[1]), BF16)
        for h in range(N_HEADS):
            vt_ref[h, 0:V_DIM, :] = zvt[h * V_DIM:(h + 1) * V_DIM, :]
            vt_ref[h, V_DIM:VT_ROWS, :] = ones
    u_ref[...] = _dot(xb, w_ref[:, 3 * ATTN_WIDTH:QKVU_COLS])


def _project(x2d, w_qkvu, cos_t, sin_t, seq_len, tm, emit_t):
    t_tokens = x2d.shape[0]
    n_tiles = t_tokens // tm
    n_pos_tiles = cos_t.shape[0] // tm
    tiles_per_seq = max(seq_len // tm, 1)
    row = lambda i: (i, 0)
    pos = lambda i: (i % n_pos_tiles, 0)
    out_shape = [jax.ShapeDtypeStruct((t_tokens, ATTN_WIDTH), BF16),
                 jax.ShapeDtypeStruct((t_tokens * N_HEADS, V_DIM), F32),
                 jax.ShapeDtypeStruct((t_tokens * N_HEADS, V_DIM), F32),
                 jax.ShapeDtypeStruct((t_tokens, SSM_WIDTH), F32)]
    out_specs = ([pl.BlockSpec((tm, ATTN_WIDTH), row)] + [pl.BlockSpec((tm * N_HEADS, V_DIM), row)] * 2
                 + [pl.BlockSpec((tm, SSM_WIDTH), row)])
    if emit_t:
        bsz = t_tokens // seq_len
        out_shape += [jax.ShapeDtypeStruct((t_tokens, ATTN_WIDTH), BF16),
                      jax.ShapeDtypeStruct((bsz, N_HEADS, VT_ROWS, seq_len), BF16)]
        out_specs += [pl.BlockSpec((tm, ATTN_WIDTH), row),
                      pl.BlockSpec((None, N_HEADS, VT_ROWS, tm),
                                   lambda i: (i // tiles_per_seq, 0, 0, i % tiles_per_seq))]
    return pl.pallas_call(
        functools.partial(_proj_kernel, emit_t=emit_t),
        grid=(n_tiles,),
        in_specs=[pl.BlockSpec((tm, D_MODEL), row),
                  pl.BlockSpec((D_MODEL, QKVU_COLS), lambda i: (0, 0), pipeline_mode=pl.Buffered(1)),
                  pl.BlockSpec((tm, LANES), pos),
                  pl.BlockSpec((tm, LANES), pos)],
        out_specs=out_specs,
        out_shape=out_shape,
        compiler_params=_cparams("parallel"),
        name="proj_t" if emit_t else "proj",
    )(x2d, w_qkvu, cos_t, sin_t)


def _diff_lambda(lq1, lk1, lq2, lk2, lam_init):
    return (jnp.exp(jnp.sum(lq1 * lk1, axis=1, keepdims=True))
            - jnp.exp(jnp.sum(lq2 * lk2, axis=1, keepdims=True)) + lam_init)


def _sub_norm(d, gain, lam_init):
    ms = jnp.mean(jnp.square(d), axis=1, keepdims=True)
    return d * lax.rsqrt(ms + RMS_EPS) * gain * (1.0 - lam_init)


def _stack_maps(q):
    lane = lax.broadcasted_iota(jnp.int32, q.shape, 1)
    zero = jnp.zeros_like(q)
    return jnp.concatenate([jnp.where(lane < HEAD_DIM, q, zero), jnp.where(lane >= HEAD_DIM, q, zero)], axis=0)


def _prompt_attn_kernel(lq1_ref, lk1_ref, lq2_ref, lk2_ref, gain_ref, q_ref, k_ref, vt_ref,
                        o_ref, m_scr, acc_scr, qx_scr, s0_scr, s1_scr, mx0_scr, mx1_scr, *, tq, lam_init):
    nq = q_ref.shape[0] // tq
    items = [(qi, j) for qi in range(nq) for j in range(qi + 1)]
    for i in range(nq):
        qx_scr[i * 2 * tq:(i + 1) * 2 * tq, :] = _stack_maps(q_ref[i * tq:(i + 1) * tq, :])
    chains = [slice(c * QUERY_LANES, (c + 1) * QUERY_LANES) for c in range(2 * tq // QUERY_LANES)]

    def visible_keys(cs, diagonal):
        return min(tq, cs.start % tq + QUERY_LANES) if diagonal else tq

    def scores(qi, j, s_scr, mx_scr):
        kt = k_ref[j * tq:(j + 1) * tq, :]
        for cs in chains:
            qx = qx_scr[qi * 2 * tq + cs.start:qi * 2 * tq + cs.stop, :]
            st = _nt_dot(kt, qx)
            s_scr[:, cs] = st
            mx_scr[:, cs] = jnp.max(st, axis=0, keepdims=True)

    def softmax_pv(j, s_scr, mx_scr, diagonal):
        start = j * tq
        for cs in chains:
            n_keys = visible_keys(cs, diagonal)
            if diagonal:
                lane = lax.broadcasted_iota(jnp.int32, (CHUNK, QUERY_LANES), 1)
                blocks = []
                for kc in range(n_keys // CHUNK):
                    blk = s_scr[kc * CHUNK:(kc + 1) * CHUNK, cs]
                    first_visible = kc * CHUNK - cs.start % tq
                    if first_visible > 0:
                        blk = jnp.where(lane >= first_visible, blk, NEG_INF)
                    blocks.append(blk)
                st = jnp.concatenate(blocks, axis=0)
                tile_max = jnp.max(st, axis=0, keepdims=True)
            else:
                st = s_scr[:, cs]
                tile_max = mx_scr[:, cs]
            vt = vt_ref[:, start:start + n_keys]
            p_scale = None
            if j == 0:
                m_new = tile_max
            else:
                m_old = m_scr[:, cs]
                m_new = jnp.maximum(m_old, tile_max)
                p_scale = jnp.exp2(m_old - m_new)
            p = jnp.exp2(st - m_new).astype(BF16)
            pv = _dot(vt, p)
            acc_scr[:, cs] = pv if p_scale is None else acc_scr[:, cs] * p_scale + pv
            m_scr[:, cs] = m_new

    def finish(qi):
        acc = acc_scr[...]
        o = acc[0:V_DIM, :] * (1.0 / acc[V_DIM:V_DIM + 1, :])
        lam = _diff_lambda(lq1_ref[...], lk1_ref[...], lq2_ref[...], lk2_ref[...], lam_init)
        d = (o[:, 0:tq] - lam * o[:, tq:2 * tq]).T
        o_ref[qi * tq:(qi + 1) * tq, :] = _sub_norm(d, gain_ref[...], lam_init).astype(BF16)

    bufs = ((s0_scr, mx0_scr), (s1_scr, mx1_scr))
    scores(*items[0], *bufs[0])

    def run_item(w):
        qi, j = items[w]
        if w + 1 < len(items):
            scores(*items[w + 1], *bufs[(w + 1) % 2])
        softmax_pv(j, *bufs[w % 2], j == qi)
        if j == qi:
            finish(qi)

    _in_regions(len(items), ATTN_ITEMS_PER_REGION, run_item)


def _prompt_attention(lams, gain, q, kb, vt, bsz, seq_len, tq, lam_init):
    nq = seq_len // tq
    small = lambda b, h: (0, 0)
    return pl.pallas_call(
        functools.partial(_prompt_attn_kernel, tq=tq, lam_init=lam_init),
        grid=(bsz, N_HEADS),
        in_specs=[pl.BlockSpec((1, HEAD_DIM), small)] * 4 + [
            pl.BlockSpec((1, V_DIM), small),
            pl.BlockSpec((seq_len, V_DIM), lambda b, h: (b, h)),
            pl.BlockSpec((seq_len, V_DIM), lambda b, h: (b, h)),
            pl.BlockSpec((None, None, VT_ROWS, seq_len), lambda b, h: (b, h, 0, 0))],
        out_specs=pl.BlockSpec((seq_len, V_DIM), lambda b, h: (b, h)),
        out_shape=jax.ShapeDtypeStruct((bsz * seq_len, ATTN_WIDTH), BF16),
        scratch_shapes=[pltpu.VMEM((1, 2 * tq), F32), pltpu.VMEM((VT_ROWS, 2 * tq), F32),
                        pltpu.VMEM((nq * 2 * tq, V_DIM), BF16),
                        pltpu.VMEM((tq, 2 * tq), F32), pltpu.VMEM((tq, 2 * tq), F32),
                        pltpu.VMEM((1, 2 * tq), F32), pltpu.VMEM((1, 2 * tq), F32)],
        compiler_params=_cparams("parallel", "parallel"),
        name="prompt_attn",
    )(*lams, gain, q, kb, vt)


def _sample_attn_kernel(lq1_ref, lk1_ref, lq2_ref, lk2_ref, gain_ref, q_ref, k_ref, v_ref, ck_ref, cv_ref,
                        o_ref, *, lam_init):
    n_streams = ck_ref.shape[0]
    n_new = q_ref.shape[0] // n_streams
    past = ck_ref.shape[1] // N_HEADS
    lam = _diff_lambda(lq1_ref[...], lk1_ref[...], lq2_ref[...], lk2_ref[...], lam_init)
    gain = gain_ref[...]
    for s in range(n_streams):
        rows = slice(s * n_new, (s + 1) * n_new)
        for h in range(N_HEADS):
            sl = slice(h * V_DIM, (h + 1) * V_DIM)
            old = pl.ds(h, past, stride=N_HEADS)
            new = pl.ds(s * n_new * N_HEADS + h, n_new, stride=N_HEADS)
            qx = _stack_maps(q_ref[rows, sl])
            s_c = _nt_dot(qx, ck_ref[s, old, :].astype(BF16))
            s_n = _nt_dot(qx, k_ref[new, :].astype(BF16))
            m = jnp.maximum(jnp.max(s_c, axis=1, keepdims=True), jnp.max(s_n, axis=1, keepdims=True))
            p_c = jnp.exp2(s_c - m).astype(BF16)
            p_n = jnp.exp2(s_n - m).astype(BF16)
            vc = jnp.concatenate([cv_ref[s, old, :].astype(BF16), jnp.ones((past, V_DIM), BF16)], axis=1)
            vn = jnp.concatenate([v_ref[new, :].astype(BF16), jnp.ones((n_new, V_DIM), BF16)], axis=1)
            o_ext = _dot(p_c, vc) + _dot(p_n, vn)
            o = o_ext[:, 0:V_DIM] * (1.0 / o_ext[:, V_DIM:2 * V_DIM])
            d = o[0:n_new, :] - lam * o[n_new:2 * n_new, :]
            o_ref[rows, sl] = _sub_norm(d, gain, lam_init).astype(BF16)


def _sample_attention(lams, gain, q, k, v, cache_k, cache_v, lam_init):
    bsz, past_rows = cache_k.shape[0], cache_k.shape[1]
    n_new = q.shape[0] // bsz
    ns = SAMPLE_STREAMS_PER_STEP if bsz % SAMPLE_STREAMS_PER_STEP == 0 else 1
    small = lambda b: (0, 0)
    row = lambda b: (b, 0)
    return pl.pallas_call(
        functools.partial(_sample_attn_kernel, lam_init=lam_init),
        grid=(bsz // ns,),
        in_specs=[pl.BlockSpec((1, HEAD_DIM), small)] * 4 + [
            pl.BlockSpec((1, V_DIM), small),
            pl.BlockSpec((ns * n_new, ATTN_WIDTH), row),
            pl.BlockSpec((ns * n_new * N_HEADS, V_DIM), row),
            pl.BlockSpec((ns * n_new * N_HEADS, V_DIM), row),
            pl.BlockSpec((ns, past_rows, V_DIM), lambda b: (b, 0, 0)),
            pl.BlockSpec((ns, past_rows, V_DIM), lambda b: (b, 0, 0))],
        out_specs=pl.BlockSpec((ns * n_new, ATTN_WIDTH), row),
        out_shape=jax.ShapeDtypeStruct(q.shape, BF16),
        compiler_params=_cparams("parallel"),
        name="sample_attn",
    )(*lams, gain, q, k, v, cache_k, cache_v)


def _ssm_params(a_re, a_im, b_re, b_im, c_re, c_im, log_dt):
    g = a_re.shape[0]
    twice = lambda v: jnp.concatenate([v, v], axis=-1)
    rows = jnp.stack([twice(a_re), twice(a_im), jnp.broadcast_to(log_dt[:, None], (g, 2 * STATE_DIM))], axis=1)
    rows = jnp.pad(rows, ((0, 0), (0, 8 - rows.shape[1]), (0, 0)))
    return rows, twice(b_re.transpose(0, 2, 1)), twice(b_im.transpose(0, 2, 1)), twice(c_re), twice(c_im)


def _group_tables(rows_ref, bre_ref, bim_ref, cre_ref, cim_ref):
    lane = lax.broadcasted_iota(jnp.int32, (1, 2 * STATE_DIM), 1)
    lo = lane < STATE_DIM
    a_re, a_im, log_dt = rows_ref[0:1, :], rows_ref[1:2, :], rows_ref[2:3, :]
    dt = jnp.exp(log_dt)
    lam_re, lam_im = a_re * dt, a_im * dt
    mag = jnp.exp(lam_re)
    ar, ai = mag * jnp.cos(lam_im), mag * jnp.sin(lam_im)
    den = jnp.square(a_re) + jnp.square(a_im)
    cr = ((ar - 1.0) * a_re + ai * a_im) / den
    ci = (ai * a_re - (ar - 1.0) * a_im) / den
    bre, bim = bre_ref[...], bim_ref[...]
    bbr = cr * bre - ci * bim
    bbi = cr * bim + ci * bre
    quarter_turn = jnp.where(lo, 0.0, 0.5 * math.pi)

    def powers(tau):
        return jnp.exp(tau * lam_re) * jnp.cos(tau * lam_im - quarter_turn)

    def outer(y, x1, x2):
        y_sw = pltpu.roll(y, STATE_DIM, 1)
        prod = y[:, None, :] * x1[None, :, :] + y_sw[:, None, :] * x2[None, :, :]
        return prod.reshape(y.shape[0] * x1.shape[0], 2 * STATE_DIM)

    frames = lax.broadcasted_iota(jnp.int32, (CHUNK, 1), 0).astype(F32)
    wt = outer(powers((CHUNK - 1.0) - frames), bbr, jnp.where(lo, -bbi, bbi))
    w = wt.T
    cre, cim = cre_ref[...], cim_ref[...]
    vm = outer(powers(frames + 1.0), jnp.where(lo, cre, -cre), -cim)
    rrev = jnp.dot(jnp.where(lo, cre, -cim), w, precision=lax.Precision.HIGHEST, preferred_element_type=F32)
    a_row = powers(jnp.full((8, 1), float(CHUNK), F32))[0:1, :]
    a_col = jnp.broadcast_to(a_row, (2 * STATE_DIM, 2 * STATE_DIM)).T
    a_swap = pltpu.roll(a_col, STATE_DIM, 0)
    top = lax.broadcasted_iota(jnp.int32, a_col.shape, 0) < STATE_DIM
    return (rrev, w.astype(BF16), vm.astype(BF16),
            jnp.where(top, a_col, a_swap), jnp.where(top, -a_swap, a_col))


def _chunk_cols_kernel(u_ref, o_ref):
    n_s, n_r, n_j = u_ref.shape[0], u_ref.shape[1], u_ref.shape[2]
    pad = LANES - n_s * n_r
    for jj in range(n_j):
        rows = [u_ref[s, :, jj, :] for s in range(n_s)]
        if pad:
            rows.append(jnp.zeros((pad, SSM_WIDTH), F32))
        cols = jnp.concatenate(rows, axis=0).T
        o_ref[:, jj * GROUP_SIZE:(jj + 1) * GROUP_SIZE, :] = cols.reshape(N_GROUPS, GROUP_SIZE, LANES).astype(BF16)


def _chunk_cols(u4, frames_per_step=16):
    s_total, n_r = u4.shape[0], u4.shape[1]
    s_step = min(s_total, LANES // n_r)
    assert s_step >= 1 and s_total % s_step == 0
    n_lane_blocks = s_total // s_step
    return pl.pallas_call(
        _chunk_cols_kernel,
        grid=(n_lane_blocks, CHUNK // frames_per_step),
        in_specs=[pl.BlockSpec((s_step, n_r, frames_per_step, SSM_WIDTH), lambda a, j: (a, 0, j, 0))],
        out_specs=pl.BlockSpec((N_GROUPS, frames_per_step * GROUP_SIZE, LANES), lambda a, j: (0, j, a)),
        out_shape=jax.ShapeDtypeStruct((N_GROUPS, CHUNK * GROUP_SIZE, n_lane_blocks * LANES), BF16),
        compiler_params=_cparams("parallel", "parallel"),
        name="ssm_in",
    )(u4)


def _unchunk_cols_kernel(y_ref, o_ref):
    n_s, n_r, n_j = o_ref.shape[0], o_ref.shape[1], o_ref.shape[2]
    for jj in range(n_j):
        cols = y_ref[:, jj * GROUP_SIZE:(jj + 1) * GROUP_SIZE, :].reshape(SSM_WIDTH, LANES)
        rows = cols.T
        for s in range(n_s):
            o_ref[s, :, jj, :] = rows[s * n_r:(s + 1) * n_r, :]


def _unchunk_cols(y_cols, s_total, n_r, frames_per_step=16):
    s_step = min(s_total, LANES // n_r)
    n_lane_blocks = s_total // s_step
    return pl.pallas_call(
        _unchunk_cols_kernel,
        grid=(n_lane_blocks, CHUNK // frames_per_step),
        in_specs=[pl.BlockSpec((N_GROUPS, frames_per_step * GROUP_SIZE, LANES), lambda a, j: (0, j, a))],
        out_specs=pl.BlockSpec((s_step, n_r, frames_per_step, SSM_WIDTH), lambda a, j: (a, 0, j, 0)),
        out_shape=jax.ShapeDtypeStruct((s_total, n_r, CHUNK, SSM_WIDTH), F32),
        compiler_params=_cparams("parallel", "parallel"),
        name="ssm_out",
    )(y_cols)


def _ssm_kernel(*refs, n_chunk):
    for gi in range(SSM_GROUPS_PER_STEP):
        _ssm_group(*[r.at[gi] for r in refs], n_chunk=n_chunk)


def _ssm_group(rows_ref, bre_ref, bim_ref, cre_ref, cim_ref, xp_ref, xs_ref, s0_ref, yp_ref, ys_ref, sp_ref, ss_ref,
               mt_scr, *, n_chunk):
    width = GROUP_SIZE * CHUNK
    n_piece = width // LANES
    lane16 = lax.broadcasted_iota(jnp.int32, (GROUP_SIZE, LANES), 1)
    rrev, w, vm_all, a_c, a_s = _group_tables(rows_ref, bre_ref, bim_ref, cre_ref, cim_ref)
    pieces = [rrev[:, k * LANES:(k + 1) * LANES] for k in range(n_piece)] + [jnp.zeros((GROUP_SIZE, LANES), F32)]
    rolled = {0: pieces}
    for b in range(GROUP_SIZE, LANES, GROUP_SIZE):
        rolled[b] = [pltpu.roll(p, LANES - b, 1) for p in pieces[:n_piece]] + [pieces[n_piece]]
    def toeplitz_rows(t0, t1, n_cols):
        for t in range(t0, t1):
            shift = GROUP_SIZE * (CHUNK - 1 - t)
            a, b = shift // LANES, shift % LANES
            for v in range(n_cols // LANES):
                k = v + a
                if k >= n_piece:
                    blk = pieces[n_piece]
                elif b == 0:
                    blk = pieces[k]
                else:
                    blk = jnp.where(lane16 < LANES - b, rolled[b][k], rolled[b][k + 1])
                mt_scr[t * GROUP_SIZE:(t + 1) * GROUP_SIZE, v * LANES:(v + 1) * LANES] = blk.astype(BF16)

    xp, xs = xp_ref[...], xs_ref[...]
    sloc_p, sloc_s = _dot(w, xp), _dot(w, xs)

    def cmul(pc, ps, s):
        return pc * s + ps * pltpu.roll(s, STATE_DIM, 0)

    lane = lax.broadcasted_iota(jnp.int32, (2 * STATE_DIM, LANES), 1) % n_chunk
    prev_cols = []
    lane_id = lax.broadcasted_iota(jnp.int32, (2 * STATE_DIM, LANES), 1)
    final = jnp.zeros((2 * STATE_DIM, LANES), F32)
    seqs_per_tile = LANES // n_chunk
    for tile in range(xp.shape[1] // LANES):
        s_inc = sloc_p[:, tile * LANES:(tile + 1) * LANES]
        pc, ps = a_c, a_s
        dist = 1
        while dist < n_chunk:
            shifted = jnp.where(lane >= dist, pltpu.roll(s_inc, dist, 1), 0.0)
            s_inc = s_inc + cmul(pc, ps, shifted)
            pc, ps = pc * pc - ps * ps, 2.0 * pc * ps
            dist *= 2
        for k in range(seqs_per_tile):
            src, dst = (k + 1) * n_chunk - 1, tile * seqs_per_tile + k
            final = jnp.where(lane_id == dst, pltpu.roll(s_inc, (dst - src) % LANES, 1), final)
        prev_cols.append(jnp.where(lane >= 1, pltpu.roll(s_inc, 1, 1), 0.0))
    sp_ref[...] = final
    s_prev_p = jnp.concatenate(prev_cols, axis=1).astype(BF16)
    s0 = s0_ref[...]
    s0b = s0.astype(BF16)
    frames = TOEPLITZ_ROWS // GROUP_SIZE
    for i in range(width // TOEPLITZ_ROWS):
        rows = slice(i * TOEPLITZ_ROWS, (i + 1) * TOEPLITZ_ROWS)
        n_cols = (i + 1) * TOEPLITZ_ROWS
        toeplitz_rows(i * frames, (i + 1) * frames, n_cols)
        mt = mt_scr[rows, 0:n_cols]
        vm = vm_all[rows, :]
        yp_ref[rows, :] = _dot(mt, xp[0:n_cols, :]) + _dot(vm, s_prev_p)
        ys_ref[rows, :] = _dot(mt, xs[0:n_cols, :]) + _dot(vm, s0b)
    ss_ref[...] = cmul(a_c, a_s, s0) + sloc_s


def _ssm(params, x_p, x_s, s0, n_chunk):
    n_groups, lanes_p = x_p.shape[0], x_p.shape[2]
    width = GROUP_SIZE * CHUNK
    per_g = lambda g: (g, 0, 0)
    state_rows = 2 * STATE_DIM
    gb = SSM_GROUPS_PER_STEP
    assert n_groups % gb == 0
    spec = lambda rows, cols: pl.BlockSpec((gb, rows, cols), per_g)
    return pl.pallas_call(
        functools.partial(_ssm_kernel, n_chunk=n_chunk),
        grid=(n_groups // gb,),
        in_specs=[spec(8, state_rows)] + [spec(GROUP_SIZE, state_rows)] * 4 + [
                  spec(width, lanes_p), spec(width, LANES), spec(state_rows, LANES)],
        out_specs=[spec(width, lanes_p), spec(width, LANES), spec(state_rows, LANES), spec(state_rows, LANES)],
        out_shape=[jax.ShapeDtypeStruct((n_groups, width, lanes_p), F32),
                   jax.ShapeDtypeStruct((n_groups, width, LANES), F32),
                   jax.ShapeDtypeStruct((n_groups, state_rows, LANES), F32),
                   jax.ShapeDtypeStruct((n_groups, state_rows, LANES), F32)],
        scratch_shapes=[pltpu.VMEM((gb, width, width), BF16)],
        compiler_params=_cparams("parallel"),
        name="ssm",
    )(*params, x_p, x_s, s0)


def _layer_norm(x, g, b):
    mu = jnp.mean(x, axis=1, keepdims=True)
    xc = x - mu
    var = jnp.mean(jnp.square(xc), axis=1, keepdims=True)
    return xc * lax.rsqrt(var + LN_EPS) * g + b


def _gelu_tanh(x):
    return 0.5 * x * (1.0 + jnp.tanh(math.sqrt(2.0 / math.pi) * (x + 0.044715 * (x * x * x))))


def _merge_kernel(x_ref, ao_ref, ys_ref, u_ref, d_ref, wg_ref, wap_ref, wglu_ref, wout_ref, g1_ref, b1_ref, h_ref):
    def sub_tile(r):
        rows = slice(r * SUB_ROWS, (r + 1) * SUB_ROWS)
        x = x_ref[rows, :]
        xb = x.astype(BF16)
        a_branch = _dot(ao_ref[rows, :], wap_ref[...])
        s_act = _gelu_tanh(ys_ref[rows, :] + d_ref[...] * u_ref[rows, :]).astype(BF16)
        s_branch = (_dot(s_act, wglu_ref[:, 0:D_MODEL])
                    * jax.nn.sigmoid(_dot(s_act, wglu_ref[:, D_MODEL:2 * D_MODEL])))
        m = (jax.nn.sigmoid(_dot(xb, wg_ref[:, 0:D_MODEL])) * a_branch
             + jax.nn.sigmoid(_dot(xb, wg_ref[:, D_MODEL:2 * D_MODEL])) * s_branch)
        h_ref[rows, :] = _layer_norm(DEEPNORM_ALPHA * x + _dot(m.astype(BF16), wout_ref[...]),
                                     g1_ref[...], b1_ref[...])

    for r in range(x_ref.shape[0] // SUB_ROWS):
        sub_tile(r)


def _merge(x2d, ao, ys, u, d, w_gate, w_ap, w_glu, w_out, ln_g, ln_b, tm):
    t_tokens = x2d.shape[0]
    row = lambda i: (i, 0)
    const = lambda i: (0, 0)
    return pl.pallas_call(
        _merge_kernel,
        grid=(t_tokens // tm,),
        in_specs=[pl.BlockSpec((tm, D_MODEL), row),
                  pl.BlockSpec((tm, ATTN_WIDTH), row),
                  pl.BlockSpec((tm, SSM_WIDTH), row),
                  pl.BlockSpec((tm, SSM_WIDTH), row),
                  pl.BlockSpec((1, SSM_WIDTH), const),
                  pl.BlockSpec((D_MODEL, 2 * D_MODEL), const, pipeline_mode=pl.Buffered(1)),
                  pl.BlockSpec((ATTN_WIDTH, D_MODEL), const, pipeline_mode=pl.Buffered(1)),
                  pl.BlockSpec((SSM_WIDTH, 2 * D_MODEL), const, pipeline_mode=pl.Buffered(1)),
                  pl.BlockSpec((D_MODEL, D_MODEL), const, pipeline_mode=pl.Buffered(1)),
                  pl.BlockSpec((1, D_MODEL), const),
                  pl.BlockSpec((1, D_MODEL), const)],
        out_specs=pl.BlockSpec((tm, D_MODEL), row),
        out_shape=jax.ShapeDtypeStruct((t_tokens, D_MODEL), F32),
        compiler_params=_cparams("parallel"),
        name="merge",
    )(x2d, ao, ys, u, d, w_gate, w_ap, w_glu, w_out, ln_g, ln_b)


def _mlp_kernel(h_ref, w1_ref, w2_ref, g2_ref, b2_ref, o_ref, *, ff_chunk):
    for r in range(h_ref.shape[0] // SUB_ROWS):
        rows = slice(r * SUB_ROWS, (r + 1) * SUB_ROWS)
        h = h_ref[rows, :]
        hb = h.astype(BF16)
        f = jnp.zeros(h.shape, F32)
        for c in range(D_FF // ff_chunk):
            sl = slice(c * ff_chunk, (c + 1) * ff_chunk)
            t = jnp.maximum(_dot(hb, w1_ref[:, sl]), 0.0)
            f = f + _dot((t * t).astype(BF16), w2_ref[sl, :])
        o_ref[rows, :] = _layer_norm(DEEPNORM_ALPHA * h + f, g2_ref[...], b2_ref[...])


def _mlp(h, w1, w2, ln_g, ln_b, tm, ff_chunk=1024):
    t_tokens = h.shape[0]
    row = lambda i: (i, 0)
    const = lambda i: (0, 0)
    return pl.pallas_call(
        functools.partial(_mlp_kernel, ff_chunk=ff_chunk),
        grid=(t_tokens // tm,),
        in_specs=[pl.BlockSpec((tm, D_MODEL), row),
                  pl.BlockSpec((D_MODEL, D_FF), const, pipeline_mode=pl.Buffered(1)),
                  pl.BlockSpec((D_FF, D_MODEL), const, pipeline_mode=pl.Buffered(1)),
                  pl.BlockSpec((1, D_MODEL), const),
                  pl.BlockSpec((1, D_MODEL), const)],
        out_specs=pl.BlockSpec((tm, D_MODEL), row),
        out_shape=jax.ShapeDtypeStruct((t_tokens, D_MODEL), F32),
        compiler_params=_cparams("parallel"),
        name="mlp",
    )(h, w1, w2, ln_g, ln_b)


def _rope_tables(pos):
    inv = 1.0 / (ROPE_THETA ** (jnp.arange(0, HEAD_DIM, 2, dtype=F32) / HEAD_DIM))
    ang = pos.astype(F32)[:, None] * inv[None, :]
    c, s = jnp.cos(ang), jnp.sin(ang)
    reps = LANES // HEAD_DIM
    return jnp.tile(jnp.concatenate([c, c], axis=1), (1, reps)), jnp.tile(jnp.concatenate([-s, s], axis=1), (1, reps))


def kernel(x_prompt, x_sample, cache_k, cache_v, state_ssm_re, state_ssm_im, w_in, lambda_q1, lambda_k1, lambda_q2, lambda_k2, subln_gain, ssm_a_re, ssm_a_im, ssm_b_re, ssm_b_im, ssm_c_re, ssm_c_im, ssm_d, ssm_log_dt, w_attn_proj, w_glu_a, w_glu_b, w_out, ln1_g, ln1_b, w_ff1, w_ff2, ln2_g, ln2_b):
    bp, n_p = x_prompt.shape[0], x_prompt.shape[1]
    bs, n_s = x_sample.shape[0], x_sample.shape[1]
    past = cache_k.shape[2]
    assert w_in.shape[0] == DEPTH and n_s == CHUNK and n_p % CHUNK == 0
    n_chunk = n_p // CHUNK
    tm = min(512, n_p)
    tm_s = min(512, bs * n_s)
    tq = min(512, n_p)
    l = 0
    lam_init = 0.8 - 0.6 * math.exp(-0.3 * l)

    xp = x_prompt.reshape(bp * n_p, D_MODEL)
    xs = x_sample.reshape(bs * n_s, D_MODEL)
    w_qkvu = w_in[l, :, 0:QKVU_COLS].astype(BF16)
    w_gate = w_in[l, :, QKVU_COLS:].astype(BF16)
    w_ap = w_attn_proj[l].astype(BF16)
    w_glu = jnp.concatenate([w_glu_a[l], w_glu_b[l]], axis=1).astype(BF16)
    w_o = w_out[l].astype(BF16)
    w1, w2 = w_ff1[l].astype(BF16), w_ff2[l].astype(BF16)
    lams = [v[l].reshape(1, HEAD_DIM) for v in (lambda_q1, lambda_k1, lambda_q2, lambda_k2)]
    gain = subln_gain[l].reshape(1, V_DIM)
    d_skip = ssm_d[l].reshape(1, SSM_WIDTH)
    lng = [v[l].reshape(1, D_MODEL) for v in (ln1_g, ln1_b, ln2_g, ln2_b)]

    cos_p, sin_p = _rope_tables(jnp.arange(n_p))
    cos_s, sin_s = _rope_tables(jnp.tile(past + jnp.arange(n_s), tm_s // n_s))

    q_p, k_p, v_p, u_p, kb_p, vt_p = _project(xp, w_qkvu, cos_p, sin_p, n_p, min(2 * tm, n_p), True)
    q_s, k_s, v_s, u_s = _project(xs, w_qkvu, cos_s, sin_s, n_s, tm_s, False)

    ao_p = _prompt_attention(lams, gain, q_p, kb_p, vt_p, bp, n_p, tq, lam_init)
    ao_s = _sample_attention(lams, gain, q_s, k_s, v_s,
                             cache_k[l].reshape(bs, past * N_HEADS, V_DIM),
                             cache_v[l].reshape(bs, past * N_HEADS, V_DIM), lam_init)

    tables = _ssm_params(ssm_a_re[l], ssm_a_im[l], ssm_b_re[l], ssm_b_im[l],
                         ssm_c_re[l], ssm_c_im[l], ssm_log_dt[l])
    x_cols_p = _chunk_cols(u_p.reshape(bp, n_chunk, CHUNK, SSM_WIDTH))
    x_cols_s = _chunk_cols(u_s.reshape(1, bs, CHUNK, SSM_WIDTH))
    s0 = jnp.concatenate([state_ssm_re[l], state_ssm_im[l]], axis=-1).transpose(1, 2, 0)
    s0 = jnp.pad(s0, ((0, 0), (0, 0), (0, LANES - bs)))
    y_cols_p, y_cols_s, st_p, st_s = _ssm(tables, x_cols_p, x_cols_s, s0, n_chunk)
    ys_p = _unchunk_cols(y_cols_p, bp, n_chunk).reshape(bp * n_p, SSM_WIDTH)
    ys_s = _unchunk_cols(y_cols_s, 1, bs).reshape(bs * n_s, SSM_WIDTH)
    sf_p = st_p[:, :, 0:bp]
    sf_s = st_s[:, :, 0:bs]

    outs = []
    for x2d, ao, ys, u, tile in ((xp, ao_p, ys_p, u_p, tm), (xs, ao_s, ys_s, u_s, tm_s)):
        big = 2 * tile if x2d.shape[0] >= 8 * tile else tile
        h = _merge(x2d, ao, ys, u, d_skip, w_gate, w_ap, w_glu, w_o, lng[0], lng[1], big)
        outs.append(_mlp(h, w1, w2, lng[2], lng[3], big))

    def states(sf):
        t = sf.transpose(2, 0, 1)
        return t[None, :, :, 0:STATE_DIM], t[None, :, :, STATE_DIM:]

    srp, sip = states(sf_p)
    srs, sis = states(sf_s)
    return (outs[0].reshape(bp, n_p, D_MODEL), outs[1].reshape(bs, n_s, D_MODEL),
            k_p.reshape(1, bp, n_p, N_HEADS, V_DIM), v_p.reshape(1, bp, n_p, N_HEADS, V_DIM), srp, sip,
            k_s.reshape(1, bs, n_s, N_HEADS, V_DIM), v_s.reshape(1, bs, n_s, N_HEADS, V_DIM), srs, sis)
```

```python
import functools
import math

import jax
import jax.numpy as jnp
from jax import lax
from jax.experimental import pallas as pl
from jax.experimental.pallas import tpu as pltpu

D_MODEL = 1024
CHUNK = 64
N_HEADS = 4
HEAD_DIM = 64
V_DIM = 2 * HEAD_DIM
ATTN_WIDTH = N_HEADS * V_DIM
SSM_WIDTH = 512
GROUP_SIZE = 16
N_GROUPS = SSM_WIDTH // GROUP_SIZE
STATE_DIM = 64
D_FF = 4 * D_MODEL
ROPE_THETA = 10000.0
LN_EPS = 1e-5
RMS_EPS = 1e-5
NEG_INF = -1e30
DEPTH = 1
DEEPNORM_ALPHA = (2.0 * DEPTH) ** 0.25
QKVU_COLS = 3 * ATTN_WIDTH + SSM_WIDTH
LOG2E = 1.4426950408889634

LANES = 128
VT_ROWS = V_DIM + 16
QUERY_LANES = 256
TOEPLITZ_ROWS = 256
SUB_ROWS = 256
ATTN_ITEMS_PER_REGION = 12
SSM_GROUPS_PER_STEP = 4
SAMPLE_STREAMS_PER_STEP = 4
VMEM_LIMIT = 56 * 1024 * 1024

F32 = jnp.float32
BF16 = jnp.bfloat16


def _cparams(*sem):
    return pltpu.CompilerParams(dimension_semantics=sem, vmem_limit_bytes=VMEM_LIMIT)


def _nt_dot(a, b):
    return lax.dot_general(a, b, (((1,), (1,)), ((), ())), preferred_element_type=F32)


def _dot(a, b):
    return jnp.dot(a, b, preferred_element_type=F32)


def _in_regions(n_steps, steps_per_region, step):
    once = jnp.minimum(pl.program_id(0) + 1, 1)
    for first in range(0, n_steps, steps_per_region):
        def region(_, carry, first=first):
            for i in range(first, min(first + steps_per_region, n_steps)):
                step(i)
            return carry
        lax.fori_loop(0, once, region, 0)


def _rotary(z, cos, sin_signed, first_half):
    swapped = jnp.where(first_half, pltpu.roll(z, 96, 1), pltpu.roll(z, 32, 1))
    return z * cos + swapped * sin_signed


def _proj_kernel(x_ref, w_ref, cos_ref, sin_ref, q_ref, k_ref, v_ref, u_ref, *rest, emit_t):
    xb = x_ref[...].astype(BF16)
    cos = cos_ref[...]
    sin = sin_ref[...]
    lane = lax.broadcasted_iota(jnp.int32, cos.shape, 1)
    first_half = (lane % HEAD_DIM) < (HEAD_DIM // 2)
    tm = xb.shape[0]
    zq = _dot(xb, w_ref[:, 0:ATTN_WIDTH])
    zk = _dot(xb, w_ref[:, ATTN_WIDTH:2 * ATTN_WIDTH])
    zv = _dot(xb, w_ref[:, 2 * ATTN_WIDTH:3 * ATTN_WIDTH])
    for h in range(N_HEADS):
        sl = slice(h * V_DIM, (h + 1) * V_DIM)
        q_ref[:, sl] = (_rotary(zq[:, sl], cos, sin, first_half) * (LOG2E * HEAD_DIM ** -0.5)).astype(BF16)
        kr = _rotary(zk[:, sl], cos, sin, first_half)
        k_ref[pl.ds(h, tm, stride=N_HEADS), :] = kr
        v_ref[pl.ds(h, tm, stride=N_HEADS), :] = zv[:, sl]
        if emit_t:
            rest[0][:, sl] = kr.astype(BF16)
    if emit_t:
        vt_ref = rest[1]
        zvt = zv.T.astype(BF16)
        ones = jnp.ones((VT_ROWS - V_DIM, zvt.shape[1]), BF16)
        for h in range(N_HEADS):
            vt_ref[h, 0:V_DIM, :] = zvt[h * V_DIM:(h + 1) * V_DIM, :]
            vt_ref[h, V_DIM:VT_ROWS, :] = ones
    u_ref[...] = _dot(xb, w_ref[:, 3 * ATTN_WIDTH:QKVU_COLS])


def _project(x2d, w_qkvu, cos_t, sin_t, seq_len, tm, emit_t):
    t_tokens = x2d.shape[0]
    n_tiles = t_tokens // tm
    n_pos_tiles = cos_t.shape[0] // tm
    tiles_per_seq = max(seq_len // tm, 1)
    row = lambda i: (i, 0)
    pos = lambda i: (i % n_pos_tiles, 0)
    out_shape = [jax.ShapeDtypeStruct((t_tokens, ATTN_WIDTH), BF16),
                 jax.ShapeDtypeStruct((t_tokens * N_HEADS, V_DIM), F32),
                 jax.ShapeDtypeStruct((t_tokens * N_HEADS, V_DIM), F32),
                 jax.ShapeDtypeStruct((t_tokens, SSM_WIDTH), F32)]
    out_specs = ([pl.BlockSpec((tm, ATTN_WIDTH), row)] + [pl.BlockSpec((tm * N_HEADS, V_DIM), row)] * 2
                 + [pl.BlockSpec((tm, SSM_WIDTH), row)])
    if emit_t:
        bsz = t_tokens // seq_len
        out_shape += [jax.ShapeDtypeStruct((t_tokens, ATTN_WIDTH), BF16),
                      jax.ShapeDtypeStruct((bsz, N_HEADS, VT_ROWS, seq_len), BF16)]
        out_specs += [pl.BlockSpec((tm, ATTN_WIDTH), row),
                      pl.BlockSpec((None, N_HEADS, VT_ROWS, tm),
                                   lambda i: (i // tiles_per_seq, 0, 0, i % tiles_per_seq))]
    return pl.pallas_call(
        functools.partial(_proj_kernel, emit_t=emit_t),
        grid=(n_tiles,),
        in_specs=[pl.BlockSpec((tm, D_MODEL), row),
                  pl.BlockSpec((D_MODEL, QKVU_COLS), lambda i: (0, 0), pipeline_mode=pl.Buffered(1)),
                  pl.BlockSpec((tm, LANES), pos),
                  pl.BlockSpec((tm, LANES), pos)],
        out_specs=out_specs,
        out_shape=out_shape,
        compiler_params=_cparams("parallel"),
        name="proj_t" if emit_t else "proj",
    )(x2d, w_qkvu, cos_t, sin_t)


def _diff_lambda(lq1, lk1, lq2, lk2, lam_init):
    return (jnp.exp(jnp.sum(lq1 * lk1, axis=1, keepdims=True))
            - jnp.exp(jnp.sum(lq2 * lk2, axis=1, keepdims=True)) + lam_init)


def _sub_norm(d, gain, lam_init):
    ms = jnp.mean(jnp.square(d), axis=1, keepdims=True)
    return d * lax.rsqrt(ms + RMS_EPS) * gain * (1.0 - lam_init)


def _stack_maps(q):
    lane = lax.broadcasted_iota(jnp.int32, q.shape, 1)
    zero = jnp.zeros_like(q)
    return jnp.concatenate([jnp.where(lane < HEAD_DIM, q, zero), jnp.where(lane >= HEAD_DIM, q, zero)], axis=0)


def _prompt_attn_kernel(lq1_ref, lk1_ref, lq2_ref, lk2_ref, gain_ref, q_ref, k_ref, vt_ref,
                        o_ref, m_scr, acc_scr, qx_scr, s0_scr, s1_scr, mx0_scr, mx1_scr, *, tq, lam_init):
    nq = q_ref.shape[0] // tq
    items = [(qi, j) for qi in range(nq) for j in range(qi + 1)]
    for i in range(nq):
        qx_scr[i * 2 * tq:(i + 1) * 2 * tq, :] = _stack_maps(q_ref[i * tq:(i + 1) * tq, :])
    chains = [slice(c * QUERY_LANES, (c + 1) * QUERY_LANES) for c in range(2 * tq // QUERY_LANES)]

    def visible_keys(cs, diagonal):
        return min(tq, cs.start % tq + QUERY_LANES) if diagonal else tq

    def scores(qi, j, s_scr, mx_scr):
        kt = k_ref[j * tq:(j + 1) * tq, :]
        for cs in chains:
            qx = qx_scr[qi * 2 * tq + cs.start:qi * 2 * tq + cs.stop, :]
            st = _nt_dot(kt, qx)
            s_scr[:, cs] = st
            mx_scr[:, cs] = jnp.max(st, axis=0, keepdims=True)

    def softmax_pv(j, s_scr, mx_scr, diagonal):
        start = j * tq
        for cs in chains:
            n_keys = visible_keys(cs, diagonal)
            if diagonal:
                lane = lax.broadcasted_iota(jnp.int32, (CHUNK, QUERY_LANES), 1)
                blocks = []
                for kc in range(n_keys // CHUNK):
                    blk = s_scr[kc * CHUNK:(kc + 1) * CHUNK, cs]
                    first_visible = kc * CHUNK - cs.start % tq
                    if first_visible > 0:
                        blk = jnp.where(lane >= first_visible, blk, NEG_INF)
                    blocks.append(blk)
                st = jnp.concatenate(blocks, axis=0)
                tile_max = jnp.max(st, axis=0, keepdims=True)
            else:
                st = s_scr[:, cs]
                tile_max = mx_scr[:, cs]
            vt = vt_ref[:, start:start + n_keys]
            p_scale = None
            if j == 0:
                m_new = tile_max
            else:
                m_old = m_scr[:, cs]
                m_new = jnp.maximum(m_old, tile_max)
                p_scale = jnp.exp2(m_old - m_new)
            p = jnp.exp2(st - m_new).astype(BF16)
            pv = _dot(vt, p)
            acc_scr[:, cs] = pv if p_scale is None else acc_scr[:, cs] * p_scale + pv
            m_scr[:, cs] = m_new

    def finish(qi):
        acc = acc_scr[...]
        o = acc[0:V_DIM, :] * (1.0 / acc[V_DIM:V_DIM + 1, :])
        lam = _diff_lambda(lq1_ref[...], lk1_ref[...], lq2_ref[...], lk2_ref[...], lam_init)
        d = (o[:, 0:tq] - lam * o[:, tq:2 * tq]).T
        o_ref[qi * tq:(qi + 1) * tq, :] = _sub_norm(d, gain_ref[...], lam_init).astype(BF16)

    bufs = ((s0_scr, mx0_scr), (s1_scr, mx1_scr))
    scores(*items[0], *bufs[0])

    def run_item(w):
        qi, j = items[w]
        if w + 1 < len(items):
            scores(*items[w + 1], *bufs[(w + 1) % 2])
        softmax_pv(j, *bufs[w % 2], j == qi)
        if j == qi:
            finish(qi)

    _in_regions(len(items), ATTN_ITEMS_PER_REGION, run_item)


def _prompt_attention(lams, gain, q, kb, vt, bsz, seq_len, tq, lam_init):
    nq = seq_len // tq
    small = lambda b, h: (0, 0)
    return pl.pallas_call(
        functools.partial(_prompt_attn_kernel, tq=tq, lam_init=lam_init),
        grid=(bsz, N_HEADS),
        in_specs=[pl.BlockSpec((1, HEAD_DIM), small)] * 4 + [
            pl.BlockSpec((1, V_DIM), small),
            pl.BlockSpec((seq_len, V_DIM), lambda b, h: (b, h)),
            pl.BlockSpec((seq_len, V_DIM), lambda b, h: (b, h)),
            pl.BlockSpec((None, None, VT_ROWS, seq_len), lambda b, h: (b, h, 0, 0))],
        out_specs=pl.BlockSpec((seq_len, V_DIM), lambda b, h: (b, h)),
        out_shape=jax.ShapeDtypeStruct((bsz * seq_len, ATTN_WIDTH), BF16),
        scratch_shapes=[pltpu.VMEM((1, 2 * tq), F32), pltpu.VMEM((VT_ROWS, 2 * tq), F32),
                        pltpu.VMEM((nq * 2 * tq, V_DIM), BF16),
                        pltpu.VMEM((tq, 2 * tq), F32), pltpu.VMEM((tq, 2 * tq), F32),
                        pltpu.VMEM((1, 2 * tq), F32), pltpu.VMEM((1, 2 * tq), F32)],
        compiler_params=_cparams("parallel", "parallel"),
        name="prompt_attn",
    )(*lams, gain, q, kb, vt)


def _sample_attn_kernel(lq1_ref, lk1_ref, lq2_ref, lk2_ref, gain_ref, q_ref, k_ref, v_ref, ck_ref, cv_ref,
                        o_ref, *, lam_init):
    n_streams = ck_ref.shape[0]
    n_new = q_ref.shape[0] // n_streams
    past = ck_ref.shape[1] // N_HEADS
    lam = _diff_lambda(lq1_ref[...], lk1_ref[...], lq2_ref[...], lk2_ref[...], lam_init)
    gain = gain_ref[...]
    for s in range(n_streams):
        rows = slice(s * n_new, (s + 1) * n_new)
        for h in range(N_HEADS):
            sl = slice(h * V_DIM, (h + 1) * V_DIM)
            old = pl.ds(h, past, stride=N_HEADS)
            new = pl.ds(s * n_new * N_HEADS + h, n_new, stride=N_HEADS)
            qx = _stack_maps(q_ref[rows, sl])
            s_c = _nt_dot(qx, ck_ref[s, old, :].astype(BF16))
            s_n = _nt_dot(qx, k_ref[new, :].astype(BF16))
            m = jnp.maximum(jnp.max(s_c, axis=1, keepdims=True), jnp.max(s_n, axis=1, keepdims=True))
            p_c = jnp.exp2(s_c - m).astype(BF16)
            p_n = jnp.exp2(s_n - m).astype(BF16)
            vc = jnp.concatenate([cv_ref[s, old, :].astype(BF16), jnp.ones((past, V_DIM), BF16)], axis=1)
            vn = jnp.concatenate([v_ref[new, :].astype(BF16), jnp.ones((n_new, V_DIM), BF16)], axis=1)
            o_ext = _dot(p_c, vc) + _dot(p_n, vn)
            o = o_ext[:, 0:V_DIM] * (1.0 / o_ext[:, V_DIM:2 * V_DIM])
            d = o[0:n_new, :] - lam * o[n_new:2 * n_new, :]
            o_ref[rows, sl] = _sub_norm(d, gain, lam_init).astype(BF16)


def _sample_attention(lams, gain, q, k, v, cache_k, cache_v, lam_init):
    bsz, past_rows = cache_k.shape[0], cache_k.shape[1]
    n_new = q.shape[0] // bsz
    ns = SAMPLE_STREAMS_PER_STEP if bsz % SAMPLE_STREAMS_PER_STEP == 0 else 1
    small = lambda b: (0, 0)
    row = lambda b: (b, 0)
    return pl.pallas_call(
        functools.partial(_sample_attn_kernel, lam_init=lam_init),
        grid=(bsz // ns,),
        in_specs=[pl.BlockSpec((1, HEAD_DIM), small)] * 4 + [
            pl.BlockSpec((1, V_DIM), small),
            pl.BlockSpec((ns * n_new, ATTN_WIDTH), row),
            pl.BlockSpec((ns * n_new * N_HEADS, V_DIM), row),
            pl.BlockSpec((ns * n_new * N_HEADS, V_DIM), row),
            pl.BlockSpec((ns, past_rows, V_DIM), lambda b: (b, 0, 0)),
            pl.BlockSpec((ns, past_rows, V_DIM), lambda b: (b, 0, 0))],
        out_specs=pl.BlockSpec((ns * n_new, ATTN_WIDTH), row),
        out_shape=jax.ShapeDtypeStruct(q.shape, BF16),
        compiler_params=_cparams("parallel"),
        name="sample_attn",
    )(*lams, gain, q, k, v, cache_k, cache_v)


def _ssm_params(a_re, a_im, b_re, b_im, c_re, c_im, log_dt):
    g = a_re.shape[0]
    twice = lambda v: jnp.concatenate([v, v], axis=-1)
    rows = jnp.stack([twice(a_re), twice(a_im), jnp.broadcast_to(log_dt[:, None], (g, 2 * STATE_DIM))], axis=1)
    rows = jnp.pad(rows, ((0, 0), (0, 8 - rows.shape[1]), (0, 0)))
    return rows, twice(b_re.transpose(0, 2, 1)), twice(b_im.transpose(0, 2, 1)), twice(c_re), twice(c_im)


def _group_tables(rows_ref, bre_ref, bim_ref, cre_ref, cim_ref):
    lane = lax.broadcasted_iota(jnp.int32, (1, 2 * STATE_DIM), 1)
    lo = lane < STATE_DIM
    a_re, a_im, log_dt = rows_ref[0:1, :], rows_ref[1:2, :], rows_ref[2:3, :]
    dt = jnp.exp(log_dt)
    lam_re, lam_im = a_re * dt, a_im * dt
    mag = jnp.exp(lam_re)
    ar, ai = mag * jnp.cos(lam_im), mag * jnp.sin(lam_im)
    den = jnp.square(a_re) + jnp.square(a_im)
    cr = ((ar - 1.0) * a_re + ai * a_im) / den
    ci = (ai * a_re - (ar - 1.0) * a_im) / den
    bre, bim = bre_ref[...], bim_ref[...]
    bbr = cr * bre - ci * bim
    bbi = cr * bim + ci * bre
    quarter_turn = jnp.where(lo, 0.0, 0.5 * math.pi)

    def powers(tau):
        return jnp.exp(tau * lam_re) * jnp.cos(tau * lam_im - quarter_turn)

    def outer(y, x1, x2):
        y_sw = pltpu.roll(y, STATE_DIM, 1)
        prod = y[:, None, :] * x1[None, :, :] + y_sw[:, None, :] * x2[None, :, :]
        return prod.reshape(y.shape[0] * x1.shape[0], 2 * STATE_DIM)

    frames = lax.broadcasted_iota(jnp.int32, (CHUNK, 1), 0).astype(F32)
    wt = outer(powers((CHUNK - 1.0) - frames), bbr, jnp.where(lo, -bbi, bbi))
    w = wt.T
    cre, cim = cre_ref[...], cim_ref[...]
    vm = outer(powers(frames + 1.0), jnp.where(lo, cre, -cre), -cim)
    rrev = jnp.dot(jnp.where(lo, cre, -cim), w, precision=lax.Precision.HIGHEST, preferred_element_type=F32)
    a_row = powers(jnp.full((8, 1), float(CHUNK), F32))[0:1, :]
    a_col = jnp.broadcast_to(a_row, (2 * STATE_DIM, 2 * STATE_DIM)).T
    a_swap = pltpu.roll(a_col, STATE_DIM, 0)
    top = lax.broadcasted_iota(jnp.int32, a_col.shape, 0) < STATE_DIM
    return (rrev, w.astype(BF16), vm.astype(BF16),
            jnp.where(top, a_col, a_swap), jnp.where(top, -a_swap, a_col))


def _chunk_cols_kernel(u_ref, o_ref):
    n_s, n_r, n_j = u_ref.shape[0], u_ref.shape[1], u_ref.shape[2]
    pad = LANES - n_s * n_r
    for jj in range(n_j):
        rows = [u_ref[s, :, jj, :] for s in range(n_s)]
        if pad:
            rows.append(jnp.zeros((pad, SSM_WIDTH), F32))
        cols = jnp.concatenate(rows, axis=0).T
        o_ref[:, jj * GROUP_SIZE:(jj + 1) * GROUP_SIZE, :] = cols.reshape(N_GROUPS, GROUP_SIZE, LANES).astype(BF16)


def _chunk_cols(u4, frames_per_step=16):
    s_total, n_r = u4.shape[0], u4.shape[1]
    s_step = min(s_total, LANES // n_r)
    assert s_step >= 1 and s_total % s_step == 0
    n_lane_blocks = s_total // s_step
    return pl.pallas_call(
        _chunk_cols_kernel,
        grid=(n_lane_blocks, CHUNK // frames_per_step),
        in_specs=[pl.BlockSpec((s_step, n_r, frames_per_step, SSM_WIDTH), lambda a, j: (a, 0, j, 0))],
        out_specs=pl.BlockSpec((N_GROUPS, frames_per_step * GROUP_SIZE, LANES), lambda a, j: (0, j, a)),
        out_shape=jax.ShapeDtypeStruct((N_GROUPS, CHUNK * GROUP_SIZE, n_lane_blocks * LANES), BF16),
        compiler_params=_cparams("parallel", "parallel"),
        name="ssm_in",
    )(u4)


def _unchunk_cols_kernel(y_ref, o_ref):
    n_s, n_r, n_j = o_ref.shape[0], o_ref.shape[1], o_ref.shape[2]
    for jj in range(n_j):
        cols = y_ref[:, jj * GROUP_SIZE:(jj + 1) * GROUP_SIZE, :].astype(F32).reshape(SSM_WIDTH, LANES)
        rows = cols.T
        for s in range(n_s):
            o_ref[s, :, jj, :] = rows[s * n_r:(s + 1) * n_r, :]


def _unchunk_cols(y_cols, s_total, n_r, frames_per_step=16):
    s_step = min(s_total, LANES // n_r)
    n_lane_blocks = s_total // s_step
    return pl.pallas_call(
        _unchunk_cols_kernel,
        grid=(n_lane_blocks, CHUNK // frames_per_step),
        in_specs=[pl.BlockSpec((N_GROUPS, frames_per_step * GROUP_SIZE, LANES), lambda a, j: (0, j, a))],
        out_specs=pl.BlockSpec((s_step, n_r, frames_per_step, SSM_WIDTH), lambda a, j: (a, 0, j, 0)),
        out_shape=jax.ShapeDtypeStruct((s_total, n_r, CHUNK, SSM_WIDTH), F32),
        compiler_params=_cparams("parallel", "parallel"),
        name="ssm_out",
    )(y_cols)


def _ssm_kernel(*refs, n_chunk):
    for gi in range(SSM_GROUPS_PER_STEP):
        _ssm_group(*[r.at[gi] for r in refs], n_chunk=n_chunk)


def _ssm_group(rows_ref, bre_ref, bim_ref, cre_ref, cim_ref, xp_ref, xs_ref, s0_ref, yp_ref, ys_ref, sp_ref, ss_ref,
               mt_scr, *, n_chunk):
    width = GROUP_SIZE * CHUNK
    n_piece = width // LANES
    lane16 = lax.broadcasted_iota(jnp.int32, (GROUP_SIZE, LANES), 1)
    rrev, w, vm_all, a_c, a_s = _group_tables(rows_ref, bre_ref, bim_ref, cre_ref, cim_ref)
    pieces = [rrev[:, k * LANES:(k + 1) * LANES] for k in range(n_piece)] + [jnp.zeros((GROUP_SIZE, LANES), F32)]
    rolled = {0: pieces}
    for b in range(GROUP_SIZE, LANES, GROUP_SIZE):
        rolled[b] = [pltpu.roll(p, LANES - b, 1) for p in pieces[:n_piece]] + [pieces[n_piece]]
    def toeplitz_rows(t0, t1, n_cols):
        for t in range(t0, t1):
            shift = GROUP_SIZE * (CHUNK - 1 - t)
            a, b = shift // LANES, shift % LANES
            for v in range(n_cols // LANES):
                k = v + a
                if k >= n_piece:
                    blk = pieces[n_piece]
                elif b == 0:
                    blk = pieces[k]
                else:
                    blk = jnp.where(lane16 < LANES - b, rolled[b][k], rolled[b][k + 1])
                mt_scr[t * GROUP_SIZE:(t + 1) * GROUP_SIZE, v * LANES:(v + 1) * LANES] = blk.astype(BF16)

    xp, xs = xp_ref[...], xs_ref[...]
    sloc_p, sloc_s = _dot(w, xp), _dot(w, xs)

    def cmul(pc, ps, s):
        return pc * s + ps * pltpu.roll(s, STATE_DIM, 0)

    lane = lax.broadcasted_iota(jnp.int32, (2 * STATE_DIM, LANES), 1) % n_chunk
    prev_cols = []
    lane_id = lax.broadcasted_iota(jnp.int32, (2 * STATE_DIM, LANES), 1)
    final = jnp.zeros((2 * STATE_DIM, LANES), F32)
    seqs_per_tile = LANES // n_chunk
    for tile in range(xp.shape[1] // LANES):
        s_inc = sloc_p[:, tile * LANES:(tile + 1) * LANES]
        pc, ps = a_c, a_s
        dist = 1
        while dist < n_chunk:
            shifted = jnp.where(lane >= dist, pltpu.roll(s_inc, dist, 1), 0.0)
            s_inc = s_inc + cmul(pc, ps, shifted)
            pc, ps = pc * pc - ps * ps, 2.0 * pc * ps
            dist *= 2
        for k in range(seqs_per_tile):
            src, dst = (k + 1) * n_chunk - 1, tile * seqs_per_tile + k
            final = jnp.where(lane_id == dst, pltpu.roll(s_inc, (dst - src) % LANES, 1), final)
        prev_cols.append(jnp.where(lane >= 1, pltpu.roll(s_inc, 1, 1), 0.0))
    sp_ref[...] = final
    s_prev_p = jnp.concatenate(prev_cols, axis=1).astype(BF16)
    s0 = s0_ref[...]
    s0b = s0.astype(BF16)
    frames = TOEPLITZ_ROWS // GROUP_SIZE
    for i in range(width // TOEPLITZ_ROWS):
        rows = slice(i * TOEPLITZ_ROWS, (i + 1) * TOEPLITZ_ROWS)
        n_cols = (i + 1) * TOEPLITZ_ROWS
        toeplitz_rows(i * frames, (i + 1) * frames, n_cols)
        mt = mt_scr[rows, 0:n_cols]
        vm = vm_all[rows, :]
        yp_ref[rows, :] = (_dot(mt, xp[0:n_cols, :]) + _dot(vm, s_prev_p)).astype(BF16)
        ys_ref[rows, :] = (_dot(mt, xs[0:n_cols, :]) + _dot(vm, s0b)).astype(BF16)
    ss_ref[...] = cmul(a_c, a_s, s0) + sloc_s


def _ssm(params, x_p, x_s, s0, n_chunk):
    n_groups, lanes_p = x_p.shape[0], x_p.shape[2]
    width = GROUP_SIZE * CHUNK
    per_g = lambda g: (g, 0, 0)
    state_rows = 2 * STATE_DIM
    gb = SSM_GROUPS_PER_STEP
    assert n_groups % gb == 0
    spec = lambda rows, cols: pl.BlockSpec((gb, rows, cols), per_g)
    return pl.pallas_call(
        functools.partial(_ssm_kernel, n_chunk=n_chunk),
        grid=(n_groups // gb,),
        in_specs=[spec(8, state_rows)] + [spec(GROUP_SIZE, state_rows)] * 4 + [
                  spec(width, lanes_p), spec(width, LANES), spec(state_rows, LANES)],
        out_specs=[spec(width, lanes_p), spec(width, LANES), spec(state_rows, LANES), spec(state_rows, LANES)],
        out_shape=[jax.ShapeDtypeStruct((n_groups, width, lanes_p), BF16),
                   jax.ShapeDtypeStruct((n_groups, width, LANES), BF16),
                   jax.ShapeDtypeStruct((n_groups, state_rows, LANES), F32),
                   jax.ShapeDtypeStruct((n_groups, state_rows, LANES), F32)],
        scratch_shapes=[pltpu.VMEM((gb, width, width), BF16)],
        compiler_params=_cparams("parallel"),
        name="ssm",
    )(*params, x_p, x_s, s0)


def _layer_norm(x, g, b):
    mu = jnp.mean(x, axis=1, keepdims=True)
    xc = x - mu
    var = jnp.mean(jnp.square(xc), axis=1, keepdims=True)
    return xc * lax.rsqrt(var + LN_EPS) * g + b


def _gelu_tanh(x):
    return 0.5 * x * (1.0 + jnp.tanh(math.sqrt(2.0 / math.pi) * (x + 0.044715 * (x * x * x))))


def _merge_kernel(x_ref, ao_ref, ys_ref, u_ref, d_ref, wg_ref, wap_ref, wglu_ref, wout_ref, g1_ref, b1_ref, h_ref):
    def sub_tile(r):
        rows = slice(r * SUB_ROWS, (r + 1) * SUB_ROWS)
        x = x_ref[rows, :]
        xb = x.astype(BF16)
        a_branch = _dot(ao_ref[rows, :], wap_ref[...])
        s_act = _gelu_tanh(ys_ref[rows, :] + d_ref[...] * u_ref[rows, :]).astype(BF16)
        s_branch = (_dot(s_act, wglu_ref[:, 0:D_MODEL])
                    * jax.nn.sigmoid(_dot(s_act, wglu_ref[:, D_MODEL:2 * D_MODEL])))
        m = (jax.nn.sigmoid(_dot(xb, wg_ref[:, 0:D_MODEL])) * a_branch
             + jax.nn.sigmoid(_dot(xb, wg_ref[:, D_MODEL:2 * D_MODEL])) * s_branch)
        h_ref[rows, :] = _layer_norm(DEEPNORM_ALPHA * x + _dot(m.astype(BF16), wout_ref[...]),
                                     g1_ref[...], b1_ref[...])

    for r in range(x_ref.shape[0] // SUB_ROWS):
        sub_tile(r)


def _merge(x2d, ao, ys, u, d, w_gate, w_ap, w_glu, w_out, ln_g, ln_b, tm):
    t_tokens = x2d.shape[0]
    row = lambda i: (i, 0)
    const = lambda i: (0, 0)
    return pl.pallas_call(
        _merge_kernel,
        grid=(t_tokens // tm,),
        in_specs=[pl.BlockSpec((tm, D_MODEL), row),
                  pl.BlockSpec((tm, ATTN_WIDTH), row),
                  pl.BlockSpec((tm, SSM_WIDTH), row),
                  pl.BlockSpec((tm, SSM_WIDTH), row),
                  pl.BlockSpec((1, SSM_WIDTH), const),
                  pl.BlockSpec((D_MODEL, 2 * D_MODEL), const, pipeline_mode=pl.Buffered(1)),
                  pl.BlockSpec((ATTN_WIDTH, D_MODEL), const, pipeline_mode=pl.Buffered(1)),
                  pl.BlockSpec((SSM_WIDTH, 2 * D_MODEL), const, pipeline_mode=pl.Buffered(1)),
                  pl.BlockSpec((D_MODEL, D_MODEL), const, pipeline_mode=pl.Buffered(1)),
                  pl.BlockSpec((1, D_MODEL), const),
                  pl.BlockSpec((1, D_MODEL), const)],
        out_specs=pl.BlockSpec((tm, D_MODEL), row),
        out_shape=jax.ShapeDtypeStruct((t_tokens, D_MODEL), F32),
        compiler_params=_cparams("parallel"),
        name="merge",
    )(x2d, ao, ys, u, d, w_gate, w_ap, w_glu, w_out, ln_g, ln_b)


def _mlp_kernel(h_ref, w1_ref, w2_ref, g2_ref, b2_ref, o_ref, *, ff_chunk):
    for r in range(h_ref.shape[0] // SUB_ROWS):
        rows = slice(r * SUB_ROWS, (r + 1) * SUB_ROWS)
        h = h_ref[rows, :]
        hb = h.astype(BF16)
        f = jnp.zeros(h.shape, F32)
        for c in range(D_FF // ff_chunk):
            sl = slice(c * ff_chunk, (c + 1) * ff_chunk)
            t = jnp.maximum(_dot(hb, w1_ref[:, sl]), 0.0)
            f = f + _dot((t * t).astype(BF16), w2_ref[sl, :])
        o_ref[rows, :] = _layer_norm(DEEPNORM_ALPHA * h + f, g2_ref[...], b2_ref[...])


def _mlp(h, w1, w2, ln_g, ln_b, tm, ff_chunk=1024):
    t_tokens = h.shape[0]
    row = lambda i: (i, 0)
    const = lambda i: (0, 0)
    return pl.pallas_call(
        functools.partial(_mlp_kernel, ff_chunk=ff_chunk),
        grid=(t_tokens // tm,),
        in_specs=[pl.BlockSpec((tm, D_MODEL), row),
                  pl.BlockSpec((D_MODEL, D_FF), const, pipeline_mode=pl.Buffered(1)),
                  pl.BlockSpec((D_FF, D_MODEL), const, pipeline_mode=pl.Buffered(1)),
                  pl.BlockSpec((1, D_MODEL), const),
                  pl.BlockSpec((1, D_MODEL), const)],
        out_specs=pl.BlockSpec((tm, D_MODEL), row),
        out_shape=jax.ShapeDtypeStruct((t_tokens, D_MODEL), F32),
        compiler_params=_cparams("parallel"),
        name="mlp",
    )(h, w1, w2, ln_g, ln_b)


def _rope_tables(pos):
    inv = 1.0 / (ROPE_THETA ** (jnp.arange(0, HEAD_DIM, 2, dtype=F32) / HEAD_DIM))
    ang = pos.astype(F32)[:, None] * inv[None, :]
    c, s = jnp.cos(ang), jnp.sin(ang)
    reps = LANES // HEAD_DIM
    return jnp.tile(jnp.concatenate([c, c], axis=1), (1, reps)), jnp.tile(jnp.concatenate([-s, s], axis=1), (1, reps))


def kernel(x_prompt, x_sample, cache_k, cache_v, state_ssm_re, state_ssm_im, w_in, lambda_q1, lambda_k1, lambda_q2, lambda_k2, subln_gain, ssm_a_re, ssm_a_im, ssm_b_re, ssm_b_im, ssm_c_re, ssm_c_im, ssm_d, ssm_log_dt, w_attn_proj, w_glu_a, w_glu_b, w_out, ln1_g, ln1_b, w_ff1, w_ff2, ln2_g, ln2_b):
    bp, n_p = x_prompt.shape[0], x_prompt.shape[1]
    bs, n_s = x_sample.shape[0], x_sample.shape[1]
    past = cache_k.shape[2]
    assert w_in.shape[0] == DEPTH and n_s == CHUNK and n_p % CHUNK == 0
    n_chunk = n_p // CHUNK
    tm = min(512, n_p)
    tm_s = min(512, bs * n_s)
    tq = min(512, n_p)
    l = 0
    lam_init = 0.8 - 0.6 * math.exp(-0.3 * l)

    xp = x_prompt.reshape(bp * n_p, D_MODEL)
    xs = x_sample.reshape(bs * n_s, D_MODEL)
    w_qkvu = w_in[l, :, 0:QKVU_COLS].astype(BF16)
    w_gate = w_in[l, :, QKVU_COLS:].astype(BF16)
    w_ap = w_attn_proj[l].astype(BF16)
    w_glu = jnp.concatenate([w_glu_a[l], w_glu_b[l]], axis=1).astype(BF16)
    w_o = w_out[l].astype(BF16)
    w1, w2 = w_ff1[l].astype(BF16), w_ff2[l].astype(BF16)
    lams = [v[l].reshape(1, HEAD_DIM) for v in (lambda_q1, lambda_k1, lambda_q2, lambda_k2)]
    gain = subln_gain[l].reshape(1, V_DIM)
    d_skip = ssm_d[l].reshape(1, SSM_WIDTH)
    lng = [v[l].reshape(1, D_MODEL) for v in (ln1_g, ln1_b, ln2_g, ln2_b)]

    cos_p, sin_p = _rope_tables(jnp.arange(n_p))
    cos_s, sin_s = _rope_tables(jnp.tile(past + jnp.arange(n_s), tm_s // n_s))

    q_p, k_p, v_p, u_p, kb_p, vt_p = _project(xp, w_qkvu, cos_p, sin_p, n_p, min(2 * tm, n_p), True)
    q_s, k_s, v_s, u_s = _project(xs, w_qkvu, cos_s, sin_s, n_s, tm_s, False)

    ao_p = _prompt_attention(lams, gain, q_p, kb_p, vt_p, bp, n_p, tq, lam_init)
    ao_s = _sample_attention(lams, gain, q_s, k_s, v_s,
                             cache_k[l].reshape(bs, past * N_HEADS, V_DIM),
                             cache_v[l].reshape(bs, past * N_HEADS, V_DIM), lam_init)

    tables = _ssm_params(ssm_a_re[l], ssm_a_im[l], ssm_b_re[l], ssm_b_im[l],
                         ssm_c_re[l], ssm_c_im[l], ssm_log_dt[l])
    x_cols_p = _chunk_cols(u_p.reshape(bp, n_chunk, CHUNK, SSM_WIDTH))
    x_cols_s = _chunk_cols(u_s.reshape(1, bs, CHUNK, SSM_WIDTH))
    s0 = jnp.concatenate([state_ssm_re[l], state_ssm_im[l]], axis=-1).transpose(1, 2, 0)
    s0 = jnp.pad(s0, ((0, 0), (0, 0), (0, LANES - bs)))
    y_cols_p, y_cols_s, st_p, st_s = _ssm(tables, x_cols_p, x_cols_s, s0, n_chunk)
    ys_p = _unchunk_cols(y_cols_p, bp, n_chunk).reshape(bp * n_p, SSM_WIDTH)
    ys_s = _unchunk_cols(y_cols_s, 1, bs).reshape(bs * n_s, SSM_WIDTH)
    sf_p = st_p[:, :, 0:bp]
    sf_s = st_s[:, :, 0:bs]

    outs = []
    for x2d, ao, ys, u, tile in ((xp, ao_p, ys_p, u_p, tm), (xs, ao_s, ys_s, u_s, tm_s)):
        big = 2 * tile if x2d.shape[0] >= 8 * tile else tile
        h = _merge(x2d, ao, ys, u, d_skip, w_gate, w_ap, w_glu, w_o, lng[0], lng[1], big)
        outs.append(_mlp(h, w1, w2, lng[2], lng[3], big))

    def states(sf):
        t = sf.transpose(2, 0, 1)
        return t[None, :, :, 0:STATE_DIM], t[None, :, :, STATE_DIM:]

    srp, sip = states(sf_p)
    srs, sis = states(sf_s)
    return (outs[0].reshape(bp, n_p, D_MODEL), outs[1].reshape(bs, n_s, D_MODEL),
            k_p.reshape(1, bp, n_p, N_HEADS, V_DIM), v_p.reshape(1, bp, n_p, N_HEADS, V_DIM), srp, sip,
            k_s.reshape(1, bs, n_s, N_HEADS, V_DIM), v_s.reshape(1, bs, n_s, N_HEADS, V_DIM), srs, sis)
```

```python
import functools
import math

import jax
import jax.numpy as jnp
from jax import lax
from jax.experimental import pallas as pl
from jax.experimental.pallas import tpu as pltpu

D_MODEL = 1024
CHUNK = 64
N_HEADS = 4
HEAD_DIM = 64
V_DIM = 2 * HEAD_DIM
ATTN_WIDTH = N_HEADS * V_DIM
SSM_WIDTH = 512
GROUP_SIZE = 16
N_GROUPS = SSM_WIDTH // GROUP_SIZE
STATE_DIM = 64
D_FF = 4 * D_MODEL
ROPE_THETA = 10000.0
LN_EPS = 1e-5
RMS_EPS = 1e-5
NEG_INF = -1e30
DEPTH = 1
DEEPNORM_ALPHA = (2.0 * DEPTH) ** 0.25
QKVU_COLS = 3 * ATTN_WIDTH + SSM_WIDTH
LOG2E = 1.4426950408889634

LANES = 128
VT_ROWS = V_DIM + 16
QUERY_LANES = 256
TOEPLITZ_ROWS = 256
SUB_ROWS = 256
ATTN_ITEMS_PER_REGION = 12
SSM_GROUPS_PER_STEP = 4
SAMPLE_STREAMS_PER_STEP = 4
VMEM_LIMIT = 56 * 1024 * 1024

F32 = jnp.float32
BF16 = jnp.bfloat16


def _cparams(*sem):
    return pltpu.CompilerParams(dimension_semantics=sem, vmem_limit_bytes=VMEM_LIMIT)


def _nt_dot(a, b):
    return lax.dot_general(a, b, (((1,), (1,)), ((), ())), preferred_element_type=F32)


def _dot(a, b):
    return jnp.dot(a, b, preferred_element_type=F32)


def _in_regions(n_steps, steps_per_region, step):
    once = jnp.minimum(pl.program_id(0) + 1, 1)
    for first in range(0, n_steps, steps_per_region):
        def region(_, carry, first=first):
            for i in range(first, min(first + steps_per_region, n_steps)):
                step(i)
            return carry
        lax.fori_loop(0, once, region, 0)


def _rotary(z, cos, sin_signed, first_half):
    swapped = jnp.where(first_half, pltpu.roll(z, 96, 1), pltpu.roll(z, 32, 1))
    return z * cos + swapped * sin_signed


def _proj_kernel(x_ref, w_ref, cos_ref, sin_ref, q_ref, k_ref, v_ref, u_ref, *rest, emit_t):
    xb = x_ref[...].astype(BF16)
    cos = cos_ref[...]
    sin = sin_ref[...]
    lane = lax.broadcasted_iota(jnp.int32, cos.shape, 1)
    first_half = (lane % HEAD_DIM) < (HEAD_DIM // 2)
    tm = xb.shape[0]
    zq = _dot(xb, w_ref[:, 0:ATTN_WIDTH])
    zk = _dot(xb, w_ref[:, ATTN_WIDTH:2 * ATTN_WIDTH])
    zv = _dot(xb, w_ref[:, 2 * ATTN_WIDTH:3 * ATTN_WIDTH])
    for h in range(N_HEADS):
        sl = slice(h * V_DIM, (h + 1) * V_DIM)
        q_ref[:, sl] = (_rotary(zq[:, sl], cos, sin, first_half) * (LOG2E * HEAD_DIM ** -0.5)).astype(BF16)
        kr = _rotary(zk[:, sl], cos, sin, first_half)
        k_ref[pl.ds(h, tm, stride=N_HEADS), :] = kr
        v_ref[pl.ds(h, tm, stride=N_HEADS), :] = zv[:, sl]
        if emit_t:
            rest[0][:, sl] = kr.astype(BF16)
    if emit_t:
        vt_ref = rest[1]
        zvt = zv.T.astype(BF16)
        ones = jnp.ones((VT_ROWS - V_DIM, zvt.shape[1]), BF16)
        for h in range(N_HEADS):
            vt_ref[h, 0:V_DIM, :] = zvt[h * V_DIM:(h + 1) * V_DIM, :]
            vt_ref[h, V_DIM:VT_ROWS, :] = ones
    u_ref[...] = _dot(xb, w_ref[:, 3 * ATTN_WIDTH:QKVU_COLS])


def _project(x2d, w_qkvu, cos_t, sin_t, seq_len, tm, emit_t):
    t_tokens = x2d.shape[0]
    n_tiles = t_tokens // tm
    n_pos_tiles = cos_t.shape[0] // tm
    tiles_per_seq = max(seq_len // tm, 1)
    row = lambda i: (i, 0)
    pos = lambda i: (i % n_pos_tiles, 0)
    out_shape = [jax.ShapeDtypeStruct((t_tokens, ATTN_WIDTH), BF16),
                 jax.ShapeDtypeStruct((t_tokens * N_HEADS, V_DIM), F32),
                 jax.ShapeDtypeStruct((t_tokens * N_HEADS, V_DIM), F32),
                 jax.ShapeDtypeStruct((t_tokens, SSM_WIDTH), F32)]
    out_specs = ([pl.BlockSpec((tm, ATTN_WIDTH), row)] + [pl.BlockSpec((tm * N_HEADS, V_DIM), row)] * 2
                 + [pl.BlockSpec((tm, SSM_WIDTH), row)])
    if emit_t:
        bsz = t_tokens // seq_len
        out_shape += [jax.ShapeDtypeStruct((t_tokens, ATTN_WIDTH), BF16),
                      jax.ShapeDtypeStruct((bsz, N_HEADS, VT_ROWS, seq_len), BF16)]
        out_specs += [pl.BlockSpec((tm, ATTN_WIDTH), row),
                      pl.BlockSpec((None, N_HEADS, VT_ROWS, tm),
                                   lambda i: (i // tiles_per_seq, 0, 0, i % tiles_per_seq))]
    return pl.pallas_call(
        functools.partial(_proj_kernel, emit_t=emit_t),
        grid=(n_tiles,),
        in_specs=[pl.BlockSpec((tm, D_MODEL), row),
                  pl.BlockSpec((D_MODEL, QKVU_COLS), lambda i: (0, 0), pipeline_mode=pl.Buffered(1)),
                  pl.BlockSpec((tm, LANES), pos),
                  pl.BlockSpec((tm, LANES), pos)],
        out_specs=out_specs,
        out_shape=out_shape,
        compiler_params=_cparams("parallel"),
        name="proj_t" if emit_t else "proj",
    )(x2d, w_qkvu, cos_t, sin_t)


def _diff_lambda(lq1, lk1, lq2, lk2, lam_init):
    return (jnp.exp(jnp.sum(lq1 * lk1, axis=1, keepdims=True))
            - jnp.exp(jnp.sum(lq2 * lk2, axis=1, keepdims=True)) + lam_init)


def _sub_norm(d, gain, lam_init):
    ms = jnp.mean(jnp.square(d), axis=1, keepdims=True)
    return d * lax.rsqrt(ms + RMS_EPS) * gain * (1.0 - lam_init)


def _stack_maps(q):
    lane = lax.broadcasted_iota(jnp.int32, q.shape, 1)
    zero = jnp.zeros_like(q)
    return jnp.concatenate([jnp.where(lane < HEAD_DIM, q, zero), jnp.where(lane >= HEAD_DIM, q, zero)], axis=0)


def _prompt_attn_kernel(lq1_ref, lk1_ref, lq2_ref, lk2_ref, gain_ref, q_ref, k_ref, vt_ref,
                        o_ref, m_scr, acc_scr, qx_scr, s0_scr, s1_scr, mx0_scr, mx1_scr, *, tq, lam_init):
    nq = q_ref.shape[0] // tq
    items = [(qi, j) for qi in range(nq) for j in range(qi + 1)]
    for i in range(nq):
        qx_scr[i * 2 * tq:(i + 1) * 2 * tq, :] = _stack_maps(q_ref[i * tq:(i + 1) * tq, :])
    chains = [slice(c * QUERY_LANES, (c + 1) * QUERY_LANES) for c in range(2 * tq // QUERY_LANES)]

    def visible_keys(cs, diagonal):
        return min(tq, cs.start % tq + QUERY_LANES) if diagonal else tq

    def scores(qi, j, s_scr, mx_scr):
        for cs in chains:
            qx = qx_scr[qi * 2 * tq + cs.start:qi * 2 * tq + cs.stop, :]
            st = _nt_dot(k_ref[j * tq:(j + 1) * tq, :], qx)
            s_scr[:, cs] = st
            mx_scr[:, cs] = jnp.max(st, axis=0, keepdims=True)

    def softmax_pv(j, s_scr, mx_scr, diagonal):
        start = j * tq
        for cs in chains:
            n_keys = visible_keys(cs, diagonal)
            if diagonal:
                lane = lax.broadcasted_iota(jnp.int32, (CHUNK, QUERY_LANES), 1)
                blocks = []
                for kc in range(n_keys // CHUNK):
                    blk = s_scr[kc * CHUNK:(kc + 1) * CHUNK, cs]
                    first_visible = kc * CHUNK - cs.start % tq
                    if first_visible > 0:
                        blk = jnp.where(lane >= first_visible, blk, NEG_INF)
                    blocks.append(blk)
                st = jnp.concatenate(blocks, axis=0)
                tile_max = jnp.max(st, axis=0, keepdims=True)
            else:
                st = s_scr[:, cs]
                tile_max = mx_scr[:, cs]
            vt = vt_ref[:, start:start + n_keys]
            p_scale = None
            if j == 0:
                m_new = tile_max
            else:
                m_old = m_scr[:, cs]
                m_new = jnp.maximum(m_old, tile_max)
                p_scale = jnp.exp2(m_old - m_new)
            p = jnp.exp2(st - m_new).astype(BF16)
            pv = _dot(vt, p)
            acc_scr[:, cs] = pv if p_scale is None else acc_scr[:, cs] * p_scale + pv
            m_scr[:, cs] = m_new

    def finish(qi):
        lam = _diff_lambda(lq1_ref[...], lk1_ref[...], lq2_ref[...], lk2_ref[...], lam_init)
        for c0 in range(0, tq, QUERY_LANES):
            maps = []
            for lanes in (slice(c0, c0 + QUERY_LANES), slice(tq + c0, tq + c0 + QUERY_LANES)):
                maps.append(acc_scr[0:V_DIM, lanes] * (1.0 / acc_scr[V_DIM:V_DIM + 1, lanes]))
            d = (maps[0] - lam * maps[1]).T
            o_ref[qi * tq + c0:qi * tq + c0 + QUERY_LANES, :] = _sub_norm(d, gain_ref[...], lam_init).astype(BF16)

    bufs = ((s0_scr, mx0_scr), (s1_scr, mx1_scr))
    scores(*items[0], *bufs[0])

    def run_item(w):
        qi, j = items[w]
        if w + 1 < len(items):
            scores(*items[w + 1], *bufs[(w + 1) % 2])
        softmax_pv(j, *bufs[w % 2], j == qi)
        if j == qi:
            finish(qi)

    _in_regions(len(items), ATTN_ITEMS_PER_REGION, run_item)


def _prompt_attention(lams, gain, q, kb, vt, bsz, seq_len, tq, lam_init):
    nq = seq_len // tq
    small = lambda b, h: (0, 0)
    return pl.pallas_call(
        functools.partial(_prompt_attn_kernel, tq=tq, lam_init=lam_init),
        grid=(bsz, N_HEADS),
        in_specs=[pl.BlockSpec((1, HEAD_DIM), small)] * 4 + [
            pl.BlockSpec((1, V_DIM), small),
            pl.BlockSpec((seq_len, V_DIM), lambda b, h: (b, h)),
            pl.BlockSpec((seq_len, V_DIM), lambda b, h: (b, h)),
            pl.BlockSpec((None, None, VT_ROWS, seq_len), lambda b, h: (b, h, 0, 0))],
        out_specs=pl.BlockSpec((seq_len, V_DIM), lambda b, h: (b, h)),
        out_shape=jax.ShapeDtypeStruct((bsz * seq_len, ATTN_WIDTH), BF16),
        scratch_shapes=[pltpu.VMEM((1, 2 * tq), F32), pltpu.VMEM((VT_ROWS, 2 * tq), F32),
                        pltpu.VMEM((nq * 2 * tq, V_DIM), BF16),
                        pltpu.VMEM((tq, 2 * tq), F32), pltpu.VMEM((tq, 2 * tq), F32),
                        pltpu.VMEM((1, 2 * tq), F32), pltpu.VMEM((1, 2 * tq), F32)],
        compiler_params=_cparams("parallel", "parallel"),
        name="prompt_attn",
    )(*lams, gain, q, kb, vt)


def _sample_attn_kernel(lq1_ref, lk1_ref, lq2_ref, lk2_ref, gain_ref, q_ref, k_ref, v_ref, ck_ref, cv_ref,
                        o_ref, *, lam_init):
    n_streams = ck_ref.shape[0]
    n_new = q_ref.shape[0] // n_streams
    past = ck_ref.shape[1] // N_HEADS
    lam = _diff_lambda(lq1_ref[...], lk1_ref[...], lq2_ref[...], lk2_ref[...], lam_init)
    gain = gain_ref[...]
    for s in range(n_streams):
        rows = slice(s * n_new, (s + 1) * n_new)
        for h in range(N_HEADS):
            sl = slice(h * V_DIM, (h + 1) * V_DIM)
            old = pl.ds(h, past, stride=N_HEADS)
            new = pl.ds(s * n_new * N_HEADS + h, n_new, stride=N_HEADS)
            qx = _stack_maps(q_ref[rows, sl])
            s_c = _nt_dot(qx, ck_ref[s, old, :].astype(BF16))
            s_n = _nt_dot(qx, k_ref[new, :].astype(BF16))
            m = jnp.maximum(jnp.max(s_c, axis=1, keepdims=True), jnp.max(s_n, axis=1, keepdims=True))
            p_c = jnp.exp2(s_c - m).astype(BF16)
            p_n = jnp.exp2(s_n - m).astype(BF16)
            vc = jnp.concatenate([cv_ref[s, old, :].astype(BF16), jnp.ones((past, V_DIM), BF16)], axis=1)
            vn = jnp.concatenate([v_ref[new, :].astype(BF16), jnp.ones((n_new, V_DIM), BF16)], axis=1)
            o_ext = _dot(p_c, vc) + _dot(p_n, vn)
            o = o_ext[:, 0:V_DIM] * (1.0 / o_ext[:, V_DIM:2 * V_DIM])
            d = o[0:n_new, :] - lam * o[n_new:2 * n_new, :]
            o_ref[rows, sl] = _sub_norm(d, gain, lam_init).astype(BF16)


def _sample_attention(lams, gain, q, k, v, cache_k, cache_v, lam_init):
    bsz, past_rows = cache_k.shape[0], cache_k.shape[1]
    n_new = q.shape[0] // bsz
    ns = SAMPLE_STREAMS_PER_STEP if bsz % SAMPLE_STREAMS_PER_STEP == 0 else 1
    small = lambda b: (0, 0)
    row = lambda b: (b, 0)
    return pl.pallas_call(
        functools.partial(_sample_attn_kernel, lam_init=lam_init),
        grid=(bsz // ns,),
        in_specs=[pl.BlockSpec((1, HEAD_DIM), small)] * 4 + [
            pl.BlockSpec((1, V_DIM), small),
            pl.BlockSpec((ns * n_new, ATTN_WIDTH), row),
            pl.BlockSpec((ns * n_new * N_HEADS, V_DIM), row),
            pl.BlockSpec((ns * n_new * N_HEADS, V_DIM), row),
            pl.BlockSpec((ns, past_rows, V_DIM), lambda b: (b, 0, 0)),
            pl.BlockSpec((ns, past_rows, V_DIM), lambda b: (b, 0, 0))],
        out_specs=pl.BlockSpec((ns * n_new, ATTN_WIDTH), row),
        out_shape=jax.ShapeDtypeStruct(q.shape, BF16),
        compiler_params=_cparams("parallel"),
        name="sample_attn",
    )(*lams, gain, q, k, v, cache_k, cache_v)


def _ssm_params(a_re, a_im, b_re, b_im, c_re, c_im, log_dt):
    g = a_re.shape[0]
    twice = lambda v: jnp.concatenate([v, v], axis=-1)
    rows = jnp.stack([twice(a_re), twice(a_im), jnp.broadcast_to(log_dt[:, None], (g, 2 * STATE_DIM))], axis=1)
    rows = jnp.pad(rows, ((0, 0), (0, 8 - rows.shape[1]), (0, 0)))
    return rows, twice(b_re.transpose(0, 2, 1)), twice(b_im.transpose(0, 2, 1)), twice(c_re), twice(c_im)


def _group_tables(rows_ref, bre_ref, bim_ref, cre_ref, cim_ref):
    lane = lax.broadcasted_iota(jnp.int32, (1, 2 * STATE_DIM), 1)
    lo = lane < STATE_DIM
    a_re, a_im, log_dt = rows_ref[0:1, :], rows_ref[1:2, :], rows_ref[2:3, :]
    dt = jnp.exp(log_dt)
    lam_re, lam_im = a_re * dt, a_im * dt
    mag = jnp.exp(lam_re)
    ar, ai = mag * jnp.cos(lam_im), mag * jnp.sin(lam_im)
    den = jnp.square(a_re) + jnp.square(a_im)
    cr = ((ar - 1.0) * a_re + ai * a_im) / den
    ci = (ai * a_re - (ar - 1.0) * a_im) / den
    bre, bim = bre_ref[...], bim_ref[...]
    bbr = cr * bre - ci * bim
    bbi = cr * bim + ci * bre
    quarter_turn = jnp.where(lo, 0.0, 0.5 * math.pi)

    def powers(tau):
        return jnp.exp(tau * lam_re) * jnp.cos(tau * lam_im - quarter_turn)

    def outer(y, x1, x2):
        y_sw = pltpu.roll(y, STATE_DIM, 1)
        prod = y[:, None, :] * x1[None, :, :] + y_sw[:, None, :] * x2[None, :, :]
        return prod.reshape(y.shape[0] * x1.shape[0], 2 * STATE_DIM)

    frames = lax.broadcasted_iota(jnp.int32, (CHUNK, 1), 0).astype(F32)
    wt = outer(powers((CHUNK - 1.0) - frames), bbr, jnp.where(lo, -bbi, bbi))
    w = wt.T
    cre, cim = cre_ref[...], cim_ref[...]
    vm = outer(powers(frames + 1.0), jnp.where(lo, cre, -cre), -cim)
    rrev = jnp.dot(jnp.where(lo, cre, -cim), w, precision=lax.Precision.HIGHEST, preferred_element_type=F32)
    a_row = powers(jnp.full((8, 1), float(CHUNK), F32))[0:1, :]
    a_col = jnp.broadcast_to(a_row, (2 * STATE_DIM, 2 * STATE_DIM)).T
    a_swap = pltpu.roll(a_col, STATE_DIM, 0)
    top = lax.broadcasted_iota(jnp.int32, a_col.shape, 0) < STATE_DIM
    return (rrev, w.astype(BF16), vm.astype(BF16),
            jnp.where(top, a_col, a_swap), jnp.where(top, -a_swap, a_col))


def _chunk_cols_kernel(u_ref, o_ref):
    n_s, n_r, n_j = u_ref.shape[0], u_ref.shape[1], u_ref.shape[2]
    pad = LANES - n_s * n_r
    for jj in range(n_j):
        rows = [u_ref[s, :, jj, :] for s in range(n_s)]
        if pad:
            rows.append(jnp.zeros((pad, SSM_WIDTH), F32))
        cols = jnp.concatenate(rows, axis=0).T
        o_ref[:, jj * GROUP_SIZE:(jj + 1) * GROUP_SIZE, :] = cols.reshape(N_GROUPS, GROUP_SIZE, LANES).astype(BF16)


def _chunk_cols(u4, frames_per_step=16):
    s_total, n_r = u4.shape[0], u4.shape[1]
    s_step = min(s_total, LANES // n_r)
    assert s_step >= 1 and s_total % s_step == 0
    n_lane_blocks = s_total // s_step
    return pl.pallas_call(
        _chunk_cols_kernel,
        grid=(n_lane_blocks, CHUNK // frames_per_step),
        in_specs=[pl.BlockSpec((s_step, n_r, frames_per_step, SSM_WIDTH), lambda a, j: (a, 0, j, 0))],
        out_specs=pl.BlockSpec((N_GROUPS, frames_per_step * GROUP_SIZE, LANES), lambda a, j: (0, j, a)),
        out_shape=jax.ShapeDtypeStruct((N_GROUPS, CHUNK * GROUP_SIZE, n_lane_blocks * LANES), BF16),
        compiler_params=_cparams("parallel", "parallel"),
        name="ssm_in",
    )(u4)


def _unchunk_cols_kernel(y_ref, o_ref):
    n_s, n_r, n_j = o_ref.shape[0], o_ref.shape[1], o_ref.shape[2]
    for jj in range(n_j):
        cols = y_ref[:, jj * GROUP_SIZE:(jj + 1) * GROUP_SIZE, :].astype(F32).reshape(SSM_WIDTH, LANES)
        rows = cols.T
        for s in range(n_s):
            o_ref[s, :, jj, :] = rows[s * n_r:(s + 1) * n_r, :]


def _unchunk_cols(y_cols, s_total, n_r, frames_per_step=16):
    s_step = min(s_total, LANES // n_r)
    n_lane_blocks = s_total // s_step
    return pl.pallas_call(
        _unchunk_cols_kernel,
        grid=(n_lane_blocks, CHUNK // frames_per_step),
        in_specs=[pl.BlockSpec((N_GROUPS, frames_per_step * GROUP_SIZE, LANES), lambda a, j: (0, j, a))],
        out_specs=pl.BlockSpec((s_step, n_r, frames_per_step, SSM_WIDTH), lambda a, j: (a, 0, j, 0)),
        out_shape=jax.ShapeDtypeStruct((s_total, n_r, CHUNK, SSM_WIDTH), F32),
        compiler_params=_cparams("parallel", "parallel"),
        name="ssm_out",
    )(y_cols)


def _ssm_kernel(*refs, n_chunk):
    for gi in range(SSM_GROUPS_PER_STEP):
        _ssm_group(*[r.at[gi] for r in refs], n_chunk=n_chunk)


def _ssm_group(rows_ref, bre_ref, bim_ref, cre_ref, cim_ref, xp_ref, xs_ref, s0_ref, yp_ref, ys_ref, sp_ref, ss_ref,
               mt_scr, *, n_chunk):
    width = GROUP_SIZE * CHUNK
    n_piece = width // LANES
    lane16 = lax.broadcasted_iota(jnp.int32, (GROUP_SIZE, LANES), 1)
    rrev, w, vm_all, a_c, a_s = _group_tables(rows_ref, bre_ref, bim_ref, cre_ref, cim_ref)
    pieces = [rrev[:, k * LANES:(k + 1) * LANES] for k in range(n_piece)] + [jnp.zeros((GROUP_SIZE, LANES), F32)]
    rolled = {0: pieces}
    for b in range(GROUP_SIZE, LANES, GROUP_SIZE):
        rolled[b] = [pltpu.roll(p, LANES - b, 1) for p in pieces[:n_piece]] + [pieces[n_piece]]
    def toeplitz_rows(t0, t1, n_cols):
        for t in range(t0, t1):
            shift = GROUP_SIZE * (CHUNK - 1 - t)
            a, b = shift // LANES, shift % LANES
            for v in range(n_cols // LANES):
                k = v + a
                if k >= n_piece:
                    blk = pieces[n_piece]
                elif b == 0:
                    blk = pieces[k]
                else:
                    blk = jnp.where(lane16 < LANES - b, rolled[b][k], rolled[b][k + 1])
                mt_scr[t * GROUP_SIZE:(t + 1) * GROUP_SIZE, v * LANES:(v + 1) * LANES] = blk.astype(BF16)

    xp, xs = xp_ref[...], xs_ref[...]
    sloc_p, sloc_s = _dot(w, xp), _dot(w, xs)

    def cmul(pc, ps, s):
        return pc * s + ps * pltpu.roll(s, STATE_DIM, 0)

    lane = lax.broadcasted_iota(jnp.int32, (2 * STATE_DIM, LANES), 1) % n_chunk
    prev_cols = []
    lane_id = lax.broadcasted_iota(jnp.int32, (2 * STATE_DIM, LANES), 1)
    final = jnp.zeros((2 * STATE_DIM, LANES), F32)
    seqs_per_tile = LANES // n_chunk
    for tile in range(xp.shape[1] // LANES):
        s_inc = sloc_p[:, tile * LANES:(tile + 1) * LANES]
        pc, ps = a_c, a_s
        dist = 1
        while dist < n_chunk:
            shifted = jnp.where(lane >= dist, pltpu.roll(s_inc, dist, 1), 0.0)
            s_inc = s_inc + cmul(pc, ps, shifted)
            pc, ps = pc * pc - ps * ps, 2.0 * pc * ps
            dist *= 2
        for k in range(seqs_per_tile):
            src, dst = (k + 1) * n_chunk - 1, tile * seqs_per_tile + k
            final = jnp.where(lane_id == dst, pltpu.roll(s_inc, (dst - src) % LANES, 1), final)
        prev_cols.append(jnp.where(lane >= 1, pltpu.roll(s_inc, 1, 1), 0.0))
    sp_ref[...] = final
    s_prev_p = jnp.concatenate(prev_cols, axis=1).astype(BF16)
    s0 = s0_ref[...]
    s0b = s0.astype(BF16)
    frames = TOEPLITZ_ROWS // GROUP_SIZE
    for i in range(width // TOEPLITZ_ROWS):
        rows = slice(i * TOEPLITZ_ROWS, (i + 1) * TOEPLITZ_ROWS)
        n_cols = (i + 1) * TOEPLITZ_ROWS
        toeplitz_rows(i * frames, (i + 1) * frames, n_cols)
        mt = mt_scr[rows, 0:n_cols]
        vm = vm_all[rows, :]
        yp_ref[rows, :] = (_dot(mt, xp[0:n_cols, :]) + _dot(vm, s_prev_p)).astype(BF16)
        ys_ref[rows, :] = (_dot(mt, xs[0:n_cols, :]) + _dot(vm, s0b)).astype(BF16)
    ss_ref[...] = cmul(a_c, a_s, s0) + sloc_s


def _ssm(params, x_p, x_s, s0, n_chunk):
    n_groups, lanes_p = x_p.shape[0], x_p.shape[2]
    width = GROUP_SIZE * CHUNK
    per_g = lambda g: (g, 0, 0)
    state_rows = 2 * STATE_DIM
    gb = SSM_GROUPS_PER_STEP
    assert n_groups % gb == 0
    spec = lambda rows, cols: pl.BlockSpec((gb, rows, cols), per_g)
    return pl.pallas_call(
        functools.partial(_ssm_kernel, n_chunk=n_chunk),
        grid=(n_groups // gb,),
        in_specs=[spec(8, state_rows)] + [spec(GROUP_SIZE, state_rows)] * 4 + [
                  spec(width, lanes_p), spec(width, LANES), spec(state_rows, LANES)],
        out_specs=[spec(width, lanes_p), spec(width, LANES), spec(state_rows, LANES), spec(state_rows, LANES)],
        out_shape=[jax.ShapeDtypeStruct((n_groups, width, lanes_p), BF16),
                   jax.ShapeDtypeStruct((n_groups, width, LANES), BF16),
                   jax.ShapeDtypeStruct((n_groups, state_rows, LANES), F32),
                   jax.ShapeDtypeStruct((n_groups, state_rows, LANES), F32)],
        scratch_shapes=[pltpu.VMEM((gb, width, width), BF16)],
        compiler_params=_cparams("parallel"),
        name="ssm",
    )(*params, x_p, x_s, s0)


def _layer_norm(x, g, b):
    mu = jnp.mean(x, axis=1, keepdims=True)
    xc = x - mu
    var = jnp.mean(jnp.square(xc), axis=1, keepdims=True)
    return xc * lax.rsqrt(var + LN_EPS) * g + b


def _gelu_tanh(x):
    return 0.5 * x * (1.0 + jnp.tanh(math.sqrt(2.0 / math.pi) * (x + 0.044715 * (x * x * x))))


def _merge_kernel(x_ref, ao_ref, ys_ref, u_ref, d_ref, wg_ref, wap_ref, wglu_ref, wout_ref, g1_ref, b1_ref, h_ref):
    def sub_tile(r):
        rows = slice(r * SUB_ROWS, (r + 1) * SUB_ROWS)
        x = x_ref[rows, :]
        xb = x.astype(BF16)
        a_branch = _dot(ao_ref[rows, :], wap_ref[...])
        s_act = _gelu_tanh(ys_ref[rows, :] + d_ref[...] * u_ref[rows, :]).astype(BF16)
        s_branch = (_dot(s_act, wglu_ref[:, 0:D_MODEL])
                    * jax.nn.sigmoid(_dot(s_act, wglu_ref[:, D_MODEL:2 * D_MODEL])))
        m = (jax.nn.sigmoid(_dot(xb, wg_ref[:, 0:D_MODEL])) * a_branch
             + jax.nn.sigmoid(_dot(xb, wg_ref[:, D_MODEL:2 * D_MODEL])) * s_branch)
        h_ref[rows, :] = _layer_norm(DEEPNORM_ALPHA * x + _dot(m.astype(BF16), wout_ref[...]),
                                     g1_ref[...], b1_ref[...])

    for r in range(x_ref.shape[0] // SUB_ROWS):
        sub_tile(r)


def _merge(x2d, ao, ys, u, d, w_gate, w_ap, w_glu, w_out, ln_g, ln_b, tm):
    t_tokens = x2d.shape[0]
    row = lambda i: (i, 0)
    const = lambda i: (0, 0)
    return pl.pallas_call(
        _merge_kernel,
        grid=(t_tokens // tm,),
        in_specs=[pl.BlockSpec((tm, D_MODEL), row),
                  pl.BlockSpec((tm, ATTN_WIDTH), row),
                  pl.BlockSpec((tm, SSM_WIDTH), row),
                  pl.BlockSpec((tm, SSM_WIDTH), row),
                  pl.BlockSpec((1, SSM_WIDTH), const),
                  pl.BlockSpec((D_MODEL, 2 * D_MODEL), const, pipeline_mode=pl.Buffered(1)),
                  pl.BlockSpec((ATTN_WIDTH, D_MODEL), const, pipeline_mode=pl.Buffered(1)),
                  pl.BlockSpec((SSM_WIDTH, 2 * D_MODEL), const, pipeline_mode=pl.Buffered(1)),
                  pl.BlockSpec((D_MODEL, D_MODEL), const, pipeline_mode=pl.Buffered(1)),
                  pl.BlockSpec((1, D_MODEL), const),
                  pl.BlockSpec((1, D_MODEL), const)],
        out_specs=pl.BlockSpec((tm, D_MODEL), row),
        out_shape=jax.ShapeDtypeStruct((t_tokens, D_MODEL), F32),
        compiler_params=_cparams("parallel"),
        name="merge",
    )(x2d, ao, ys, u, d, w_gate, w_ap, w_glu, w_out, ln_g, ln_b)


def _mlp_kernel(h_ref, w1_ref, w2_ref, g2_ref, b2_ref, o_ref, *, ff_chunk):
    for r in range(h_ref.shape[0] // SUB_ROWS):
        rows = slice(r * SUB_ROWS, (r + 1) * SUB_ROWS)
        h = h_ref[rows, :]
        hb = h.astype(BF16)
        f = jnp.zeros(h.shape, F32)
        for c in range(D_FF // ff_chunk):
            sl = slice(c * ff_chunk, (c + 1) * ff_chunk)
            t = jnp.maximum(_dot(hb, w1_ref[:, sl]), 0.0)
            f = f + _dot((t * t).astype(BF16), w2_ref[sl, :])
        o_ref[rows, :] = _layer_norm(DEEPNORM_ALPHA * h + f, g2_ref[...], b2_ref[...])


def _mlp(h, w1, w2, ln_g, ln_b, tm, ff_chunk=1024):
    t_tokens = h.shape[0]
    row = lambda i: (i, 0)
    const = lambda i: (0, 0)
    return pl.pallas_call(
        functools.partial(_mlp_kernel, ff_chunk=ff_chunk),
        grid=(t_tokens // tm,),
        in_specs=[pl.BlockSpec((tm, D_MODEL), row),
                  pl.BlockSpec((D_MODEL, D_FF), const, pipeline_mode=pl.Buffered(1)),
                  pl.BlockSpec((D_FF, D_MODEL), const, pipeline_mode=pl.Buffered(1)),
                  pl.BlockSpec((1, D_MODEL), const),
                  pl.BlockSpec((1, D_MODEL), const)],
        out_specs=pl.BlockSpec((tm, D_MODEL), row),
        out_shape=jax.ShapeDtypeStruct((t_tokens, D_MODEL), F32),
        compiler_params=_cparams("parallel"),
        name="mlp",
    )(h, w1, w2, ln_g, ln_b)


def _rope_tables(pos):
    inv = 1.0 / (ROPE_THETA ** (jnp.arange(0, HEAD_DIM, 2, dtype=F32) / HEAD_DIM))
    ang = pos.astype(F32)[:, None] * inv[None, :]
    c, s = jnp.cos(ang), jnp.sin(ang)
    reps = LANES // HEAD_DIM
    return jnp.tile(jnp.concatenate([c, c], axis=1), (1, reps)), jnp.tile(jnp.concatenate([-s, s], axis=1), (1, reps))


def kernel(x_prompt, x_sample, cache_k, cache_v, state_ssm_re, state_ssm_im, w_in, lambda_q1, lambda_k1, lambda_q2, lambda_k2, subln_gain, ssm_a_re, ssm_a_im, ssm_b_re, ssm_b_im, ssm_c_re, ssm_c_im, ssm_d, ssm_log_dt, w_attn_proj, w_glu_a, w_glu_b, w_out, ln1_g, ln1_b, w_ff1, w_ff2, ln2_g, ln2_b):
    bp, n_p = x_prompt.shape[0], x_prompt.shape[1]
    bs, n_s = x_sample.shape[0], x_sample.shape[1]
    past = cache_k.shape[2]
    assert w_in.shape[0] == DEPTH and n_s == CHUNK and n_p % CHUNK == 0
    n_chunk = n_p // CHUNK
    tm = min(512, n_p)
    tm_s = min(512, bs * n_s)
    tq = min(512, n_p)
    l = 0
    lam_init = 0.8 - 0.6 * math.exp(-0.3 * l)

    xp = x_prompt.reshape(bp * n_p, D_MODEL)
    xs = x_sample.reshape(bs * n_s, D_MODEL)
    w_qkvu = w_in[l, :, 0:QKVU_COLS].astype(BF16)
    w_gate = w_in[l, :, QKVU_COLS:].astype(BF16)
    w_ap = w_attn_proj[l].astype(BF16)
    w_glu = jnp.concatenate([w_glu_a[l], w_glu_b[l]], axis=1).astype(BF16)
    w_o = w_out[l].astype(BF16)
    w1, w2 = w_ff1[l].astype(BF16), w_ff2[l].astype(BF16)
    lams = [v[l].reshape(1, HEAD_DIM) for v in (lambda_q1, lambda_k1, lambda_q2, lambda_k2)]
    gain = subln_gain[l].reshape(1, V_DIM)
    d_skip = ssm_d[l].reshape(1, SSM_WIDTH)
    lng = [v[l].reshape(1, D_MODEL) for v in (ln1_g, ln1_b, ln2_g, ln2_b)]

    cos_p, sin_p = _rope_tables(jnp.arange(n_p))
    cos_s, sin_s = _rope_tables(jnp.tile(past + jnp.arange(n_s), tm_s // n_s))

    q_p, k_p, v_p, u_p, kb_p, vt_p = _project(xp, w_qkvu, cos_p, sin_p, n_p, min(2 * tm, n_p), True)
    q_s, k_s, v_s, u_s = _project(xs, w_qkvu, cos_s, sin_s, n_s, tm_s, False)

    ao_p = _prompt_attention(lams, gain, q_p, kb_p, vt_p, bp, n_p, tq, lam_init)
    ao_s = _sample_attention(lams, gain, q_s, k_s, v_s,
                             cache_k[l].reshape(bs, past * N_HEADS, V_DIM),
                             cache_v[l].reshape(bs, past * N_HEADS, V_DIM), lam_init)

    tables = _ssm_params(ssm_a_re[l], ssm_a_im[l], ssm_b_re[l], ssm_b_im[l],
                         ssm_c_re[l], ssm_c_im[l], ssm_log_dt[l])
    x_cols_p = _chunk_cols(u_p.reshape(bp, n_chunk, CHUNK, SSM_WIDTH))
    x_cols_s = _chunk_cols(u_s.reshape(1, bs, CHUNK, SSM_WIDTH))
    s0 = jnp.concatenate([state_ssm_re[l], state_ssm_im[l]], axis=-1).transpose(1, 2, 0)
    s0 = jnp.pad(s0, ((0, 0), (0, 0), (0, LANES - bs)))
    y_cols_p, y_cols_s, st_p, st_s = _ssm(tables, x_cols_p, x_cols_s, s0, n_chunk)
    ys_p = _unchunk_cols(y_cols_p, bp, n_chunk).reshape(bp * n_p, SSM_WIDTH)
    ys_s = _unchunk_cols(y_cols_s, 1, bs).reshape(bs * n_s, SSM_WIDTH)
    sf_p = st_p[:, :, 0:bp]
    sf_s = st_s[:, :, 0:bs]

    outs = []
    for x2d, ao, ys, u, tile in ((xp, ao_p, ys_p, u_p, tm), (xs, ao_s, ys_s, u_s, tm_s)):
        big = 2 * tile if x2d.shape[0] >= 8 * tile else tile
        h = _merge(x2d, ao, ys, u, d_skip, w_gate, w_ap, w_glu, w_o, lng[0], lng[1], big)
        outs.append(_mlp(h, w1, w2, lng[2], lng[3], big))

    def states(sf):
        t = sf.transpose(2, 0, 1)
        return t[None, :, :, 0:STATE_DIM], t[None, :, :, STATE_DIM:]

    srp, sip = states(sf_p)
    srs, sis = states(sf_s)
    return (outs[0].reshape(bp, n_p, D_MODEL), outs[1].reshape(bs, n_s, D_MODEL),
            k_p.reshape(1, bp, n_p, N_HEADS, V_DIM), v_p.reshape(1, bp, n_p, N_HEADS, V_DIM), srp, sip,
            k_s.reshape(1, bs, n_s, N_HEADS, V_DIM), v_s.reshape(1, bs, n_s, N_HEADS, V_DIM), srs, sis)
```
